```python
import math
import jax
import jax.numpy as jnp
from jax import lax
import numpy as np

D_MODEL = 1024
BATCH = 8
SEQ = 8192
DEPTH = 2

N_MIXERS = 2
Q_BLOCK = 128
NORM_EPS = 1e-6
FOX_HEADS = 16
FOX_HEAD_DIM = D_MODEL // FOX_HEADS
FOX_WIDTH = FOX_HEADS * FOX_HEAD_DIM
MLA_HEADS = 16
MLA_NOPE_DIM = 64
MLA_ROPE_DIM = 32
MLA_V_DIM = 64
MLA_Q_RANK = 384
MLA_KV_RANK = 256
ROPE_BASE = 10000.0
D_FF = 2816
CONV_WIDTH = 3
N_FOX_LAYERS = (DEPTH + 1) // 2
N_MLA_LAYERS = DEPTH // 2

kernel_name = 'hybrid_fox_mla_convffn_adaln'


def rms_norm(x):
    xf = x.astype(jnp.float32)
    y = xf * lax.rsqrt(jnp.mean(xf * xf, axis=-1, keepdims=True) + NORM_EPS)
    return y.astype(x.dtype)


def ada_modulation(c, w, b):
    mod = jax.nn.silu(c) @ w + b
    shift, scale, gate = jnp.split(mod[:, None, :], 3, axis=-1)
    return shift, scale, gate


def causal_mask(start, seq):
    q_pos = start + jnp.arange(Q_BLOCK)
    return q_pos[:, None] >= jnp.arange(seq)[None, :]


def masked_softmax(s, mask):
    s = jnp.where(mask, s.astype(jnp.float32), -jnp.inf)
    return jax.nn.softmax(s, axis=-1)


def causal_block_sweep(block_fn, batch, seq):
    starts = jnp.arange(seq // Q_BLOCK, dtype=jnp.int32) * Q_BLOCK
    out = lax.map(block_fn, starts)
    return jnp.moveaxis(out, 0, 1).reshape(batch, seq, -1)


def forgetting_attention(h, w_in, b_f, w_o):
    B, S, _ = h.shape
    proj = h @ w_in
    q, k, v, f_logit = jnp.split(proj, [FOX_WIDTH, 2 * FOX_WIDTH, 3 * FOX_WIDTH], axis=-1)
    q = q.reshape(B, S, FOX_HEADS, FOX_HEAD_DIM)
    k = k.reshape(B, S, FOX_HEADS, FOX_HEAD_DIM)
    v = v.reshape(B, S, FOX_HEADS, FOX_HEAD_DIM)
    log_f = jax.nn.log_sigmoid((f_logit + b_f).astype(jnp.float32))
    cum = jnp.cumsum(log_f, axis=1).transpose(0, 2, 1)
    scale = FOX_HEAD_DIM ** -0.5

    def block(start):
        qb = lax.dynamic_slice_in_dim(q, start, Q_BLOCK, axis=1)
        cq = lax.dynamic_slice_in_dim(cum, start, Q_BLOCK, axis=2)
        s = jnp.einsum('bqhd,bkhd->bhqk', qb, k).astype(jnp.float32) * scale
        s = s + (cq[..., :, None] - cum[..., None, :])
        p = masked_softmax(s, causal_mask(start, S))
        return jnp.einsum('bhqk,bkhd->bqhd', p.astype(v.dtype), v)

    o = causal_block_sweep(block, B, S)
    return o @ w_o


def apply_rope(x, cos, sin):
    half = x.shape[-1] // 2
    x1, x2 = x[..., :half], x[..., half:]
    return jnp.concatenate([x1 * cos - x2 * sin, x2 * cos + x1 * sin], axis=-1)


def latent_attention(h, w_a, g_q, g_kv, w_uq, w_ukv, w_o):
    B, S, _ = h.shape
    a = h @ w_a
    c_q, c_kv, k_rope = jnp.split(a, [MLA_Q_RANK, MLA_Q_RANK + MLA_KV_RANK], axis=-1)
    c_q = rms_norm(c_q) * g_q
    c_kv = rms_norm(c_kv) * g_kv
    q = (c_q @ w_uq).reshape(B, S, MLA_HEADS, MLA_NOPE_DIM + MLA_ROPE_DIM)
    q_nope, q_rope = q[..., :MLA_NOPE_DIM], q[..., MLA_NOPE_DIM:]
    kv = (c_kv @ w_ukv).reshape(B, S, MLA_HEADS, MLA_NOPE_DIM + MLA_V_DIM)
    k_nope, v = kv[..., :MLA_NOPE_DIM], kv[..., MLA_NOPE_DIM:]

    pos = jnp.arange(S, dtype=jnp.float32)
    inv_freq = ROPE_BASE ** (-jnp.arange(0, MLA_ROPE_DIM, 2, dtype=jnp.float32) / MLA_ROPE_DIM)
    ang = pos[:, None] * inv_freq[None, :]
    cos = jnp.cos(ang).astype(h.dtype)
    sin = jnp.sin(ang).astype(h.dtype)
    q_rope = apply_rope(q_rope, cos[:, None, :], sin[:, None, :])
    k_rope = apply_rope(k_rope, cos, sin)
    scale = (MLA_NOPE_DIM + MLA_ROPE_DIM) ** -0.5

    def block(start):
        qn = lax.dynamic_slice_in_dim(q_nope, start, Q_BLOCK, axis=1)
        qr = lax.dynamic_slice_in_dim(q_rope, start, Q_BLOCK, axis=1)
        s = jnp.einsum('bqhd,bkhd->bhqk', qn, k_nope) + jnp.einsum('bqhr,bkr->bhqk', qr, k_rope)
        p = masked_softmax(s.astype(jnp.float32) * scale, causal_mask(start, S))
        return jnp.einsum('bhqk,bkhd->bqhd', p.astype(v.dtype), v)

    o = causal_block_sweep(block, B, S)
    return o @ w_o


def conv_gated_mlp(h, w_in, conv_w, conv_b, w_out):
    S = h.shape[1]
    u = h @ w_in
    u_pad = jnp.pad(u, ((0, 0), (CONV_WIDTH - 1, 0), (0, 0)))
    y = conv_b
    for j in range(CONV_WIDTH):
        y = y + conv_w[j] * u_pad[:, j:j + S, :]
    gate, val = jnp.split(y, 2, axis=-1)
    return (jax.nn.silu(gate) * val) @ w_out


def _fwd_setup_inputs(seed: int = 0) -> dict:
    key = jax.random.key(seed)
    ks = jax.random.split(key, 20)
    D = D_MODEL
    nrm = jax.random.normal
    f32 = jnp.float32
    x = nrm(ks[0], (BATCH, SEQ, D), f32)
    c = nrm(ks[1], (BATCH, D), f32)
    ada_w = nrm(ks[2], (DEPTH, 2, D, 3 * D), f32) * D ** -0.5
    ada_b = 0.02 * nrm(ks[3], (DEPTH, 2, 3 * D), f32)
    fox_w_in = nrm(ks[4], (N_FOX_LAYERS, D, 3 * FOX_WIDTH + FOX_HEADS), f32) * D ** -0.5
    fox_w_in = fox_w_in.at[..., 3 * FOX_WIDTH:].multiply(0.1)
    fox_b_f = jnp.linspace(2.0, 6.0, FOX_HEADS, dtype=f32)[None, :] + 0.1 * nrm(ks[5], (N_FOX_LAYERS, FOX_HEADS), f32)
    fox_w_o = nrm(ks[6], (N_FOX_LAYERS, FOX_WIDTH, D), f32) * FOX_WIDTH ** -0.5
    mla_w_a = nrm(ks[7], (N_MLA_LAYERS, D, MLA_Q_RANK + MLA_KV_RANK + MLA_ROPE_DIM), f32) * D ** -0.5
    mla_g_q = 1.0 + 0.02 * nrm(ks[8], (N_MLA_LAYERS, MLA_Q_RANK), f32)
    mla_g_kv = 1.0 + 0.02 * nrm(ks[9], (N_MLA_LAYERS, MLA_KV_RANK), f32)
    mla_w_uq = nrm(ks[10], (N_MLA_LAYERS, MLA_Q_RANK, MLA_HEADS * (MLA_NOPE_DIM + MLA_ROPE_DIM)), f32) * MLA_Q_RANK ** -0.5
    mla_w_ukv = nrm(ks[11], (N_MLA_LAYERS, MLA_KV_RANK, MLA_HEADS * (MLA_NOPE_DIM + MLA_V_DIM)), f32) * MLA_KV_RANK ** -0.5
    mla_w_o = nrm(ks[12], (N_MLA_LAYERS, MLA_HEADS * MLA_V_DIM, D), f32) * (MLA_HEADS * MLA_V_DIM) ** -0.5
    ffn_w_in = nrm(ks[13], (DEPTH, D, 2 * D_FF), f32) * D ** -0.5
    ffn_conv_w = nrm(ks[14], (DEPTH, CONV_WIDTH, 2 * D_FF), f32) * CONV_WIDTH ** -0.5
    ffn_conv_b = 0.02 * nrm(ks[15], (DEPTH, 2 * D_FF), f32)
    ffn_w_out = nrm(ks[16], (DEPTH, D_FF, D), f32) * D_FF ** -0.5
    final_g = 1.0 + 0.02 * nrm(ks[17], (D,), f32)
    return {'x': x, 'c': c, 'ada_w': ada_w, 'ada_b': ada_b,
            'fox_w_in': fox_w_in, 'fox_b_f': fox_b_f, 'fox_w_o': fox_w_o,
            'mla_w_a': mla_w_a, 'mla_g_q': mla_g_q, 'mla_g_kv': mla_g_kv,
            'mla_w_uq': mla_w_uq, 'mla_w_ukv': mla_w_ukv, 'mla_w_o': mla_w_o,
            'ffn_w_in': ffn_w_in, 'ffn_conv_w': ffn_conv_w, 'ffn_conv_b': ffn_conv_b,
            'ffn_w_out': ffn_w_out, 'final_g': final_g}


def _fwd_reference(x, c, ada_w, ada_b, fox_w_in, fox_b_f, fox_w_o, mla_w_a, mla_g_q, mla_g_kv,
              mla_w_uq, mla_w_ukv, mla_w_o, ffn_w_in, ffn_conv_w, ffn_conv_b, ffn_w_out, final_g):
    for i in range(DEPTH):
        j = i // N_MIXERS
        shift, scale, gate = ada_modulation(c, ada_w[i, 0], ada_b[i, 0])
        h = rms_norm(x) * (1.0 + scale) + shift
        if i % N_MIXERS == 0:
            y = forgetting_attention(h, fox_w_in[j], fox_b_f[j], fox_w_o[j])
        else:
            y = latent_attention(h, mla_w_a[j], mla_g_q[j], mla_g_kv[j],
                                 mla_w_uq[j], mla_w_ukv[j], mla_w_o[j])
        x = x + gate * y
        shift, scale, gate = ada_modulation(c, ada_w[i, 1], ada_b[i, 1])
        h = rms_norm(x) * (1.0 + scale) + shift
        x = x + gate * conv_gated_mlp(h, ffn_w_in[i], ffn_conv_w[i], ffn_conv_b[i], ffn_w_out[i])
    return rms_norm(x) * final_g


import jax as _jax
import jax.numpy as _jnp

TWIN_FORMAT = 'train_step'
FWD_PARAMS = ['x', 'c', 'ada_w', 'ada_b', 'fox_w_in', 'fox_b_f', 'fox_w_o', 'mla_w_a', 'mla_g_q', 'mla_g_kv', 'mla_w_uq', 'mla_w_ukv', 'mla_w_o', 'ffn_w_in', 'ffn_conv_w', 'ffn_conv_b', 'ffn_w_out', 'final_g']
TWIN_WEIGHTS = ['ada_w', 'ada_b', 'fox_w_in', 'fox_b_f', 'fox_w_o', 'mla_w_a', 'mla_g_q', 'mla_g_kv', 'mla_w_uq', 'mla_w_ukv', 'mla_w_o', 'ffn_w_in', 'ffn_conv_w', 'ffn_conv_b', 'ffn_w_out', 'final_g']
TWIN_DIFF_INPUT = 'x'
TWIN_INPUTS = ['x', 'c', 'ada_w', 'ada_b', 'fox_w_in', 'fox_b_f', 'fox_w_o', 'mla_w_a', 'mla_g_q', 'mla_g_kv', 'mla_w_uq', 'mla_w_ukv', 'mla_w_o', 'ffn_w_in', 'ffn_conv_w', 'ffn_conv_b', 'ffn_w_out', 'final_g', 'loss_target', 'm_ada_w', 'm_ada_b', 'm_fox_w_in', 'm_fox_b_f', 'm_fox_w_o', 'm_mla_w_a', 'm_mla_g_q', 'm_mla_g_kv', 'm_mla_w_uq', 'm_mla_w_ukv', 'm_mla_w_o', 'm_ffn_w_in', 'm_ffn_conv_w', 'm_ffn_conv_b', 'm_ffn_w_out', 'm_final_g', 'v_ada_w', 'v_ada_b', 'v_fox_w_in', 'v_fox_b_f', 'v_fox_w_o', 'v_mla_w_a', 'v_mla_g_q', 'v_mla_g_kv', 'v_mla_w_uq', 'v_mla_w_ukv', 'v_mla_w_o', 'v_ffn_w_in', 'v_ffn_conv_w', 'v_ffn_conv_b', 'v_ffn_w_out', 'v_final_g']
TWIN_OUTPUTS = ['loss', 'grad_x', 'grad_ada_w', 'grad_ada_b', 'grad_fox_w_in', 'grad_fox_b_f', 'grad_fox_w_o', 'grad_mla_w_a', 'grad_mla_g_q', 'grad_mla_g_kv', 'grad_mla_w_uq', 'grad_mla_w_ukv', 'grad_mla_w_o', 'grad_ffn_w_in', 'grad_ffn_conv_w', 'grad_ffn_conv_b', 'grad_ffn_w_out', 'grad_final_g', 'delta_ada_w', 'delta_ada_b', 'delta_fox_w_in', 'delta_fox_b_f', 'delta_fox_w_o', 'delta_mla_w_a', 'delta_mla_g_q', 'delta_mla_g_kv', 'delta_mla_w_uq', 'delta_mla_w_ukv', 'delta_mla_w_o', 'delta_ffn_w_in', 'delta_ffn_conv_w', 'delta_ffn_conv_b', 'delta_ffn_w_out', 'delta_final_g', 'new_m_ada_w', 'new_m_ada_b', 'new_m_fox_w_in', 'new_m_fox_b_f', 'new_m_fox_w_o', 'new_m_mla_w_a', 'new_m_mla_g_q', 'new_m_mla_g_kv', 'new_m_mla_w_uq', 'new_m_mla_w_ukv', 'new_m_mla_w_o', 'new_m_ffn_w_in', 'new_m_ffn_conv_w', 'new_m_ffn_conv_b', 'new_m_ffn_w_out', 'new_m_final_g', 'new_v_ada_w', 'new_v_ada_b', 'new_v_fox_w_in', 'new_v_fox_b_f', 'new_v_fox_w_o', 'new_v_mla_w_a', 'new_v_mla_g_q', 'new_v_mla_g_kv', 'new_v_mla_w_uq', 'new_v_mla_w_ukv', 'new_v_mla_w_o', 'new_v_ffn_w_in', 'new_v_ffn_conv_w', 'new_v_ffn_conv_b', 'new_v_ffn_w_out', 'new_v_final_g']
TWIN_LEAF_KINDS = {'loss': 'loss', 'grad_x': 'grad_x', 'grad_ada_w': 'grad_w', 'grad_ada_b': 'grad_w', 'grad_fox_w_in': 'grad_w', 'grad_fox_b_f': 'grad_w', 'grad_fox_w_o': 'grad_w', 'grad_mla_w_a': 'grad_w', 'grad_mla_g_q': 'grad_w', 'grad_mla_g_kv': 'grad_w', 'grad_mla_w_uq': 'grad_w', 'grad_mla_w_ukv': 'grad_w', 'grad_mla_w_o': 'grad_w', 'grad_ffn_w_in': 'grad_w', 'grad_ffn_conv_w': 'grad_w', 'grad_ffn_conv_b': 'grad_w', 'grad_ffn_w_out': 'grad_w', 'grad_final_g': 'grad_w', 'delta_ada_w': 'delta_w', 'delta_ada_b': 'delta_w', 'delta_fox_w_in': 'delta_w', 'delta_fox_b_f': 'delta_w', 'delta_fox_w_o': 'delta_w', 'delta_mla_w_a': 'delta_w', 'delta_mla_g_q': 'delta_w', 'delta_mla_g_kv': 'delta_w', 'delta_mla_w_uq': 'delta_w', 'delta_mla_w_ukv': 'delta_w', 'delta_mla_w_o': 'delta_w', 'delta_ffn_w_in': 'delta_w', 'delta_ffn_conv_w': 'delta_w', 'delta_ffn_conv_b': 'delta_w', 'delta_ffn_w_out': 'delta_w', 'delta_final_g': 'delta_w', 'new_m_ada_w': 'new_m', 'new_m_ada_b': 'new_m', 'new_m_fox_w_in': 'new_m', 'new_m_fox_b_f': 'new_m', 'new_m_fox_w_o': 'new_m', 'new_m_mla_w_a': 'new_m', 'new_m_mla_g_q': 'new_m', 'new_m_mla_g_kv': 'new_m', 'new_m_mla_w_uq': 'new_m', 'new_m_mla_w_ukv': 'new_m', 'new_m_mla_w_o': 'new_m', 'new_m_ffn_w_in': 'new_m', 'new_m_ffn_conv_w': 'new_m', 'new_m_ffn_conv_b': 'new_m', 'new_m_ffn_w_out': 'new_m', 'new_m_final_g': 'new_m', 'new_v_ada_w': 'new_v', 'new_v_ada_b': 'new_v', 'new_v_fox_w_in': 'new_v', 'new_v_fox_b_f': 'new_v', 'new_v_fox_w_o': 'new_v', 'new_v_mla_w_a': 'new_v', 'new_v_mla_g_q': 'new_v', 'new_v_mla_g_kv': 'new_v', 'new_v_mla_w_uq': 'new_v', 'new_v_mla_w_ukv': 'new_v', 'new_v_mla_w_o': 'new_v', 'new_v_ffn_w_in': 'new_v', 'new_v_ffn_conv_w': 'new_v', 'new_v_ffn_conv_b': 'new_v', 'new_v_ffn_w_out': 'new_v', 'new_v_final_g': 'new_v'}


def _forward(args):
    return _fwd_reference(*[args[k] for k in FWD_PARAMS])


def _output_shape():
    def fwd():
        inp = _fwd_setup_inputs(0)
        return _fwd_reference(*[inp[k] for k in FWD_PARAMS])
    out = _jax.eval_shape(fwd)
    return out.shape, out.dtype

N_MICROBATCH = 1
ADAM_LR = 0.001
ADAM_B1 = 0.9
ADAM_B2 = 0.999
ADAM_EPS = 1e-08
ADAM_WD = 0.01
ADAM_STEP = 10
PER_EXAMPLE_BATCH_AXIS = {'x': 0, 'c': 0, 'loss_target': 0}
SHARED_INPUTS = []
_WEIGHT_DTYPES = {'ada_w': _jnp.float32, 'ada_b': _jnp.float32, 'fox_w_in': _jnp.float32, 'fox_b_f': _jnp.float32, 'fox_w_o': _jnp.float32, 'mla_w_a': _jnp.float32, 'mla_g_q': _jnp.float32, 'mla_g_kv': _jnp.float32, 'mla_w_uq': _jnp.float32, 'mla_w_ukv': _jnp.float32, 'mla_w_o': _jnp.float32, 'ffn_w_in': _jnp.float32, 'ffn_conv_w': _jnp.float32, 'ffn_conv_b': _jnp.float32, 'ffn_w_out': _jnp.float32, 'final_g': _jnp.float32}
MOMENT_SCALE = {'ada_w': 1.064544e-01, 'ada_b': 1.833981e-01, 'fox_w_in': 9.249944e-02, 'fox_b_f': 3.026231e-01, 'fox_w_o': 1.294517e-01, 'mla_w_a': 1.037958e-01, 'mla_g_q': 2.904935e-02, 'mla_g_kv': 1.452803e-01, 'mla_w_uq': 1.412656e-02, 'mla_w_ukv': 5.978248e-02, 'mla_w_o': 8.664233e-02, 'ffn_w_in': 7.806114e-02, 'ffn_conv_w': 8.060023e-02, 'ffn_conv_b': 6.169290e-02, 'ffn_w_out': 1.305239e-01, 'final_g': 6.515634e+01}


def _to_microbatches(a, axis):
    t = _jnp.moveaxis(a, axis, 0)
    t = t.reshape((N_MICROBATCH, t.shape[0] // N_MICROBATCH) + t.shape[1:])
    return _jnp.moveaxis(t, 1, axis + 1)


def setup_inputs(seed: int = 0) -> dict:
    inp = _fwd_setup_inputs(seed)
    key = _jax.random.fold_in(_jax.random.key(seed), 7919)
    shape, _ = _output_shape()
    out = dict(inp)
    out["loss_target"] = _jax.random.normal(_jax.random.fold_in(key, 0), shape, _jnp.float32)
    for i, name in enumerate(TWIN_WEIGHTS):
        w = inp[name].astype(_jnp.float32)
        if MOMENT_SCALE is None:
            s = _jnp.sqrt(_jnp.mean(_jnp.square(w)) + 1e-30)
        else:
            s = MOMENT_SCALE[name]
        km, kv = _jax.random.split(_jax.random.fold_in(key, i + 1))
        out[name] = w
        out["m_" + name] = s * _jax.random.normal(km, w.shape, _jnp.float32)
        out["v_" + name] = (s * s) * _jax.random.uniform(kv, w.shape, _jnp.float32, 0.5, 1.5)
    if N_MICROBATCH > 1:
        for name, axis in PER_EXAMPLE_BATCH_AXIS.items():
            out[name] = _to_microbatches(out[name], axis)
    return {'x': out['x'], 'c': out['c'], 'ada_w': out['ada_w'], 'ada_b': out['ada_b'], 'fox_w_in': out['fox_w_in'], 'fox_b_f': out['fox_b_f'], 'fox_w_o': out['fox_w_o'], 'mla_w_a': out['mla_w_a'], 'mla_g_q': out['mla_g_q'], 'mla_g_kv': out['mla_g_kv'], 'mla_w_uq': out['mla_w_uq'], 'mla_w_ukv': out['mla_w_ukv'], 'mla_w_o': out['mla_w_o'], 'ffn_w_in': out['ffn_w_in'], 'ffn_conv_w': out['ffn_conv_w'], 'ffn_conv_b': out['ffn_conv_b'], 'ffn_w_out': out['ffn_w_out'], 'final_g': out['final_g'], 'loss_target': out['loss_target'], 'm_ada_w': out['m_ada_w'], 'm_ada_b': out['m_ada_b'], 'm_fox_w_in': out['m_fox_w_in'], 'm_fox_b_f': out['m_fox_b_f'], 'm_fox_w_o': out['m_fox_w_o'], 'm_mla_w_a': out['m_mla_w_a'], 'm_mla_g_q': out['m_mla_g_q'], 'm_mla_g_kv': out['m_mla_g_kv'], 'm_mla_w_uq': out['m_mla_w_uq'], 'm_mla_w_ukv': out['m_mla_w_ukv'], 'm_mla_w_o': out['m_mla_w_o'], 'm_ffn_w_in': out['m_ffn_w_in'], 'm_ffn_conv_w': out['m_ffn_conv_w'], 'm_ffn_conv_b': out['m_ffn_conv_b'], 'm_ffn_w_out': out['m_ffn_w_out'], 'm_final_g': out['m_final_g'], 'v_ada_w': out['v_ada_w'], 'v_ada_b': out['v_ada_b'], 'v_fox_w_in': out['v_fox_w_in'], 'v_fox_b_f': out['v_fox_b_f'], 'v_fox_w_o': out['v_fox_w_o'], 'v_mla_w_a': out['v_mla_w_a'], 'v_mla_g_q': out['v_mla_g_q'], 'v_mla_g_kv': out['v_mla_g_kv'], 'v_mla_w_uq': out['v_mla_w_uq'], 'v_mla_w_ukv': out['v_mla_w_ukv'], 'v_mla_w_o': out['v_mla_w_o'], 'v_ffn_w_in': out['v_ffn_w_in'], 'v_ffn_conv_w': out['v_ffn_conv_w'], 'v_ffn_conv_b': out['v_ffn_conv_b'], 'v_ffn_w_out': out['v_ffn_w_out'], 'v_final_g': out['v_final_g']}


def _loss(weights, diff, rest, loss_target):
    with _jax.named_scope("forward"):
        args = {**rest, TWIN_DIFF_INPUT: diff, **{k: w.astype(_WEIGHT_DTYPES[k]) for k, w in weights.items()}}
        y = _forward(args)
    with _jax.named_scope("loss_head"):
        err = _jnp.square(y.astype(_jnp.float32) - loss_target)
        return 0.5 * _jnp.sum(_jnp.mean(err, axis=-1)) if err.ndim else 0.5 * err


def _adamw(w, g, m, v):
    m = ADAM_B1 * m + (1.0 - ADAM_B1) * g
    v = ADAM_B2 * v + (1.0 - ADAM_B2) * _jnp.square(g)
    m_hat = m / (1.0 - ADAM_B1 ** ADAM_STEP)
    v_hat = v / (1.0 - ADAM_B2 ** ADAM_STEP)
    delta = -ADAM_LR * (m_hat / (_jnp.sqrt(v_hat) + ADAM_EPS) + ADAM_WD * w)
    return delta, m, v


def reference(x, c, ada_w, ada_b, fox_w_in, fox_b_f, fox_w_o, mla_w_a, mla_g_q, mla_g_kv, mla_w_uq, mla_w_ukv, mla_w_o, ffn_w_in, ffn_conv_w, ffn_conv_b, ffn_w_out, final_g, loss_target, m_ada_w, m_ada_b, m_fox_w_in, m_fox_b_f, m_fox_w_o, m_mla_w_a, m_mla_g_q, m_mla_g_kv, m_mla_w_uq, m_mla_w_ukv, m_mla_w_o, m_ffn_w_in, m_ffn_conv_w, m_ffn_conv_b, m_ffn_w_out, m_final_g, v_ada_w, v_ada_b, v_fox_w_in, v_fox_b_f, v_fox_w_o, v_mla_w_a, v_mla_g_q, v_mla_g_kv, v_mla_w_uq, v_mla_w_ukv, v_mla_w_o, v_ffn_w_in, v_ffn_conv_w, v_ffn_conv_b, v_ffn_w_out, v_final_g):
    given = dict(x=x, c=c, ada_w=ada_w, ada_b=ada_b, fox_w_in=fox_w_in, fox_b_f=fox_b_f, fox_w_o=fox_w_o, mla_w_a=mla_w_a, mla_g_q=mla_g_q, mla_g_kv=mla_g_kv, mla_w_uq=mla_w_uq, mla_w_ukv=mla_w_ukv, mla_w_o=mla_w_o, ffn_w_in=ffn_w_in, ffn_conv_w=ffn_conv_w, ffn_conv_b=ffn_conv_b, ffn_w_out=ffn_w_out, final_g=final_g, loss_target=loss_target, m_ada_w=m_ada_w, m_ada_b=m_ada_b, m_fox_w_in=m_fox_w_in, m_fox_b_f=m_fox_b_f, m_fox_w_o=m_fox_w_o, m_mla_w_a=m_mla_w_a, m_mla_g_q=m_mla_g_q, m_mla_g_kv=m_mla_g_kv, m_mla_w_uq=m_mla_w_uq, m_mla_w_ukv=m_mla_w_ukv, m_mla_w_o=m_mla_w_o, m_ffn_w_in=m_ffn_w_in, m_ffn_conv_w=m_ffn_conv_w, m_ffn_conv_b=m_ffn_conv_b, m_ffn_w_out=m_ffn_w_out, m_final_g=m_final_g, v_ada_w=v_ada_w, v_ada_b=v_ada_b, v_fox_w_in=v_fox_w_in, v_fox_b_f=v_fox_b_f, v_fox_w_o=v_fox_w_o, v_mla_w_a=v_mla_w_a, v_mla_g_q=v_mla_g_q, v_mla_g_kv=v_mla_g_kv, v_mla_w_uq=v_mla_w_uq, v_mla_w_ukv=v_mla_w_ukv, v_mla_w_o=v_mla_w_o, v_ffn_w_in=v_ffn_w_in, v_ffn_conv_w=v_ffn_conv_w, v_ffn_conv_b=v_ffn_conv_b, v_ffn_w_out=v_ffn_w_out, v_final_g=v_final_g)
    weights = {n: given[n] for n in TWIN_WEIGHTS}
    shared = {n: given[n] for n in SHARED_INPUTS}
    per_example = {n: given[n] for n in ['x', 'c']}
    grad_fn = _jax.value_and_grad(_loss, argnums=(0, 1))

    def one_microbatch(ex, loss_target):
        ex = dict(ex)
        diff = ex.pop(TWIN_DIFF_INPUT)
        return grad_fn(weights, diff, {**shared, **ex}, loss_target)

    if N_MICROBATCH == 1:
        loss, (grad_w, grad_x) = one_microbatch(per_example, given["loss_target"])
    else:
        def body(carry, xs):
            loss_sum, grad_sum = carry
            l_k, (gw_k, gx_k) = one_microbatch(xs[0], xs[1])
            with _jax.named_scope("update"):
                return (loss_sum + l_k, _jax.tree.map(_jnp.add, grad_sum, gw_k)), gx_k

        init = (_jnp.zeros((), _jnp.float32), _jax.tree.map(_jnp.zeros_like, weights))
        (loss, grad_w), grad_x = _jax.lax.scan(body, init, (per_example, given["loss_target"]))
    with _jax.named_scope("update"):
        delta_w, new_m, new_v = {}, {}, {}
        for n in TWIN_WEIGHTS:
            delta_w[n], new_m[n], new_v[n] = _adamw(weights[n], grad_w[n], given["m_" + n], given["v_" + n])
    return (loss, grad_x, *[grad_w[n] for n in TWIN_WEIGHTS], *[delta_w[n] for n in TWIN_WEIGHTS],
            *[new_m[n] for n in TWIN_WEIGHTS], *[new_v[n] for n in TWIN_WEIGHTS])
```

```python
import functools
import math

import numpy as np
import jax
import jax.numpy as jnp
from jax import lax
from jax.experimental import pallas as pl
from jax.experimental.pallas import tpu as pltpu

F32 = jnp.float32
BF16 = jnp.bfloat16
MESH = pl.DeviceIdType.MESH

D_MODEL = 1024
HEADS = 16
HEAD_PAD = 128
FOX_HEAD = 64
MLA_NOPE = 64
MLA_ROPE = 32
MLA_V = 64
MLA_QR = 384
MLA_KVR = 256
D_FF = 2816
NORM_EPS = 1e-6
ROPE_BASE = 10000.0
ADAM_LR = 0.001
ADAM_B1 = 0.9
ADAM_B2 = 0.999
ADAM_EPS = 1e-08
ADAM_WD = 0.01
ADAM_STEP = 10
NEG_BIG = -1e30
VMEM_LIMIT = 56 * 1024 * 1024
LANE = 128
PACK_ROW = 1024
PACK_ROWS_ALIGN = 256

NT = (((1,), (1,)), ((), ()))
TN = (((0,), (0,)), ((), ()))
NN = (((1,), (0,)), ((), ()))


def _cparams():
    return pltpu.CompilerParams(vmem_limit_bytes=VMEM_LIMIT)


def _tile(n, cap):
    if n <= cap:
        return n
    for t in range(cap - cap % LANE, 0, -LANE):
        if n % t == 0:
            return t
    raise ValueError((n, cap))


def _row_tile(rows, cols, limit_bytes=1 << 20):
    best = None
    for t in range(8, rows + 1, 8):
        if rows % t == 0 and t * cols * 4 <= limit_bytes:
            best = t
    return best if best is not None else rows


def _me():
    return lax.axis_index("x"), lax.axis_index("y"), lax.axis_index("c")


def _all_gather8(v, name):
    R, N = v.shape

    def body(v_ref, o_ref, ssem, rsem):
        x, y, c = _me()
        me = 4 * x + 2 * y + c
        o_ref[me] = v_ref[...]
        copies = []
        for k in range(1, 8):
            peer = (1 - x if k & 4 else x, 1 - y if k & 2 else y, 1 - c if k & 1 else c)
            cp = pltpu.make_async_remote_copy(
                src_ref=v_ref, dst_ref=o_ref.at[me], send_sem=ssem.at[k - 1], recv_sem=rsem.at[k - 1],
                device_id=peer, device_id_type=MESH)
            cp.start()
            copies.append(cp)
        for cp in copies:
            cp.wait()

    return pl.pallas_call(
        body, name=name,
        out_shape=jax.ShapeDtypeStruct((8, R, N), v.dtype),
        in_specs=[pl.BlockSpec(memory_space=pltpu.VMEM)],
        out_specs=pl.BlockSpec(memory_space=pltpu.VMEM),
        scratch_shapes=[pltpu.SemaphoreType.DMA((7,)), pltpu.SemaphoreType.DMA((7,))],
    )(v)


def _all_gather_chips(shards, name):
    n = len(shards)

    def body(*refs):
        ins, outs = refs[:n], refs[n:2 * n]
        ssem, rsem, lsem = refs[2 * n:]
        x, y, c = _me()
        me = 2 * x + y
        local, remote = [], []
        for i in range(n):
            cp = pltpu.make_async_copy(ins[i], outs[i].at[me], lsem.at[i])
            cp.start()
            local.append(cp)
            for k in (1, 2, 3):
                peer = (1 - x if k & 2 else x, 1 - y if k & 1 else y, c)
                cp = pltpu.make_async_remote_copy(
                    src_ref=ins[i], dst_ref=outs[i].at[me], send_sem=ssem.at[3 * i + k - 1],
                    recv_sem=rsem.at[3 * i + k - 1], device_id=peer, device_id_type=MESH)
                cp.start()
                remote.append(cp)
        for cp in local:
            cp.wait()
        for cp in remote:
            cp.wait()

    return pl.pallas_call(
        body, name=name,
        out_shape=[jax.ShapeDtypeStruct((4,) + s.shape, s.dtype) for s in shards],
        in_specs=[pl.BlockSpec(memory_space=pl.ANY)] * n,
        out_specs=[pl.BlockSpec(memory_space=pl.ANY)] * n,
        scratch_shapes=[pltpu.SemaphoreType.DMA((3 * n,)), pltpu.SemaphoreType.DMA((3 * n,)),
                        pltpu.SemaphoreType.DMA((n,))],
    )(*shards)


def _exchange_halves(g):
    _, _, R, P = g.shape

    def body(g_ref, o_ref, ssem, rsem):
        x, y, c = _me()
        copies = []
        for s in range(4):
            cp = pltpu.make_async_remote_copy(
                src_ref=g_ref.at[s, 1 - c], dst_ref=o_ref.at[s], send_sem=ssem.at[s], recv_sem=rsem.at[s],
                device_id=(x, y, 1 - c), device_id_type=MESH)
            cp.start()
            copies.append(cp)
        for cp in copies:
            cp.wait()

    return pl.pallas_call(
        body, name="grad_exchange_sibling",
        out_shape=jax.ShapeDtypeStruct((4, R, P), g.dtype),
        in_specs=[pl.BlockSpec(memory_space=pl.ANY)],
        out_specs=pl.BlockSpec(memory_space=pl.ANY),
        scratch_shapes=[pltpu.SemaphoreType.DMA((4,)), pltpu.SemaphoreType.DMA((4,))],
    )(g)


def _exchange_chips(a):
    _, R, P = a.shape

    def body(a_ref, o_ref, ssem, rsem, lsem):
        x, y, c = _me()
        me = 2 * x + y
        own = pltpu.make_async_copy(a_ref.at[me], o_ref.at[me], lsem)
        own.start()
        copies = []
        for k in (1, 2, 3):
            px, py = (1 - x if k & 2 else x), (1 - y if k & 1 else y)
            cp = pltpu.make_async_remote_copy(
                src_ref=a_ref.at[2 * px + py], dst_ref=o_ref.at[me], send_sem=ssem.at[k - 1],
                recv_sem=rsem.at[k - 1], device_id=(px, py, c), device_id_type=MESH)
            cp.start()
            copies.append(cp)
        own.wait()
        for cp in copies:
            cp.wait()

    return pl.pallas_call(
        body, name="grad_exchange_chips",
        out_shape=jax.ShapeDtypeStruct((4, R, P), a.dtype),
        in_specs=[pl.BlockSpec(memory_space=pl.ANY)],
        out_specs=pl.BlockSpec(memory_space=pl.ANY),
        scratch_shapes=[pltpu.SemaphoreType.DMA((3,)), pltpu.SemaphoreType.DMA((3,)), pltpu.SemaphoreType.DMA],
    )(a)


def _share_halves(f):
    R, P = f.shape

    def body(f_ref, o_ref, ssem, rsem, lsem):
        x, y, c = _me()
        own = pltpu.make_async_copy(f_ref, o_ref.at[c], lsem)
        own.start()
        cp = pltpu.make_async_remote_copy(
            src_ref=f_ref, dst_ref=o_ref.at[c], send_sem=ssem, recv_sem=rsem,
            device_id=(x, y, 1 - c), device_id_type=MESH)
        cp.start()
        own.wait()
        cp.wait()

    return pl.pallas_call(
        body, name="grad_share_sibling",
        out_shape=jax.ShapeDtypeStruct((2, R, P), f.dtype),
        in_specs=[pl.BlockSpec(memory_space=pl.ANY)],
        out_specs=pl.BlockSpec(memory_space=pl.ANY),
        scratch_shapes=[pltpu.SemaphoreType.DMA, pltpu.SemaphoreType.DMA, pltpu.SemaphoreType.DMA],
    )(f)


def _add_halves(g, r, c_idx):
    _, _, R, P = g.shape
    t = PACK_ROWS_ALIGN

    def body(c_ref, g_ref, r_ref, o_ref):
        o_ref[...] = g_ref[0] + r_ref[...]

    return pl.pallas_call(
        body, name="grad_add_halves",
        out_shape=jax.ShapeDtypeStruct((4, R, P), F32),
        grid_spec=pltpu.PrefetchScalarGridSpec(
            num_scalar_prefetch=1, grid=(4, R // t),
            in_specs=[pl.BlockSpec((1, 1, t, P), lambda s, i, c_ref: (s, c_ref[0], i, 0)),
                      pl.BlockSpec((1, t, P), lambda s, i, c_ref: (s, i, 0))],
            out_specs=pl.BlockSpec((1, t, P), lambda s, i, c_ref: (s, i, 0))),
        compiler_params=_cparams(),
    )(c_idx, g, r)


def _add_chips(b):
    _, R, P = b.shape
    t = PACK_ROWS_ALIGN

    def body(b_ref, o_ref):
        o_ref[...] = ((b_ref[0] + b_ref[1]) + b_ref[2]) + b_ref[3]

    return pl.pallas_call(
        body, name="grad_add_chips",
        out_shape=jax.ShapeDtypeStruct((R, P), F32),
        grid=(R // t,),
        in_specs=[pl.BlockSpec((4, t, P), lambda i: (0, i, 0))],
        out_specs=pl.BlockSpec((t, P), lambda i: (i, 0)),
        compiler_params=_cparams(),
    )(b)


def _mm(a, b, *, ta=False, tb=False, out_dtype=F32, res=None, name):
    if ta:
        K, M = a.shape
    else:
        M, K = a.shape
    N = b.shape[0] if tb else b.shape[1]
    assert (b.shape[1] if tb else b.shape[0]) == K, (a.shape, b.shape, ta, tb)
    tm, tn, tk = _tile(M, 512), _tile(N, 1408), _tile(K, 1024)
    nk = K // tk
    has_res = res is not None
    dims = (((0 if ta else 1,), (1 if tb else 0,)), ((), ()))

    def body(*refs):
        if has_res:
            a_ref, b_ref, r_ref, o_ref, acc_ref = refs
        else:
            a_ref, b_ref, o_ref, acc_ref = refs
        k = pl.program_id(2)

        @pl.when(k == 0)
        def _():
            acc_ref[...] = jnp.zeros_like(acc_ref)

        acc_ref[...] += lax.dot_general(a_ref[...].astype(BF16), b_ref[...].astype(BF16), dims,
                                        preferred_element_type=F32)

        @pl.when(k == nk - 1)
        def _():
            r = acc_ref[...]
            if has_res:
                r = r + r_ref[...]
            o_ref[...] = r.astype(out_dtype)

    a_spec = (pl.BlockSpec((tk, tm), lambda i, j, k: (k, i)) if ta
              else pl.BlockSpec((tm, tk), lambda i, j, k: (i, k)))
    b_spec = (pl.BlockSpec((tn, tk), lambda i, j, k: (j, k)) if tb
              else pl.BlockSpec((tk, tn), lambda i, j, k: (k, j)))
    in_specs = [a_spec, b_spec]
    args = [a, b]
    if has_res:
        in_specs.append(pl.BlockSpec((tm, tn), lambda i, j, k: (i, j)))
        args.append(res)
    return pl.pallas_call(
        body, name=name,
        out_shape=jax.ShapeDtypeStruct((M, N), out_dtype),
        grid=(M // tm, N // tn, nk),
        in_specs=in_specs,
        out_specs=pl.BlockSpec((tm, tn), lambda i, j, k: (i, j)),
        scratch_shapes=[pltpu.VMEM((tm, tn), F32)],
        compiler_params=_cparams(),
    )(*args)


def _rstd(xv):
    return lax.rsqrt(jnp.mean(xv * xv, axis=-1, keepdims=True) + NORM_EPS)


def _norm_bwd(n, r, dn):
    return r * (dn - n * jnp.mean(dn * n, axis=-1, keepdims=True))


def _adaln_fwd(x, scale, shift, name):
    S, D = x.shape
    t = _tile(S, 512)

    def body(x_ref, sc_ref, sh_ref, h_ref):
        xv = x_ref[...]
        h_ref[...] = (xv * _rstd(xv) * (1.0 + sc_ref[...]) + sh_ref[...]).astype(BF16)

    row = pl.BlockSpec((t, D), lambda i: (i, 0))
    vec = pl.BlockSpec((1, D), lambda i: (0, 0))
    return pl.pallas_call(
        body, name=name, out_shape=jax.ShapeDtypeStruct((S, D), BF16), grid=(S // t,),
        in_specs=[row, vec, vec], out_specs=row, compiler_params=_cparams(),
    )(x, scale, shift)


def _adaln_bwd(x, dh, dx_next, scale, name):
    S, D = x.shape
    t = _tile(S, 512)

    def body(x_ref, dh_ref, dxn_ref, sc_ref, dx_ref, dsc_ref, dsh_ref):
        @pl.when(pl.program_id(0) == 0)
        def _():
            dsc_ref[...] = jnp.zeros_like(dsc_ref)
            dsh_ref[...] = jnp.zeros_like(dsh_ref)

        xv = x_ref[...]
        r = _rstd(xv)
        n = xv * r
        dh = dh_ref[...]
        dx_ref[...] = dxn_ref[...] + _norm_bwd(n, r, dh * (1.0 + sc_ref[...]))
        dsc_ref[...] += jnp.sum(dh * n, axis=0, keepdims=True)
        dsh_ref[...] += jnp.sum(dh, axis=0, keepdims=True)

    row = pl.BlockSpec((t, D), lambda i: (i, 0))
    vec = pl.BlockSpec((1, D), lambda i: (0, 0))
    return pl.pallas_call(
        body, name=name,
        out_shape=[jax.ShapeDtypeStruct((S, D), F32), jax.ShapeDtypeStruct((1, D), F32),
                   jax.ShapeDtypeStruct((1, D), F32)],
        grid=(S // t,), in_specs=[row, row, row, vec], out_specs=[row, vec, vec],
        compiler_params=_cparams(),
    )(x, dh, dx_next, scale)


def _residual(x, gate, y, name):
    S, D = x.shape
    t = _tile(S, 512)

    def body(x_ref, g_ref, y_ref, o_ref):
        o_ref[...] = x_ref[...] + g_ref[...] * y_ref[...]

    row = pl.BlockSpec((t, D), lambda i: (i, 0))
    vec = pl.BlockSpec((1, D), lambda i: (0, 0))
    return pl.pallas_call(
        body, name=name, out_shape=jax.ShapeDtypeStruct((S, D), F32), grid=(S // t,),
        in_specs=[row, vec, row], out_specs=row, compiler_params=_cparams(),
    )(x, gate, y)


def _gate_bwd(dx, y, gate, name):
    S, D = dx.shape
    t = _tile(S, 512)

    def body(dx_ref, y_ref, g_ref, dy_ref, dg_ref):
        @pl.when(pl.program_id(0) == 0)
        def _():
            dg_ref[...] = jnp.zeros_like(dg_ref)

        dxv = dx_ref[...]
        dy_ref[...] = (dxv * g_ref[...]).astype(BF16)
        dg_ref[...] += jnp.sum(dxv * y_ref[...], axis=0, keepdims=True)

    row = pl.BlockSpec((t, D), lambda i: (i, 0))
    vec = pl.BlockSpec((1, D), lambda i: (0, 0))
    return pl.pallas_call(
        body, name=name,
        out_shape=[jax.ShapeDtypeStruct((S, D), BF16), jax.ShapeDtypeStruct((1, D), F32)],
        grid=(S // t,), in_specs=[row, row, vec], out_specs=[row, vec], compiler_params=_cparams(),
    )(dx, y, gate)


def _final_loss(x, g, target):
    S, D = x.shape
    t = _tile(S, 512)

    def body(x_ref, g_ref, t_ref, loss_ref, dx_ref, dg_ref):
        @pl.when(pl.program_id(0) == 0)
        def _():
            loss_ref[...] = jnp.zeros_like(loss_ref)
            dg_ref[...] = jnp.zeros_like(dg_ref)

        xv = x_ref[...]
        gv = g_ref[...]
        r = _rstd(xv)
        n = xv * r
        err = n * gv - t_ref[...]
        part = jnp.sum(jnp.mean(err * err, axis=-1, keepdims=True), axis=0, keepdims=True)
        loss_ref[...] += jnp.broadcast_to(0.5 * part, loss_ref.shape)
        dy = err * (1.0 / D)
        dg_ref[...] += jnp.sum(dy * n, axis=0, keepdims=True)
        dx_ref[...] = _norm_bwd(n, r, dy * gv)

    row = pl.BlockSpec((t, D), lambda i: (i, 0))
    vec = pl.BlockSpec((1, D), lambda i: (0, 0))
    one = pl.BlockSpec((1, LANE), lambda i: (0, 0))
    return pl.pallas_call(
        body, name="final_loss",
        out_shape=[jax.ShapeDtypeStruct((1, LANE), F32), jax.ShapeDtypeStruct((S, D), F32),
                   jax.ShapeDtypeStruct((1, D), F32)],
        grid=(S // t,), in_specs=[row, vec, row], out_specs=[one, row, vec], compiler_params=_cparams(),
    )(x, g, target)


def _scores(q, k, scale, cq, ck, i, j, t):
    s = lax.dot_general(q, k, NT, preferred_element_type=F32) * scale
    if cq is not None:
        s = s + (cq - ck)
    rows = i * t + lax.broadcasted_iota(jnp.int32, (t, t), 0)
    cols = j * t + lax.broadcasted_iota(jnp.int32, (t, t), 1)
    return jnp.where(rows >= cols, s, NEG_BIG)


def _attn_fwd(q_arr, q_off, k_arr, k_off, v_arr, v_off, cq, ck, scale, name):
    S = q_arr.shape[0]
    t = _tile(S, 512)
    n = S // t
    has_bias = cq is not None

    def body(*refs):
        if has_bias:
            q_ref, k_ref, v_ref, cq_ref, ck_ref, o_ref, lse_ref, m_s, l_s, acc_s = refs
        else:
            q_ref, k_ref, v_ref, o_ref, lse_ref, m_s, l_s, acc_s = refs
        i, j = pl.program_id(1), pl.program_id(2)

        @pl.when(j == 0)
        def _():
            m_s[...] = jnp.full_like(m_s, NEG_BIG)
            l_s[...] = jnp.zeros_like(l_s)
            acc_s[...] = jnp.zeros_like(acc_s)

        @pl.when(j <= i)
        def _():
            s = _scores(q_ref[...], k_ref[...], scale, cq_ref[0] if has_bias else None,
                        ck_ref[0] if has_bias else None, i, j, t)
            m_old = m_s[...]
            m_new = jnp.maximum(m_old, jnp.max(s, axis=1, keepdims=True))
            alpha = jnp.exp(m_old - m_new)
            p = jnp.exp(s - m_new)
            l_s[...] = alpha * l_s[...] + jnp.sum(p, axis=1, keepdims=True)
            acc_s[...] = alpha * acc_s[...] + lax.dot_general(p.astype(BF16), v_ref[...], NN,
                                                              preferred_element_type=F32)
            m_s[...] = m_new

        @pl.when(j == n - 1)
        def _():
            l = l_s[...]
            o_ref[...] = (acc_s[...] / l).astype(BF16)
            lse_ref[0] = m_s[...] + jnp.log(l)

    in_specs = [pl.BlockSpec((t, HEAD_PAD), lambda h, i, j: (i, q_off + h)),
                pl.BlockSpec((t, HEAD_PAD), lambda h, i, j: (jnp.minimum(i, j), k_off + h)),
                pl.BlockSpec((t, HEAD_PAD), lambda h, i, j: (jnp.minimum(i, j), v_off + h))]
    args = [q_arr, k_arr, v_arr]
    if has_bias:
        in_specs += [pl.BlockSpec((1, t, 1), lambda h, i, j: (h, i, 0)),
                     pl.BlockSpec((1, 1, t), lambda h, i, j: (h, 0, jnp.minimum(i, j)))]
        args += [cq, ck]
    return pl.pallas_call(
        body, name=name,
        out_shape=[jax.ShapeDtypeStruct((S, HEADS * HEAD_PAD), BF16), jax.ShapeDtypeStruct((HEADS, S, 1), F32)],
        grid=(HEADS, n, n), in_specs=in_specs,
        out_specs=[pl.BlockSpec((t, HEAD_PAD), lambda h, i, j: (i, h)),
                   pl.BlockSpec((1, t, 1), lambda h, i, j: (h, i, 0))],
        scratch_shapes=[pltpu.VMEM((t, 1), F32), pltpu.VMEM((t, 1), F32), pltpu.VMEM((t, HEAD_PAD), F32)],
        compiler_params=_cparams(),
    )(*args)


def _attn_delta(do, o, name):
    S = do.shape[0]
    t = _tile(S, 512)

    def body(do_ref, o_ref, d_ref):
        d_ref[0] = jnp.sum(do_ref[...].astype(F32) * o_ref[...].astype(F32), axis=1, keepdims=True)

    blk = pl.BlockSpec((t, HEAD_PAD), lambda h, i: (i, h))
    return pl.pallas_call(
        body, name=name, out_shape=jax.ShapeDtypeStruct((HEADS, S, 1), F32), grid=(HEADS, S // t),
        in_specs=[blk, blk], out_specs=pl.BlockSpec((1, t, 1), lambda h, i: (h, i, 0)),
        compiler_params=_cparams(),
    )(do, o)


def _attn_bwd(q_arr, q_off, k_arr, k_off, v_arr, v_off, do, lse, delta, cq, ck, scale, name):
    S = q_arr.shape[0]
    t = _tile(S, 512)
    n = S // t
    has_bias = cq is not None

    def body(*refs):
        if has_bias:
            (q_ref, k_ref, v_ref, do_ref, lse_ref, dl_ref, cq_ref, ck_ref,
             dq_ref, dk_ref, dv_ref, dcq_ref, dck_ref, dk_s, dv_s, dck_s) = refs
        else:
            (q_ref, k_ref, v_ref, do_ref, lse_ref, dl_ref,
             dq_ref, dk_ref, dv_ref, dk_s, dv_s) = refs
        j, i = pl.program_id(1), pl.program_id(2)

        @pl.when(jnp.logical_and(j == 0, i == 0))
        def _():
            dq_ref[...] = jnp.zeros_like(dq_ref)
            if has_bias:
                dcq_ref[...] = jnp.zeros_like(dcq_ref)

        @pl.when(i == 0)
        def _():
            dk_s[...] = jnp.zeros_like(dk_s)
            dv_s[...] = jnp.zeros_like(dv_s)
            if has_bias:
                dck_s[...] = jnp.zeros_like(dck_s)

        @pl.when(i >= j)
        def _():
            q, k, v, dov = q_ref[...], k_ref[...], v_ref[...], do_ref[...]
            s = _scores(q, k, scale, cq_ref[0] if has_bias else None,
                        ck_ref[0] if has_bias else None, i, j, t)
            p = jnp.exp(s - lse_ref[0])
            dv_s[...] += lax.dot_general(p.astype(BF16), dov, TN, preferred_element_type=F32)
            dp = lax.dot_general(dov, v, NT, preferred_element_type=F32)
            ds = p * (dp - dl_ref[0])
            rows = pl.ds(pl.multiple_of(i * t, t), t)
            if has_bias:
                dck_s[...] -= jnp.sum(ds, axis=0, keepdims=True)
                dcq_ref[0, rows, :] += jnp.sum(ds, axis=1, keepdims=True)
            dsb = (ds * scale).astype(BF16)
            dk_s[...] += lax.dot_general(dsb, q, TN, preferred_element_type=F32)
            dq_ref[rows, :] += lax.dot_general(dsb, k, NN, preferred_element_type=F32)

        @pl.when(i == n - 1)
        def _():
            dk_ref[...] = dk_s[...].astype(BF16)
            dv_ref[...] = dv_s[...].astype(BF16)
            if has_bias:
                dck_ref[0] = dck_s[...]

    qside = lambda h, j, i: (jnp.maximum(i, j), h)
    stat = pl.BlockSpec((1, t, 1), lambda h, j, i: (h, jnp.maximum(i, j), 0))
    in_specs = [pl.BlockSpec((t, HEAD_PAD), lambda h, j, i: (jnp.maximum(i, j), q_off + h)),
                pl.BlockSpec((t, HEAD_PAD), lambda h, j, i: (j, k_off + h)),
                pl.BlockSpec((t, HEAD_PAD), lambda h, j, i: (j, v_off + h)),
                pl.BlockSpec((t, HEAD_PAD), qside), stat, stat]
    args = [q_arr, k_arr, v_arr, do, lse, delta]
    wide = (S, HEADS * HEAD_PAD)
    out_shape = [jax.ShapeDtypeStruct(wide, F32), jax.ShapeDtypeStruct(wide, BF16),
                 jax.ShapeDtypeStruct(wide, BF16)]
    kblk = pl.BlockSpec((t, HEAD_PAD), lambda h, j, i: (j, h))
    out_specs = [pl.BlockSpec((S, HEAD_PAD), lambda h, j, i: (0, h)), kblk, kblk]
    scratch = [pltpu.VMEM((t, HEAD_PAD), F32), pltpu.VMEM((t, HEAD_PAD), F32)]
    if has_bias:
        in_specs += [stat, pl.BlockSpec((1, 1, t), lambda h, j, i: (h, 0, j))]
        args += [cq, ck]
        out_shape += [jax.ShapeDtypeStruct((HEADS, S, 1), F32), jax.ShapeDtypeStruct((HEADS, 1, S), F32)]
        out_specs += [pl.BlockSpec((1, S, 1), lambda h, j, i: (h, 0, 0)),
                      pl.BlockSpec((1, 1, t), lambda h, j, i: (h, 0, j))]
        scratch.append(pltpu.VMEM((1, t), F32))
    outs = pl.pallas_call(
        body, name=name, out_shape=out_shape, grid=(HEADS, n, n), in_specs=in_specs, out_specs=out_specs,
        scratch_shapes=scratch, compiler_params=_cparams(),
    )(*args)
    return outs if has_bias else (*outs, None, None)


def _split3(x):
    hi = x.astype(BF16)
    r1 = x - hi.astype(F32)
    mid = r1.astype(BF16)
    lo = (r1 - mid.astype(F32)).astype(BF16)
    return hi, mid, lo


def _tri_matmul(tri, x):
    acc = None
    for part in _split3(x):
        d = lax.dot_general(tri, part, NN, preferred_element_type=F32)
        acc = d if acc is None else acc + d
    return acc


def _log_sigmoid(z):
    return jnp.minimum(z, 0.0) - jnp.log1p(jnp.exp(-jnp.abs(z)))


def _fox_gate_fwd(fl, bf):
    S = fl.shape[0]
    t = _tile(S, 256)

    def body(fl_ref, bf_ref, cum_ref, carry):
        @pl.when(pl.program_id(0) == 0)
        def _():
            carry[...] = jnp.zeros_like(carry)

        logf = _log_sigmoid(fl_ref[...] + bf_ref[...])
        tri = (lax.broadcasted_iota(jnp.int32, (t, t), 0) >= lax.broadcasted_iota(jnp.int32, (t, t), 1)).astype(BF16)
        cum_ref[...] = _tri_matmul(tri, logf) + carry[...]
        carry[...] += jnp.sum(logf, axis=0, keepdims=True)

    row = pl.BlockSpec((t, LANE), lambda i: (i, 0))
    return pl.pallas_call(
        body, name="fox_gate_fwd", out_shape=jax.ShapeDtypeStruct((S, LANE), F32), grid=(S // t,),
        in_specs=[row, pl.BlockSpec((1, LANE), lambda i: (0, 0))], out_specs=row,
        scratch_shapes=[pltpu.VMEM((1, LANE), F32)], compiler_params=_cparams(),
    )(fl, bf)


def _fox_gate_bwd(dcum, fl, bf):
    S = fl.shape[0]
    t = _tile(S, 256)
    n = S // t

    def body(dc_ref, fl_ref, bf_ref, df_ref, db_ref, carry):
        @pl.when(pl.program_id(0) == 0)
        def _():
            carry[...] = jnp.zeros_like(carry)
            db_ref[...] = jnp.zeros_like(db_ref)

        dc = dc_ref[...]
        tri = (lax.broadcasted_iota(jnp.int32, (t, t), 1) >= lax.broadcasted_iota(jnp.int32, (t, t), 0)).astype(BF16)
        dlogf = _tri_matmul(tri, dc) + carry[...]
        carry[...] += jnp.sum(dc, axis=0, keepdims=True)
        z = fl_ref[...] + bf_ref[...]
        e = jnp.exp(-jnp.abs(z))
        sig_neg = jnp.where(z >= 0, e, 1.0) / (1.0 + e)
        lanes = lax.broadcasted_iota(jnp.int32, (t, LANE), 1)
        df = jnp.where(lanes < HEADS, dlogf * sig_neg, 0.0)
        df_ref[...] = df
        db_ref[...] += jnp.sum(df, axis=0, keepdims=True)

    row = pl.BlockSpec((t, LANE), lambda i: (n - 1 - i, 0))
    vec = pl.BlockSpec((1, LANE), lambda i: (0, 0))
    return pl.pallas_call(
        body, name="fox_gate_bwd",
        out_shape=[jax.ShapeDtypeStruct((S, LANE), F32), jax.ShapeDtypeStruct((1, LANE), F32)],
        grid=(n,), in_specs=[row, row, vec], out_specs=[row, vec],
        scratch_shapes=[pltpu.VMEM((1, LANE), F32)], compiler_params=_cparams(),
    )(dcum, fl, bf)


def _rope_tables(S):
    pos = jnp.arange(S, dtype=F32)
    inv_freq = ROPE_BASE ** (-jnp.arange(0, MLA_ROPE, 2, dtype=F32) / MLA_ROPE)
    ang = pos[:, None] * inv_freq[None, :]
    cos, sin = jnp.cos(ang), jnp.sin(ang)
    half = MLA_ROPE // 2
    ones = jnp.ones((S, MLA_NOPE), F32)
    z = lambda w: jnp.zeros((S, w), F32)
    keep = jnp.concatenate([ones, cos, cos, z(HEAD_PAD - MLA_NOPE - MLA_ROPE)], axis=1)
    from_above = jnp.concatenate([z(MLA_NOPE), -sin, z(HEAD_PAD - MLA_NOPE - half)], axis=1)
    from_below = jnp.concatenate([z(MLA_NOPE + half), sin, z(HEAD_PAD - MLA_NOPE - MLA_ROPE)], axis=1)
    return keep, from_above, from_below


def _rope(x, keep, up, down, transpose):
    half = MLA_ROPE // 2
    if transpose:
        return x * keep + pltpu.roll(x * up, half, 1) + pltpu.roll(x * down, HEAD_PAD - half, 1)
    return x * keep + pltpu.roll(x, HEAD_PAD - half, 1) * up + pltpu.roll(x, half, 1) * down


def _rope_heads(x, tables, transpose, name):
    S = x.shape[0]
    t = _tile(S, 512)

    def body(x_ref, a_ref, b_ref, c_ref, o_ref):
        o_ref[...] = _rope(x_ref[...], a_ref[...], b_ref[...], c_ref[...], transpose).astype(BF16)

    blk = pl.BlockSpec((t, HEAD_PAD), lambda i, h: (i, h))
    tab = pl.BlockSpec((t, HEAD_PAD), lambda i, h: (i, 0))
    return pl.pallas_call(
        body, name=name, out_shape=jax.ShapeDtypeStruct(x.shape, BF16), grid=(S // t, HEADS),
        in_specs=[blk, tab, tab, tab], out_specs=blk, compiler_params=_cparams(),
    )(x, *tables)


MLA_A_PAD = MLA_QR + MLA_KVR + HEAD_PAD
MLA_KV_IN = MLA_KVR + HEAD_PAD


def _mla_mid_fwd(a, gq, gkv, tables):
    S = a.shape[0]
    t = _tile(S, 512)

    def body(a_ref, gq_ref, gkv_ref, ta_ref, tb_ref, tc_ref, q_ref, kv_ref):
        cq = a_ref[:, :MLA_QR]
        ckv = a_ref[:, MLA_QR:MLA_QR + MLA_KVR]
        kr = a_ref[:, MLA_QR + MLA_KVR:]
        q_ref[...] = (cq * _rstd(cq) * gq_ref[...]).astype(BF16)
        kv_ref[:, :MLA_KVR] = (ckv * _rstd(ckv) * gkv_ref[...]).astype(BF16)
        kv_ref[:, MLA_KVR:] = _rope(kr, ta_ref[...], tb_ref[...], tc_ref[...], False).astype(BF16)

    tab = pl.BlockSpec((t, HEAD_PAD), lambda i: (i, 0))
    return pl.pallas_call(
        body, name="mla_mid_fwd",
        out_shape=[jax.ShapeDtypeStruct((S, MLA_QR), BF16), jax.ShapeDtypeStruct((S, MLA_KV_IN), BF16)],
        grid=(S // t,),
        in_specs=[pl.BlockSpec((t, MLA_A_PAD), lambda i: (i, 0)), pl.BlockSpec((1, MLA_QR), lambda i: (0, 0)),
                  pl.BlockSpec((1, MLA_KVR), lambda i: (0, 0)), tab, tab, tab],
        out_specs=[pl.BlockSpec((t, MLA_QR), lambda i: (i, 0)), pl.BlockSpec((t, MLA_KV_IN), lambda i: (i, 0))],
        compiler_params=_cparams(),
    )(a, gq, gkv, *tables)


def _mla_mid_bwd(a, dq, dkv, gq, gkv, tables):
    S = a.shape[0]
    t = _tile(S, 512)

    def body(a_ref, dq_ref, dkv_ref, gq_ref, gkv_ref, ta_ref, tb_ref, tc_ref, da_ref, dgq_ref, dgkv_ref):
        @pl.when(pl.program_id(0) == 0)
        def _():
            dgq_ref[...] = jnp.zeros_like(dgq_ref)
            dgkv_ref[...] = jnp.zeros_like(dgkv_ref)

        def one(c, dout, g):
            r = _rstd(c)
            n = c * r
            return _norm_bwd(n, r, dout * g), jnp.sum(dout * n, axis=0, keepdims=True)

        dcq, dgq = one(a_ref[:, :MLA_QR], dq_ref[...], gq_ref[...])
        dckv, dgkv = one(a_ref[:, MLA_QR:MLA_QR + MLA_KVR], dkv_ref[:, :MLA_KVR], gkv_ref[...])
        da_ref[:, :MLA_QR] = dcq.astype(BF16)
        da_ref[:, MLA_QR:MLA_QR + MLA_KVR] = dckv.astype(BF16)
        da_ref[:, MLA_QR + MLA_KVR:] = _rope(dkv_ref[:, MLA_KVR:], ta_ref[...], tb_ref[...], tc_ref[...],
                                             True).astype(BF16)
        dgq_ref[...] += dgq
        dgkv_ref[...] += dgkv

    tab = pl.BlockSpec((t, HEAD_PAD), lambda i: (i, 0))
    vq = pl.BlockSpec((1, MLA_QR), lambda i: (0, 0))
    vkv = pl.BlockSpec((1, MLA_KVR), lambda i: (0, 0))
    return pl.pallas_call(
        body, name="mla_mid_bwd",
        out_shape=[jax.ShapeDtypeStruct((S, MLA_A_PAD), BF16), jax.ShapeDtypeStruct((1, MLA_QR), F32),
                   jax.ShapeDtypeStruct((1, MLA_KVR), F32)],
        grid=(S // t,),
        in_specs=[pl.BlockSpec((t, MLA_A_PAD), lambda i: (i, 0)), pl.BlockSpec((t, MLA_QR), lambda i: (i, 0)),
                  pl.BlockSpec((t, MLA_KV_IN), lambda i: (i, 0)), vq, vkv, tab, tab, tab],
        out_specs=[pl.BlockSpec((t, MLA_A_PAD), lambda i: (i, 0)), vq, vkv],
        compiler_params=_cparams(),
    )(a, dq, dkv, gq, gkv, *tables)


FFN_ROWS = 256
FFN_COLS = 1408
HALO = 8


def _shift_down(cur, halo, s, first):
    rolled = pltpu.roll(cur, s, 0)
    hrolled = jnp.where(first, 0.0, pltpu.roll(halo, s, 0))
    row = lax.broadcasted_iota(jnp.int32, (HALO, cur.shape[1]), 0)
    top = jnp.where(row < s, hrolled, rolled[:HALO])
    return jnp.concatenate([top, rolled[HALO:]], axis=0)


def _shift_up(cur, halo, s, last):
    rows = cur.shape[0]
    rolled = pltpu.roll(cur, rows - s, 0)
    hrolled = jnp.where(last, 0.0, pltpu.roll(halo, HALO - s, 0))
    row = lax.broadcasted_iota(jnp.int32, (HALO, cur.shape[1]), 0)
    bottom = jnp.where(row >= HALO - s, hrolled, rolled[rows - HALO:])
    return jnp.concatenate([rolled[:rows - HALO], bottom], axis=0)


def _conv(u, halo, w_ref, b, first):
    u1 = _shift_down(u, halo, 1, first)
    u2 = _shift_down(u, halo, 2, first)
    y = b + w_ref[0:1, :] * u2
    y = y + w_ref[1:2, :] * u1
    y = y + w_ref[2:3, :] * u
    return y, u1, u2


def _ffn_specs(S, tr, tn):
    per = tr // HALO
    tile = pl.BlockSpec((tr, tn), lambda j, i: (i, j))
    before = pl.BlockSpec((HALO, tn), lambda j, i: (jnp.maximum(i * per - 1, 0), j))
    after = pl.BlockSpec((HALO, tn), lambda j, i: (jnp.minimum((i + 1) * per, S // HALO - 1), j))
    w3 = pl.BlockSpec((3, tn), lambda j, i: (0, j))
    w1 = pl.BlockSpec((1, tn), lambda j, i: (0, j))
    return tile, before, after, w3, w1


def _ffn_act_fwd(ug, uv, wg, wv, bg, bv, name):
    S, F = ug.shape
    tr, tn = _tile(S, FFN_ROWS), _tile(F, FFN_COLS)
    tile, before, _, w3, w1 = _ffn_specs(S, tr, tn)

    def body(ug_ref, uv_ref, hg_ref, hv_ref, wg_ref, wv_ref, bg_ref, bv_ref, o_ref):
        first = pl.program_id(1) == 0
        g, _, _ = _conv(ug_ref[...], hg_ref[...], wg_ref, bg_ref[...], first)
        v, _, _ = _conv(uv_ref[...], hv_ref[...], wv_ref, bv_ref[...], first)
        o_ref[...] = (g * jax.nn.sigmoid(g) * v).astype(BF16)

    return pl.pallas_call(
        body, name=name, out_shape=jax.ShapeDtypeStruct((S, F), BF16), grid=(F // tn, S // tr),
        in_specs=[tile, tile, before, before, w3, w3, w1, w1], out_specs=tile, compiler_params=_cparams(),
    )(ug, uv, ug, uv, wg, wv, bg, bv)


def _ffn_bwd_gate(ug, uv, dact, wg, wv, bg, bv, name):
    S, F = ug.shape
    tr, tn = _tile(S, FFN_ROWS), _tile(F, FFN_COLS)
    tile, before, _, w3, w1 = _ffn_specs(S, tr, tn)

    def body(ug_ref, uv_ref, hg_ref, hv_ref, da_ref, wg_ref, wv_ref, bg_ref, bv_ref,
             dyg_ref, dyv_ref, dwg_ref, dwv_ref, dbg_ref, dbv_ref):
        first = pl.program_id(1) == 0

        @pl.when(first)
        def _():
            for r in (dwg_ref, dwv_ref, dbg_ref, dbv_ref):
                r[...] = jnp.zeros_like(r)

        ugv, uvv = ug_ref[...], uv_ref[...]
        g, ug1, ug2 = _conv(ugv, hg_ref[...], wg_ref, bg_ref[...], first)
        v, uv1, uv2 = _conv(uvv, hv_ref[...], wv_ref, bv_ref[...], first)
        da = da_ref[...]
        sg = jax.nn.sigmoid(g)
        dyg = da * v * (sg * (1.0 + g * (1.0 - sg)))
        dyv = da * (g * sg)
        dyg_ref[...] = dyg
        dyv_ref[...] = dyv

        def add_taps(dw_ref, dy, taps):
            for k, u in enumerate(taps):
                dw_ref[k:k + 1, :] += jnp.sum(dy * u, axis=0, keepdims=True)

        add_taps(dwg_ref, dyg, (ug2, ug1, ugv))
        add_taps(dwv_ref, dyv, (uv2, uv1, uvv))
        dbg_ref[...] += jnp.sum(dyg, axis=0, keepdims=True)
        dbv_ref[...] += jnp.sum(dyv, axis=0, keepdims=True)

    big = jax.ShapeDtypeStruct((S, F), F32)
    s3, s1 = jax.ShapeDtypeStruct((3, F), F32), jax.ShapeDtypeStruct((1, F), F32)
    return pl.pallas_call(
        body, name=name, out_shape=[big, big, s3, s3, s1, s1], grid=(F // tn, S // tr),
        in_specs=[tile, tile, before, before, tile, w3, w3, w1, w1],
        out_specs=[tile, tile, w3, w3, w1, w1], compiler_params=_cparams(),
    )(ug, uv, ug, uv, dact, wg, wv, bg, bv)


def _ffn_bwd_conv(dy, w, name):
    S, F = dy.shape
    tr, tn = _tile(S, FFN_ROWS), _tile(F, FFN_COLS)
    tile, _, after, w3, _ = _ffn_specs(S, tr, tn)
    n = S // tr

    def body(dy_ref, h_ref, w_ref, o_ref):
        last = pl.program_id(1) == n - 1
        dyv = dy_ref[...]
        d1 = _shift_up(dyv, h_ref[...], 1, last)
        d2 = _shift_up(dyv, h_ref[...], 2, last)
        o_ref[...] = (w_ref[2:3, :] * dyv + w_ref[1:2, :] * d1 + w_ref[0:1, :] * d2).astype(BF16)

    return pl.pallas_call(
        body, name=name, out_shape=jax.ShapeDtypeStruct((S, F), BF16), grid=(F // tn, n),
        in_specs=[tile, after, w3], out_specs=tile, compiler_params=_cparams(),
    )(dy, dy, w)


def _dot3(a, b):
    a_hi = a.astype(BF16)
    a_lo = (a - a_hi.astype(F32)).astype(BF16)
    b_hi = b.astype(BF16)
    b_lo = (b - b_hi.astype(F32)).astype(BF16)
    d = lambda p, q: lax.dot_general(p, q, NN, preferred_element_type=F32)
    return d(a_hi, b_hi) + (d(a_hi, b_lo) + d(a_lo, b_hi))


def _ada_mod(c_all, w, b):
    nmod, D, n = w.shape

    def body(c_ref, w_ref, b_ref, o_ref):
        cv = c_ref[...]
        o_ref[0] = _dot3(cv * jax.nn.sigmoid(cv), w_ref[0]) + b_ref[0]

    return pl.pallas_call(
        body, name="ada_mod", out_shape=jax.ShapeDtypeStruct((nmod, 8, n), F32), grid=(nmod,),
        in_specs=[pl.BlockSpec((8, D), lambda m: (0, 0)), pl.BlockSpec((1, D, n), lambda m: (m, 0, 0)),
                  pl.BlockSpec((1, 1, n), lambda m: (m, 0, 0))],
        out_specs=pl.BlockSpec((1, 8, n), lambda m: (m, 0, 0)), compiler_params=_cparams(),
    )(c_all, w, b)


def _ada_grad(c_all_t, dmod):
    D = c_all_t.shape[0]
    nmod, _, n = dmod.shape
    tr = 256

    def body(c_ref, d_ref, dw_ref, db_ref):
        cv = c_ref[...]
        sc = cv * jax.nn.sigmoid(cv)
        dv = d_ref[0]
        acc = sc[:, 0:1] * dv[0:1, :]
        tot = dv[0:1, :]
        for k in range(1, 8):
            acc = acc + sc[:, k:k + 1] * dv[k:k + 1, :]
            tot = tot + dv[k:k + 1, :]
        dw_ref[0] = acc
        db_ref[0] = tot

    return pl.pallas_call(
        body, name="ada_grad",
        out_shape=[jax.ShapeDtypeStruct((nmod, D, n), F32), jax.ShapeDtypeStruct((nmod, 1, n), F32)],
        grid=(nmod, D // tr),
        in_specs=[pl.BlockSpec((tr, 8), lambda m, i: (i, 0)), pl.BlockSpec((1, 8, n), lambda m, i: (m, 0, 0))],
        out_specs=[pl.BlockSpec((1, tr, n), lambda m, i: (m, i, 0)), pl.BlockSpec((1, 1, n), lambda m, i: (m, 0, 0))],
        compiler_params=_cparams(),
    )(c_all_t, dmod)


def _adamw(w, g, m, v, name):
    R, C = w.shape
    t = _row_tile(R, C)

    def body(w_ref, g_ref, m_ref, v_ref, d_ref, nm_ref, nv_ref):
        gv = g_ref[...]
        mn = ADAM_B1 * m_ref[...] + (1.0 - ADAM_B1) * gv
        vn = ADAM_B2 * v_ref[...] + (1.0 - ADAM_B2) * (gv * gv)
        m_hat = mn / (1.0 - ADAM_B1 ** ADAM_STEP)
        v_hat = vn / (1.0 - ADAM_B2 ** ADAM_STEP)
        d_ref[...] = -ADAM_LR * (m_hat / (jnp.sqrt(v_hat) + ADAM_EPS) + ADAM_WD * w_ref[...])
        nm_ref[...] = mn
        nv_ref[...] = vn

    blk = pl.BlockSpec((t, C), lambda i: (i, 0))
    shp = jax.ShapeDtypeStruct((R, C), F32)
    return pl.pallas_call(
        body, name=name, out_shape=[shp, shp, shp], grid=(R // t,), in_specs=[blk] * 4, out_specs=[blk] * 3,
        compiler_params=_cparams(),
    )(w, g, m, v)


def _cat_cols(g4):
    return jnp.concatenate([g4[s] for s in range(4)], axis=-1)


def _cat_rows(g4):
    return jnp.concatenate([g4[s] for s in range(4)], axis=-2)


def _pad_head_cols(w, width):
    K = w.shape[0]
    w = w.reshape(K, HEADS, width)
    return jnp.pad(w, ((0, 0), (0, 0), (0, HEAD_PAD - width))).reshape(K, HEADS * HEAD_PAD)


def _unpad_head_cols(w, width):
    K = w.shape[0]
    return w.reshape(K, HEADS, HEAD_PAD)[:, :, :width].reshape(K, HEADS * width)


def _pad_head_rows(w, width):
    N = w.shape[1]
    w = w.reshape(HEADS, width, N)
    return jnp.pad(w, ((0, 0), (0, HEAD_PAD - width), (0, 0))).reshape(HEADS * HEAD_PAD, N)


def _unpad_head_rows(w, width):
    N = w.shape[1]
    return w.reshape(HEADS, HEAD_PAD, N)[:, :width, :].reshape(HEADS * width, N)


def _prepare_weights(gathered):
    fox_in, fox_o, mla_a, mla_uq, mla_ukv, mla_o, ffn_in, ffn_out = gathered
    W = {}
    w = _cat_cols(fox_in[:, 0])
    hw = HEADS * FOX_HEAD
    qkv = [_pad_head_cols(w[:, p * hw:(p + 1) * hw], FOX_HEAD) for p in range(3)]
    f = jnp.pad(w[:, 3 * hw:], ((0, 0), (0, LANE - HEADS)))
    W["fox_qkv"] = jnp.concatenate(qkv, axis=1)
    W["fox_f"] = f
    W["fox_all"] = jnp.concatenate(qkv + [f], axis=1)
    W["fox_o"] = _pad_head_rows(_cat_rows(fox_o[:, 0]), FOX_HEAD)
    w = _cat_rows(mla_a[:, 0])
    lat = MLA_QR + MLA_KVR
    W["mla_a"] = jnp.concatenate(
        [w[:, :lat], jnp.zeros((D_MODEL, MLA_NOPE), BF16), w[:, lat:],
         jnp.zeros((D_MODEL, HEAD_PAD - MLA_NOPE - MLA_ROPE), BF16)], axis=1)
    W["mla_uq"] = _pad_head_cols(_cat_cols(mla_uq[:, 0]), MLA_NOPE + MLA_ROPE)
    w = _cat_cols(mla_ukv[:, 0]).reshape(MLA_KVR, HEADS, MLA_NOPE + MLA_V)
    kn = _pad_head_cols(w[:, :, :MLA_NOPE].reshape(MLA_KVR, -1), MLA_NOPE)
    vv = _pad_head_cols(w[:, :, MLA_NOPE:].reshape(MLA_KVR, -1), MLA_V)
    eye = np.zeros((HEAD_PAD, HEADS, HEAD_PAD), np.float32)
    for j in range(MLA_NOPE, MLA_NOPE + MLA_ROPE):
        eye[j, :, j] = 1.0
    eye = jnp.asarray(eye.reshape(HEAD_PAD, HEADS * HEAD_PAD), BF16)
    bottom = jnp.concatenate([eye, jnp.zeros((HEAD_PAD, HEADS * HEAD_PAD), BF16)], axis=1)
    W["mla_kv"] = jnp.concatenate([jnp.concatenate([kn, vv], axis=1), bottom], axis=0)
    W["mla_o"] = _pad_head_rows(_cat_rows(mla_o[:, 0]), MLA_V)
    w = _cat_cols(ffn_in)
    W["ffn_g"] = [w[i, :, :D_FF] for i in range(2)]
    W["ffn_v"] = [w[i, :, D_FF:] for i in range(2)]
    w = _cat_rows(ffn_out)
    W["ffn_out"] = [w[i] for i in range(2)]
    return W


def _ffn_forward(xin, mod, W, i, conv_w, conv_b):
    shift, scale, gate = mod
    h = _adaln_fwd(xin, scale, shift, name=f"ffn{i}_adaln")
    ug = _mm(h, W["ffn_g"][i], name=f"ffn{i}_up_gate")
    uv = _mm(h, W["ffn_v"][i], name=f"ffn{i}_up_val")
    wg, wv = conv_w[i][:, :D_FF], conv_w[i][:, D_FF:]
    bg, bv = conv_b[i][None, :D_FF], conv_b[i][None, D_FF:]
    act = _ffn_act_fwd(ug, uv, wg, wv, bg, bv, name=f"ffn{i}_act")
    y = _mm(act, W["ffn_out"][i], name=f"ffn{i}_down")
    xout = _residual(xin, gate, y, name=f"ffn{i}_residual")
    return xout, (xin, h, ug, uv, act, y, wg, wv, bg, bv)


def _ffn_backward(dx_out, saved, mod, W, i):
    xin, h, ug, uv, act, y, wg, wv, bg, bv = saved
    shift, scale, gate = mod
    dy, dgate = _gate_bwd(dx_out, y, gate, name=f"ffn{i}_gate_bwd")
    d_out = _mm(act, dy, ta=True, name=f"ffn{i}_dw_out")
    dact = _mm(dy, W["ffn_out"][i], tb=True, name=f"ffn{i}_dact")
    dyg, dyv, dwg, dwv, dbg, dbv = _ffn_bwd_gate(ug, uv, dact, wg, wv, bg, bv, name=f"ffn{i}_bwd_gate")
    dug = _ffn_bwd_conv(dyg, wg, name=f"ffn{i}_bwd_conv_g")
    duv = _ffn_bwd_conv(dyv, wv, name=f"ffn{i}_bwd_conv_v")
    d_in = jnp.concatenate([_mm(h, dug, ta=True, name=f"ffn{i}_dw_gate"),
                            _mm(h, duv, ta=True, name=f"ffn{i}_dw_val")], axis=1)
    dh = _mm(dug, W["ffn_g"][i], tb=True, name=f"ffn{i}_dh_gate")
    dh = _mm(duv, W["ffn_v"][i], tb=True, res=dh, name=f"ffn{i}_dh_val")
    dx, dscale, dshift = _adaln_bwd(xin, dh, dx_out, scale, name=f"ffn{i}_adaln_bwd")
    grads = dict(ffn_w_in=d_in, ffn_w_out=d_out, ffn_conv_w=jnp.concatenate([dwg, dwv], axis=1),
                 ffn_conv_b=jnp.concatenate([dbg, dbv], axis=1)[0])
    return dx, jnp.concatenate([dshift, dscale, dgate], axis=1)[0], grads


def _local_step(x, target, mods, W, small):
    S = x.shape[0]
    tables = _rope_tables(S)
    wide = HEADS * HEAD_PAD
    bf = jnp.pad(small["fox_b_f"], ((0, 0), (0, LANE - HEADS)))
    gq, gkv = small["mla_g_q"], small["mla_g_kv"]
    fox_scale = FOX_HEAD ** -0.5
    mla_scale = (MLA_NOPE + MLA_ROPE) ** -0.5

    shift, scale, gate = mods[0]
    h0 = _adaln_fwd(x, scale, shift, name="fox_adaln")
    qkv = _mm(h0, W["fox_qkv"], out_dtype=BF16, name="fox_qkv")
    fl = _mm(h0, W["fox_f"], name="fox_gate_logits")
    cum = _fox_gate_fwd(fl, bf)
    cum_t = cum[:, :HEADS].T
    cq, ck = cum_t[:, :, None], cum_t[:, None, :]
    o0, lse0 = _attn_fwd(qkv, 0, qkv, HEADS, qkv, 2 * HEADS, cq, ck, fox_scale, name="fox_attn_fwd")
    y0 = _mm(o0, W["fox_o"], name="fox_out")
    x1 = _residual(x, gate, y0, name="fox_residual")
    x2, ffn0 = _ffn_forward(x1, mods[1], W, 0, small["ffn_conv_w"], small["ffn_conv_b"])

    shift, scale, gate = mods[2]
    h2 = _adaln_fwd(x2, scale, shift, name="mla_adaln")
    a = _mm(h2, W["mla_a"], name="mla_down")
    cqn, ckv_in = _mla_mid_fwd(a, gq, gkv, tables)
    q_raw = _mm(cqn, W["mla_uq"], name="mla_up_q")
    q_cat = _rope_heads(q_raw, tables, False, name="mla_rope_q")
    kv = _mm(ckv_in, W["mla_kv"], out_dtype=BF16, name="mla_up_kv")
    o1, lse1 = _attn_fwd(q_cat, 0, kv, 0, kv, HEADS, None, None, mla_scale, name="mla_attn_fwd")
    y2 = _mm(o1, W["mla_o"], name="mla_out")
    x3 = _residual(x2, gate, y2, name="mla_residual")
    x4, ffn1 = _ffn_forward(x3, mods[3], W, 1, small["ffn_conv_w"], small["ffn_conv_b"])

    loss, dx4, d_final_g = _final_loss(x4, small["final_g"], target)

    dx3, dmod3, g_ffn1 = _ffn_backward(dx4, ffn1, mods[3], W, 1)

    shift, scale, gate = mods[2]
    dy, dgate = _gate_bwd(dx3, y2, gate, name="mla_gate_bwd")
    d_mla_o = _mm(o1, dy, ta=True, name="mla_dw_out")
    do = _mm(dy, W["mla_o"], tb=True, out_dtype=BF16, name="mla_do")
    delta = _attn_delta(do, o1, name="mla_attn_delta")
    dq, dk, dv, _, _ = _attn_bwd(q_cat, 0, kv, 0, kv, HEADS, do, lse1, delta, None, None, mla_scale,
                                 name="mla_attn_bwd")
    dq_raw = _rope_heads(dq, tables, True, name="mla_rope_q_bwd")
    d_mla_uq = _mm(cqn, dq_raw, ta=True, name="mla_dw_uq")
    dcqn = _mm(dq_raw, W["mla_uq"], tb=True, name="mla_dcq")
    dkv = jnp.concatenate([dk, dv], axis=1)
    d_mla_kv = _mm(ckv_in, dkv, ta=True, name="mla_dw_kv")
    dckv_in = _mm(dkv, W["mla_kv"], tb=True, name="mla_dckv")
    da, dgq, dgkv = _mla_mid_bwd(a, dcqn, dckv_in, gq, gkv, tables)
    d_mla_a = _mm(h2, da, ta=True, name="mla_dw_a")
    dh = _mm(da, W["mla_a"], tb=True, name="mla_dh")
    dx2, dscale, dshift = _adaln_bwd(x2, dh, dx3, scale, name="mla_adaln_bwd")
    dmod2 = jnp.concatenate([dshift, dscale, dgate], axis=1)[0]

    dx1, dmod1, g_ffn0 = _ffn_backward(dx2, ffn0, mods[1], W, 0)

    shift, scale, gate = mods[0]
    dy, dgate = _gate_bwd(dx1, y0, gate, name="fox_gate_bwd")
    d_fox_o = _mm(o0, dy, ta=True, name="fox_dw_out")
    do = _mm(dy, W["fox_o"], tb=True, out_dtype=BF16, name="fox_do")
    delta = _attn_delta(do, o0, name="fox_attn_delta")
    dq, dk, dv, dcq, dck = _attn_bwd(qkv, 0, qkv, HEADS, qkv, 2 * HEADS, do, lse0, delta, cq, ck, fox_scale,
                                     name="fox_attn_bwd")
    dcum = jnp.pad((dcq[:, :, 0] + dck[:, 0, :]).T, ((0, 0), (0, LANE - HEADS)))
    dfl, dbf = _fox_gate_bwd(dcum, fl, bf)
    dproj = jnp.concatenate([dq.astype(BF16), dk, dv, dfl.astype(BF16)], axis=1)
    d_fox_all = _mm(h0, dproj, ta=True, name="fox_dw_in")
    dh = _mm(dproj, W["fox_all"], tb=True, name="fox_dh")
    dx0, dscale, dshift = _adaln_bwd(x, dh, dx1, scale, name="fox_adaln_bwd")
    dmod0 = jnp.concatenate([dshift, dscale, dgate], axis=1)[0]

    G = {}
    G["fox_w_in"] = jnp.concatenate(
        [_unpad_head_cols(d_fox_all[:, p * wide:(p + 1) * wide], FOX_HEAD) for p in range(3)]
        + [d_fox_all[:, 3 * wide:3 * wide + HEADS]], axis=1)[None]
    G["fox_b_f"] = dbf[:, :HEADS]
    G["fox_w_o"] = _unpad_head_rows(d_fox_o, FOX_HEAD)[None]
    lat = MLA_QR + MLA_KVR
    G["mla_w_a"] = jnp.concatenate([d_mla_a[:, :lat], d_mla_a[:, lat + MLA_NOPE:lat + MLA_NOPE + MLA_ROPE]],
                                   axis=1)[None]
    G["mla_g_q"] = dgq
    G["mla_g_kv"] = dgkv
    G["mla_w_uq"] = _unpad_head_cols(d_mla_uq, MLA_NOPE + MLA_ROPE)[None]
    dkn = d_mla_kv[:MLA_KVR, :wide].reshape(MLA_KVR, HEADS, HEAD_PAD)[:, :, :MLA_NOPE]
    dvv = d_mla_kv[:MLA_KVR, wide:].reshape(MLA_KVR, HEADS, HEAD_PAD)[:, :, :MLA_V]
    G["mla_w_ukv"] = jnp.concatenate([dkn, dvv], axis=2).reshape(MLA_KVR, -1)[None]
    G["mla_w_o"] = _unpad_head_rows(d_mla_o, MLA_V)[None]
    for name in ("ffn_w_in", "ffn_w_out", "ffn_conv_w", "ffn_conv_b"):
        G[name] = jnp.stack([g_ffn0[name], g_ffn1[name]])
    G["final_g"] = d_final_g[0]
    dmods = jnp.stack([dmod0, dmod1, dmod2, dmod3])
    return loss, dx0, dmods, G


PACKED = [("fox_w_in", 2), ("fox_w_o", 1), ("mla_w_a", 1), ("mla_w_uq", 2), ("mla_w_ukv", 2), ("mla_w_o", 1),
          ("ffn_w_in", 2), ("ffn_w_out", 1), ("fox_b_f", None), ("mla_g_q", 1), ("mla_g_kv", 1),
          ("ffn_conv_w", 2), ("ffn_conv_b", None), ("final_g", None), ("loss", None)]


def _shard_of(g, axis, s):
    if axis is None:
        return g
    n = g.shape[axis] // 4
    return lax.slice_in_dim(g, s * n, (s + 1) * n, axis=axis)


def _pack(G):
    rows = []
    for s in range(4):
        rows.append(jnp.concatenate([_shard_of(G[name], axis, s).reshape(-1) for name, axis in PACKED]))
    L = rows[0].shape[0]
    unit = 2 * PACK_ROWS_ALIGN * PACK_ROW
    Lp = -(-L // unit) * unit
    packed = jnp.pad(jnp.stack(rows), ((0, 0), (0, Lp - L)))
    return packed.reshape(4, 2, Lp // (2 * PACK_ROW), PACK_ROW)


def _unpack(flat, G_like):
    out, off = {}, 0
    for name, axis in PACKED:
        shape = _shard_of(G_like[name], axis, 0).shape
        size = math.prod(shape)
        out[name] = flat[off:off + size].reshape(shape)
        off += size
    return out


WEIGHTS = ['ada_w', 'ada_b', 'fox_w_in', 'fox_b_f', 'fox_w_o', 'mla_w_a', 'mla_g_q', 'mla_g_kv', 'mla_w_uq',
           'mla_w_ukv', 'mla_w_o', 'ffn_w_in', 'ffn_conv_w', 'ffn_conv_b', 'ffn_w_out', 'final_g']
BIG = ['fox_w_in', 'fox_w_o', 'mla_w_a', 'mla_w_uq', 'mla_w_ukv', 'mla_w_o', 'ffn_w_in', 'ffn_w_out']
SMALL = ['ada_b', 'fox_b_f', 'mla_g_q', 'mla_g_kv', 'ffn_conv_w', 'ffn_conv_b', 'final_g']


def _as2d(a):
    return a.reshape(-1, a.shape[-1])


def kernel(x, c, ada_w, ada_b, fox_w_in, fox_b_f, fox_w_o, mla_w_a, mla_g_q, mla_g_kv, mla_w_uq, mla_w_ukv, mla_w_o, ffn_w_in, ffn_conv_w, ffn_conv_b, ffn_w_out, final_g, loss_target, m_ada_w, m_ada_b, m_fox_w_in, m_fox_b_f, m_fox_w_o, m_mla_w_a, m_mla_g_q, m_mla_g_kv, m_mla_w_uq, m_mla_w_ukv, m_mla_w_o, m_ffn_w_in, m_ffn_conv_w, m_ffn_conv_b, m_ffn_w_out, m_final_g, v_ada_w, v_ada_b, v_fox_w_in, v_fox_b_f, v_fox_w_o, v_mla_w_a, v_mla_g_q, v_mla_g_kv, v_mla_w_uq, v_mla_w_ukv, v_mla_w_o, v_ffn_w_in, v_ffn_conv_w, v_ffn_conv_b, v_ffn_w_out, v_final_g):
    w = dict(ada_w=ada_w, ada_b=ada_b, fox_w_in=fox_w_in, fox_b_f=fox_b_f, fox_w_o=fox_w_o, mla_w_a=mla_w_a,
             mla_g_q=mla_g_q, mla_g_kv=mla_g_kv, mla_w_uq=mla_w_uq, mla_w_ukv=mla_w_ukv, mla_w_o=mla_w_o,
             ffn_w_in=ffn_w_in, ffn_conv_w=ffn_conv_w, ffn_conv_b=ffn_conv_b, ffn_w_out=ffn_w_out, final_g=final_g)
    m = dict(ada_w=m_ada_w, ada_b=m_ada_b, fox_w_in=m_fox_w_in, fox_b_f=m_fox_b_f, fox_w_o=m_fox_w_o,
             mla_w_a=m_mla_w_a, mla_g_q=m_mla_g_q, mla_g_kv=m_mla_g_kv, mla_w_uq=m_mla_w_uq,
             mla_w_ukv=m_mla_w_ukv, mla_w_o=m_mla_w_o, ffn_w_in=m_ffn_w_in, ffn_conv_w=m_ffn_conv_w,
             ffn_conv_b=m_ffn_conv_b, ffn_w_out=m_ffn_w_out, final_g=m_final_g)
    v = dict(ada_w=v_ada_w, ada_b=v_ada_b, fox_w_in=v_fox_w_in, fox_b_f=v_fox_b_f, fox_w_o=v_fox_w_o,
             mla_w_a=v_mla_w_a, mla_g_q=v_mla_g_q, mla_g_kv=v_mla_g_kv, mla_w_uq=v_mla_w_uq,
             mla_w_ukv=v_mla_w_ukv, mla_w_o=v_mla_w_o, ffn_w_in=v_ffn_w_in, ffn_conv_w=v_ffn_conv_w,
             ffn_conv_b=v_ffn_conv_b, ffn_w_out=v_ffn_w_out, final_g=v_final_g)
    D = D_MODEL
    xi, yi, ci = lax.axis_index("x"), lax.axis_index("y"), lax.axis_index("c")
    dev = 4 * xi + 2 * yi + ci

    n_ada = ada_w.shape[-1]
    small_parts = [c.reshape(-1), ffn_conv_w.reshape(-1), mla_g_q.reshape(-1), mla_g_kv.reshape(-1)]
    sizes = [p.shape[0] for p in small_parts]
    flat = jnp.concatenate(small_parts)
    flat = jnp.pad(flat, (0, -flat.shape[0] % LANE))[None]
    got = _all_gather8(flat, name="gather_cond")[:, 0]
    offs = np.cumsum([0] + sizes)
    c_all = got[:, offs[0]:offs[1]]
    chips = got[0::2]
    conv_w_full = jnp.concatenate(
        [chips[s, offs[1]:offs[2]].reshape(ffn_conv_w.shape) for s in range(4)], axis=2)
    gq_full = jnp.concatenate([chips[s, offs[2]:offs[3]] for s in range(4)])[None]
    gkv_full = jnp.concatenate([chips[s, offs[3]:offs[4]] for s in range(4)])[None]

    mod_shard = _ada_mod(c_all, ada_w.reshape(4, D, n_ada), ada_b.reshape(4, 1, n_ada))
    mod_all = _all_gather8(mod_shard.reshape(4 * 8, n_ada), name="gather_mod")
    mod_all = mod_all[0::2].reshape(4, 4, 8, n_ada)
    mine = lax.dynamic_index_in_dim(mod_all, dev, axis=2, keepdims=False)
    mod_rows = jnp.transpose(mine, (1, 0, 2)).reshape(4, 4 * n_ada)
    mods = [(mod_rows[k:k + 1, :D], mod_rows[k:k + 1, D:2 * D], mod_rows[k:k + 1, 2 * D:]) for k in range(4)]

    gathered = _all_gather_chips([w[name].astype(BF16) for name in BIG], name="gather_weights")
    W = _prepare_weights(gathered)
    small = dict(fox_b_f=fox_b_f, mla_g_q=gq_full, mla_g_kv=gkv_full, ffn_conv_w=conv_w_full,
                 ffn_conv_b=ffn_conv_b, final_g=final_g[None])

    loss, dx, dmods, G = _local_step(x[0], loss_target[0], mods, W, small)
    G["loss"] = loss[0, :1]

    dmod_all = _all_gather8(dmods, name="gather_dmod")
    chip = 2 * xi + yi
    dmod_cols = lax.dynamic_slice_in_dim(dmod_all, chip * n_ada, n_ada, axis=2)
    g_ada_w, g_ada_b = _ada_grad(c_all.T, jnp.transpose(dmod_cols, (1, 0, 2)))
    grads = dict(ada_w=g_ada_w.reshape(ada_w.shape), ada_b=g_ada_b.reshape(ada_b.shape))

    packed = _pack(G)
    recv = _exchange_halves(packed)
    part = _add_halves(packed, recv, ci.reshape(1).astype(jnp.int32))
    from_chips = _exchange_chips(part)
    half = _add_chips(from_chips)
    both = _share_halves(half)
    grads.update(_unpack(both.reshape(-1), G))
    loss_total = grads.pop("loss")[0]

    delta, new_m, new_v = {}, {}, {}
    for name in BIG + ["ada_w"]:
        shape = w[name].shape
        d_, m_, v_ = _adamw(_as2d(w[name]), _as2d(grads[name]), _as2d(m[name]), _as2d(v[name]), name=f"adamw_{name}")
        delta[name], new_m[name], new_v[name] = d_.reshape(shape), m_.reshape(shape), v_.reshape(shape)
        grads[name] = grads[name].reshape(shape)
    sizes = [math.prod(w[name].shape) for name in SMALL]
    total = sum(sizes)
    rows = -(-total // LANE)
    rows += -rows % 8

    def pack_small(d):
        flat = jnp.concatenate([d[name].reshape(-1) for name in SMALL])
        return jnp.pad(flat, (0, rows * LANE - total)).reshape(rows, LANE)

    outs = _adamw(pack_small(w), pack_small(grads), pack_small(m), pack_small(v), name="adamw_small")
    off = 0
    for name, size in zip(SMALL, sizes):
        shape = w[name].shape
        for dst, src in zip((delta, new_m, new_v), outs):
            dst[name] = src.reshape(-1)[off:off + size].reshape(shape)
        grads[name] = grads[name].reshape(shape)
        off += size

    return (loss_total, dx[None], *[grads[n] for n in WEIGHTS], *[delta[n] for n in WEIGHTS],
            *[new_m[n] for n in WEIGHTS], *[new_v[n] for n in WEIGHTS])
```

```python
import functools
import math

import numpy as np
import jax
import jax.numpy as jnp
from jax import lax
from jax.experimental import pallas as pl
from jax.experimental.pallas import tpu as pltpu

F32 = jnp.float32
BF16 = jnp.bfloat16
MESH = pl.DeviceIdType.MESH

D_MODEL = 1024
HEADS = 16
HEAD_PAD = 128
FOX_HEAD = 64
MLA_NOPE = 64
MLA_ROPE = 32
MLA_V = 64
MLA_QR = 384
MLA_KVR = 256
D_FF = 2816
NORM_EPS = 1e-6
ROPE_BASE = 10000.0
ADAM_LR = 0.001
ADAM_B1 = 0.9
ADAM_B2 = 0.999
ADAM_EPS = 1e-08
ADAM_WD = 0.01
ADAM_STEP = 10
FOX_SCALE = FOX_HEAD ** -0.5
MLA_SCALE = (MLA_NOPE + MLA_ROPE) ** -0.5
NEG_BIG = -1e30
VMEM_LIMIT = 56 * 1024 * 1024
LANE = 128
PACK_ROW = 1024
PACK_ROWS_ALIGN = 256

NT = (((1,), (1,)), ((), ()))
TN = (((0,), (0,)), ((), ()))
NN = (((1,), (0,)), ((), ()))


def _cparams():
    return pltpu.CompilerParams(vmem_limit_bytes=VMEM_LIMIT)


def _tile(n, cap):
    if n <= cap:
        return n
    for t in range(cap - cap % LANE, 0, -LANE):
        if n % t == 0:
            return t
    raise ValueError((n, cap))


def _row_tile(rows, cols, limit_bytes=1 << 20):
    best = None
    for t in range(8, rows + 1, 8):
        if rows % t == 0 and t * cols * 4 <= limit_bytes:
            best = t
    return best if best is not None else rows


def _me():
    return lax.axis_index("x"), lax.axis_index("y"), lax.axis_index("c")


def _all_gather8(v, name):
    R, N = v.shape

    def body(v_ref, o_ref, ssem, rsem):
        x, y, c = _me()
        me = 4 * x + 2 * y + c
        o_ref[me] = v_ref[...]
        copies = []
        for k in range(1, 8):
            peer = (1 - x if k & 4 else x, 1 - y if k & 2 else y, 1 - c if k & 1 else c)
            cp = pltpu.make_async_remote_copy(
                src_ref=v_ref, dst_ref=o_ref.at[me], send_sem=ssem.at[k - 1], recv_sem=rsem.at[k - 1],
                device_id=peer, device_id_type=MESH)
            cp.start()
            copies.append(cp)
        for cp in copies:
            cp.wait()

    return pl.pallas_call(
        body, name=name,
        out_shape=jax.ShapeDtypeStruct((8, R, N), v.dtype),
        in_specs=[pl.BlockSpec(memory_space=pltpu.VMEM)],
        out_specs=pl.BlockSpec(memory_space=pltpu.VMEM),
        scratch_shapes=[pltpu.SemaphoreType.DMA((7,)), pltpu.SemaphoreType.DMA((7,))],
    )(v)


def _all_gather_chips(shards, name):
    n = len(shards)

    def body(*refs):
        ins, outs = refs[:n], refs[n:2 * n]
        ssem, rsem, lsem = refs[2 * n:]
        x, y, c = _me()
        me = 2 * x + y
        local, remote = [], []
        for i in range(n):
            cp = pltpu.make_async_copy(ins[i], outs[i].at[me], lsem.at[i])
            cp.start()
            local.append(cp)
            for k in (1, 2, 3):
                peer = (1 - x if k & 2 else x, 1 - y if k & 1 else y, c)
                cp = pltpu.make_async_remote_copy(
                    src_ref=ins[i], dst_ref=outs[i].at[me], send_sem=ssem.at[3 * i + k - 1],
                    recv_sem=rsem.at[3 * i + k - 1], device_id=peer, device_id_type=MESH)
                cp.start()
                remote.append(cp)
        for cp in local:
            cp.wait()
        for cp in remote:
            cp.wait()

    return pl.pallas_call(
        body, name=name,
        out_shape=[jax.ShapeDtypeStruct((4,) + s.shape, s.dtype) for s in shards],
        in_specs=[pl.BlockSpec(memory_space=pl.ANY)] * n,
        out_specs=[pl.BlockSpec(memory_space=pl.ANY)] * n,
        scratch_shapes=[pltpu.SemaphoreType.DMA((3 * n,)), pltpu.SemaphoreType.DMA((3 * n,)),
                        pltpu.SemaphoreType.DMA((n,))],
    )(*shards)


def _exchange_halves(g):
    _, _, R, P = g.shape

    def body(g_ref, o_ref, ssem, rsem):
        x, y, c = _me()
        copies = []
        for s in range(4):
            cp = pltpu.make_async_remote_copy(
                src_ref=g_ref.at[s, 1 - c], dst_ref=o_ref.at[s], send_sem=ssem.at[s], recv_sem=rsem.at[s],
                device_id=(x, y, 1 - c), device_id_type=MESH)
            cp.start()
            copies.append(cp)
        for cp in copies:
            cp.wait()

    return pl.pallas_call(
        body, name="grad_exchange_sibling",
        out_shape=jax.ShapeDtypeStruct((4, R, P), g.dtype),
        in_specs=[pl.BlockSpec(memory_space=pl.ANY)],
        out_specs=pl.BlockSpec(memory_space=pl.ANY),
        scratch_shapes=[pltpu.SemaphoreType.DMA((4,)), pltpu.SemaphoreType.DMA((4,))],
    )(g)


def _exchange_chips(a):
    _, R, P = a.shape

    def body(a_ref, o_ref, ssem, rsem, lsem):
        x, y, c = _me()
        me = 2 * x + y
        own = pltpu.make_async_copy(a_ref.at[me], o_ref.at[me], lsem)
        own.start()
        copies = []
        for k in (1, 2, 3):
            px, py = (1 - x if k & 2 else x), (1 - y if k & 1 else y)
            cp = pltpu.make_async_remote_copy(
                src_ref=a_ref.at[2 * px + py], dst_ref=o_ref.at[me], send_sem=ssem.at[k - 1],
                recv_sem=rsem.at[k - 1], device_id=(px, py, c), device_id_type=MESH)
            cp.start()
            copies.append(cp)
        own.wait()
        for cp in copies:
            cp.wait()

    return pl.pallas_call(
        body, name="grad_exchange_chips",
        out_shape=jax.ShapeDtypeStruct((4, R, P), a.dtype),
        in_specs=[pl.BlockSpec(memory_space=pl.ANY)],
        out_specs=pl.BlockSpec(memory_space=pl.ANY),
        scratch_shapes=[pltpu.SemaphoreType.DMA((3,)), pltpu.SemaphoreType.DMA((3,)), pltpu.SemaphoreType.DMA],
    )(a)


def _share_halves(f):
    R, P = f.shape

    def body(f_ref, o_ref, ssem, rsem, lsem):
        x, y, c = _me()
        own = pltpu.make_async_copy(f_ref, o_ref.at[c], lsem)
        own.start()
        cp = pltpu.make_async_remote_copy(
            src_ref=f_ref, dst_ref=o_ref.at[c], send_sem=ssem, recv_sem=rsem,
            device_id=(x, y, 1 - c), device_id_type=MESH)
        cp.start()
        own.wait()
        cp.wait()

    return pl.pallas_call(
        body, name="grad_share_sibling",
        out_shape=jax.ShapeDtypeStruct((2, R, P), f.dtype),
        in_specs=[pl.BlockSpec(memory_space=pl.ANY)],
        out_specs=pl.BlockSpec(memory_space=pl.ANY),
        scratch_shapes=[pltpu.SemaphoreType.DMA, pltpu.SemaphoreType.DMA, pltpu.SemaphoreType.DMA],
    )(f)


def _add_halves(g, r, c_idx):
    _, _, R, P = g.shape
    t = PACK_ROWS_ALIGN

    def body(c_ref, g_ref, r_ref, o_ref):
        o_ref[...] = g_ref[0] + r_ref[...]

    return pl.pallas_call(
        body, name="grad_add_halves",
        out_shape=jax.ShapeDtypeStruct((4, R, P), F32),
        grid_spec=pltpu.PrefetchScalarGridSpec(
            num_scalar_prefetch=1, grid=(4, R // t),
            in_specs=[pl.BlockSpec((1, 1, t, P), lambda s, i, c_ref: (s, c_ref[0], i, 0)),
                      pl.BlockSpec((1, t, P), lambda s, i, c_ref: (s, i, 0))],
            out_specs=pl.BlockSpec((1, t, P), lambda s, i, c_ref: (s, i, 0))),
        compiler_params=_cparams(),
    )(c_idx, g, r)


def _add_chips(b):
    _, R, P = b.shape
    t = PACK_ROWS_ALIGN

    def body(b_ref, o_ref):
        o_ref[...] = ((b_ref[0] + b_ref[1]) + b_ref[2]) + b_ref[3]

    return pl.pallas_call(
        body, name="grad_add_chips",
        out_shape=jax.ShapeDtypeStruct((R, P), F32),
        grid=(R // t,),
        in_specs=[pl.BlockSpec((4, t, P), lambda i: (0, i, 0))],
        out_specs=pl.BlockSpec((t, P), lambda i: (i, 0)),
        compiler_params=_cparams(),
    )(b)


def _mm(a, b, *, ta=False, tb=False, out_dtype=F32, res=None, name):
    if ta:
        K, M = a.shape
    else:
        M, K = a.shape
    N = b.shape[0] if tb else b.shape[1]
    assert (b.shape[1] if tb else b.shape[0]) == K, (a.shape, b.shape, ta, tb)
    tm, tn, tk = _tile(M, 512), _tile(N, 1408), _tile(K, 1408)
    nk = K // tk
    has_res = res is not None
    dims = (((0 if ta else 1,), (1 if tb else 0,)), ((), ()))

    def body(*refs):
        if has_res:
            a_ref, b_ref, r_ref, o_ref, acc_ref = refs
        else:
            a_ref, b_ref, o_ref, acc_ref = refs
        k = pl.program_id(2)

        @pl.when(k == 0)
        def _():
            acc_ref[...] = jnp.zeros_like(acc_ref)

        acc_ref[...] += lax.dot_general(a_ref[...].astype(BF16), b_ref[...].astype(BF16), dims,
                                        preferred_element_type=F32)

        @pl.when(k == nk - 1)
        def _():
            r = acc_ref[...]
            if has_res:
                r = r + r_ref[...]
            o_ref[...] = r.astype(out_dtype)

    a_spec = (pl.BlockSpec((tk, tm), lambda i, j, k: (k, i)) if ta
              else pl.BlockSpec((tm, tk), lambda i, j, k: (i, k)))
    b_spec = (pl.BlockSpec((tn, tk), lambda i, j, k: (j, k)) if tb
              else pl.BlockSpec((tk, tn), lambda i, j, k: (k, j)))
    in_specs = [a_spec, b_spec]
    args = [a, b]
    if has_res:
        in_specs.append(pl.BlockSpec((tm, tn), lambda i, j, k: (i, j)))
        args.append(res)
    return pl.pallas_call(
        body, name=name,
        out_shape=jax.ShapeDtypeStruct((M, N), out_dtype),
        grid=(M // tm, N // tn, nk),
        in_specs=in_specs,
        out_specs=pl.BlockSpec((tm, tn), lambda i, j, k: (i, j)),
        scratch_shapes=[pltpu.VMEM((tm, tn), F32)],
        compiler_params=_cparams(),
    )(*args)


def _rstd(xv):
    return lax.rsqrt(jnp.mean(xv * xv, axis=-1, keepdims=True) + NORM_EPS)


def _norm_bwd(n, r, dn):
    return r * (dn - n * jnp.mean(dn * n, axis=-1, keepdims=True))


def _adaln_fwd(x, scale, shift, name):
    S, D = x.shape
    t = _tile(S, 512)

    def body(x_ref, sc_ref, sh_ref, h_ref):
        xv = x_ref[...]
        h_ref[...] = (xv * _rstd(xv) * (1.0 + sc_ref[...]) + sh_ref[...]).astype(BF16)

    row = pl.BlockSpec((t, D), lambda i: (i, 0))
    vec = pl.BlockSpec((1, D), lambda i: (0, 0))
    return pl.pallas_call(
        body, name=name, out_shape=jax.ShapeDtypeStruct((S, D), BF16), grid=(S // t,),
        in_specs=[row, vec, vec], out_specs=row, compiler_params=_cparams(),
    )(x, scale, shift)


def _adaln_bwd(x, dh, dx_next, scale, name):
    S, D = x.shape
    t = _tile(S, 512)

    def body(x_ref, dh_ref, dxn_ref, sc_ref, dx_ref, dsc_ref, dsh_ref):
        @pl.when(pl.program_id(0) == 0)
        def _():
            dsc_ref[...] = jnp.zeros_like(dsc_ref)
            dsh_ref[...] = jnp.zeros_like(dsh_ref)

        xv = x_ref[...]
        r = _rstd(xv)
        n = xv * r
        dh = dh_ref[...]
        dx_ref[...] = dxn_ref[...] + _norm_bwd(n, r, dh * (1.0 + sc_ref[...]))
        dsc_ref[...] += jnp.sum(dh * n, axis=0, keepdims=True)
        dsh_ref[...] += jnp.sum(dh, axis=0, keepdims=True)

    row = pl.BlockSpec((t, D), lambda i: (i, 0))
    vec = pl.BlockSpec((1, D), lambda i: (0, 0))
    return pl.pallas_call(
        body, name=name,
        out_shape=[jax.ShapeDtypeStruct((S, D), F32), jax.ShapeDtypeStruct((1, D), F32),
                   jax.ShapeDtypeStruct((1, D), F32)],
        grid=(S // t,), in_specs=[row, row, row, vec], out_specs=[row, vec, vec],
        compiler_params=_cparams(),
    )(x, dh, dx_next, scale)


def _residual(x, gate, y, name):
    S, D = x.shape
    t = _tile(S, 512)

    def body(x_ref, g_ref, y_ref, o_ref):
        o_ref[...] = x_ref[...] + g_ref[...] * y_ref[...]

    row = pl.BlockSpec((t, D), lambda i: (i, 0))
    vec = pl.BlockSpec((1, D), lambda i: (0, 0))
    return pl.pallas_call(
        body, name=name, out_shape=jax.ShapeDtypeStruct((S, D), F32), grid=(S // t,),
        in_specs=[row, vec, row], out_specs=row, compiler_params=_cparams(),
    )(x, gate, y)


def _gate_bwd(dx, y, gate, name):
    S, D = dx.shape
    t = _tile(S, 512)

    def body(dx_ref, y_ref, g_ref, dy_ref, dg_ref):
        @pl.when(pl.program_id(0) == 0)
        def _():
            dg_ref[...] = jnp.zeros_like(dg_ref)

        dxv = dx_ref[...]
        dy_ref[...] = (dxv * g_ref[...]).astype(BF16)
        dg_ref[...] += jnp.sum(dxv * y_ref[...], axis=0, keepdims=True)

    row = pl.BlockSpec((t, D), lambda i: (i, 0))
    vec = pl.BlockSpec((1, D), lambda i: (0, 0))
    return pl.pallas_call(
        body, name=name,
        out_shape=[jax.ShapeDtypeStruct((S, D), BF16), jax.ShapeDtypeStruct((1, D), F32)],
        grid=(S // t,), in_specs=[row, row, vec], out_specs=[row, vec], compiler_params=_cparams(),
    )(dx, y, gate)


def _final_loss(x, g, target):
    S, D = x.shape
    t = _tile(S, 512)

    def body(x_ref, g_ref, t_ref, loss_ref, dx_ref, dg_ref):
        @pl.when(pl.program_id(0) == 0)
        def _():
            loss_ref[...] = jnp.zeros_like(loss_ref)
            dg_ref[...] = jnp.zeros_like(dg_ref)

        xv = x_ref[...]
        gv = g_ref[...]
        r = _rstd(xv)
        n = xv * r
        err = n * gv - t_ref[...]
        part = jnp.sum(jnp.mean(err * err, axis=-1, keepdims=True), axis=0, keepdims=True)
        loss_ref[...] += jnp.broadcast_to(0.5 * part, loss_ref.shape)
        dy = err * (1.0 / D)
        dg_ref[...] += jnp.sum(dy * n, axis=0, keepdims=True)
        dx_ref[...] = _norm_bwd(n, r, dy * gv)

    row = pl.BlockSpec((t, D), lambda i: (i, 0))
    vec = pl.BlockSpec((1, D), lambda i: (0, 0))
    one = pl.BlockSpec((1, LANE), lambda i: (0, 0))
    return pl.pallas_call(
        body, name="final_loss",
        out_shape=[jax.ShapeDtypeStruct((1, LANE), F32), jax.ShapeDtypeStruct((S, D), F32),
                   jax.ShapeDtypeStruct((1, D), F32)],
        grid=(S // t,), in_specs=[row, vec, row], out_specs=[one, row, vec], compiler_params=_cparams(),
    )(x, g, target)


ATTN_TILE = 512
ATTN_ROWS = 64
ONES_LANE = 127


def _causal_pairs(n, k_major):
    if k_major:
        pairs = [(i, j) for j in range(n) for i in range(j, n)]
    else:
        pairs = [(i, j) for i in range(n) for j in range(i + 1)]
    return (jnp.asarray(np.array([p[0] for p in pairs], np.int32)),
            jnp.asarray(np.array([p[1] for p in pairs], np.int32)))


def _attn_fwd(q_arr, q_off, k_arr, k_off, v_arr, v_off, name):
    S = q_arr.shape[0]
    t = _tile(S, ATTN_TILE)
    n = S // t
    rc = min(ATTN_ROWS, t)
    ii, jj = _causal_pairs(n, k_major=False)

    def body(ii_ref, jj_ref, q_ref, k_ref, v_ref, o_ref, lse_ref, s_s, m_s, a_s, acc_s):
        step = pl.program_id(1)
        i, j = ii_ref[step], jj_ref[step]
        lane = lax.broadcasted_iota(jnp.int32, (t, HEAD_PAD), 1)

        @pl.when(j == 0)
        def _():
            m_s[...] = jnp.full_like(m_s, NEG_BIG)
            acc_s[...] = jnp.zeros_like(acc_s)

        def tile(diagonal):
            s_s[...] = lax.dot_general(q_ref[...], k_ref[...], NT, preferred_element_type=F32)
            for r in range(t // rc):
                rows = pl.ds(r * rc, rc)
                sc = s_s[rows, :]
                if diagonal:
                    qpos = r * rc + lax.broadcasted_iota(jnp.int32, (rc, t), 0)
                    kpos = lax.broadcasted_iota(jnp.int32, (rc, t), 1)
                    sc = jnp.where(qpos >= kpos, sc, NEG_BIG)
                m_old = m_s[rows, :]
                m_new = jnp.maximum(m_old, jnp.max(sc, axis=1, keepdims=True))
                s_s[rows, :] = jnp.exp(sc - m_new)
                a_s[rows, :] = jnp.exp(m_old - m_new)
                m_s[rows, :] = m_new
            v_ones = jnp.where(lane == ONES_LANE, 1.0, v_ref[...].astype(F32)).astype(BF16)
            acc_s[...] = a_s[...] * acc_s[...] + lax.dot_general(s_s[...].astype(BF16), v_ones, NN,
                                                                 preferred_element_type=F32)

        @pl.when(j < i)
        def _():
            tile(False)

        @pl.when(j == i)
        def _():
            tile(True)
            acc = acc_s[...]
            l = jnp.sum(jnp.where(lane == ONES_LANE, acc, 0.0), axis=1, keepdims=True)
            o_ref[...] = jnp.where(lane == ONES_LANE, 0.0, acc / l).astype(BF16)
            lse_ref[0] = m_s[...] + jnp.log(l)

    grid_spec = pltpu.PrefetchScalarGridSpec(
        num_scalar_prefetch=2, grid=(HEADS, ii.shape[0]),
        in_specs=[pl.BlockSpec((t, HEAD_PAD), lambda h, s, ii, jj: (ii[s], q_off + h)),
                  pl.BlockSpec((t, HEAD_PAD), lambda h, s, ii, jj: (jj[s], k_off + h)),
                  pl.BlockSpec((t, HEAD_PAD), lambda h, s, ii, jj: (jj[s], v_off + h))],
        out_specs=[pl.BlockSpec((t, HEAD_PAD), lambda h, s, ii, jj: (ii[s], h)),
                   pl.BlockSpec((1, t, 1), lambda h, s, ii, jj: (h, ii[s], 0))],
        scratch_shapes=[pltpu.VMEM((t, t), F32), pltpu.VMEM((t, 1), F32), pltpu.VMEM((t, 1), F32),
                        pltpu.VMEM((t, HEAD_PAD), F32)])
    return pl.pallas_call(
        body, name=name, grid_spec=grid_spec,
        out_shape=[jax.ShapeDtypeStruct((S, HEADS * HEAD_PAD), BF16), jax.ShapeDtypeStruct((HEADS, S, 1), F32)],
        compiler_params=_cparams(),
    )(ii, jj, q_arr, k_arr, v_arr)


def _attn_delta(do, o, name):
    S = do.shape[0]
    t = _tile(S, 512)

    def body(do_ref, o_ref, d_ref):
        lane = lax.broadcasted_iota(jnp.int32, (t, LANE), 1)
        acc = jnp.zeros((t, LANE), F32)
        for h in range(HEADS):
            cols = pl.ds(h * HEAD_PAD, HEAD_PAD)
            d = jnp.sum(do_ref[:, cols].astype(F32) * o_ref[:, cols].astype(F32), axis=1, keepdims=True)
            acc = jnp.where(lane == h, d, acc)
        d_ref[...] = acc

    blk = pl.BlockSpec((t, HEADS * HEAD_PAD), lambda i: (i, 0))
    by_lane = pl.pallas_call(
        body, name=name, out_shape=jax.ShapeDtypeStruct((S, LANE), F32), grid=(S // t,),
        in_specs=[blk, blk], out_specs=pl.BlockSpec((t, LANE), lambda i: (i, 0)),
        compiler_params=_cparams(),
    )(do, o)
    return by_lane[:, :HEADS].T.reshape(HEADS, 1, S)


def _attn_bwd(q_arr, q_off, k_arr, k_off, v_arr, v_off, do, lse, delta, name):
    S = q_arr.shape[0]
    t = _tile(S, ATTN_TILE)
    n = S // t
    rc = min(ATTN_ROWS, t)
    ii, jj = _causal_pairs(n, k_major=True)

    def body(ii_ref, jj_ref, q_ref, k_ref, v_ref, do_ref, lse_ref, dl_ref, dq_ref, dk_ref, dv_ref,
             s_s, dp_s, dk_s, dv_s):
        step = pl.program_id(1)
        i, j = ii_ref[step], jj_ref[step]

        @pl.when(step == 0)
        def _():
            dq_ref[...] = jnp.zeros_like(dq_ref)

        @pl.when(i == j)
        def _():
            dk_s[...] = jnp.zeros_like(dk_s)
            dv_s[...] = jnp.zeros_like(dv_s)

        def tile(diagonal):
            q, k, dov = q_ref[...], k_ref[...], do_ref[...]
            s_s[...] = lax.dot_general(k, q, NT, preferred_element_type=F32)
            dp_s[...] = lax.dot_general(v_ref[...], dov, NT, preferred_element_type=F32)
            lse_row, dl_row = lse_ref[0], dl_ref[0]
            for r in range(t // rc):
                rows = pl.ds(r * rc, rc)
                sc = s_s[rows, :]
                if diagonal:
                    kpos = r * rc + lax.broadcasted_iota(jnp.int32, (rc, t), 0)
                    qpos = lax.broadcasted_iota(jnp.int32, (rc, t), 1)
                    sc = jnp.where(qpos >= kpos, sc, NEG_BIG)
                p = jnp.exp(sc - lse_row)
                s_s[rows, :] = p
                dp_s[rows, :] = p * (dp_s[rows, :] - dl_row)
            ds = dp_s[...].astype(BF16)
            dv_s[...] += lax.dot_general(s_s[...].astype(BF16), dov, NN, preferred_element_type=F32)
            dk_s[...] += lax.dot_general(ds, q, NN, preferred_element_type=F32)
            rows = pl.ds(pl.multiple_of(i * t, t), t)
            dq_ref[rows, :] += lax.dot_general(ds, k, TN, preferred_element_type=F32)

        @pl.when(i > j)
        def _():
            tile(False)

        @pl.when(i == j)
        def _():
            tile(True)

        @pl.when(i == n - 1)
        def _():
            dk_ref[...] = dk_s[...]
            dv_ref[...] = dv_s[...].astype(BF16)

    qblk = lambda off: pl.BlockSpec((t, HEAD_PAD), lambda h, s, ii, jj: (ii[s], off + h))
    kblk = lambda off: pl.BlockSpec((t, HEAD_PAD), lambda h, s, ii, jj: (jj[s], off + h))
    stat = pl.BlockSpec((1, 1, t), lambda h, s, ii, jj: (h, 0, ii[s]))
    wide = (S, HEADS * HEAD_PAD)
    grid_spec = pltpu.PrefetchScalarGridSpec(
        num_scalar_prefetch=2, grid=(HEADS, ii.shape[0]),
        in_specs=[qblk(q_off), kblk(k_off), kblk(v_off), qblk(0), stat, stat],
        out_specs=[pl.BlockSpec((S, HEAD_PAD), lambda h, s, ii, jj: (0, h)), kblk(0), kblk(0)],
        scratch_shapes=[pltpu.VMEM((t, t), F32), pltpu.VMEM((t, t), F32), pltpu.VMEM((t, HEAD_PAD), F32),
                        pltpu.VMEM((t, HEAD_PAD), F32)])
    return pl.pallas_call(
        body, name=name, grid_spec=grid_spec,
        out_shape=[jax.ShapeDtypeStruct(wide, F32), jax.ShapeDtypeStruct(wide, F32),
                   jax.ShapeDtypeStruct(wide, BF16)],
        compiler_params=_cparams(),
    )(ii, jj, q_arr, k_arr, v_arr, do, lse, delta)


BIAS_LANE = FOX_HEAD


def _fox_bias_lanes(qkv, cum, name):
    S = qkv.shape[0]
    t = _tile(S, 512)

    def body(q_ref, k_ref, c_ref, qo_ref, ko_ref):
        h = pl.program_id(1)
        lane = lax.broadcasted_iota(jnp.int32, (t, HEAD_PAD), 1)
        c = jnp.sum(jnp.where(lane == h, c_ref[...], 0.0), axis=1, keepdims=True)
        pieces = [x.astype(F32) for x in _split3(c)]

        def fill(x, first, second):
            out = x.astype(F32)
            for n_, (a, b) in enumerate(zip(first, second)):
                out = jnp.where(lane == BIAS_LANE + n_, a, out)
                out = jnp.where(lane == BIAS_LANE + 3 + n_, b, out)
            return out.astype(BF16)

        ones = [1.0, 1.0, 1.0]
        qo_ref[...] = fill(q_ref[...], ones, pieces)
        ko_ref[...] = fill(k_ref[...], [-x for x in pieces], ones)

    blk = lambda off: pl.BlockSpec((t, HEAD_PAD), lambda i, h: (i, off + h))
    shp = jax.ShapeDtypeStruct((S, HEADS * HEAD_PAD), BF16)
    return pl.pallas_call(
        body, name=name, out_shape=[shp, shp], grid=(S // t, HEADS),
        in_specs=[blk(0), blk(HEADS), pl.BlockSpec((t, LANE), lambda i, h: (i, 0))],
        out_specs=[blk(0), blk(0)], compiler_params=_cparams(),
    )(qkv, qkv, cum)


def _split3(x):
    hi = x.astype(BF16)
    r1 = x - hi.astype(F32)
    mid = r1.astype(BF16)
    lo = (r1 - mid.astype(F32)).astype(BF16)
    return hi, mid, lo


def _tri_matmul(tri, x):
    acc = None
    for part in _split3(x):
        d = lax.dot_general(tri, part, NN, preferred_element_type=F32)
        acc = d if acc is None else acc + d
    return acc


def _log_sigmoid(z):
    return jnp.minimum(z, 0.0) - jnp.log1p(jnp.exp(-jnp.abs(z)))


def _fox_gate_fwd(fl, bf):
    S = fl.shape[0]
    t = _tile(S, 256)

    def body(fl_ref, bf_ref, cum_ref, carry):
        @pl.when(pl.program_id(0) == 0)
        def _():
            carry[...] = jnp.zeros_like(carry)

        logf = _log_sigmoid(fl_ref[...] + bf_ref[...])
        tri = (lax.broadcasted_iota(jnp.int32, (t, t), 0) >= lax.broadcasted_iota(jnp.int32, (t, t), 1)).astype(BF16)
        cum_ref[...] = _tri_matmul(tri, logf) + carry[...]
        carry[...] += jnp.sum(logf, axis=0, keepdims=True)

    row = pl.BlockSpec((t, LANE), lambda i: (i, 0))
    return pl.pallas_call(
        body, name="fox_gate_fwd", out_shape=jax.ShapeDtypeStruct((S, LANE), F32), grid=(S // t,),
        in_specs=[row, pl.BlockSpec((1, LANE), lambda i: (0, 0))], out_specs=row,
        scratch_shapes=[pltpu.VMEM((1, LANE), F32)], compiler_params=_cparams(),
    )(fl, bf)


def _fox_gate_bwd(dcum, fl, bf):
    S = fl.shape[0]
    t = _tile(S, 256)
    n = S // t

    def body(dc_ref, fl_ref, bf_ref, df_ref, db_ref, carry):
        @pl.when(pl.program_id(0) == 0)
        def _():
            carry[...] = jnp.zeros_like(carry)
            db_ref[...] = jnp.zeros_like(db_ref)

        dc = dc_ref[...]
        tri = (lax.broadcasted_iota(jnp.int32, (t, t), 1) >= lax.broadcasted_iota(jnp.int32, (t, t), 0)).astype(BF16)
        dlogf = _tri_matmul(tri, dc) + carry[...]
        carry[...] += jnp.sum(dc, axis=0, keepdims=True)
        z = fl_ref[...] + bf_ref[...]
        e = jnp.exp(-jnp.abs(z))
        sig_neg = jnp.where(z >= 0, e, 1.0) / (1.0 + e)
        lanes = lax.broadcasted_iota(jnp.int32, (t, LANE), 1)
        df = jnp.where(lanes < HEADS, dlogf * sig_neg, 0.0)
        df_ref[...] = df
        db_ref[...] += jnp.sum(df, axis=0, keepdims=True)

    row = pl.BlockSpec((t, LANE), lambda i: (n - 1 - i, 0))
    vec = pl.BlockSpec((1, LANE), lambda i: (0, 0))
    return pl.pallas_call(
        body, name="fox_gate_bwd",
        out_shape=[jax.ShapeDtypeStruct((S, LANE), F32), jax.ShapeDtypeStruct((1, LANE), F32)],
        grid=(n,), in_specs=[row, row, vec], out_specs=[row, vec],
        scratch_shapes=[pltpu.VMEM((1, LANE), F32)], compiler_params=_cparams(),
    )(dcum, fl, bf)


def _rope_tables(S):
    pos = jnp.arange(S, dtype=F32)
    inv_freq = ROPE_BASE ** (-jnp.arange(0, MLA_ROPE, 2, dtype=F32) / MLA_ROPE)
    ang = pos[:, None] * inv_freq[None, :]
    cos, sin = jnp.cos(ang), jnp.sin(ang)
    half = MLA_ROPE // 2
    ones = jnp.ones((S, MLA_NOPE), F32)
    z = lambda w: jnp.zeros((S, w), F32)
    keep = jnp.concatenate([ones, cos, cos, z(HEAD_PAD - MLA_NOPE - MLA_ROPE)], axis=1)
    from_above = jnp.concatenate([z(MLA_NOPE), -sin, z(HEAD_PAD - MLA_NOPE - half)], axis=1)
    from_below = jnp.concatenate([z(MLA_NOPE + half), sin, z(HEAD_PAD - MLA_NOPE - MLA_ROPE)], axis=1)
    return keep, from_above, from_below


def _rope(x, keep, up, down, transpose):
    half = MLA_ROPE // 2
    if transpose:
        return x * keep + pltpu.roll(x * up, half, 1) + pltpu.roll(x * down, HEAD_PAD - half, 1)
    return x * keep + pltpu.roll(x, HEAD_PAD - half, 1) * up + pltpu.roll(x, half, 1) * down


def _rope_heads(x, tables, transpose, scale, name):
    S = x.shape[0]
    t = _tile(S, 256)

    def body(x_ref, a_ref, b_ref, c_ref, o_ref):
        keep, up, down = a_ref[...], b_ref[...], c_ref[...]
        for h in range(HEADS):
            cols = pl.ds(h * HEAD_PAD, HEAD_PAD)
            o_ref[:, cols] = (_rope(x_ref[:, cols], keep, up, down, transpose) * scale).astype(BF16)

    blk = pl.BlockSpec((t, HEADS * HEAD_PAD), lambda i: (i, 0))
    tab = pl.BlockSpec((t, HEAD_PAD), lambda i: (i, 0))
    return pl.pallas_call(
        body, name=name, out_shape=jax.ShapeDtypeStruct(x.shape, BF16), grid=(S // t,),
        in_specs=[blk, tab, tab, tab], out_specs=blk, compiler_params=_cparams(),
    )(x, *tables)


MLA_A_PAD = MLA_QR + MLA_KVR + HEAD_PAD
MLA_KV_IN = MLA_KVR + HEAD_PAD


def _mla_mid_fwd(a, gq, gkv, tables):
    S = a.shape[0]
    t = _tile(S, 512)

    def body(a_ref, gq_ref, gkv_ref, ta_ref, tb_ref, tc_ref, q_ref, kv_ref):
        cq = a_ref[:, :MLA_QR]
        ckv = a_ref[:, MLA_QR:MLA_QR + MLA_KVR]
        kr = a_ref[:, MLA_QR + MLA_KVR:]
        q_ref[...] = (cq * _rstd(cq) * gq_ref[...]).astype(BF16)
        kv_ref[:, :MLA_KVR] = (ckv * _rstd(ckv) * gkv_ref[...]).astype(BF16)
        kv_ref[:, MLA_KVR:] = _rope(kr, ta_ref[...], tb_ref[...], tc_ref[...], False).astype(BF16)

    tab = pl.BlockSpec((t, HEAD_PAD), lambda i: (i, 0))
    return pl.pallas_call(
        body, name="mla_mid_fwd",
        out_shape=[jax.ShapeDtypeStruct((S, MLA_QR), BF16), jax.ShapeDtypeStruct((S, MLA_KV_IN), BF16)],
        grid=(S // t,),
        in_specs=[pl.BlockSpec((t, MLA_A_PAD), lambda i: (i, 0)), pl.BlockSpec((1, MLA_QR), lambda i: (0, 0)),
                  pl.BlockSpec((1, MLA_KVR), lambda i: (0, 0)), tab, tab, tab],
        out_specs=[pl.BlockSpec((t, MLA_QR), lambda i: (i, 0)), pl.BlockSpec((t, MLA_KV_IN), lambda i: (i, 0))],
        compiler_params=_cparams(),
    )(a, gq, gkv, *tables)


def _mla_mid_bwd(a, dq, dkv, gq, gkv, tables):
    S = a.shape[0]
    t = _tile(S, 512)

    def body(a_ref, dq_ref, dkv_ref, gq_ref, gkv_ref, ta_ref, tb_ref, tc_ref, da_ref, dgq_ref, dgkv_ref):
        @pl.when(pl.program_id(0) == 0)
        def _():
            dgq_ref[...] = jnp.zeros_like(dgq_ref)
            dgkv_ref[...] = jnp.zeros_like(dgkv_ref)

        def one(c, dout, g):
            r = _rstd(c)
            n = c * r
            return _norm_bwd(n, r, dout * g), jnp.sum(dout * n, axis=0, keepdims=True)

        dcq, dgq = one(a_ref[:, :MLA_QR], dq_ref[...], gq_ref[...])
        dckv, dgkv = one(a_ref[:, MLA_QR:MLA_QR + MLA_KVR], dkv_ref[:, :MLA_KVR], gkv_ref[...])
        da_ref[:, :MLA_QR] = dcq.astype(BF16)
        da_ref[:, MLA_QR:MLA_QR + MLA_KVR] = dckv.astype(BF16)
        da_ref[:, MLA_QR + MLA_KVR:] = _rope(dkv_ref[:, MLA_KVR:], ta_ref[...], tb_ref[...], tc_ref[...],
                                             True).astype(BF16)
        dgq_ref[...] += dgq
        dgkv_ref[...] += dgkv

    tab = pl.BlockSpec((t, HEAD_PAD), lambda i: (i, 0))
    vq = pl.BlockSpec((1, MLA_QR), lambda i: (0, 0))
    vkv = pl.BlockSpec((1, MLA_KVR), lambda i: (0, 0))
    return pl.pallas_call(
        body, name="mla_mid_bwd",
        out_shape=[jax.ShapeDtypeStruct((S, MLA_A_PAD), BF16), jax.ShapeDtypeStruct((1, MLA_QR), F32),
                   jax.ShapeDtypeStruct((1, MLA_KVR), F32)],
        grid=(S // t,),
        in_specs=[pl.BlockSpec((t, MLA_A_PAD), lambda i: (i, 0)), pl.BlockSpec((t, MLA_QR), lambda i: (i, 0)),
                  pl.BlockSpec((t, MLA_KV_IN), lambda i: (i, 0)), vq, vkv, tab, tab, tab],
        out_specs=[pl.BlockSpec((t, MLA_A_PAD), lambda i: (i, 0)), vq, vkv],
        compiler_params=_cparams(),
    )(a, dq, dkv, gq, gkv, *tables)


FFN_ROWS = 256
FFN_COLS = 1408
HALO = 8


def _shift_down(cur, halo, s, first):
    rolled = pltpu.roll(cur, s, 0)
    hrolled = jnp.where(first, 0.0, pltpu.roll(halo, s, 0))
    row = lax.broadcasted_iota(jnp.int32, (HALO, cur.shape[1]), 0)
    top = jnp.where(row < s, hrolled, rolled[:HALO])
    return jnp.concatenate([top, rolled[HALO:]], axis=0)


def _shift_up(cur, halo, s, last):
    rows = cur.shape[0]
    rolled = pltpu.roll(cur, rows - s, 0)
    hrolled = jnp.where(last, 0.0, pltpu.roll(halo, HALO - s, 0))
    row = lax.broadcasted_iota(jnp.int32, (HALO, cur.shape[1]), 0)
    bottom = jnp.where(row >= HALO - s, hrolled, rolled[rows - HALO:])
    return jnp.concatenate([rolled[:rows - HALO], bottom], axis=0)


def _conv(u, halo, w_ref, b, first):
    u1 = _shift_down(u, halo, 1, first)
    u2 = _shift_down(u, halo, 2, first)
    y = b + w_ref[0:1, :] * u2
    y = y + w_ref[1:2, :] * u1
    y = y + w_ref[2:3, :] * u
    return y, u1, u2


def _ffn_specs(S, tr, tn):
    per = tr // HALO
    tile = pl.BlockSpec((tr, tn), lambda j, i: (i, j))
    before = pl.BlockSpec((HALO, tn), lambda j, i: (jnp.maximum(i * per - 1, 0), j))
    after = pl.BlockSpec((HALO, tn), lambda j, i: (jnp.minimum((i + 1) * per, S // HALO - 1), j))
    w3 = pl.BlockSpec((3, tn), lambda j, i: (0, j))
    w1 = pl.BlockSpec((1, tn), lambda j, i: (0, j))
    return tile, before, after, w3, w1


def _ffn_act_fwd(ug, uv, wg, wv, bg, bv, name):
    S, F = ug.shape
    tr, tn = _tile(S, FFN_ROWS), _tile(F, FFN_COLS)
    tile, before, _, w3, w1 = _ffn_specs(S, tr, tn)

    def body(ug_ref, uv_ref, hg_ref, hv_ref, wg_ref, wv_ref, bg_ref, bv_ref, o_ref):
        first = pl.program_id(1) == 0
        g, _, _ = _conv(ug_ref[...], hg_ref[...], wg_ref, bg_ref[...], first)
        v, _, _ = _conv(uv_ref[...], hv_ref[...], wv_ref, bv_ref[...], first)
        o_ref[...] = (g * jax.nn.sigmoid(g) * v).astype(BF16)

    return pl.pallas_call(
        body, name=name, out_shape=jax.ShapeDtypeStruct((S, F), BF16), grid=(F // tn, S // tr),
        in_specs=[tile, tile, before, before, w3, w3, w1, w1], out_specs=tile, compiler_params=_cparams(),
    )(ug, uv, ug, uv, wg, wv, bg, bv)


def _ffn_bwd_gate(ug, uv, dact, wg, wv, bg, bv, name):
    S, F = ug.shape
    tr, tn = _tile(S, FFN_ROWS), _tile(F, FFN_COLS)
    tile, before, _, w3, w1 = _ffn_specs(S, tr, tn)

    def body(ug_ref, uv_ref, hg_ref, hv_ref, da_ref, wg_ref, wv_ref, bg_ref, bv_ref,
             dyg_ref, dyv_ref, dwg_ref, dwv_ref, dbg_ref, dbv_ref):
        first = pl.program_id(1) == 0

        @pl.when(first)
        def _():
            for r in (dwg_ref, dwv_ref, dbg_ref, dbv_ref):
                r[...] = jnp.zeros_like(r)

        ugv, uvv = ug_ref[...], uv_ref[...]
        g, ug1, ug2 = _conv(ugv, hg_ref[...], wg_ref, bg_ref[...], first)
        v, uv1, uv2 = _conv(uvv, hv_ref[...], wv_ref, bv_ref[...], first)
        da = da_ref[...]
        sg = jax.nn.sigmoid(g)
        dyg = da * v * (sg * (1.0 + g * (1.0 - sg)))
        dyv = da * (g * sg)
        dyg_ref[...] = dyg
        dyv_ref[...] = dyv

        def add_taps(dw_ref, dy, taps):
            for k, u in enumerate(taps):
                dw_ref[k:k + 1, :] += jnp.sum(dy * u, axis=0, keepdims=True)

        add_taps(dwg_ref, dyg, (ug2, ug1, ugv))
        add_taps(dwv_ref, dyv, (uv2, uv1, uvv))
        dbg_ref[...] += jnp.sum(dyg, axis=0, keepdims=True)
        dbv_ref[...] += jnp.sum(dyv, axis=0, keepdims=True)

    big = jax.ShapeDtypeStruct((S, F), F32)
    s3, s1 = jax.ShapeDtypeStruct((3, F), F32), jax.ShapeDtypeStruct((1, F), F32)
    return pl.pallas_call(
        body, name=name, out_shape=[big, big, s3, s3, s1, s1], grid=(F // tn, S // tr),
        in_specs=[tile, tile, before, before, tile, w3, w3, w1, w1],
        out_specs=[tile, tile, w3, w3, w1, w1], compiler_params=_cparams(),
    )(ug, uv, ug, uv, dact, wg, wv, bg, bv)


def _ffn_bwd_conv(dy, w, name):
    S, F = dy.shape
    tr, tn = _tile(S, FFN_ROWS), _tile(F, FFN_COLS)
    tile, _, after, w3, _ = _ffn_specs(S, tr, tn)
    n = S // tr

    def body(dy_ref, h_ref, w_ref, o_ref):
        last = pl.program_id(1) == n - 1
        dyv = dy_ref[...]
        d1 = _shift_up(dyv, h_ref[...], 1, last)
        d2 = _shift_up(dyv, h_ref[...], 2, last)
        o_ref[...] = (w_ref[2:3, :] * dyv + w_ref[1:2, :] * d1 + w_ref[0:1, :] * d2).astype(BF16)

    return pl.pallas_call(
        body, name=name, out_shape=jax.ShapeDtypeStruct((S, F), BF16), grid=(F // tn, n),
        in_specs=[tile, after, w3], out_specs=tile, compiler_params=_cparams(),
    )(dy, dy, w)


def _dot3(a, b):
    a_hi = a.astype(BF16)
    a_lo = (a - a_hi.astype(F32)).astype(BF16)
    b_hi = b.astype(BF16)
    b_lo = (b - b_hi.astype(F32)).astype(BF16)
    d = lambda p, q: lax.dot_general(p, q, NN, preferred_element_type=F32)
    return d(a_hi, b_hi) + (d(a_hi, b_lo) + d(a_lo, b_hi))


def _ada_mod(c_all, w, b):
    nmod, D, n = w.shape

    def body(c_ref, w_ref, b_ref, o_ref):
        cv = c_ref[...]
        o_ref[0] = _dot3(cv * jax.nn.sigmoid(cv), w_ref[0]) + b_ref[0]

    return pl.pallas_call(
        body, name="ada_mod", out_shape=jax.ShapeDtypeStruct((nmod, 8, n), F32), grid=(nmod,),
        in_specs=[pl.BlockSpec((8, D), lambda m: (0, 0)), pl.BlockSpec((1, D, n), lambda m: (m, 0, 0)),
                  pl.BlockSpec((1, 1, n), lambda m: (m, 0, 0))],
        out_specs=pl.BlockSpec((1, 8, n), lambda m: (m, 0, 0)), compiler_params=_cparams(),
    )(c_all, w, b)


def _ada_grad(c_all_t, dmod):
    D = c_all_t.shape[0]
    nmod, _, n = dmod.shape
    tr = 256

    def body(c_ref, d_ref, dw_ref, db_ref):
        cv = c_ref[...]
        sc = cv * jax.nn.sigmoid(cv)
        dv = d_ref[0]
        acc = sc[:, 0:1] * dv[0:1, :]
        tot = dv[0:1, :]
        for k in range(1, 8):
            acc = acc + sc[:, k:k + 1] * dv[k:k + 1, :]
            tot = tot + dv[k:k + 1, :]
        dw_ref[0] = acc
        db_ref[0] = tot

    return pl.pallas_call(
        body, name="ada_grad",
        out_shape=[jax.ShapeDtypeStruct((nmod, D, n), F32), jax.ShapeDtypeStruct((nmod, 1, n), F32)],
        grid=(nmod, D // tr),
        in_specs=[pl.BlockSpec((tr, 8), lambda m, i: (i, 0)), pl.BlockSpec((1, 8, n), lambda m, i: (m, 0, 0))],
        out_specs=[pl.BlockSpec((1, tr, n), lambda m, i: (m, i, 0)), pl.BlockSpec((1, 1, n), lambda m, i: (m, 0, 0))],
        compiler_params=_cparams(),
    )(c_all_t, dmod)


def _adamw(w, g, m, v, name):
    R, C = w.shape
    t = _row_tile(R, C)

    def body(w_ref, g_ref, m_ref, v_ref, d_ref, nm_ref, nv_ref):
        gv = g_ref[...]
        mn = ADAM_B1 * m_ref[...] + (1.0 - ADAM_B1) * gv
        vn = ADAM_B2 * v_ref[...] + (1.0 - ADAM_B2) * (gv * gv)
        m_hat = mn / (1.0 - ADAM_B1 ** ADAM_STEP)
        v_hat = vn / (1.0 - ADAM_B2 ** ADAM_STEP)
        d_ref[...] = -ADAM_LR * (m_hat / (jnp.sqrt(v_hat) + ADAM_EPS) + ADAM_WD * w_ref[...])
        nm_ref[...] = mn
        nv_ref[...] = vn

    blk = pl.BlockSpec((t, C), lambda i: (i, 0))
    shp = jax.ShapeDtypeStruct((R, C), F32)
    return pl.pallas_call(
        body, name=name, out_shape=[shp, shp, shp], grid=(R // t,), in_specs=[blk] * 4, out_specs=[blk] * 3,
        compiler_params=_cparams(),
    )(w, g, m, v)


def _cat_cols(g4):
    return jnp.concatenate([g4[s] for s in range(4)], axis=-1)


def _cat_rows(g4):
    return jnp.concatenate([g4[s] for s in range(4)], axis=-2)


def _pad_head_cols(w, width):
    K = w.shape[0]
    w = w.reshape(K, HEADS, width)
    return jnp.pad(w, ((0, 0), (0, 0), (0, HEAD_PAD - width))).reshape(K, HEADS * HEAD_PAD)


def _unpad_head_cols(w, width):
    K = w.shape[0]
    return w.reshape(K, HEADS, HEAD_PAD)[:, :, :width].reshape(K, HEADS * width)


def _pad_head_rows(w, width):
    N = w.shape[1]
    w = w.reshape(HEADS, width, N)
    return jnp.pad(w, ((0, 0), (0, HEAD_PAD - width), (0, 0))).reshape(HEADS * HEAD_PAD, N)


def _unpad_head_rows(w, width):
    N = w.shape[1]
    return w.reshape(HEADS, HEAD_PAD, N)[:, :width, :].reshape(HEADS * width, N)


def _prepare_weights(gathered):
    fox_in, fox_o, mla_a, mla_uq, mla_ukv, mla_o, ffn_in, ffn_out = gathered
    W = {}
    w = _cat_cols(fox_in[:, 0])
    hw = HEADS * FOX_HEAD
    qkv = [_pad_head_cols(w[:, p * hw:(p + 1) * hw], FOX_HEAD) for p in range(3)]
    qkv[0] = qkv[0] * FOX_SCALE
    f = jnp.pad(w[:, 3 * hw:], ((0, 0), (0, LANE - HEADS)))
    W["fox_qkv"] = jnp.concatenate(qkv, axis=1)
    W["fox_f"] = f
    W["fox_all"] = jnp.concatenate(qkv + [f], axis=1)
    W["fox_o"] = _pad_head_rows(_cat_rows(fox_o[:, 0]), FOX_HEAD)
    w = _cat_rows(mla_a[:, 0])
    lat = MLA_QR + MLA_KVR
    W["mla_a"] = jnp.concatenate(
        [w[:, :lat], jnp.zeros((D_MODEL, MLA_NOPE), BF16), w[:, lat:],
         jnp.zeros((D_MODEL, HEAD_PAD - MLA_NOPE - MLA_ROPE), BF16)], axis=1)
    W["mla_uq"] = _pad_head_cols(_cat_cols(mla_uq[:, 0]), MLA_NOPE + MLA_ROPE)
    w = _cat_cols(mla_ukv[:, 0]).reshape(MLA_KVR, HEADS, MLA_NOPE + MLA_V)
    kn = _pad_head_cols(w[:, :, :MLA_NOPE].reshape(MLA_KVR, -1), MLA_NOPE)
    vv = _pad_head_cols(w[:, :, MLA_NOPE:].reshape(MLA_KVR, -1), MLA_V)
    eye = np.zeros((HEAD_PAD, HEADS, HEAD_PAD), np.float32)
    for j in range(MLA_NOPE, MLA_NOPE + MLA_ROPE):
        eye[j, :, j] = 1.0
    eye = jnp.asarray(eye.reshape(HEAD_PAD, HEADS * HEAD_PAD), BF16)
    bottom = jnp.concatenate([eye, jnp.zeros((HEAD_PAD, HEADS * HEAD_PAD), BF16)], axis=1)
    W["mla_kv"] = jnp.concatenate([jnp.concatenate([kn, vv], axis=1), bottom], axis=0)
    W["mla_o"] = _pad_head_rows(_cat_rows(mla_o[:, 0]), MLA_V)
    w = _cat_cols(ffn_in)
    W["ffn_g"] = [w[i, :, :D_FF] for i in range(2)]
    W["ffn_v"] = [w[i, :, D_FF:] for i in range(2)]
    w = _cat_rows(ffn_out)
    W["ffn_out"] = [w[i] for i in range(2)]
    return W


def _ffn_forward(xin, mod, W, i, conv_w, conv_b):
    shift, scale, gate = mod
    h = _adaln_fwd(xin, scale, shift, name=f"ffn{i}_adaln")
    ug = _mm(h, W["ffn_g"][i], name=f"ffn{i}_up_gate")
    uv = _mm(h, W["ffn_v"][i], name=f"ffn{i}_up_val")
    wg, wv = conv_w[i][:, :D_FF], conv_w[i][:, D_FF:]
    bg, bv = conv_b[i][None, :D_FF], conv_b[i][None, D_FF:]
    act = _ffn_act_fwd(ug, uv, wg, wv, bg, bv, name=f"ffn{i}_act")
    y = _mm(act, W["ffn_out"][i], name=f"ffn{i}_down")
    xout = _residual(xin, gate, y, name=f"ffn{i}_residual")
    return xout, (xin, h, ug, uv, act, y, wg, wv, bg, bv)


def _ffn_backward(dx_out, saved, mod, W, i):
    xin, h, ug, uv, act, y, wg, wv, bg, bv = saved
    shift, scale, gate = mod
    dy, dgate = _gate_bwd(dx_out, y, gate, name=f"ffn{i}_gate_bwd")
    d_out = _mm(act, dy, ta=True, name=f"ffn{i}_dw_out")
    dact = _mm(dy, W["ffn_out"][i], tb=True, name=f"ffn{i}_dact")
    dyg, dyv, dwg, dwv, dbg, dbv = _ffn_bwd_gate(ug, uv, dact, wg, wv, bg, bv, name=f"ffn{i}_bwd_gate")
    dug = _ffn_bwd_conv(dyg, wg, name=f"ffn{i}_bwd_conv_g")
    duv = _ffn_bwd_conv(dyv, wv, name=f"ffn{i}_bwd_conv_v")
    d_in = jnp.concatenate([_mm(h, dug, ta=True, name=f"ffn{i}_dw_gate"),
                            _mm(h, duv, ta=True, name=f"ffn{i}_dw_val")], axis=1)
    dh = _mm(dug, W["ffn_g"][i], tb=True, name=f"ffn{i}_dh_gate")
    dh = _mm(duv, W["ffn_v"][i], tb=True, res=dh, name=f"ffn{i}_dh_val")
    dx, dscale, dshift = _adaln_bwd(xin, dh, dx_out, scale, name=f"ffn{i}_adaln_bwd")
    grads = dict(ffn_w_in=d_in, ffn_w_out=d_out, ffn_conv_w=jnp.concatenate([dwg, dwv], axis=1),
                 ffn_conv_b=jnp.concatenate([dbg, dbv], axis=1)[0])
    return dx, jnp.concatenate([dshift, dscale, dgate], axis=1)[0], grads


def _local_step(x, target, mods, W, small):
    S = x.shape[0]
    tables = _rope_tables(S)
    wide = HEADS * HEAD_PAD
    bf = jnp.pad(small["fox_b_f"], ((0, 0), (0, LANE - HEADS)))
    gq, gkv = small["mla_g_q"], small["mla_g_kv"]

    shift, scale, gate = mods[0]
    h0 = _adaln_fwd(x, scale, shift, name="fox_adaln")
    qkv = _mm(h0, W["fox_qkv"], out_dtype=BF16, name="fox_qkv")
    fl = _mm(h0, W["fox_f"], name="fox_gate_logits")
    cum = _fox_gate_fwd(fl, bf)
    q_fox, k_fox = _fox_bias_lanes(qkv, cum, name="fox_bias_lanes")
    o0, lse0 = _attn_fwd(q_fox, 0, k_fox, 0, qkv, 2 * HEADS, name="fox_attn_fwd")
    y0 = _mm(o0, W["fox_o"], name="fox_out")
    x1 = _residual(x, gate, y0, name="fox_residual")
    x2, ffn0 = _ffn_forward(x1, mods[1], W, 0, small["ffn_conv_w"], small["ffn_conv_b"])

    shift, scale, gate = mods[2]
    h2 = _adaln_fwd(x2, scale, shift, name="mla_adaln")
    a = _mm(h2, W["mla_a"], name="mla_down")
    cqn, ckv_in = _mla_mid_fwd(a, gq, gkv, tables)
    q_raw = _mm(cqn, W["mla_uq"], name="mla_up_q")
    q_cat = _rope_heads(q_raw, tables, False, MLA_SCALE, name="mla_rope_q")
    kv = _mm(ckv_in, W["mla_kv"], out_dtype=BF16, name="mla_up_kv")
    o1, lse1 = _attn_fwd(q_cat, 0, kv, 0, kv, HEADS, name="mla_attn_fwd")
    y2 = _mm(o1, W["mla_o"], name="mla_out")
    x3 = _residual(x2, gate, y2, name="mla_residual")
    x4, ffn1 = _ffn_forward(x3, mods[3], W, 1, small["ffn_conv_w"], small["ffn_conv_b"])

    loss, dx4, d_final_g = _final_loss(x4, small["final_g"], target)

    dx3, dmod3, g_ffn1 = _ffn_backward(dx4, ffn1, mods[3], W, 1)

    shift, scale, gate = mods[2]
    dy, dgate = _gate_bwd(dx3, y2, gate, name="mla_gate_bwd")
    d_mla_o = _mm(o1, dy, ta=True, name="mla_dw_out")
    do = _mm(dy, W["mla_o"], tb=True, out_dtype=BF16, name="mla_do")
    delta = _attn_delta(do, o1, name="mla_attn_delta")
    dq, dk, dv = _attn_bwd(q_cat, 0, kv, 0, kv, HEADS, do, lse1.reshape(HEADS, 1, S),
                           delta, name="mla_attn_bwd")
    dq_raw = _rope_heads(dq, tables, True, MLA_SCALE, name="mla_rope_q_bwd")
    d_mla_uq = _mm(cqn, dq_raw, ta=True, name="mla_dw_uq")
    dcqn = _mm(dq_raw, W["mla_uq"], tb=True, name="mla_dcq")
    dkv = jnp.concatenate([dk.astype(BF16), dv], axis=1)
    d_mla_kv = _mm(ckv_in, dkv, ta=True, name="mla_dw_kv")
    dckv_in = _mm(dkv, W["mla_kv"], tb=True, name="mla_dckv")
    da, dgq, dgkv = _mla_mid_bwd(a, dcqn, dckv_in, gq, gkv, tables)
    d_mla_a = _mm(h2, da, ta=True, name="mla_dw_a")
    dh = _mm(da, W["mla_a"], tb=True, name="mla_dh")
    dx2, dscale, dshift = _adaln_bwd(x2, dh, dx3, scale, name="mla_adaln_bwd")
    dmod2 = jnp.concatenate([dshift, dscale, dgate], axis=1)[0]

    dx1, dmod1, g_ffn0 = _ffn_backward(dx2, ffn0, mods[1], W, 0)

    shift, scale, gate = mods[0]
    dy, dgate = _gate_bwd(dx1, y0, gate, name="fox_gate_bwd")
    d_fox_o = _mm(o0, dy, ta=True, name="fox_dw_out")
    do = _mm(dy, W["fox_o"], tb=True, out_dtype=BF16, name="fox_do")
    delta = _attn_delta(do, o0, name="fox_attn_delta")
    dq, dk, dv = _attn_bwd(q_fox, 0, k_fox, 0, qkv, 2 * HEADS, do, lse0.reshape(HEADS, 1, S),
                           delta, name="fox_attn_bwd")
    dcq = dq.reshape(S, HEADS, HEAD_PAD)[:, :, BIAS_LANE + 3]
    dck = dk.reshape(S, HEADS, HEAD_PAD)[:, :, BIAS_LANE]
    dcum = jnp.pad(dcq - dck, ((0, 0), (0, LANE - HEADS)))
    dfl, dbf = _fox_gate_bwd(dcum, fl, bf)
    dproj = jnp.concatenate([dq.astype(BF16), dk.astype(BF16), dv, dfl.astype(BF16)], axis=1)
    d_fox_all = _mm(h0, dproj, ta=True, name="fox_dw_in")
    dh = _mm(dproj, W["fox_all"], tb=True, name="fox_dh")
    dx0, dscale, dshift = _adaln_bwd(x, dh, dx1, scale, name="fox_adaln_bwd")
    dmod0 = jnp.concatenate([dshift, dscale, dgate], axis=1)[0]

    G = {}
    d_qkv = [_unpad_head_cols(d_fox_all[:, p * wide:(p + 1) * wide], FOX_HEAD) for p in range(3)]
    d_qkv[0] = d_qkv[0] * FOX_SCALE
    G["fox_w_in"] = jnp.concatenate(d_qkv + [d_fox_all[:, 3 * wide:3 * wide + HEADS]], axis=1)[None]
    G["fox_b_f"] = dbf[:, :HEADS]
    G["fox_w_o"] = _unpad_head_rows(d_fox_o, FOX_HEAD)[None]
    lat = MLA_QR + MLA_KVR
    G["mla_w_a"] = jnp.concatenate([d_mla_a[:, :lat], d_mla_a[:, lat + MLA_NOPE:lat + MLA_NOPE + MLA_ROPE]],
                                   axis=1)[None]
    G["mla_g_q"] = dgq
    G["mla_g_kv"] = dgkv
    G["mla_w_uq"] = _unpad_head_cols(d_mla_uq, MLA_NOPE + MLA_ROPE)[None]
    dkn = d_mla_kv[:MLA_KVR, :wide].reshape(MLA_KVR, HEADS, HEAD_PAD)[:, :, :MLA_NOPE]
    dvv = d_mla_kv[:MLA_KVR, wide:].reshape(MLA_KVR, HEADS, HEAD_PAD)[:, :, :MLA_V]
    G["mla_w_ukv"] = jnp.concatenate([dkn, dvv], axis=2).reshape(MLA_KVR, -1)[None]
    G["mla_w_o"] = _unpad_head_rows(d_mla_o, MLA_V)[None]
    for name in ("ffn_w_in", "ffn_w_out", "ffn_conv_w", "ffn_conv_b"):
        G[name] = jnp.stack([g_ffn0[name], g_ffn1[name]])
    G["final_g"] = d_final_g[0]
    dmods = jnp.stack([dmod0, dmod1, dmod2, dmod3])
    return loss, dx0, dmods, G


PACKED = [("fox_w_in", 2), ("fox_w_o", 1), ("mla_w_a", 1), ("mla_w_uq", 2), ("mla_w_ukv", 2), ("mla_w_o", 1),
          ("ffn_w_in", 2), ("ffn_w_out", 1), ("fox_b_f", None), ("mla_g_q", 1), ("mla_g_kv", 1),
          ("ffn_conv_w", 2), ("ffn_conv_b", None), ("final_g", None), ("loss", None)]


def _shard_of(g, axis, s):
    if axis is None:
        return g
    n = g.shape[axis] // 4
    return lax.slice_in_dim(g, s * n, (s + 1) * n, axis=axis)


def _pack(G):
    rows = []
    for s in range(4):
        rows.append(jnp.concatenate([_shard_of(G[name], axis, s).reshape(-1) for name, axis in PACKED]))
    L = rows[0].shape[0]
    unit = 2 * PACK_ROWS_ALIGN * PACK_ROW
    Lp = -(-L // unit) * unit
    packed = jnp.pad(jnp.stack(rows), ((0, 0), (0, Lp - L)))
    return packed.reshape(4, 2, Lp // (2 * PACK_ROW), PACK_ROW)


def _unpack(flat, G_like):
    out, off = {}, 0
    for name, axis in PACKED:
        shape = _shard_of(G_like[name], axis, 0).shape
        size = math.prod(shape)
        out[name] = flat[off:off + size].reshape(shape)
        off += size
    return out


WEIGHTS = ['ada_w', 'ada_b', 'fox_w_in', 'fox_b_f', 'fox_w_o', 'mla_w_a', 'mla_g_q', 'mla_g_kv', 'mla_w_uq',
           'mla_w_ukv', 'mla_w_o', 'ffn_w_in', 'ffn_conv_w', 'ffn_conv_b', 'ffn_w_out', 'final_g']
BIG = ['fox_w_in', 'fox_w_o', 'mla_w_a', 'mla_w_uq', 'mla_w_ukv', 'mla_w_o', 'ffn_w_in', 'ffn_w_out']
SMALL = ['ada_b', 'fox_b_f', 'mla_g_q', 'mla_g_kv', 'ffn_conv_w', 'ffn_conv_b', 'final_g']


def _as2d(a):
    return a.reshape(-1, a.shape[-1])


def kernel(x, c, ada_w, ada_b, fox_w_in, fox_b_f, fox_w_o, mla_w_a, mla_g_q, mla_g_kv, mla_w_uq, mla_w_ukv, mla_w_o, ffn_w_in, ffn_conv_w, ffn_conv_b, ffn_w_out, final_g, loss_target, m_ada_w, m_ada_b, m_fox_w_in, m_fox_b_f, m_fox_w_o, m_mla_w_a, m_mla_g_q, m_mla_g_kv, m_mla_w_uq, m_mla_w_ukv, m_mla_w_o, m_ffn_w_in, m_ffn_conv_w, m_ffn_conv_b, m_ffn_w_out, m_final_g, v_ada_w, v_ada_b, v_fox_w_in, v_fox_b_f, v_fox_w_o, v_mla_w_a, v_mla_g_q, v_mla_g_kv, v_mla_w_uq, v_mla_w_ukv, v_mla_w_o, v_ffn_w_in, v_ffn_conv_w, v_ffn_conv_b, v_ffn_w_out, v_final_g):
    w = dict(ada_w=ada_w, ada_b=ada_b, fox_w_in=fox_w_in, fox_b_f=fox_b_f, fox_w_o=fox_w_o, mla_w_a=mla_w_a,
             mla_g_q=mla_g_q, mla_g_kv=mla_g_kv, mla_w_uq=mla_w_uq, mla_w_ukv=mla_w_ukv, mla_w_o=mla_w_o,
             ffn_w_in=ffn_w_in, ffn_conv_w=ffn_conv_w, ffn_conv_b=ffn_conv_b, ffn_w_out=ffn_w_out, final_g=final_g)
    m = dict(ada_w=m_ada_w, ada_b=m_ada_b, fox_w_in=m_fox_w_in, fox_b_f=m_fox_b_f, fox_w_o=m_fox_w_o,
             mla_w_a=m_mla_w_a, mla_g_q=m_mla_g_q, mla_g_kv=m_mla_g_kv, mla_w_uq=m_mla_w_uq,
             mla_w_ukv=m_mla_w_ukv, mla_w_o=m_mla_w_o, ffn_w_in=m_ffn_w_in, ffn_conv_w=m_ffn_conv_w,
             ffn_conv_b=m_ffn_conv_b, ffn_w_out=m_ffn_w_out, final_g=m_final_g)
    v = dict(ada_w=v_ada_w, ada_b=v_ada_b, fox_w_in=v_fox_w_in, fox_b_f=v_fox_b_f, fox_w_o=v_fox_w_o,
             mla_w_a=v_mla_w_a, mla_g_q=v_mla_g_q, mla_g_kv=v_mla_g_kv, mla_w_uq=v_mla_w_uq,
             mla_w_ukv=v_mla_w_ukv, mla_w_o=v_mla_w_o, ffn_w_in=v_ffn_w_in, ffn_conv_w=v_ffn_conv_w,
             ffn_conv_b=v_ffn_conv_b, ffn_w_out=v_ffn_w_out, final_g=v_final_g)
    D = D_MODEL
    xi, yi, ci = lax.axis_index("x"), lax.axis_index("y"), lax.axis_index("c")
    dev = 4 * xi + 2 * yi + ci

    n_ada = ada_w.shape[-1]
    small_parts = [c.reshape(-1), ffn_conv_w.reshape(-1), mla_g_q.reshape(-1), mla_g_kv.reshape(-1)]
    sizes = [p.shape[0] for p in small_parts]
    flat = jnp.concatenate(small_parts)
    flat = jnp.pad(flat, (0, -flat.shape[0] % LANE))[None]
    got = _all_gather8(flat, name="gather_cond")[:, 0]
    offs = np.cumsum([0] + sizes)
    c_all = got[:, offs[0]:offs[1]]
    chips = got[0::2]
    conv_w_full = jnp.concatenate(
        [chips[s, offs[1]:offs[2]].reshape(ffn_conv_w.shape) for s in range(4)], axis=2)
    gq_full = jnp.concatenate([chips[s, offs[2]:offs[3]] for s in range(4)])[None]
    gkv_full = jnp.concatenate([chips[s, offs[3]:offs[4]] for s in range(4)])[None]

    mod_shard = _ada_mod(c_all, ada_w.reshape(4, D, n_ada), ada_b.reshape(4, 1, n_ada))
    mod_all = _all_gather8(mod_shard.reshape(4 * 8, n_ada), name="gather_mod")
    mod_all = mod_all[0::2].reshape(4, 4, 8, n_ada)
    mine = lax.dynamic_index_in_dim(mod_all, dev, axis=2, keepdims=False)
    mod_rows = jnp.transpose(mine, (1, 0, 2)).reshape(4, 4 * n_ada)
    mods = [(mod_rows[k:k + 1, :D], mod_rows[k:k + 1, D:2 * D], mod_rows[k:k + 1, 2 * D:]) for k in range(4)]

    gathered = _all_gather_chips([w[name].astype(BF16) for name in BIG], name="gather_weights")
    W = _prepare_weights(gathered)
    small = dict(fox_b_f=fox_b_f, mla_g_q=gq_full, mla_g_kv=gkv_full, ffn_conv_w=conv_w_full,
                 ffn_conv_b=ffn_conv_b, final_g=final_g[None])

    loss, dx, dmods, G = _local_step(x[0], loss_target[0], mods, W, small)
    G["loss"] = loss[0, :1]

    dmod_all = _all_gather8(dmods, name="gather_dmod")
    chip = 2 * xi + yi
    dmod_cols = lax.dynamic_slice_in_dim(dmod_all, chip * n_ada, n_ada, axis=2)
    g_ada_w, g_ada_b = _ada_grad(c_all.T, jnp.transpose(dmod_cols, (1, 0, 2)))
    grads = dict(ada_w=g_ada_w.reshape(ada_w.shape), ada_b=g_ada_b.reshape(ada_b.shape))

    packed = _pack(G)
    recv = _exchange_halves(packed)
    part = _add_halves(packed, recv, ci.reshape(1).astype(jnp.int32))
    from_chips = _exchange_chips(part)
    half = _add_chips(from_chips)
    both = _share_halves(half)
    grads.update(_unpack(both.reshape(-1), G))
    loss_total = grads.pop("loss")[0]

    delta, new_m, new_v = {}, {}, {}
    for name in BIG + ["ada_w"]:
        shape = w[name].shape
        d_, m_, v_ = _adamw(_as2d(w[name]), _as2d(grads[name]), _as2d(m[name]), _as2d(v[name]), name=f"adamw_{name}")
        delta[name], new_m[name], new_v[name] = d_.reshape(shape), m_.reshape(shape), v_.reshape(shape)
        grads[name] = grads[name].reshape(shape)
    sizes = [math.prod(w[name].shape) for name in SMALL]
    total = sum(sizes)
    rows = -(-total // LANE)
    rows += -rows % 8

    def pack_small(d):
        flat = jnp.concatenate([d[name].reshape(-1) for name in SMALL])
        return jnp.pad(flat, (0, rows * LANE - total)).reshape(rows, LANE)

    outs = _adamw(pack_small(w), pack_small(grads), pack_small(m), pack_small(v), name="adamw_small")
    off = 0
    for name, size in zip(SMALL, sizes):
        shape = w[name].shape
        for dst, src in zip((delta, new_m, new_v), outs):
            dst[name] = src.reshape(-1)[off:off + size].reshape(shape)
        grads[name] = grads[name].reshape(shape)
        off += size

    return (loss_total, dx[None], *[grads[n] for n in WEIGHTS], *[delta[n] for n in WEIGHTS],
            *[new_m[n] for n in WEIGHTS], *[new_v[n] for n in WEIGHTS])
```

```python
import functools
import math

import numpy as np
import jax
import jax.numpy as jnp
from jax import lax
from jax.experimental import pallas as pl
from jax.experimental.pallas import tpu as pltpu

F32 = jnp.float32
BF16 = jnp.bfloat16
MESH = pl.DeviceIdType.MESH

D_MODEL = 1024
HEADS = 16
HEAD_PAD = 128
FOX_HEAD = 64
MLA_NOPE = 64
MLA_ROPE = 32
MLA_V = 64
MLA_QR = 384
MLA_KVR = 256
D_FF = 2816
NORM_EPS = 1e-6
ROPE_BASE = 10000.0
ADAM_LR = 0.001
ADAM_B1 = 0.9
ADAM_B2 = 0.999
ADAM_EPS = 1e-08
ADAM_WD = 0.01
ADAM_STEP = 10
FOX_SCALE = FOX_HEAD ** -0.5
MLA_SCALE = (MLA_NOPE + MLA_ROPE) ** -0.5
NEG_BIG = -1e30
VMEM_LIMIT = 56 * 1024 * 1024
LANE = 128
PACK_ROW = 1024
PACK_ROWS_ALIGN = 256

NT = (((1,), (1,)), ((), ()))
TN = (((0,), (0,)), ((), ()))
NN = (((1,), (0,)), ((), ()))


def _cparams():
    return pltpu.CompilerParams(vmem_limit_bytes=VMEM_LIMIT)


def _tile(n, cap):
    if n <= cap:
        return n
    for t in range(cap - cap % LANE, 0, -LANE):
        if n % t == 0:
            return t
    raise ValueError((n, cap))


def _row_tile(rows, cols, limit_bytes=1 << 20):
    best = None
    for t in range(8, rows + 1, 8):
        if rows % t == 0 and t * cols * 4 <= limit_bytes:
            best = t
    return best if best is not None else rows


def _me():
    return lax.axis_index("x"), lax.axis_index("y"), lax.axis_index("c")


def _all_gather8(v, name):
    R, N = v.shape

    def body(v_ref, o_ref, ssem, rsem):
        x, y, c = _me()
        me = 4 * x + 2 * y + c
        o_ref[me] = v_ref[...]
        copies = []
        for k in range(1, 8):
            peer = (1 - x if k & 4 else x, 1 - y if k & 2 else y, 1 - c if k & 1 else c)
            cp = pltpu.make_async_remote_copy(
                src_ref=v_ref, dst_ref=o_ref.at[me], send_sem=ssem.at[k - 1], recv_sem=rsem.at[k - 1],
                device_id=peer, device_id_type=MESH)
            cp.start()
            copies.append(cp)
        for cp in copies:
            cp.wait()

    return pl.pallas_call(
        body, name=name,
        out_shape=jax.ShapeDtypeStruct((8, R, N), v.dtype),
        in_specs=[pl.BlockSpec(memory_space=pltpu.VMEM)],
        out_specs=pl.BlockSpec(memory_space=pltpu.VMEM),
        scratch_shapes=[pltpu.SemaphoreType.DMA((7,)), pltpu.SemaphoreType.DMA((7,))],
    )(v)


def _all_gather_chips(shards, name):
    n = len(shards)

    def body(*refs):
        copies = _chip_gather_copies(refs[:n], refs[n:2 * n], *refs[2 * n:])
        for cp in copies:
            cp.start()
        for cp in copies:
            cp.wait()

    return pl.pallas_call(
        body, name=name,
        out_shape=[jax.ShapeDtypeStruct((4,) + s.shape, s.dtype) for s in shards],
        in_specs=[pl.BlockSpec(memory_space=pl.ANY)] * n,
        out_specs=[pl.BlockSpec(memory_space=pl.ANY)] * n,
        scratch_shapes=[pltpu.SemaphoreType.DMA((3 * n,)), pltpu.SemaphoreType.DMA((3 * n,)),
                        pltpu.SemaphoreType.DMA((n,))],
    )(*shards)


def _exchange_halves(g):
    _, _, R, P = g.shape

    def body(g_ref, o_ref, ssem, rsem):
        x, y, c = _me()
        copies = []
        for s in range(4):
            cp = pltpu.make_async_remote_copy(
                src_ref=g_ref.at[s, 1 - c], dst_ref=o_ref.at[s], send_sem=ssem.at[s], recv_sem=rsem.at[s],
                device_id=(x, y, 1 - c), device_id_type=MESH)
            cp.start()
            copies.append(cp)
        for cp in copies:
            cp.wait()

    return pl.pallas_call(
        body, name="grad_exchange_sibling",
        out_shape=jax.ShapeDtypeStruct((4, R, P), g.dtype),
        in_specs=[pl.BlockSpec(memory_space=pl.ANY)],
        out_specs=pl.BlockSpec(memory_space=pl.ANY),
        scratch_shapes=[pltpu.SemaphoreType.DMA((4,)), pltpu.SemaphoreType.DMA((4,))],
    )(g)


def _exchange_chips(a):
    _, R, P = a.shape

    def body(a_ref, o_ref, ssem, rsem, lsem):
        x, y, c = _me()
        me = 2 * x + y
        own = pltpu.make_async_copy(a_ref.at[me], o_ref.at[me], lsem)
        own.start()
        copies = []
        for k in (1, 2, 3):
            px, py = (1 - x if k & 2 else x), (1 - y if k & 1 else y)
            cp = pltpu.make_async_remote_copy(
                src_ref=a_ref.at[2 * px + py], dst_ref=o_ref.at[me], send_sem=ssem.at[k - 1],
                recv_sem=rsem.at[k - 1], device_id=(px, py, c), device_id_type=MESH)
            cp.start()
            copies.append(cp)
        own.wait()
        for cp in copies:
            cp.wait()

    return pl.pallas_call(
        body, name="grad_exchange_chips",
        out_shape=jax.ShapeDtypeStruct((4, R, P), a.dtype),
        in_specs=[pl.BlockSpec(memory_space=pl.ANY)],
        out_specs=pl.BlockSpec(memory_space=pl.ANY),
        scratch_shapes=[pltpu.SemaphoreType.DMA((3,)), pltpu.SemaphoreType.DMA((3,)), pltpu.SemaphoreType.DMA],
    )(a)


def _share_halves(f):
    R, P = f.shape

    def body(f_ref, o_ref, ssem, rsem, lsem):
        x, y, c = _me()
        own = pltpu.make_async_copy(f_ref, o_ref.at[c], lsem)
        own.start()
        cp = pltpu.make_async_remote_copy(
            src_ref=f_ref, dst_ref=o_ref.at[c], send_sem=ssem, recv_sem=rsem,
            device_id=(x, y, 1 - c), device_id_type=MESH)
        cp.start()
        own.wait()
        cp.wait()

    return pl.pallas_call(
        body, name="grad_share_sibling",
        out_shape=jax.ShapeDtypeStruct((2, R, P), f.dtype),
        in_specs=[pl.BlockSpec(memory_space=pl.ANY)],
        out_specs=pl.BlockSpec(memory_space=pl.ANY),
        scratch_shapes=[pltpu.SemaphoreType.DMA, pltpu.SemaphoreType.DMA, pltpu.SemaphoreType.DMA],
    )(f)


def _add_halves(g, r, c_idx):
    _, _, R, P = g.shape
    t = PACK_ROWS_ALIGN

    def body(c_ref, g_ref, r_ref, o_ref):
        o_ref[...] = (g_ref[0] + r_ref[...]).astype(BF16)

    return pl.pallas_call(
        body, name="grad_add_halves",
        out_shape=jax.ShapeDtypeStruct((4, R, P), BF16),
        grid_spec=pltpu.PrefetchScalarGridSpec(
            num_scalar_prefetch=1, grid=(4, R // t),
            in_specs=[pl.BlockSpec((1, 1, t, P), lambda s, i, c_ref: (s, c_ref[0], i, 0)),
                      pl.BlockSpec((1, t, P), lambda s, i, c_ref: (s, i, 0))],
            out_specs=pl.BlockSpec((1, t, P), lambda s, i, c_ref: (s, i, 0))),
        compiler_params=_cparams(),
    )(c_idx, g, r)


def _add_chips(b):
    _, R, P = b.shape
    t = PACK_ROWS_ALIGN

    def body(b_ref, o_ref):
        b0, b1, b2, b3 = (b_ref[k].astype(F32) for k in range(4))
        o_ref[...] = ((b0 + b1) + b2) + b3

    return pl.pallas_call(
        body, name="grad_add_chips",
        out_shape=jax.ShapeDtypeStruct((R, P), F32),
        grid=(R // t,),
        in_specs=[pl.BlockSpec((4, t, P), lambda i: (0, i, 0))],
        out_specs=pl.BlockSpec((t, P), lambda i: (i, 0)),
        compiler_params=_cparams(),
    )(b)


def _mm(a, b, *, ta=False, tb=False, out_dtype=F32, res=None, name):
    if ta:
        K, M = a.shape
    else:
        M, K = a.shape
    N = b.shape[0] if tb else b.shape[1]
    assert (b.shape[1] if tb else b.shape[0]) == K, (a.shape, b.shape, ta, tb)
    tm, tn, tk = _tile(M, 512), _tile(N, 1408), _tile(K, 1408)
    nk = K // tk
    has_res = res is not None
    dims = (((0 if ta else 1,), (1 if tb else 0,)), ((), ()))

    def body(*refs):
        if has_res:
            a_ref, b_ref, r_ref, o_ref, acc_ref = refs
        else:
            a_ref, b_ref, o_ref, acc_ref = refs
        k = pl.program_id(2)

        @pl.when(k == 0)
        def _():
            acc_ref[...] = jnp.zeros_like(acc_ref)

        acc_ref[...] += lax.dot_general(a_ref[...].astype(BF16), b_ref[...].astype(BF16), dims,
                                        preferred_element_type=F32)

        @pl.when(k == nk - 1)
        def _():
            r = acc_ref[...]
            if has_res:
                r = r + r_ref[...]
            o_ref[...] = r.astype(out_dtype)

    a_spec = (pl.BlockSpec((tk, tm), lambda i, j, k: (k, i)) if ta
              else pl.BlockSpec((tm, tk), lambda i, j, k: (i, k)))
    b_spec = (pl.BlockSpec((tn, tk), lambda i, j, k: (j, k)) if tb
              else pl.BlockSpec((tk, tn), lambda i, j, k: (k, j)))
    in_specs = [a_spec, b_spec]
    args = [a, b]
    if has_res:
        in_specs.append(pl.BlockSpec((tm, tn), lambda i, j, k: (i, j)))
        args.append(res)
    return pl.pallas_call(
        body, name=name,
        out_shape=jax.ShapeDtypeStruct((M, N), out_dtype),
        grid=(M // tm, N // tn, nk),
        in_specs=in_specs,
        out_specs=pl.BlockSpec((tm, tn), lambda i, j, k: (i, j)),
        scratch_shapes=[pltpu.VMEM((tm, tn), F32)],
        compiler_params=_cparams(),
    )(*args)


def _rstd(xv):
    return lax.rsqrt(jnp.mean(xv * xv, axis=-1, keepdims=True) + NORM_EPS)


def _norm_bwd(n, r, dn):
    return r * (dn - n * jnp.mean(dn * n, axis=-1, keepdims=True))


def _adaln_fwd(x, scale, shift, name):
    S, D = x.shape
    t = _tile(S, 512)

    def body(x_ref, sc_ref, sh_ref, h_ref):
        xv = x_ref[...]
        h_ref[...] = (xv * _rstd(xv) * (1.0 + sc_ref[...]) + sh_ref[...]).astype(BF16)

    row = pl.BlockSpec((t, D), lambda i: (i, 0))
    vec = pl.BlockSpec((1, D), lambda i: (0, 0))
    return pl.pallas_call(
        body, name=name, out_shape=jax.ShapeDtypeStruct((S, D), BF16), grid=(S // t,),
        in_specs=[row, vec, vec], out_specs=row, compiler_params=_cparams(),
    )(x, scale, shift)


def _adaln_bwd(x, dh, dx_next, scale, name):
    S, D = x.shape
    t = _tile(S, 512)

    def body(x_ref, dh_ref, dxn_ref, sc_ref, dx_ref, dsc_ref, dsh_ref):
        @pl.when(pl.program_id(0) == 0)
        def _():
            dsc_ref[...] = jnp.zeros_like(dsc_ref)
            dsh_ref[...] = jnp.zeros_like(dsh_ref)

        xv = x_ref[...]
        r = _rstd(xv)
        n = xv * r
        dh = dh_ref[...]
        dx_ref[...] = dxn_ref[...] + _norm_bwd(n, r, dh * (1.0 + sc_ref[...]))
        dsc_ref[...] += jnp.sum(dh * n, axis=0, keepdims=True)
        dsh_ref[...] += jnp.sum(dh, axis=0, keepdims=True)

    row = pl.BlockSpec((t, D), lambda i: (i, 0))
    vec = pl.BlockSpec((1, D), lambda i: (0, 0))
    return pl.pallas_call(
        body, name=name,
        out_shape=[jax.ShapeDtypeStruct((S, D), F32), jax.ShapeDtypeStruct((1, D), F32),
                   jax.ShapeDtypeStruct((1, D), F32)],
        grid=(S // t,), in_specs=[row, row, row, vec], out_specs=[row, vec, vec],
        compiler_params=_cparams(),
    )(x, dh, dx_next, scale)


def _residual(x, gate, y, name):
    S, D = x.shape
    t = _tile(S, 512)

    def body(x_ref, g_ref, y_ref, o_ref):
        o_ref[...] = x_ref[...] + g_ref[...] * y_ref[...]

    row = pl.BlockSpec((t, D), lambda i: (i, 0))
    vec = pl.BlockSpec((1, D), lambda i: (0, 0))
    return pl.pallas_call(
        body, name=name, out_shape=jax.ShapeDtypeStruct((S, D), F32), grid=(S // t,),
        in_specs=[row, vec, row], out_specs=row, compiler_params=_cparams(),
    )(x, gate, y)


def _gate_bwd(dx, y, gate, name):
    S, D = dx.shape
    t = _tile(S, 512)

    def body(dx_ref, y_ref, g_ref, dy_ref, dg_ref):
        @pl.when(pl.program_id(0) == 0)
        def _():
            dg_ref[...] = jnp.zeros_like(dg_ref)

        dxv = dx_ref[...]
        dy_ref[...] = (dxv * g_ref[...]).astype(BF16)
        dg_ref[...] += jnp.sum(dxv * y_ref[...], axis=0, keepdims=True)

    row = pl.BlockSpec((t, D), lambda i: (i, 0))
    vec = pl.BlockSpec((1, D), lambda i: (0, 0))
    return pl.pallas_call(
        body, name=name,
        out_shape=[jax.ShapeDtypeStruct((S, D), BF16), jax.ShapeDtypeStruct((1, D), F32)],
        grid=(S // t,), in_specs=[row, row, vec], out_specs=[row, vec], compiler_params=_cparams(),
    )(dx, y, gate)


def _final_loss(x, g, target):
    S, D = x.shape
    t = _tile(S, 512)

    def body(x_ref, g_ref, t_ref, loss_ref, dx_ref, dg_ref):
        @pl.when(pl.program_id(0) == 0)
        def _():
            loss_ref[...] = jnp.zeros_like(loss_ref)
            dg_ref[...] = jnp.zeros_like(dg_ref)

        xv = x_ref[...]
        gv = g_ref[...]
        r = _rstd(xv)
        n = xv * r
        err = n * gv - t_ref[...]
        part = jnp.sum(jnp.mean(err * err, axis=-1, keepdims=True), axis=0, keepdims=True)
        loss_ref[...] += jnp.broadcast_to(0.5 * part, loss_ref.shape)
        dy = err * (1.0 / D)
        dg_ref[...] += jnp.sum(dy * n, axis=0, keepdims=True)
        dx_ref[...] = _norm_bwd(n, r, dy * gv)

    row = pl.BlockSpec((t, D), lambda i: (i, 0))
    vec = pl.BlockSpec((1, D), lambda i: (0, 0))
    one = pl.BlockSpec((1, LANE), lambda i: (0, 0))
    return pl.pallas_call(
        body, name="final_loss",
        out_shape=[jax.ShapeDtypeStruct((1, LANE), F32), jax.ShapeDtypeStruct((S, D), F32),
                   jax.ShapeDtypeStruct((1, D), F32)],
        grid=(S // t,), in_specs=[row, vec, row], out_specs=[one, row, vec], compiler_params=_cparams(),
    )(x, g, target)


ATTN_TILE = 512
ATTN_ROWS = 16
ONES_LANE = 127


def _causal_pairs(n, k_major):
    if k_major:
        pairs = [(i, j) for j in range(n) for i in range(j, n)]
    else:
        pairs = [(i, j) for i in range(n) for j in range(i + 1)]
    return (jnp.asarray(np.array([p[0] for p in pairs], np.int32)),
            jnp.asarray(np.array([p[1] for p in pairs], np.int32)))


def _chip_gather_copies(ins, outs, ssem, rsem, lsem):
    x, y, c = _me()
    me = 2 * x + y
    copies = []
    for i, (src, dst) in enumerate(zip(ins, outs)):
        copies.append(pltpu.make_async_copy(src, dst.at[me], lsem.at[i]))
        for k in (1, 2, 3):
            peer = (1 - x if k & 2 else x, 1 - y if k & 1 else y, c)
            copies.append(pltpu.make_async_remote_copy(
                src_ref=src, dst_ref=dst.at[me], send_sem=ssem.at[3 * i + k - 1],
                recv_sem=rsem.at[3 * i + k - 1], device_id=peer, device_id_type=MESH))
    return copies


def _attn_fwd(q_arr, q_off, k_arr, k_off, v_arr, v_off, name, gather=()):
    S = q_arr.shape[0]
    t = _tile(S, ATTN_TILE)
    n = S // t
    rc = min(ATTN_ROWS, t)
    ii, jj = _causal_pairs(n, k_major=False)
    ng = len(gather)
    last_step = ii.shape[0] - 1

    def body(ii_ref, jj_ref, q_ref, k_ref, v_ref, *rest):
        g_in, (o_ref, lse_ref), g_out = rest[:ng], rest[ng:ng + 2], rest[ng + 2:2 * ng + 2]
        s_s, m_s, a_s, acc_s = rest[2 * ng + 2:2 * ng + 6]
        step = pl.program_id(1)
        i, j = ii_ref[step], jj_ref[step]
        lane = lax.broadcasted_iota(jnp.int32, (t, HEAD_PAD), 1)
        if ng:
            copies = _chip_gather_copies(g_in, g_out, *rest[2 * ng + 6:])

            @pl.when(jnp.logical_and(pl.program_id(0) == 0, step == 0))
            def _():
                for cp in copies:
                    cp.start()

            @pl.when(jnp.logical_and(pl.program_id(0) == HEADS - 1, step == last_step))
            def _():
                for cp in copies:
                    cp.wait()

        @pl.when(j == 0)
        def _():
            m_s[...] = jnp.full_like(m_s, NEG_BIG)
            acc_s[...] = jnp.zeros_like(acc_s)

        def tile(diagonal):
            s_s[...] = lax.dot_general(q_ref[...], k_ref[...], NT, preferred_element_type=F32)
            for r in range(t // rc):
                rows = pl.ds(r * rc, rc)
                sc = s_s[rows, :]
                if diagonal:
                    qpos = r * rc + lax.broadcasted_iota(jnp.int32, (rc, t), 0)
                    kpos = lax.broadcasted_iota(jnp.int32, (rc, t), 1)
                    sc = jnp.where(qpos >= kpos, sc, NEG_BIG)
                m_old = m_s[rows, :]
                m_new = jnp.maximum(m_old, jnp.max(sc, axis=1, keepdims=True))
                s_s[rows, :] = jnp.exp(sc - m_new)
                a_s[rows, :] = jnp.exp(m_old - m_new)
                m_s[rows, :] = m_new
            v_ones = jnp.where(lane == ONES_LANE, 1.0, v_ref[...].astype(F32)).astype(BF16)
            acc_s[...] = a_s[...] * acc_s[...] + lax.dot_general(s_s[...].astype(BF16), v_ones, NN,
                                                                 preferred_element_type=F32)

        @pl.when(j < i)
        def _():
            tile(False)

        @pl.when(j == i)
        def _():
            tile(True)
            acc = acc_s[...]
            l = jnp.sum(jnp.where(lane == ONES_LANE, acc, 0.0), axis=1, keepdims=True)
            o_ref[...] = jnp.where(lane == ONES_LANE, 0.0, acc / l).astype(BF16)
            lse_ref[0] = m_s[...] + jnp.log(l)

    hbm = pl.BlockSpec(memory_space=pl.ANY)
    sems = [pltpu.SemaphoreType.DMA((3 * ng,)), pltpu.SemaphoreType.DMA((3 * ng,)),
            pltpu.SemaphoreType.DMA((ng,))] if ng else []
    grid_spec = pltpu.PrefetchScalarGridSpec(
        num_scalar_prefetch=2, grid=(HEADS, ii.shape[0]),
        in_specs=[pl.BlockSpec((t, HEAD_PAD), lambda h, s, ii, jj: (ii[s], q_off + h)),
                  pl.BlockSpec((t, HEAD_PAD), lambda h, s, ii, jj: (jj[s], k_off + h)),
                  pl.BlockSpec((t, HEAD_PAD), lambda h, s, ii, jj: (jj[s], v_off + h))] + [hbm] * ng,
        out_specs=[pl.BlockSpec((t, HEAD_PAD), lambda h, s, ii, jj: (ii[s], h)),
                   pl.BlockSpec((1, t, 1), lambda h, s, ii, jj: (h, ii[s], 0))] + [hbm] * ng,
        scratch_shapes=[pltpu.VMEM((t, t), F32), pltpu.VMEM((t, 1), F32), pltpu.VMEM((t, 1), F32),
                        pltpu.VMEM((t, HEAD_PAD), F32)] + sems)
    outs = pl.pallas_call(
        body, name=name, grid_spec=grid_spec,
        out_shape=[jax.ShapeDtypeStruct((S, HEADS * HEAD_PAD), BF16), jax.ShapeDtypeStruct((HEADS, S, 1), F32)]
        + [jax.ShapeDtypeStruct((4,) + g.shape, g.dtype) for g in gather],
        compiler_params=_cparams(),
    )(ii, jj, q_arr, k_arr, v_arr, *gather)
    return outs[0], outs[1], list(outs[2:])


def _attn_delta(do, o, name):
    S = do.shape[0]
    t = _tile(S, 512)

    def body(do_ref, o_ref, d_ref):
        lane = lax.broadcasted_iota(jnp.int32, (t, LANE), 1)
        acc = jnp.zeros((t, LANE), F32)
        for h in range(HEADS):
            cols = pl.ds(h * HEAD_PAD, HEAD_PAD)
            d = jnp.sum(do_ref[:, cols].astype(F32) * o_ref[:, cols].astype(F32), axis=1, keepdims=True)
            acc = jnp.where(lane == h, d, acc)
        d_ref[...] = acc

    blk = pl.BlockSpec((t, HEADS * HEAD_PAD), lambda i: (i, 0))
    by_lane = pl.pallas_call(
        body, name=name, out_shape=jax.ShapeDtypeStruct((S, LANE), F32), grid=(S // t,),
        in_specs=[blk, blk], out_specs=pl.BlockSpec((t, LANE), lambda i: (i, 0)),
        compiler_params=_cparams(),
    )(do, o)
    return by_lane[:, :HEADS].T.reshape(HEADS, 1, S)


def _attn_bwd(q_arr, q_off, k_arr, k_off, v_arr, v_off, do, lse, delta, name):
    S = q_arr.shape[0]
    t = _tile(S, ATTN_TILE)
    n = S // t
    rc = min(ATTN_ROWS, t)
    ii, jj = _causal_pairs(n, k_major=True)

    def body(ii_ref, jj_ref, q_ref, k_ref, v_ref, do_ref, lse_ref, dl_ref, dq_ref, dk_ref, dv_ref,
             s_s, dp_s, dk_s, dv_s):
        step = pl.program_id(1)
        i, j = ii_ref[step], jj_ref[step]

        @pl.when(step == 0)
        def _():
            dq_ref[...] = jnp.zeros_like(dq_ref)

        @pl.when(i == j)
        def _():
            dk_s[...] = jnp.zeros_like(dk_s)
            dv_s[...] = jnp.zeros_like(dv_s)

        def tile(diagonal):
            q, k, dov = q_ref[...], k_ref[...], do_ref[...]
            s_s[...] = lax.dot_general(k, q, NT, preferred_element_type=F32)
            dp_s[...] = lax.dot_general(v_ref[...], dov, NT, preferred_element_type=F32)
            lse_row, dl_row = lse_ref[0], dl_ref[0]
            for r in range(t // rc):
                rows = pl.ds(r * rc, rc)
                sc = s_s[rows, :]
                if diagonal:
                    kpos = r * rc + lax.broadcasted_iota(jnp.int32, (rc, t), 0)
                    qpos = lax.broadcasted_iota(jnp.int32, (rc, t), 1)
                    sc = jnp.where(qpos >= kpos, sc, NEG_BIG)
                p = jnp.exp(sc - lse_row)
                s_s[rows, :] = p
                dp_s[rows, :] = p * (dp_s[rows, :] - dl_row)
            ds = dp_s[...].astype(BF16)
            dv_s[...] += lax.dot_general(s_s[...].astype(BF16), dov, NN, preferred_element_type=F32)
            dk_s[...] += lax.dot_general(ds, q, NN, preferred_element_type=F32)
            rows = pl.ds(pl.multiple_of(i * t, t), t)
            dq_ref[rows, :] += lax.dot_general(ds, k, TN, preferred_element_type=F32)

        @pl.when(i > j)
        def _():
            tile(False)

        @pl.when(i == j)
        def _():
            tile(True)

        @pl.when(i == n - 1)
        def _():
            dk_ref[...] = dk_s[...]
            dv_ref[...] = dv_s[...].astype(BF16)

    qblk = lambda off: pl.BlockSpec((t, HEAD_PAD), lambda h, s, ii, jj: (ii[s], off + h))
    kblk = lambda off: pl.BlockSpec((t, HEAD_PAD), lambda h, s, ii, jj: (jj[s], off + h))
    stat = pl.BlockSpec((1, 1, t), lambda h, s, ii, jj: (h, 0, ii[s]))
    wide = (S, HEADS * HEAD_PAD)
    grid_spec = pltpu.PrefetchScalarGridSpec(
        num_scalar_prefetch=2, grid=(HEADS, ii.shape[0]),
        in_specs=[qblk(q_off), kblk(k_off), kblk(v_off), qblk(0), stat, stat],
        out_specs=[pl.BlockSpec((S, HEAD_PAD), lambda h, s, ii, jj: (0, h)), kblk(0), kblk(0)],
        scratch_shapes=[pltpu.VMEM((t, t), F32), pltpu.VMEM((t, t), F32), pltpu.VMEM((t, HEAD_PAD), F32),
                        pltpu.VMEM((t, HEAD_PAD), F32)])
    return pl.pallas_call(
        body, name=name, grid_spec=grid_spec,
        out_shape=[jax.ShapeDtypeStruct(wide, F32), jax.ShapeDtypeStruct(wide, F32),
                   jax.ShapeDtypeStruct(wide, BF16)],
        compiler_params=_cparams(),
    )(ii, jj, q_arr, k_arr, v_arr, do, lse, delta)


BIAS_LANE = FOX_HEAD


def _fox_bias_lanes(qkv, cum, name):
    S = qkv.shape[0]
    t = _tile(S, 256)
    wide = HEADS * HEAD_PAD

    def body(q_ref, k_ref, c_ref, qo_ref, ko_ref):
        lane = lax.broadcasted_iota(jnp.int32, (t, HEAD_PAD), 1)
        cum_all = c_ref[...]

        def fill(x, first, second):
            out = x.astype(F32)
            for n_, (a, b) in enumerate(zip(first, second)):
                out = jnp.where(lane == BIAS_LANE + n_, a, out)
                out = jnp.where(lane == BIAS_LANE + 3 + n_, b, out)
            return out.astype(BF16)

        ones = [1.0, 1.0, 1.0]
        for h in range(HEADS):
            cols = pl.ds(h * HEAD_PAD, HEAD_PAD)
            c = jnp.sum(jnp.where(lane == h, cum_all, 0.0), axis=1, keepdims=True)
            pieces = [x.astype(F32) for x in _split3(c)]
            qo_ref[:, cols] = fill(q_ref[:, cols], ones, pieces)
            ko_ref[:, cols] = fill(k_ref[:, cols], [-x for x in pieces], ones)

    blk = lambda off: pl.BlockSpec((t, wide), lambda i: (i, off))
    shp = jax.ShapeDtypeStruct((S, wide), BF16)
    return pl.pallas_call(
        body, name=name, out_shape=[shp, shp], grid=(S // t,),
        in_specs=[blk(0), blk(1), pl.BlockSpec((t, LANE), lambda i: (i, 0))],
        out_specs=[blk(0), blk(0)], compiler_params=_cparams(),
    )(qkv, qkv, cum)


def _split3(x):
    hi = x.astype(BF16)
    r1 = x - hi.astype(F32)
    mid = r1.astype(BF16)
    lo = (r1 - mid.astype(F32)).astype(BF16)
    return hi, mid, lo


def _tri_matmul(tri, x):
    acc = None
    for part in _split3(x):
        d = lax.dot_general(tri, part, NN, preferred_element_type=F32)
        acc = d if acc is None else acc + d
    return acc


def _log_sigmoid(z):
    return jnp.minimum(z, 0.0) - jnp.log1p(jnp.exp(-jnp.abs(z)))


def _fox_gate_fwd(fl, bf):
    S = fl.shape[0]
    t = _tile(S, 256)

    def body(fl_ref, bf_ref, cum_ref, carry):
        @pl.when(pl.program_id(0) == 0)
        def _():
            carry[...] = jnp.zeros_like(carry)

        logf = _log_sigmoid(fl_ref[...] + bf_ref[...])
        tri = (lax.broadcasted_iota(jnp.int32, (t, t), 0) >= lax.broadcasted_iota(jnp.int32, (t, t), 1)).astype(BF16)
        cum_ref[...] = _tri_matmul(tri, logf) + carry[...]
        carry[...] += jnp.sum(logf, axis=0, keepdims=True)

    row = pl.BlockSpec((t, LANE), lambda i: (i, 0))
    return pl.pallas_call(
        body, name="fox_gate_fwd", out_shape=jax.ShapeDtypeStruct((S, LANE), F32), grid=(S // t,),
        in_specs=[row, pl.BlockSpec((1, LANE), lambda i: (0, 0))], out_specs=row,
        scratch_shapes=[pltpu.VMEM((1, LANE), F32)], compiler_params=_cparams(),
    )(fl, bf)


def _fox_gate_bwd(dcum, fl, bf):
    S = fl.shape[0]
    t = _tile(S, 256)
    n = S // t

    def body(dc_ref, fl_ref, bf_ref, df_ref, db_ref, carry):
        @pl.when(pl.program_id(0) == 0)
        def _():
            carry[...] = jnp.zeros_like(carry)
            db_ref[...] = jnp.zeros_like(db_ref)

        dc = dc_ref[...]
        tri = (lax.broadcasted_iota(jnp.int32, (t, t), 1) >= lax.broadcasted_iota(jnp.int32, (t, t), 0)).astype(BF16)
        dlogf = _tri_matmul(tri, dc) + carry[...]
        carry[...] += jnp.sum(dc, axis=0, keepdims=True)
        z = fl_ref[...] + bf_ref[...]
        e = jnp.exp(-jnp.abs(z))
        sig_neg = jnp.where(z >= 0, e, 1.0) / (1.0 + e)
        lanes = lax.broadcasted_iota(jnp.int32, (t, LANE), 1)
        df = jnp.where(lanes < HEADS, dlogf * sig_neg, 0.0)
        df_ref[...] = df
        db_ref[...] += jnp.sum(df, axis=0, keepdims=True)

    row = pl.BlockSpec((t, LANE), lambda i: (n - 1 - i, 0))
    vec = pl.BlockSpec((1, LANE), lambda i: (0, 0))
    return pl.pallas_call(
        body, name="fox_gate_bwd",
        out_shape=[jax.ShapeDtypeStruct((S, LANE), F32), jax.ShapeDtypeStruct((1, LANE), F32)],
        grid=(n,), in_specs=[row, row, vec], out_specs=[row, vec],
        scratch_shapes=[pltpu.VMEM((1, LANE), F32)], compiler_params=_cparams(),
    )(dcum, fl, bf)


def _rope_tables(S):
    pos = jnp.arange(S, dtype=F32)
    inv_freq = ROPE_BASE ** (-jnp.arange(0, MLA_ROPE, 2, dtype=F32) / MLA_ROPE)
    ang = pos[:, None] * inv_freq[None, :]
    cos, sin = jnp.cos(ang), jnp.sin(ang)
    half = MLA_ROPE // 2
    ones = jnp.ones((S, MLA_NOPE), F32)
    z = lambda w: jnp.zeros((S, w), F32)
    keep = jnp.concatenate([ones, cos, cos, z(HEAD_PAD - MLA_NOPE - MLA_ROPE)], axis=1)
    from_above = jnp.concatenate([z(MLA_NOPE), -sin, z(HEAD_PAD - MLA_NOPE - half)], axis=1)
    from_below = jnp.concatenate([z(MLA_NOPE + half), sin, z(HEAD_PAD - MLA_NOPE - MLA_ROPE)], axis=1)
    return keep, from_above, from_below


def _rope(x, keep, up, down, transpose):
    half = MLA_ROPE // 2
    if transpose:
        return x * keep + pltpu.roll(x * up, half, 1) + pltpu.roll(x * down, HEAD_PAD - half, 1)
    return x * keep + pltpu.roll(x, HEAD_PAD - half, 1) * up + pltpu.roll(x, half, 1) * down


def _rope_heads(x, tables, transpose, scale, name):
    S = x.shape[0]
    t = _tile(S, 256)

    def body(x_ref, a_ref, b_ref, c_ref, o_ref):
        keep, up, down = a_ref[...], b_ref[...], c_ref[...]
        for h in range(HEADS):
            cols = pl.ds(h * HEAD_PAD, HEAD_PAD)
            o_ref[:, cols] = (_rope(x_ref[:, cols], keep, up, down, transpose) * scale).astype(BF16)

    blk = pl.BlockSpec((t, HEADS * HEAD_PAD), lambda i: (i, 0))
    tab = pl.BlockSpec((t, HEAD_PAD), lambda i: (i, 0))
    return pl.pallas_call(
        body, name=name, out_shape=jax.ShapeDtypeStruct(x.shape, BF16), grid=(S // t,),
        in_specs=[blk, tab, tab, tab], out_specs=blk, compiler_params=_cparams(),
    )(x, *tables)


MLA_A_PAD = MLA_QR + MLA_KVR + HEAD_PAD
MLA_KV_IN = MLA_KVR + HEAD_PAD


def _mla_mid_fwd(a, gq, gkv, tables):
    S = a.shape[0]
    t = _tile(S, 512)

    def body(a_ref, gq_ref, gkv_ref, ta_ref, tb_ref, tc_ref, q_ref, kv_ref):
        cq = a_ref[:, :MLA_QR]
        ckv = a_ref[:, MLA_QR:MLA_QR + MLA_KVR]
        kr = a_ref[:, MLA_QR + MLA_KVR:]
        q_ref[...] = (cq * _rstd(cq) * gq_ref[...]).astype(BF16)
        kv_ref[:, :MLA_KVR] = (ckv * _rstd(ckv) * gkv_ref[...]).astype(BF16)
        kv_ref[:, MLA_KVR:] = _rope(kr, ta_ref[...], tb_ref[...], tc_ref[...], False).astype(BF16)

    tab = pl.BlockSpec((t, HEAD_PAD), lambda i: (i, 0))
    return pl.pallas_call(
        body, name="mla_mid_fwd",
        out_shape=[jax.ShapeDtypeStruct((S, MLA_QR), BF16), jax.ShapeDtypeStruct((S, MLA_KV_IN), BF16)],
        grid=(S // t,),
        in_specs=[pl.BlockSpec((t, MLA_A_PAD), lambda i: (i, 0)), pl.BlockSpec((1, MLA_QR), lambda i: (0, 0)),
                  pl.BlockSpec((1, MLA_KVR), lambda i: (0, 0)), tab, tab, tab],
        out_specs=[pl.BlockSpec((t, MLA_QR), lambda i: (i, 0)), pl.BlockSpec((t, MLA_KV_IN), lambda i: (i, 0))],
        compiler_params=_cparams(),
    )(a, gq, gkv, *tables)


def _mla_mid_bwd(a, dq, dkv, gq, gkv, tables):
    S = a.shape[0]
    t = _tile(S, 512)

    def body(a_ref, dq_ref, dkv_ref, gq_ref, gkv_ref, ta_ref, tb_ref, tc_ref, da_ref, dgq_ref, dgkv_ref):
        @pl.when(pl.program_id(0) == 0)
        def _():
            dgq_ref[...] = jnp.zeros_like(dgq_ref)
            dgkv_ref[...] = jnp.zeros_like(dgkv_ref)

        def one(c, dout, g):
            r = _rstd(c)
            n = c * r
            return _norm_bwd(n, r, dout * g), jnp.sum(dout * n, axis=0, keepdims=True)

        dcq, dgq = one(a_ref[:, :MLA_QR], dq_ref[...], gq_ref[...])
        dckv, dgkv = one(a_ref[:, MLA_QR:MLA_QR + MLA_KVR], dkv_ref[:, :MLA_KVR], gkv_ref[...])
        da_ref[:, :MLA_QR] = dcq.astype(BF16)
        da_ref[:, MLA_QR:MLA_QR + MLA_KVR] = dckv.astype(BF16)
        da_ref[:, MLA_QR + MLA_KVR:] = _rope(dkv_ref[:, MLA_KVR:], ta_ref[...], tb_ref[...], tc_ref[...],
                                             True).astype(BF16)
        dgq_ref[...] += dgq
        dgkv_ref[...] += dgkv

    tab = pl.BlockSpec((t, HEAD_PAD), lambda i: (i, 0))
    vq = pl.BlockSpec((1, MLA_QR), lambda i: (0, 0))
    vkv = pl.BlockSpec((1, MLA_KVR), lambda i: (0, 0))
    return pl.pallas_call(
        body, name="mla_mid_bwd",
        out_shape=[jax.ShapeDtypeStruct((S, MLA_A_PAD), BF16), jax.ShapeDtypeStruct((1, MLA_QR), F32),
                   jax.ShapeDtypeStruct((1, MLA_KVR), F32)],
        grid=(S // t,),
        in_specs=[pl.BlockSpec((t, MLA_A_PAD), lambda i: (i, 0)), pl.BlockSpec((t, MLA_QR), lambda i: (i, 0)),
                  pl.BlockSpec((t, MLA_KV_IN), lambda i: (i, 0)), vq, vkv, tab, tab, tab],
        out_specs=[pl.BlockSpec((t, MLA_A_PAD), lambda i: (i, 0)), vq, vkv],
        compiler_params=_cparams(),
    )(a, dq, dkv, gq, gkv, *tables)


FFN_ROWS = 256
FFN_COLS = 1408
HALO = 8


def _shift_down(cur, halo, s, first):
    rolled = pltpu.roll(cur, s, 0)
    hrolled = jnp.where(first, 0.0, pltpu.roll(halo, s, 0))
    row = lax.broadcasted_iota(jnp.int32, (HALO, cur.shape[1]), 0)
    top = jnp.where(row < s, hrolled, rolled[:HALO])
    return jnp.concatenate([top, rolled[HALO:]], axis=0)


def _shift_up(cur, halo, s, last):
    rows = cur.shape[0]
    rolled = pltpu.roll(cur, rows - s, 0)
    hrolled = jnp.where(last, 0.0, pltpu.roll(halo, HALO - s, 0))
    row = lax.broadcasted_iota(jnp.int32, (HALO, cur.shape[1]), 0)
    bottom = jnp.where(row >= HALO - s, hrolled, rolled[rows - HALO:])
    return jnp.concatenate([rolled[:rows - HALO], bottom], axis=0)


def _conv(u, halo, w_ref, b, first):
    u1 = _shift_down(u, halo, 1, first)
    u2 = _shift_down(u, halo, 2, first)
    y = b + w_ref[0:1, :] * u2
    y = y + w_ref[1:2, :] * u1
    y = y + w_ref[2:3, :] * u
    return y, u1, u2


def _ffn_specs(S, tr, tn):
    per = tr // HALO
    tile = pl.BlockSpec((tr, tn), lambda j, i: (i, j))
    before = pl.BlockSpec((HALO, tn), lambda j, i: (jnp.maximum(i * per - 1, 0), j))
    after = pl.BlockSpec((HALO, tn), lambda j, i: (jnp.minimum((i + 1) * per, S // HALO - 1), j))
    w3 = pl.BlockSpec((3, tn), lambda j, i: (0, j))
    w1 = pl.BlockSpec((1, tn), lambda j, i: (0, j))
    return tile, before, after, w3, w1


def _ffn_act_fwd(ug, uv, wg, wv, bg, bv, name):
    S, F = ug.shape
    tr, tn = _tile(S, FFN_ROWS), _tile(F, FFN_COLS)
    tile, before, _, w3, w1 = _ffn_specs(S, tr, tn)

    def body(ug_ref, uv_ref, hg_ref, hv_ref, wg_ref, wv_ref, bg_ref, bv_ref, o_ref):
        first = pl.program_id(1) == 0
        g, _, _ = _conv(ug_ref[...], hg_ref[...], wg_ref, bg_ref[...], first)
        v, _, _ = _conv(uv_ref[...], hv_ref[...], wv_ref, bv_ref[...], first)
        o_ref[...] = (g * jax.nn.sigmoid(g) * v).astype(BF16)

    return pl.pallas_call(
        body, name=name, out_shape=jax.ShapeDtypeStruct((S, F), BF16), grid=(F // tn, S // tr),
        in_specs=[tile, tile, before, before, w3, w3, w1, w1], out_specs=tile, compiler_params=_cparams(),
    )(ug, uv, ug, uv, wg, wv, bg, bv)


def _ffn_bwd_gate(ug, uv, dact, wg, wv, bg, bv, name):
    S, F = ug.shape
    tr, tn = _tile(S, FFN_ROWS), _tile(F, FFN_COLS)
    tile, before, _, w3, w1 = _ffn_specs(S, tr, tn)

    def body(ug_ref, uv_ref, hg_ref, hv_ref, da_ref, wg_ref, wv_ref, bg_ref, bv_ref,
             dyg_ref, dyv_ref, dwg_ref, dwv_ref, dbg_ref, dbv_ref):
        first = pl.program_id(1) == 0

        @pl.when(first)
        def _():
            for r in (dwg_ref, dwv_ref, dbg_ref, dbv_ref):
                r[...] = jnp.zeros_like(r)

        ugv, uvv = ug_ref[...], uv_ref[...]
        g, ug1, ug2 = _conv(ugv, hg_ref[...], wg_ref, bg_ref[...], first)
        v, uv1, uv2 = _conv(uvv, hv_ref[...], wv_ref, bv_ref[...], first)
        da = da_ref[...]
        sg = jax.nn.sigmoid(g)
        dyg = da * v * (sg * (1.0 + g * (1.0 - sg)))
        dyv = da * (g * sg)
        dyg_ref[...] = dyg
        dyv_ref[...] = dyv

        def add_taps(dw_ref, dy, taps):
            for k, u in enumerate(taps):
                dw_ref[k:k + 1, :] += jnp.sum(dy * u, axis=0, keepdims=True)

        add_taps(dwg_ref, dyg, (ug2, ug1, ugv))
        add_taps(dwv_ref, dyv, (uv2, uv1, uvv))
        dbg_ref[...] += jnp.sum(dyg, axis=0, keepdims=True)
        dbv_ref[...] += jnp.sum(dyv, axis=0, keepdims=True)

    big = jax.ShapeDtypeStruct((S, F), F32)
    s3, s1 = jax.ShapeDtypeStruct((3, F), F32), jax.ShapeDtypeStruct((1, F), F32)
    return pl.pallas_call(
        body, name=name, out_shape=[big, big, s3, s3, s1, s1], grid=(F // tn, S // tr),
        in_specs=[tile, tile, before, before, tile, w3, w3, w1, w1],
        out_specs=[tile, tile, w3, w3, w1, w1], compiler_params=_cparams(),
    )(ug, uv, ug, uv, dact, wg, wv, bg, bv)


def _ffn_bwd_conv(dy, w, name):
    S, F = dy.shape
    tr, tn = _tile(S, FFN_ROWS), _tile(F, FFN_COLS)
    tile, _, after, w3, _ = _ffn_specs(S, tr, tn)
    n = S // tr

    def body(dy_ref, h_ref, w_ref, o_ref):
        last = pl.program_id(1) == n - 1
        dyv = dy_ref[...]
        d1 = _shift_up(dyv, h_ref[...], 1, last)
        d2 = _shift_up(dyv, h_ref[...], 2, last)
        o_ref[...] = (w_ref[2:3, :] * dyv + w_ref[1:2, :] * d1 + w_ref[0:1, :] * d2).astype(BF16)

    return pl.pallas_call(
        body, name=name, out_shape=jax.ShapeDtypeStruct((S, F), BF16), grid=(F // tn, n),
        in_specs=[tile, after, w3], out_specs=tile, compiler_params=_cparams(),
    )(dy, dy, w)


def _dot3(a, b):
    a_hi = a.astype(BF16)
    a_lo = (a - a_hi.astype(F32)).astype(BF16)
    b_hi = b.astype(BF16)
    b_lo = (b - b_hi.astype(F32)).astype(BF16)
    d = lambda p, q: lax.dot_general(p, q, NN, preferred_element_type=F32)
    return d(a_hi, b_hi) + (d(a_hi, b_lo) + d(a_lo, b_hi))


def _ada_mod(c_all, w, b):
    nmod, D, n = w.shape

    def body(c_ref, w_ref, b_ref, o_ref):
        cv = c_ref[...]
        o_ref[0] = _dot3(cv * jax.nn.sigmoid(cv), w_ref[0]) + b_ref[0]

    return pl.pallas_call(
        body, name="ada_mod", out_shape=jax.ShapeDtypeStruct((nmod, 8, n), F32), grid=(nmod,),
        in_specs=[pl.BlockSpec((8, D), lambda m: (0, 0)), pl.BlockSpec((1, D, n), lambda m: (m, 0, 0)),
                  pl.BlockSpec((1, 1, n), lambda m: (m, 0, 0))],
        out_specs=pl.BlockSpec((1, 8, n), lambda m: (m, 0, 0)), compiler_params=_cparams(),
    )(c_all, w, b)


def _ada_grad(c_all_t, dmod):
    D = c_all_t.shape[0]
    nmod, _, n = dmod.shape
    tr = 256

    def body(c_ref, d_ref, dw_ref, db_ref):
        cv = c_ref[...]
        sc = cv * jax.nn.sigmoid(cv)
        dv = d_ref[0]
        acc = sc[:, 0:1] * dv[0:1, :]
        tot = dv[0:1, :]
        for k in range(1, 8):
            acc = acc + sc[:, k:k + 1] * dv[k:k + 1, :]
            tot = tot + dv[k:k + 1, :]
        dw_ref[0] = acc
        db_ref[0] = tot

    return pl.pallas_call(
        body, name="ada_grad",
        out_shape=[jax.ShapeDtypeStruct((nmod, D, n), F32), jax.ShapeDtypeStruct((nmod, 1, n), F32)],
        grid=(nmod, D // tr),
        in_specs=[pl.BlockSpec((tr, 8), lambda m, i: (i, 0)), pl.BlockSpec((1, 8, n), lambda m, i: (m, 0, 0))],
        out_specs=[pl.BlockSpec((1, tr, n), lambda m, i: (m, i, 0)), pl.BlockSpec((1, 1, n), lambda m, i: (m, 0, 0))],
        compiler_params=_cparams(),
    )(c_all_t, dmod)


def _adamw(w, g, m, v, name):
    R, C = w.shape
    t = _row_tile(R, C)

    def body(w_ref, g_ref, m_ref, v_ref, d_ref, nm_ref, nv_ref):
        gv = g_ref[...]
        mn = ADAM_B1 * m_ref[...] + (1.0 - ADAM_B1) * gv
        vn = ADAM_B2 * v_ref[...] + (1.0 - ADAM_B2) * (gv * gv)
        m_hat = mn / (1.0 - ADAM_B1 ** ADAM_STEP)
        v_hat = vn / (1.0 - ADAM_B2 ** ADAM_STEP)
        d_ref[...] = -ADAM_LR * (m_hat / (jnp.sqrt(v_hat) + ADAM_EPS) + ADAM_WD * w_ref[...])
        nm_ref[...] = mn
        nv_ref[...] = vn

    blk = pl.BlockSpec((t, C), lambda i: (i, 0))
    shp = jax.ShapeDtypeStruct((R, C), F32)
    return pl.pallas_call(
        body, name=name, out_shape=[shp, shp, shp], grid=(R // t,), in_specs=[blk] * 4, out_specs=[blk] * 3,
        compiler_params=_cparams(),
    )(w, g, m, v)


def _cat_cols(g4):
    return jnp.concatenate([g4[s] for s in range(4)], axis=-1)


def _cat_rows(g4):
    return jnp.concatenate([g4[s] for s in range(4)], axis=-2)


def _pad_head_cols(w, width):
    K = w.shape[0]
    w = w.reshape(K, HEADS, width)
    return jnp.pad(w, ((0, 0), (0, 0), (0, HEAD_PAD - width))).reshape(K, HEADS * HEAD_PAD)


def _unpad_head_cols(w, width):
    K = w.shape[0]
    return w.reshape(K, HEADS, HEAD_PAD)[:, :, :width].reshape(K, HEADS * width)


def _pad_head_rows(w, width):
    N = w.shape[1]
    w = w.reshape(HEADS, width, N)
    return jnp.pad(w, ((0, 0), (0, HEAD_PAD - width), (0, 0))).reshape(HEADS * HEAD_PAD, N)


def _unpad_head_rows(w, width):
    N = w.shape[1]
    return w.reshape(HEADS, HEAD_PAD, N)[:, :width, :].reshape(HEADS * width, N)


def _prepare_fox_in(fox_in):
    W = {}
    w = _cat_cols(fox_in[:, 0])
    hw = HEADS * FOX_HEAD
    qkv = [_pad_head_cols(w[:, p * hw:(p + 1) * hw], FOX_HEAD) for p in range(3)]
    qkv[0] = qkv[0] * FOX_SCALE
    f = jnp.pad(w[:, 3 * hw:], ((0, 0), (0, LANE - HEADS)))
    W["fox_qkv"] = jnp.concatenate(qkv, axis=1)
    W["fox_f"] = f
    W["fox_all"] = jnp.concatenate(qkv + [f], axis=1)
    return W


def _prepare_weights(gathered):
    fox_o, mla_a, mla_uq, mla_ukv, mla_o, ffn_in, ffn_out = gathered
    W = {}
    W["fox_o"] = _pad_head_rows(_cat_rows(fox_o[:, 0]), FOX_HEAD)
    w = _cat_rows(mla_a[:, 0])
    lat = MLA_QR + MLA_KVR
    W["mla_a"] = jnp.concatenate(
        [w[:, :lat], jnp.zeros((D_MODEL, MLA_NOPE), BF16), w[:, lat:],
         jnp.zeros((D_MODEL, HEAD_PAD - MLA_NOPE - MLA_ROPE), BF16)], axis=1)
    W["mla_uq"] = _pad_head_cols(_cat_cols(mla_uq[:, 0]), MLA_NOPE + MLA_ROPE)
    w = _cat_cols(mla_ukv[:, 0]).reshape(MLA_KVR, HEADS, MLA_NOPE + MLA_V)
    kn = _pad_head_cols(w[:, :, :MLA_NOPE].reshape(MLA_KVR, -1), MLA_NOPE)
    vv = _pad_head_cols(w[:, :, MLA_NOPE:].reshape(MLA_KVR, -1), MLA_V)
    eye = np.zeros((HEAD_PAD, HEADS, HEAD_PAD), np.float32)
    for j in range(MLA_NOPE, MLA_NOPE + MLA_ROPE):
        eye[j, :, j] = 1.0
    eye = jnp.asarray(eye.reshape(HEAD_PAD, HEADS * HEAD_PAD), BF16)
    bottom = jnp.concatenate([eye, jnp.zeros((HEAD_PAD, HEADS * HEAD_PAD), BF16)], axis=1)
    W["mla_kv"] = jnp.concatenate([jnp.concatenate([kn, vv], axis=1), bottom], axis=0)
    W["mla_o"] = _pad_head_rows(_cat_rows(mla_o[:, 0]), MLA_V)
    w = _cat_cols(ffn_in)
    W["ffn_g"] = [w[i, :, :D_FF] for i in range(2)]
    W["ffn_v"] = [w[i, :, D_FF:] for i in range(2)]
    w = _cat_rows(ffn_out)
    W["ffn_out"] = [w[i] for i in range(2)]
    return W


def _ffn_forward(xin, mod, W, i, conv_w, conv_b):
    shift, scale, gate = mod
    h = _adaln_fwd(xin, scale, shift, name=f"ffn{i}_adaln")
    ug = _mm(h, W["ffn_g"][i], name=f"ffn{i}_up_gate")
    uv = _mm(h, W["ffn_v"][i], name=f"ffn{i}_up_val")
    wg, wv = conv_w[i][:, :D_FF], conv_w[i][:, D_FF:]
    bg, bv = conv_b[i][None, :D_FF], conv_b[i][None, D_FF:]
    act = _ffn_act_fwd(ug, uv, wg, wv, bg, bv, name=f"ffn{i}_act")
    y = _mm(act, W["ffn_out"][i], name=f"ffn{i}_down")
    xout = _residual(xin, gate, y, name=f"ffn{i}_residual")
    return xout, (xin, h, ug, uv, act, y, wg, wv, bg, bv)


def _ffn_backward(dx_out, saved, mod, W, i):
    xin, h, ug, uv, act, y, wg, wv, bg, bv = saved
    shift, scale, gate = mod
    dy, dgate = _gate_bwd(dx_out, y, gate, name=f"ffn{i}_gate_bwd")
    d_out = _mm(act, dy, ta=True, name=f"ffn{i}_dw_out")
    dact = _mm(dy, W["ffn_out"][i], tb=True, name=f"ffn{i}_dact")
    dyg, dyv, dwg, dwv, dbg, dbv = _ffn_bwd_gate(ug, uv, dact, wg, wv, bg, bv, name=f"ffn{i}_bwd_gate")
    dug = _ffn_bwd_conv(dyg, wg, name=f"ffn{i}_bwd_conv_g")
    duv = _ffn_bwd_conv(dyv, wv, name=f"ffn{i}_bwd_conv_v")
    d_in = jnp.concatenate([_mm(h, dug, ta=True, name=f"ffn{i}_dw_gate"),
                            _mm(h, duv, ta=True, name=f"ffn{i}_dw_val")], axis=1)
    dh = _mm(dug, W["ffn_g"][i], tb=True, name=f"ffn{i}_dh_gate")
    dh = _mm(duv, W["ffn_v"][i], tb=True, res=dh, name=f"ffn{i}_dh_val")
    dx, dscale, dshift = _adaln_bwd(xin, dh, dx_out, scale, name=f"ffn{i}_adaln_bwd")
    grads = dict(ffn_w_in=d_in, ffn_w_out=d_out, ffn_conv_w=jnp.concatenate([dwg, dwv], axis=1),
                 ffn_conv_b=jnp.concatenate([dbg, dbv], axis=1)[0])
    return dx, jnp.concatenate([dshift, dscale, dgate], axis=1)[0], grads


def _local_step(x, target, mods, W, small):
    S = x.shape[0]
    tables = _rope_tables(S)
    wide = HEADS * HEAD_PAD
    bf = jnp.pad(small["fox_b_f"], ((0, 0), (0, LANE - HEADS)))
    gq, gkv = small["mla_g_q"], small["mla_g_kv"]

    shift, scale, gate = mods[0]
    h0 = _adaln_fwd(x, scale, shift, name="fox_adaln")
    qkv = _mm(h0, W["fox_qkv"], out_dtype=BF16, name="fox_qkv")
    fl = _mm(h0, W["fox_f"], name="fox_gate_logits")
    cum = _fox_gate_fwd(fl, bf)
    q_fox, k_fox = _fox_bias_lanes(qkv, cum, name="fox_bias_lanes")
    o0, lse0, gathered = _attn_fwd(q_fox, 0, k_fox, 0, qkv, 2 * HEADS, name="fox_attn_fwd", gather=W["pending"])
    W = {**W, **_prepare_weights(gathered)}
    y0 = _mm(o0, W["fox_o"], name="fox_out")
    x1 = _residual(x, gate, y0, name="fox_residual")
    x2, ffn0 = _ffn_forward(x1, mods[1], W, 0, small["ffn_conv_w"], small["ffn_conv_b"])

    shift, scale, gate = mods[2]
    h2 = _adaln_fwd(x2, scale, shift, name="mla_adaln")
    a = _mm(h2, W["mla_a"], name="mla_down")
    cqn, ckv_in = _mla_mid_fwd(a, gq, gkv, tables)
    q_raw = _mm(cqn, W["mla_uq"], name="mla_up_q")
    q_cat = _rope_heads(q_raw, tables, False, MLA_SCALE, name="mla_rope_q")
    kv = _mm(ckv_in, W["mla_kv"], out_dtype=BF16, name="mla_up_kv")
    o1, lse1, _ = _attn_fwd(q_cat, 0, kv, 0, kv, HEADS, name="mla_attn_fwd")
    y2 = _mm(o1, W["mla_o"], name="mla_out")
    x3 = _residual(x2, gate, y2, name="mla_residual")
    x4, ffn1 = _ffn_forward(x3, mods[3], W, 1, small["ffn_conv_w"], small["ffn_conv_b"])

    loss, dx4, d_final_g = _final_loss(x4, small["final_g"], target)

    dx3, dmod3, g_ffn1 = _ffn_backward(dx4, ffn1, mods[3], W, 1)

    shift, scale, gate = mods[2]
    dy, dgate = _gate_bwd(dx3, y2, gate, name="mla_gate_bwd")
    d_mla_o = _mm(o1, dy, ta=True, name="mla_dw_out")
    do = _mm(dy, W["mla_o"], tb=True, out_dtype=BF16, name="mla_do")
    delta = _attn_delta(do, o1, name="mla_attn_delta")
    dq, dk, dv = _attn_bwd(q_cat, 0, kv, 0, kv, HEADS, do, lse1.reshape(HEADS, 1, S),
                           delta, name="mla_attn_bwd")
    dq_raw = _rope_heads(dq, tables, True, MLA_SCALE, name="mla_rope_q_bwd")
    d_mla_uq = _mm(cqn, dq_raw, ta=True, name="mla_dw_uq")
    dcqn = _mm(dq_raw, W["mla_uq"], tb=True, name="mla_dcq")
    dkv = jnp.concatenate([dk.astype(BF16), dv], axis=1)
    d_mla_kv = _mm(ckv_in, dkv, ta=True, name="mla_dw_kv")
    dckv_in = _mm(dkv, W["mla_kv"], tb=True, name="mla_dckv")
    da, dgq, dgkv = _mla_mid_bwd(a, dcqn, dckv_in, gq, gkv, tables)
    d_mla_a = _mm(h2, da, ta=True, name="mla_dw_a")
    dh = _mm(da, W["mla_a"], tb=True, name="mla_dh")
    dx2, dscale, dshift = _adaln_bwd(x2, dh, dx3, scale, name="mla_adaln_bwd")
    dmod2 = jnp.concatenate([dshift, dscale, dgate], axis=1)[0]

    dx1, dmod1, g_ffn0 = _ffn_backward(dx2, ffn0, mods[1], W, 0)

    shift, scale, gate = mods[0]
    dy, dgate = _gate_bwd(dx1, y0, gate, name="fox_gate_bwd")
    d_fox_o = _mm(o0, dy, ta=True, name="fox_dw_out")
    do = _mm(dy, W["fox_o"], tb=True, out_dtype=BF16, name="fox_do")
    delta = _attn_delta(do, o0, name="fox_attn_delta")
    dq, dk, dv = _attn_bwd(q_fox, 0, k_fox, 0, qkv, 2 * HEADS, do, lse0.reshape(HEADS, 1, S),
                           delta, name="fox_attn_bwd")
    dcq = dq.reshape(S, HEADS, HEAD_PAD)[:, :, BIAS_LANE + 3]
    dck = dk.reshape(S, HEADS, HEAD_PAD)[:, :, BIAS_LANE]
    dcum = jnp.pad(dcq - dck, ((0, 0), (0, LANE - HEADS)))
    dfl, dbf = _fox_gate_bwd(dcum, fl, bf)
    dproj = jnp.concatenate([dq.astype(BF16), dk.astype(BF16), dv, dfl.astype(BF16)], axis=1)
    d_fox_all = _mm(h0, dproj, ta=True, name="fox_dw_in")
    dh = _mm(dproj, W["fox_all"], tb=True, name="fox_dh")
    dx0, dscale, dshift = _adaln_bwd(x, dh, dx1, scale, name="fox_adaln_bwd")
    dmod0 = jnp.concatenate([dshift, dscale, dgate], axis=1)[0]

    G = {}
    d_qkv = [_unpad_head_cols(d_fox_all[:, p * wide:(p + 1) * wide], FOX_HEAD) for p in range(3)]
    d_qkv[0] = d_qkv[0] * FOX_SCALE
    G["fox_w_in"] = jnp.concatenate(d_qkv + [d_fox_all[:, 3 * wide:3 * wide + HEADS]], axis=1)[None]
    G["fox_b_f"] = dbf[:, :HEADS]
    G["fox_w_o"] = _unpad_head_rows(d_fox_o, FOX_HEAD)[None]
    lat = MLA_QR + MLA_KVR
    G["mla_w_a"] = jnp.concatenate([d_mla_a[:, :lat], d_mla_a[:, lat + MLA_NOPE:lat + MLA_NOPE + MLA_ROPE]],
                                   axis=1)[None]
    G["mla_g_q"] = dgq
    G["mla_g_kv"] = dgkv
    G["mla_w_uq"] = _unpad_head_cols(d_mla_uq, MLA_NOPE + MLA_ROPE)[None]
    dkn = d_mla_kv[:MLA_KVR, :wide].reshape(MLA_KVR, HEADS, HEAD_PAD)[:, :, :MLA_NOPE]
    dvv = d_mla_kv[:MLA_KVR, wide:].reshape(MLA_KVR, HEADS, HEAD_PAD)[:, :, :MLA_V]
    G["mla_w_ukv"] = jnp.concatenate([dkn, dvv], axis=2).reshape(MLA_KVR, -1)[None]
    G["mla_w_o"] = _unpad_head_rows(d_mla_o, MLA_V)[None]
    for name in ("ffn_w_in", "ffn_w_out", "ffn_conv_w", "ffn_conv_b"):
        G[name] = jnp.stack([g_ffn0[name], g_ffn1[name]])
    G["final_g"] = d_final_g[0]
    dmods = jnp.stack([dmod0, dmod1, dmod2, dmod3])
    return loss, dx0, dmods, G


PACKED = [("fox_w_in", 2), ("fox_w_o", 1), ("mla_w_a", 1), ("mla_w_uq", 2), ("mla_w_ukv", 2), ("mla_w_o", 1),
          ("ffn_w_in", 2), ("ffn_w_out", 1), ("fox_b_f", None), ("mla_g_q", 1), ("mla_g_kv", 1),
          ("ffn_conv_w", 2), ("ffn_conv_b", None), ("final_g", None)]


def _shard_of(g, axis, s):
    if axis is None:
        return g
    n = g.shape[axis] // 4
    return lax.slice_in_dim(g, s * n, (s + 1) * n, axis=axis)


def _pack_plan(G):
    shard_size = lambda name, axis: math.prod(_shard_of(G[name], axis, 0).shape)
    big = [(name, axis) for name, axis in PACKED if shard_size(name, axis) % PACK_ROW == 0]
    small = [(name, axis) for name, axis in PACKED if shard_size(name, axis) % PACK_ROW != 0]
    whole_tiles = lambda size: -(-size // (8 * PACK_ROW)) * 8
    runs = [([item], whole_tiles(shard_size(*item))) for item in big]
    runs.append((small, whole_tiles(sum(shard_size(*item) for item in small))))
    used = sum(rows for _, rows in runs)
    unit = 2 * PACK_ROWS_ALIGN
    return runs, -(-used // unit) * unit


def _pack(G):
    runs, total = _pack_plan(G)
    pieces = []
    for items, rows in runs:
        per_chip = [jnp.concatenate([_shard_of(G[name], axis, s).reshape(-1) for name, axis in items])
                    for s in range(4)]
        flat = jnp.stack(per_chip)
        flat = jnp.pad(flat, ((0, 0), (0, rows * PACK_ROW - flat.shape[1])))
        pieces.append(flat.reshape(4, rows, PACK_ROW))
    used = sum(rows for _, rows in runs)
    if total > used:
        pieces.append(jnp.zeros((4, total - used, PACK_ROW), F32))
    return jnp.concatenate(pieces, axis=1).reshape(4, 2, total // 2, PACK_ROW)


def _unpack(both, G_like):
    runs, total = _pack_plan(G_like)
    table = both.reshape(total, PACK_ROW)
    out, row = {}, 0
    for items, rows in runs:
        flat, off = table[row:row + rows].reshape(-1), 0
        for name, axis in items:
            shape = _shard_of(G_like[name], axis, 0).shape
            out[name] = flat[off:off + math.prod(shape)].reshape(shape)
            off += math.prod(shape)
        row += rows
    return out


WEIGHTS = ['ada_w', 'ada_b', 'fox_w_in', 'fox_b_f', 'fox_w_o', 'mla_w_a', 'mla_g_q', 'mla_g_kv', 'mla_w_uq',
           'mla_w_ukv', 'mla_w_o', 'ffn_w_in', 'ffn_conv_w', 'ffn_conv_b', 'ffn_w_out', 'final_g']
BIG = ['fox_w_in', 'fox_w_o', 'mla_w_a', 'mla_w_uq', 'mla_w_ukv', 'mla_w_o', 'ffn_w_in', 'ffn_w_out']
SMALL = ['ada_b', 'fox_b_f', 'mla_g_q', 'mla_g_kv', 'ffn_conv_w', 'ffn_conv_b', 'final_g']


def _as2d(a):
    return a.reshape(-1, a.shape[-1])


def kernel(x, c, ada_w, ada_b, fox_w_in, fox_b_f, fox_w_o, mla_w_a, mla_g_q, mla_g_kv, mla_w_uq, mla_w_ukv, mla_w_o, ffn_w_in, ffn_conv_w, ffn_conv_b, ffn_w_out, final_g, loss_target, m_ada_w, m_ada_b, m_fox_w_in, m_fox_b_f, m_fox_w_o, m_mla_w_a, m_mla_g_q, m_mla_g_kv, m_mla_w_uq, m_mla_w_ukv, m_mla_w_o, m_ffn_w_in, m_ffn_conv_w, m_ffn_conv_b, m_ffn_w_out, m_final_g, v_ada_w, v_ada_b, v_fox_w_in, v_fox_b_f, v_fox_w_o, v_mla_w_a, v_mla_g_q, v_mla_g_kv, v_mla_w_uq, v_mla_w_ukv, v_mla_w_o, v_ffn_w_in, v_ffn_conv_w, v_ffn_conv_b, v_ffn_w_out, v_final_g):
    w = dict(ada_w=ada_w, ada_b=ada_b, fox_w_in=fox_w_in, fox_b_f=fox_b_f, fox_w_o=fox_w_o, mla_w_a=mla_w_a,
             mla_g_q=mla_g_q, mla_g_kv=mla_g_kv, mla_w_uq=mla_w_uq, mla_w_ukv=mla_w_ukv, mla_w_o=mla_w_o,
             ffn_w_in=ffn_w_in, ffn_conv_w=ffn_conv_w, ffn_conv_b=ffn_conv_b, ffn_w_out=ffn_w_out, final_g=final_g)
    m = dict(ada_w=m_ada_w, ada_b=m_ada_b, fox_w_in=m_fox_w_in, fox_b_f=m_fox_b_f, fox_w_o=m_fox_w_o,
             mla_w_a=m_mla_w_a, mla_g_q=m_mla_g_q, mla_g_kv=m_mla_g_kv, mla_w_uq=m_mla_w_uq,
             mla_w_ukv=m_mla_w_ukv, mla_w_o=m_mla_w_o, ffn_w_in=m_ffn_w_in, ffn_conv_w=m_ffn_conv_w,
             ffn_conv_b=m_ffn_conv_b, ffn_w_out=m_ffn_w_out, final_g=m_final_g)
    v = dict(ada_w=v_ada_w, ada_b=v_ada_b, fox_w_in=v_fox_w_in, fox_b_f=v_fox_b_f, fox_w_o=v_fox_w_o,
             mla_w_a=v_mla_w_a, mla_g_q=v_mla_g_q, mla_g_kv=v_mla_g_kv, mla_w_uq=v_mla_w_uq,
             mla_w_ukv=v_mla_w_ukv, mla_w_o=v_mla_w_o, ffn_w_in=v_ffn_w_in, ffn_conv_w=v_ffn_conv_w,
             ffn_conv_b=v_ffn_conv_b, ffn_w_out=v_ffn_w_out, final_g=v_final_g)
    D = D_MODEL
    xi, yi, ci = lax.axis_index("x"), lax.axis_index("y"), lax.axis_index("c")
    dev = 4 * xi + 2 * yi + ci

    n_ada = ada_w.shape[-1]
    small_parts = [c.reshape(-1), ffn_conv_w.reshape(-1), mla_g_q.reshape(-1), mla_g_kv.reshape(-1)]
    sizes = [p.shape[0] for p in small_parts]
    flat = jnp.concatenate(small_parts)
    flat = jnp.pad(flat, (0, -flat.shape[0] % LANE))[None]
    got = _all_gather8(flat, name="gather_cond")[:, 0]
    offs = np.cumsum([0] + sizes)
    c_all = got[:, offs[0]:offs[1]]
    chips = got[0::2]
    conv_w_full = jnp.concatenate(
        [chips[s, offs[1]:offs[2]].reshape(ffn_conv_w.shape) for s in range(4)], axis=2)
    gq_full = jnp.concatenate([chips[s, offs[2]:offs[3]] for s in range(4)])[None]
    gkv_full = jnp.concatenate([chips[s, offs[3]:offs[4]] for s in range(4)])[None]

    mod_shard = _ada_mod(c_all, ada_w.reshape(4, D, n_ada), ada_b.reshape(4, 1, n_ada))
    mod_all = _all_gather8(mod_shard.reshape(4 * 8, n_ada), name="gather_mod")
    mod_all = mod_all[0::2].reshape(4, 4, 8, n_ada)
    mine = lax.dynamic_index_in_dim(mod_all, dev, axis=2, keepdims=False)
    mod_rows = jnp.transpose(mine, (1, 0, 2)).reshape(4, 4 * n_ada)
    mods = [(mod_rows[k:k + 1, :D], mod_rows[k:k + 1, D:2 * D], mod_rows[k:k + 1, 2 * D:]) for k in range(4)]

    shards = [w[name].astype(BF16) for name in BIG]
    W = _prepare_fox_in(_all_gather_chips(shards[:1], name="gather_weights")[0])
    W["pending"] = shards[1:]
    small = dict(fox_b_f=fox_b_f, mla_g_q=gq_full, mla_g_kv=gkv_full, ffn_conv_w=conv_w_full,
                 ffn_conv_b=ffn_conv_b, final_g=final_g[None])

    loss, dx, dmods, G = _local_step(x[0], loss_target[0], mods, W, small)

    loss_row = jnp.pad(loss[:, :1], ((0, 0), (0, dmods.shape[1] - 1)))
    gathered_rows = _all_gather8(jnp.concatenate([dmods, loss_row]), name="gather_dmod")
    dmod_all = gathered_rows[:, :4]
    loss_total = jnp.sum(gathered_rows[:, 4, 0])
    chip = 2 * xi + yi
    dmod_cols = lax.dynamic_slice_in_dim(dmod_all, chip * n_ada, n_ada, axis=2)
    g_ada_w, g_ada_b = _ada_grad(c_all.T, jnp.transpose(dmod_cols, (1, 0, 2)))
    grads = dict(ada_w=g_ada_w.reshape(ada_w.shape), ada_b=g_ada_b.reshape(ada_b.shape))

    packed = _pack(G)
    recv = _exchange_halves(packed)
    part = _add_halves(packed, recv, ci.reshape(1).astype(jnp.int32))
    from_chips = _exchange_chips(part)
    half = _add_chips(from_chips)
    both = _share_halves(half)
    grads.update(_unpack(both, G))

    delta, new_m, new_v = {}, {}, {}
    for name in BIG + ["ada_w"]:
        shape = w[name].shape
        d_, m_, v_ = _adamw(_as2d(w[name]), _as2d(grads[name]), _as2d(m[name]), _as2d(v[name]), name=f"adamw_{name}")
        delta[name], new_m[name], new_v[name] = d_.reshape(shape), m_.reshape(shape), v_.reshape(shape)
        grads[name] = grads[name].reshape(shape)
    sizes = [math.prod(w[name].shape) for name in SMALL]
    total = sum(sizes)
    rows = -(-total // LANE)
    rows += -rows % 8

    def pack_small(d):
        flat = jnp.concatenate([d[name].reshape(-1) for name in SMALL])
        return jnp.pad(flat, (0, rows * LANE - total)).reshape(rows, LANE)

    outs = _adamw(pack_small(w), pack_small(grads), pack_small(m), pack_small(v), name="adamw_small")
    off = 0
    for name, size in zip(SMALL, sizes):
        shape = w[name].shape
        for dst, src in zip((delta, new_m, new_v), outs):
            dst[name] = src.reshape(-1)[off:off + size].reshape(shape)
        grads[name] = grads[name].reshape(shape)
        off += size

    return (loss_total, dx[None], *[grads[n] for n in WEIGHTS], *[delta[n] for n in WEIGHTS],
            *[new_m[n] for n in WEIGHTS], *[new_v[n] for n in WEIGHTS])
```

```python
import functools
import math

import numpy as np
import jax
import jax.numpy as jnp
from jax import lax
from jax.experimental import pallas as pl
from jax.experimental.pallas import tpu as pltpu

F32 = jnp.float32
BF16 = jnp.bfloat16
MESH = pl.DeviceIdType.MESH

D_MODEL = 1024
HEADS = 16
HEAD_PAD = 128
FOX_HEAD = 64
MLA_NOPE = 64
MLA_ROPE = 32
MLA_V = 64
MLA_QR = 384
MLA_KVR = 256
D_FF = 2816
NORM_EPS = 1e-6
ROPE_BASE = 10000.0
ADAM_LR = 0.001
ADAM_B1 = 0.9
ADAM_B2 = 0.999
ADAM_EPS = 1e-08
ADAM_WD = 0.01
ADAM_STEP = 10
FOX_SCALE = FOX_HEAD ** -0.5
MLA_SCALE = (MLA_NOPE + MLA_ROPE) ** -0.5
NEG_BIG = -1e30
VMEM_LIMIT = 56 * 1024 * 1024
LANE = 128
PACK_ROW = 1024
PACK_ROWS_ALIGN = 256

NT = (((1,), (1,)), ((), ()))
TN = (((0,), (0,)), ((), ()))
NN = (((1,), (0,)), ((), ()))


def _cparams():
    return pltpu.CompilerParams(vmem_limit_bytes=VMEM_LIMIT)


def _tile(n, cap):
    if n <= cap:
        return n
    for t in range(cap - cap % LANE, 0, -LANE):
        if n % t == 0:
            return t
    raise ValueError((n, cap))


def _row_tile(rows, cols, limit_bytes=1 << 20):
    best = None
    for t in range(8, rows + 1, 8):
        if rows % t == 0 and t * cols * 4 <= limit_bytes:
            best = t
    return best if best is not None else rows


def _me():
    return lax.axis_index("x"), lax.axis_index("y"), lax.axis_index("c")


def _all_gather8(v, name):
    R, N = v.shape

    def body(v_ref, o_ref, ssem, rsem):
        x, y, c = _me()
        me = 4 * x + 2 * y + c
        o_ref[me] = v_ref[...]
        copies = []
        for k in range(1, 8):
            peer = (1 - x if k & 4 else x, 1 - y if k & 2 else y, 1 - c if k & 1 else c)
            cp = pltpu.make_async_remote_copy(
                src_ref=v_ref, dst_ref=o_ref.at[me], send_sem=ssem.at[k - 1], recv_sem=rsem.at[k - 1],
                device_id=peer, device_id_type=MESH)
            cp.start()
            copies.append(cp)
        for cp in copies:
            cp.wait()

    return pl.pallas_call(
        body, name=name,
        out_shape=jax.ShapeDtypeStruct((8, R, N), v.dtype),
        in_specs=[pl.BlockSpec(memory_space=pltpu.VMEM)],
        out_specs=pl.BlockSpec(memory_space=pltpu.VMEM),
        scratch_shapes=[pltpu.SemaphoreType.DMA((7,)), pltpu.SemaphoreType.DMA((7,))],
    )(v)


def _all_gather_chips(shards, name):
    n = len(shards)

    def body(*refs):
        copies = _chip_gather_copies(refs[:n], refs[n:2 * n], *refs[2 * n:])
        for cp in copies:
            cp.start()
        for cp in copies:
            cp.wait()

    return pl.pallas_call(
        body, name=name,
        out_shape=[jax.ShapeDtypeStruct((4,) + s.shape, s.dtype) for s in shards],
        in_specs=[pl.BlockSpec(memory_space=pl.ANY)] * n,
        out_specs=[pl.BlockSpec(memory_space=pl.ANY)] * n,
        scratch_shapes=[pltpu.SemaphoreType.DMA((3 * n,)), pltpu.SemaphoreType.DMA((3 * n,)),
                        pltpu.SemaphoreType.DMA((n,))],
    )(*shards)


def _exchange_halves(g):
    _, _, R, P = g.shape

    def body(g_ref, o_ref, ssem, rsem):
        x, y, c = _me()
        copies = []
        for s in range(4):
            cp = pltpu.make_async_remote_copy(
                src_ref=g_ref.at[s, 1 - c], dst_ref=o_ref.at[s], send_sem=ssem.at[s], recv_sem=rsem.at[s],
                device_id=(x, y, 1 - c), device_id_type=MESH)
            cp.start()
            copies.append(cp)
        for cp in copies:
            cp.wait()

    return pl.pallas_call(
        body, name="grad_exchange_sibling",
        out_shape=jax.ShapeDtypeStruct((4, R, P), g.dtype),
        in_specs=[pl.BlockSpec(memory_space=pl.ANY)],
        out_specs=pl.BlockSpec(memory_space=pl.ANY),
        scratch_shapes=[pltpu.SemaphoreType.DMA((4,)), pltpu.SemaphoreType.DMA((4,))],
    )(g)


def _exchange_chips(a):
    _, R, P = a.shape

    def body(a_ref, o_ref, ssem, rsem, lsem):
        x, y, c = _me()
        me = 2 * x + y
        own = pltpu.make_async_copy(a_ref.at[me], o_ref.at[me], lsem)
        own.start()
        copies = []
        for k in (1, 2, 3):
            px, py = (1 - x if k & 2 else x), (1 - y if k & 1 else y)
            cp = pltpu.make_async_remote_copy(
                src_ref=a_ref.at[2 * px + py], dst_ref=o_ref.at[me], send_sem=ssem.at[k - 1],
                recv_sem=rsem.at[k - 1], device_id=(px, py, c), device_id_type=MESH)
            cp.start()
            copies.append(cp)
        own.wait()
        for cp in copies:
            cp.wait()

    return pl.pallas_call(
        body, name="grad_exchange_chips",
        out_shape=jax.ShapeDtypeStruct((4, R, P), a.dtype),
        in_specs=[pl.BlockSpec(memory_space=pl.ANY)],
        out_specs=pl.BlockSpec(memory_space=pl.ANY),
        scratch_shapes=[pltpu.SemaphoreType.DMA((3,)), pltpu.SemaphoreType.DMA((3,)), pltpu.SemaphoreType.DMA],
    )(a)


def _share_halves(f):
    R, P = f.shape

    def body(f_ref, o_ref, ssem, rsem):
        x, y, c = _me()
        cp = pltpu.make_async_remote_copy(
            src_ref=f_ref, dst_ref=o_ref, send_sem=ssem, recv_sem=rsem,
            device_id=(x, y, 1 - c), device_id_type=MESH)
        cp.start()
        cp.wait()

    return pl.pallas_call(
        body, name="grad_share_sibling",
        out_shape=jax.ShapeDtypeStruct((R, P), f.dtype),
        in_specs=[pl.BlockSpec(memory_space=pl.ANY)],
        out_specs=pl.BlockSpec(memory_space=pl.ANY),
        scratch_shapes=[pltpu.SemaphoreType.DMA, pltpu.SemaphoreType.DMA],
    )(f)


def _add_halves(g, r, c_idx):
    _, _, R, P = g.shape
    t = PACK_ROWS_ALIGN

    def body(c_ref, g_ref, r_ref, o_ref):
        o_ref[...] = (g_ref[0] + r_ref[...]).astype(BF16)

    return pl.pallas_call(
        body, name="grad_add_halves",
        out_shape=jax.ShapeDtypeStruct((4, R, P), BF16),
        grid_spec=pltpu.PrefetchScalarGridSpec(
            num_scalar_prefetch=1, grid=(4, R // t),
            in_specs=[pl.BlockSpec((1, 1, t, P), lambda s, i, c_ref: (s, c_ref[0], i, 0)),
                      pl.BlockSpec((1, t, P), lambda s, i, c_ref: (s, i, 0))],
            out_specs=pl.BlockSpec((1, t, P), lambda s, i, c_ref: (s, i, 0))),
        compiler_params=_cparams(),
    )(c_idx, g, r)


def _add_chips(b):
    _, R, P = b.shape
    t = PACK_ROWS_ALIGN

    def body(b_ref, o_ref):
        b0, b1, b2, b3 = (b_ref[k].astype(F32) for k in range(4))
        o_ref[...] = ((b0 + b1) + b2) + b3

    return pl.pallas_call(
        body, name="grad_add_chips",
        out_shape=jax.ShapeDtypeStruct((R, P), F32),
        grid=(R // t,),
        in_specs=[pl.BlockSpec((4, t, P), lambda i: (0, i, 0))],
        out_specs=pl.BlockSpec((t, P), lambda i: (i, 0)),
        compiler_params=_cparams(),
    )(b)


def _mm(a, b, *, ta=False, tb=False, out_dtype=F32, res=None, name):
    if ta:
        K, M = a.shape
    else:
        M, K = a.shape
    N = b.shape[0] if tb else b.shape[1]
    assert (b.shape[1] if tb else b.shape[0]) == K, (a.shape, b.shape, ta, tb)
    tm, tn, tk = _tile(M, 512), _tile(N, 1408), _tile(K, 1408)
    nk = K // tk
    has_res = res is not None
    dims = (((0 if ta else 1,), (1 if tb else 0,)), ((), ()))

    def body(*refs):
        if has_res:
            a_ref, b_ref, r_ref, o_ref, acc_ref = refs
        else:
            a_ref, b_ref, o_ref, acc_ref = refs
        k = pl.program_id(2)

        @pl.when(k == 0)
        def _():
            acc_ref[...] = jnp.zeros_like(acc_ref)

        acc_ref[...] += lax.dot_general(a_ref[...].astype(BF16), b_ref[...].astype(BF16), dims,
                                        preferred_element_type=F32)

        @pl.when(k == nk - 1)
        def _():
            r = acc_ref[...]
            if has_res:
                r = r + r_ref[...]
            o_ref[...] = r.astype(out_dtype)

    a_spec = (pl.BlockSpec((tk, tm), lambda i, j, k: (k, i)) if ta
              else pl.BlockSpec((tm, tk), lambda i, j, k: (i, k)))
    b_spec = (pl.BlockSpec((tn, tk), lambda i, j, k: (j, k)) if tb
              else pl.BlockSpec((tk, tn), lambda i, j, k: (k, j)))
    in_specs = [a_spec, b_spec]
    args = [a, b]
    if has_res:
        in_specs.append(pl.BlockSpec((tm, tn), lambda i, j, k: (i, j)))
        args.append(res)
    return pl.pallas_call(
        body, name=name,
        out_shape=jax.ShapeDtypeStruct((M, N), out_dtype),
        grid=(M // tm, N // tn, nk),
        in_specs=in_specs,
        out_specs=pl.BlockSpec((tm, tn), lambda i, j, k: (i, j)),
        scratch_shapes=[pltpu.VMEM((tm, tn), F32)],
        compiler_params=_cparams(),
    )(*args)


def _rstd(xv):
    return lax.rsqrt(jnp.mean(xv * xv, axis=-1, keepdims=True) + NORM_EPS)


def _norm_bwd(n, r, dn):
    return r * (dn - n * jnp.mean(dn * n, axis=-1, keepdims=True))


def _adaln_fwd(x, scale, shift, name):
    S, D = x.shape
    t = _tile(S, 512)

    def body(x_ref, sc_ref, sh_ref, h_ref):
        xv = x_ref[...]
        h_ref[...] = (xv * _rstd(xv) * (1.0 + sc_ref[...]) + sh_ref[...]).astype(BF16)

    row = pl.BlockSpec((t, D), lambda i: (i, 0))
    vec = pl.BlockSpec((1, D), lambda i: (0, 0))
    return pl.pallas_call(
        body, name=name, out_shape=jax.ShapeDtypeStruct((S, D), BF16), grid=(S // t,),
        in_specs=[row, vec, vec], out_specs=row, compiler_params=_cparams(),
    )(x, scale, shift)


def _adaln_bwd(x, dh, dx_next, scale, name):
    S, D = x.shape
    t = _tile(S, 512)

    def body(x_ref, dh_ref, dxn_ref, sc_ref, dx_ref, dsc_ref, dsh_ref):
        @pl.when(pl.program_id(0) == 0)
        def _():
            dsc_ref[...] = jnp.zeros_like(dsc_ref)
            dsh_ref[...] = jnp.zeros_like(dsh_ref)

        xv = x_ref[...]
        r = _rstd(xv)
        n = xv * r
        dh = dh_ref[...]
        dx_ref[...] = dxn_ref[...] + _norm_bwd(n, r, dh * (1.0 + sc_ref[...]))
        dsc_ref[...] += jnp.sum(dh * n, axis=0, keepdims=True)
        dsh_ref[...] += jnp.sum(dh, axis=0, keepdims=True)

    row = pl.BlockSpec((t, D), lambda i: (i, 0))
    vec = pl.BlockSpec((1, D), lambda i: (0, 0))
    return pl.pallas_call(
        body, name=name,
        out_shape=[jax.ShapeDtypeStruct((S, D), F32), jax.ShapeDtypeStruct((1, D), F32),
                   jax.ShapeDtypeStruct((1, D), F32)],
        grid=(S // t,), in_specs=[row, row, row, vec], out_specs=[row, vec, vec],
        compiler_params=_cparams(),
    )(x, dh, dx_next, scale)


def _residual(x, gate, y, name):
    S, D = x.shape
    t = _tile(S, 512)

    def body(x_ref, g_ref, y_ref, o_ref):
        o_ref[...] = x_ref[...] + g_ref[...] * y_ref[...]

    row = pl.BlockSpec((t, D), lambda i: (i, 0))
    vec = pl.BlockSpec((1, D), lambda i: (0, 0))
    return pl.pallas_call(
        body, name=name, out_shape=jax.ShapeDtypeStruct((S, D), F32), grid=(S // t,),
        in_specs=[row, vec, row], out_specs=row, compiler_params=_cparams(),
    )(x, gate, y)


def _gate_bwd(dx, y, gate, name):
    S, D = dx.shape
    t = _tile(S, 512)

    def body(dx_ref, y_ref, g_ref, dy_ref, dg_ref):
        @pl.when(pl.program_id(0) == 0)
        def _():
            dg_ref[...] = jnp.zeros_like(dg_ref)

        dxv = dx_ref[...]
        dy_ref[...] = (dxv * g_ref[...]).astype(BF16)
        dg_ref[...] += jnp.sum(dxv * y_ref[...], axis=0, keepdims=True)

    row = pl.BlockSpec((t, D), lambda i: (i, 0))
    vec = pl.BlockSpec((1, D), lambda i: (0, 0))
    return pl.pallas_call(
        body, name=name,
        out_shape=[jax.ShapeDtypeStruct((S, D), BF16), jax.ShapeDtypeStruct((1, D), F32)],
        grid=(S // t,), in_specs=[row, row, vec], out_specs=[row, vec], compiler_params=_cparams(),
    )(dx, y, gate)


def _final_loss(x, g, target):
    S, D = x.shape
    t = _tile(S, 512)

    def body(x_ref, g_ref, t_ref, loss_ref, dx_ref, dg_ref):
        @pl.when(pl.program_id(0) == 0)
        def _():
            loss_ref[...] = jnp.zeros_like(loss_ref)
            dg_ref[...] = jnp.zeros_like(dg_ref)

        xv = x_ref[...]
        gv = g_ref[...]
        r = _rstd(xv)
        n = xv * r
        err = n * gv - t_ref[...]
        part = jnp.sum(jnp.mean(err * err, axis=-1, keepdims=True), axis=0, keepdims=True)
        loss_ref[...] += jnp.broadcast_to(0.5 * part, loss_ref.shape)
        dy = err * (1.0 / D)
        dg_ref[...] += jnp.sum(dy * n, axis=0, keepdims=True)
        dx_ref[...] = _norm_bwd(n, r, dy * gv)

    row = pl.BlockSpec((t, D), lambda i: (i, 0))
    vec = pl.BlockSpec((1, D), lambda i: (0, 0))
    one = pl.BlockSpec((1, LANE), lambda i: (0, 0))
    return pl.pallas_call(
        body, name="final_loss",
        out_shape=[jax.ShapeDtypeStruct((1, LANE), F32), jax.ShapeDtypeStruct((S, D), F32),
                   jax.ShapeDtypeStruct((1, D), F32)],
        grid=(S // t,), in_specs=[row, vec, row], out_specs=[one, row, vec], compiler_params=_cparams(),
    )(x, g, target)


ATTN_TILE = 1024
ATTN_ROWS = 16
ONES_LANE = 127


def _causal_pairs(n, k_major):
    if k_major:
        pairs = [(i, j) for j in range(n) for i in range(j, n)]
    else:
        pairs = [(i, j) for i in range(n) for j in range(i + 1)]
    return (jnp.asarray(np.array([p[0] for p in pairs], np.int32)),
            jnp.asarray(np.array([p[1] for p in pairs], np.int32)))


def _chip_gather_copies(ins, outs, ssem, rsem, lsem):
    x, y, c = _me()
    me = 2 * x + y
    copies = []
    for i, (src, dst) in enumerate(zip(ins, outs)):
        copies.append(pltpu.make_async_copy(src, dst.at[me], lsem.at[i]))
        for k in (1, 2, 3):
            peer = (1 - x if k & 2 else x, 1 - y if k & 1 else y, c)
            copies.append(pltpu.make_async_remote_copy(
                src_ref=src, dst_ref=dst.at[me], send_sem=ssem.at[3 * i + k - 1],
                recv_sem=rsem.at[3 * i + k - 1], device_id=peer, device_id_type=MESH))
    return copies


def _attn_fwd(q_arr, q_off, k_arr, k_off, v_arr, v_off, name, gather=()):
    S = q_arr.shape[0]
    t = _tile(S, ATTN_TILE)
    n = S // t
    rc = min(ATTN_ROWS, t)
    ii, jj = _causal_pairs(n, k_major=False)
    ng = len(gather)
    last_step = ii.shape[0] - 1

    def body(ii_ref, jj_ref, q_ref, k_ref, v_ref, *rest):
        g_in, (o_ref, lse_ref), g_out = rest[:ng], rest[ng:ng + 2], rest[ng + 2:2 * ng + 2]
        s_s, m_s, a_s, acc_s = rest[2 * ng + 2:2 * ng + 6]
        step = pl.program_id(1)
        i, j = ii_ref[step], jj_ref[step]
        lane = lax.broadcasted_iota(jnp.int32, (t, HEAD_PAD), 1)
        if ng:
            copies = _chip_gather_copies(g_in, g_out, *rest[2 * ng + 6:])

            @pl.when(jnp.logical_and(pl.program_id(0) == 0, step == 0))
            def _():
                for cp in copies:
                    cp.start()

            @pl.when(jnp.logical_and(pl.program_id(0) == HEADS - 1, step == last_step))
            def _():
                for cp in copies:
                    cp.wait()

        @pl.when(j == 0)
        def _():
            m_s[...] = jnp.full_like(m_s, NEG_BIG)
            acc_s[...] = jnp.zeros_like(acc_s)

        def tile(diagonal):
            s_s[...] = lax.dot_general(q_ref[...], k_ref[...], NT, preferred_element_type=F32)
            for r in range(t // rc):
                rows = pl.ds(r * rc, rc)
                sc = s_s[rows, :]
                if diagonal:
                    qpos = r * rc + lax.broadcasted_iota(jnp.int32, (rc, t), 0)
                    kpos = lax.broadcasted_iota(jnp.int32, (rc, t), 1)
                    sc = jnp.where(qpos >= kpos, sc, NEG_BIG)
                m_old = m_s[rows, :]
                m_new = jnp.maximum(m_old, jnp.max(sc, axis=1, keepdims=True))
                s_s[rows, :] = jnp.exp(sc - m_new)
                a_s[rows, :] = jnp.exp(m_old - m_new)
                m_s[rows, :] = m_new
            v_ones = jnp.where(lane == ONES_LANE, 1.0, v_ref[...].astype(F32)).astype(BF16)
            acc_s[...] = a_s[...] * acc_s[...] + lax.dot_general(s_s[...].astype(BF16), v_ones, NN,
                                                                 preferred_element_type=F32)

        @pl.when(j < i)
        def _():
            tile(False)

        @pl.when(j == i)
        def _():
            tile(True)
            acc = acc_s[...]
            l = jnp.sum(jnp.where(lane == ONES_LANE, acc, 0.0), axis=1, keepdims=True)
            o_ref[...] = jnp.where(lane == ONES_LANE, 0.0, acc / l).astype(BF16)
            lse_ref[0] = m_s[...] + jnp.log(l)

    hbm = pl.BlockSpec(memory_space=pl.ANY)
    sems = [pltpu.SemaphoreType.DMA((3 * ng,)), pltpu.SemaphoreType.DMA((3 * ng,)),
            pltpu.SemaphoreType.DMA((ng,))] if ng else []
    grid_spec = pltpu.PrefetchScalarGridSpec(
        num_scalar_prefetch=2, grid=(HEADS, ii.shape[0]),
        in_specs=[pl.BlockSpec((t, HEAD_PAD), lambda h, s, ii, jj: (ii[s], q_off + h)),
                  pl.BlockSpec((t, HEAD_PAD), lambda h, s, ii, jj: (jj[s], k_off + h)),
                  pl.BlockSpec((t, HEAD_PAD), lambda h, s, ii, jj: (jj[s], v_off + h))] + [hbm] * ng,
        out_specs=[pl.BlockSpec((t, HEAD_PAD), lambda h, s, ii, jj: (ii[s], h)),
                   pl.BlockSpec((1, t, 1), lambda h, s, ii, jj: (h, ii[s], 0))] + [hbm] * ng,
        scratch_shapes=[pltpu.VMEM((t, t), F32), pltpu.VMEM((t, 1), F32), pltpu.VMEM((t, 1), F32),
                        pltpu.VMEM((t, HEAD_PAD), F32)] + sems)
    outs = pl.pallas_call(
        body, name=name, grid_spec=grid_spec,
        out_shape=[jax.ShapeDtypeStruct((S, HEADS * HEAD_PAD), BF16), jax.ShapeDtypeStruct((HEADS, S, 1), F32)]
        + [jax.ShapeDtypeStruct((4,) + g.shape, g.dtype) for g in gather],
        compiler_params=_cparams(),
    )(ii, jj, q_arr, k_arr, v_arr, *gather)
    return outs[0], outs[1], list(outs[2:])


def _attn_delta(do, o, name):
    S = do.shape[0]
    t = _tile(S, 512)

    def body(do_ref, o_ref, d_ref):
        lane = lax.broadcasted_iota(jnp.int32, (t, LANE), 1)
        acc = jnp.zeros((t, LANE), F32)
        for h in range(HEADS):
            cols = pl.ds(h * HEAD_PAD, HEAD_PAD)
            d = jnp.sum(do_ref[:, cols].astype(F32) * o_ref[:, cols].astype(F32), axis=1, keepdims=True)
            acc = jnp.where(lane == h, d, acc)
        d_ref[...] = acc

    blk = pl.BlockSpec((t, HEADS * HEAD_PAD), lambda i: (i, 0))
    by_lane = pl.pallas_call(
        body, name=name, out_shape=jax.ShapeDtypeStruct((S, LANE), F32), grid=(S // t,),
        in_specs=[blk, blk], out_specs=pl.BlockSpec((t, LANE), lambda i: (i, 0)),
        compiler_params=_cparams(),
    )(do, o)
    return by_lane[:, :HEADS].T.reshape(HEADS, 1, S)


def _attn_bwd(q_arr, q_off, k_arr, k_off, v_arr, v_off, do, lse, delta, name):
    S = q_arr.shape[0]
    t = _tile(S, ATTN_TILE)
    n = S // t
    rc = min(ATTN_ROWS, t)
    ii, jj = _causal_pairs(n, k_major=True)

    def body(ii_ref, jj_ref, q_ref, k_ref, v_ref, do_ref, lse_ref, dl_ref, dq_ref, dk_ref, dv_ref,
             s_s, dp_s, dk_s, dv_s):
        step = pl.program_id(1)
        i, j = ii_ref[step], jj_ref[step]

        @pl.when(step == 0)
        def _():
            dq_ref[...] = jnp.zeros_like(dq_ref)

        @pl.when(i == j)
        def _():
            dk_s[...] = jnp.zeros_like(dk_s)
            dv_s[...] = jnp.zeros_like(dv_s)

        def tile(diagonal):
            q, k, dov = q_ref[...], k_ref[...], do_ref[...]
            s_s[...] = lax.dot_general(k, q, NT, preferred_element_type=F32)
            dp_s[...] = lax.dot_general(v_ref[...], dov, NT, preferred_element_type=F32)
            lse_row, dl_row = lse_ref[0], dl_ref[0]
            for r in range(t // rc):
                rows = pl.ds(r * rc, rc)
                sc = s_s[rows, :]
                if diagonal:
                    kpos = r * rc + lax.broadcasted_iota(jnp.int32, (rc, t), 0)
                    qpos = lax.broadcasted_iota(jnp.int32, (rc, t), 1)
                    sc = jnp.where(qpos >= kpos, sc, NEG_BIG)
                p = jnp.exp(sc - lse_row)
                s_s[rows, :] = p
                dp_s[rows, :] = p * (dp_s[rows, :] - dl_row)
            ds = dp_s[...].astype(BF16)
            dv_s[...] += lax.dot_general(s_s[...].astype(BF16), dov, NN, preferred_element_type=F32)
            dk_s[...] += lax.dot_general(ds, q, NN, preferred_element_type=F32)
            rows = pl.ds(pl.multiple_of(i * t, t), t)
            dq_ref[rows, :] += lax.dot_general(ds, k, TN, preferred_element_type=F32)

        @pl.when(i > j)
        def _():
            tile(False)

        @pl.when(i == j)
        def _():
            tile(True)

        @pl.when(i == n - 1)
        def _():
            dk_ref[...] = dk_s[...]
            dv_ref[...] = dv_s[...].astype(BF16)

    qblk = lambda off: pl.BlockSpec((t, HEAD_PAD), lambda h, s, ii, jj: (ii[s], off + h))
    kblk = lambda off: pl.BlockSpec((t, HEAD_PAD), lambda h, s, ii, jj: (jj[s], off + h))
    stat = pl.BlockSpec((1, 1, t), lambda h, s, ii, jj: (h, 0, ii[s]))
    wide = (S, HEADS * HEAD_PAD)
    grid_spec = pltpu.PrefetchScalarGridSpec(
        num_scalar_prefetch=2, grid=(HEADS, ii.shape[0]),
        in_specs=[qblk(q_off), kblk(k_off), kblk(v_off), qblk(0), stat, stat],
        out_specs=[pl.BlockSpec((S, HEAD_PAD), lambda h, s, ii, jj: (0, h)), kblk(0), kblk(0)],
        scratch_shapes=[pltpu.VMEM((t, t), F32), pltpu.VMEM((t, t), F32), pltpu.VMEM((t, HEAD_PAD), F32),
                        pltpu.VMEM((t, HEAD_PAD), F32)])
    return pl.pallas_call(
        body, name=name, grid_spec=grid_spec,
        out_shape=[jax.ShapeDtypeStruct(wide, F32), jax.ShapeDtypeStruct(wide, F32),
                   jax.ShapeDtypeStruct(wide, BF16)],
        compiler_params=_cparams(),
    )(ii, jj, q_arr, k_arr, v_arr, do, lse, delta)


BIAS_LANE = FOX_HEAD


def _fox_bias_lanes(qkv, cum, name):
    S = qkv.shape[0]
    t = _tile(S, 256)
    wide = HEADS * HEAD_PAD

    def body(q_ref, k_ref, c_ref, qo_ref, ko_ref):
        lane = lax.broadcasted_iota(jnp.int32, (t, HEAD_PAD), 1)
        cum_all = c_ref[...]

        def fill(x, first, second):
            out = x.astype(F32)
            for n_, (a, b) in enumerate(zip(first, second)):
                out = jnp.where(lane == BIAS_LANE + n_, a, out)
                out = jnp.where(lane == BIAS_LANE + 3 + n_, b, out)
            return out.astype(BF16)

        ones = [1.0, 1.0, 1.0]
        for h in range(HEADS):
            cols = pl.ds(h * HEAD_PAD, HEAD_PAD)
            c = jnp.sum(jnp.where(lane == h, cum_all, 0.0), axis=1, keepdims=True)
            pieces = [x.astype(F32) for x in _split3(c)]
            qo_ref[:, cols] = fill(q_ref[:, cols], ones, pieces)
            ko_ref[:, cols] = fill(k_ref[:, cols], [-x for x in pieces], ones)

    blk = lambda off: pl.BlockSpec((t, wide), lambda i: (i, off))
    shp = jax.ShapeDtypeStruct((S, wide), BF16)
    return pl.pallas_call(
        body, name=name, out_shape=[shp, shp], grid=(S // t,),
        in_specs=[blk(0), blk(1), pl.BlockSpec((t, LANE), lambda i: (i, 0))],
        out_specs=[blk(0), blk(0)], compiler_params=_cparams(),
    )(qkv, qkv, cum)


def _split3(x):
    hi = x.astype(BF16)
    r1 = x - hi.astype(F32)
    mid = r1.astype(BF16)
    lo = (r1 - mid.astype(F32)).astype(BF16)
    return hi, mid, lo


def _tri_matmul(tri, x):
    acc = None
    for part in _split3(x):
        d = lax.dot_general(tri, part, NN, preferred_element_type=F32)
        acc = d if acc is None else acc + d
    return acc


def _log_sigmoid(z):
    return jnp.minimum(z, 0.0) - jnp.log1p(jnp.exp(-jnp.abs(z)))


def _fox_gate_fwd(fl, bf):
    S = fl.shape[0]
    t = _tile(S, 256)

    def body(fl_ref, bf_ref, cum_ref, carry):
        @pl.when(pl.program_id(0) == 0)
        def _():
            carry[...] = jnp.zeros_like(carry)

        logf = _log_sigmoid(fl_ref[...] + bf_ref[...])
        tri = (lax.broadcasted_iota(jnp.int32, (t, t), 0) >= lax.broadcasted_iota(jnp.int32, (t, t), 1)).astype(BF16)
        cum_ref[...] = _tri_matmul(tri, logf) + carry[...]
        carry[...] += jnp.sum(logf, axis=0, keepdims=True)

    row = pl.BlockSpec((t, LANE), lambda i: (i, 0))
    return pl.pallas_call(
        body, name="fox_gate_fwd", out_shape=jax.ShapeDtypeStruct((S, LANE), F32), grid=(S // t,),
        in_specs=[row, pl.BlockSpec((1, LANE), lambda i: (0, 0))], out_specs=row,
        scratch_shapes=[pltpu.VMEM((1, LANE), F32)], compiler_params=_cparams(),
    )(fl, bf)


def _fox_gate_bwd(dcum, fl, bf):
    S = fl.shape[0]
    t = _tile(S, 256)
    n = S // t

    def body(dc_ref, fl_ref, bf_ref, df_ref, db_ref, carry):
        @pl.when(pl.program_id(0) == 0)
        def _():
            carry[...] = jnp.zeros_like(carry)
            db_ref[...] = jnp.zeros_like(db_ref)

        dc = dc_ref[...]
        tri = (lax.broadcasted_iota(jnp.int32, (t, t), 1) >= lax.broadcasted_iota(jnp.int32, (t, t), 0)).astype(BF16)
        dlogf = _tri_matmul(tri, dc) + carry[...]
        carry[...] += jnp.sum(dc, axis=0, keepdims=True)
        z = fl_ref[...] + bf_ref[...]
        e = jnp.exp(-jnp.abs(z))
        sig_neg = jnp.where(z >= 0, e, 1.0) / (1.0 + e)
        lanes = lax.broadcasted_iota(jnp.int32, (t, LANE), 1)
        df = jnp.where(lanes < HEADS, dlogf * sig_neg, 0.0)
        df_ref[...] = df
        db_ref[...] += jnp.sum(df, axis=0, keepdims=True)

    row = pl.BlockSpec((t, LANE), lambda i: (n - 1 - i, 0))
    vec = pl.BlockSpec((1, LANE), lambda i: (0, 0))
    return pl.pallas_call(
        body, name="fox_gate_bwd",
        out_shape=[jax.ShapeDtypeStruct((S, LANE), F32), jax.ShapeDtypeStruct((1, LANE), F32)],
        grid=(n,), in_specs=[row, row, vec], out_specs=[row, vec],
        scratch_shapes=[pltpu.VMEM((1, LANE), F32)], compiler_params=_cparams(),
    )(dcum, fl, bf)


def _rope_tables(S):
    pos = jnp.arange(S, dtype=F32)
    inv_freq = ROPE_BASE ** (-jnp.arange(0, MLA_ROPE, 2, dtype=F32) / MLA_ROPE)
    ang = pos[:, None] * inv_freq[None, :]
    cos, sin = jnp.cos(ang), jnp.sin(ang)
    half = MLA_ROPE // 2
    ones = jnp.ones((S, MLA_NOPE), F32)
    z = lambda w: jnp.zeros((S, w), F32)
    keep = jnp.concatenate([ones, cos, cos, z(HEAD_PAD - MLA_NOPE - MLA_ROPE)], axis=1)
    from_above = jnp.concatenate([z(MLA_NOPE), -sin, z(HEAD_PAD - MLA_NOPE - half)], axis=1)
    from_below = jnp.concatenate([z(MLA_NOPE + half), sin, z(HEAD_PAD - MLA_NOPE - MLA_ROPE)], axis=1)
    return keep, from_above, from_below


def _rope(x, keep, up, down, transpose):
    half = MLA_ROPE // 2
    if transpose:
        return x * keep + pltpu.roll(x * up, half, 1) + pltpu.roll(x * down, HEAD_PAD - half, 1)
    return x * keep + pltpu.roll(x, HEAD_PAD - half, 1) * up + pltpu.roll(x, half, 1) * down


def _rope_heads(x, tables, transpose, scale, name):
    S = x.shape[0]
    t = _tile(S, 256)

    def body(x_ref, a_ref, b_ref, c_ref, o_ref):
        keep, up, down = a_ref[...], b_ref[...], c_ref[...]
        for h in range(HEADS):
            cols = pl.ds(h * HEAD_PAD, HEAD_PAD)
            o_ref[:, cols] = (_rope(x_ref[:, cols], keep, up, down, transpose) * scale).astype(BF16)

    blk = pl.BlockSpec((t, HEADS * HEAD_PAD), lambda i: (i, 0))
    tab = pl.BlockSpec((t, HEAD_PAD), lambda i: (i, 0))
    return pl.pallas_call(
        body, name=name, out_shape=jax.ShapeDtypeStruct(x.shape, BF16), grid=(S // t,),
        in_specs=[blk, tab, tab, tab], out_specs=blk, compiler_params=_cparams(),
    )(x, *tables)


MLA_A_PAD = MLA_QR + MLA_KVR + HEAD_PAD
MLA_KV_IN = MLA_KVR + HEAD_PAD


def _mla_mid_fwd(a, gq, gkv, tables):
    S = a.shape[0]
    t = _tile(S, 512)

    def body(a_ref, gq_ref, gkv_ref, ta_ref, tb_ref, tc_ref, q_ref, kv_ref):
        cq = a_ref[:, :MLA_QR]
        ckv = a_ref[:, MLA_QR:MLA_QR + MLA_KVR]
        kr = a_ref[:, MLA_QR + MLA_KVR:]
        q_ref[...] = (cq * _rstd(cq) * gq_ref[...]).astype(BF16)
        kv_ref[:, :MLA_KVR] = (ckv * _rstd(ckv) * gkv_ref[...]).astype(BF16)
        kv_ref[:, MLA_KVR:] = _rope(kr, ta_ref[...], tb_ref[...], tc_ref[...], False).astype(BF16)

    tab = pl.BlockSpec((t, HEAD_PAD), lambda i: (i, 0))
    return pl.pallas_call(
        body, name="mla_mid_fwd",
        out_shape=[jax.ShapeDtypeStruct((S, MLA_QR), BF16), jax.ShapeDtypeStruct((S, MLA_KV_IN), BF16)],
        grid=(S // t,),
        in_specs=[pl.BlockSpec((t, MLA_A_PAD), lambda i: (i, 0)), pl.BlockSpec((1, MLA_QR), lambda i: (0, 0)),
                  pl.BlockSpec((1, MLA_KVR), lambda i: (0, 0)), tab, tab, tab],
        out_specs=[pl.BlockSpec((t, MLA_QR), lambda i: (i, 0)), pl.BlockSpec((t, MLA_KV_IN), lambda i: (i, 0))],
        compiler_params=_cparams(),
    )(a, gq, gkv, *tables)


def _mla_mid_bwd(a, dq, dkv, gq, gkv, tables):
    S = a.shape[0]
    t = _tile(S, 512)

    def body(a_ref, dq_ref, dkv_ref, gq_ref, gkv_ref, ta_ref, tb_ref, tc_ref, da_ref, dgq_ref, dgkv_ref):
        @pl.when(pl.program_id(0) == 0)
        def _():
            dgq_ref[...] = jnp.zeros_like(dgq_ref)
            dgkv_ref[...] = jnp.zeros_like(dgkv_ref)

        def one(c, dout, g):
            r = _rstd(c)
            n = c * r
            return _norm_bwd(n, r, dout * g), jnp.sum(dout * n, axis=0, keepdims=True)

        dcq, dgq = one(a_ref[:, :MLA_QR], dq_ref[...], gq_ref[...])
        dckv, dgkv = one(a_ref[:, MLA_QR:MLA_QR + MLA_KVR], dkv_ref[:, :MLA_KVR], gkv_ref[...])
        da_ref[:, :MLA_QR] = dcq.astype(BF16)
        da_ref[:, MLA_QR:MLA_QR + MLA_KVR] = dckv.astype(BF16)
        da_ref[:, MLA_QR + MLA_KVR:] = _rope(dkv_ref[:, MLA_KVR:], ta_ref[...], tb_ref[...], tc_ref[...],
                                             True).astype(BF16)
        dgq_ref[...] += dgq
        dgkv_ref[...] += dgkv

    tab = pl.BlockSpec((t, HEAD_PAD), lambda i: (i, 0))
    vq = pl.BlockSpec((1, MLA_QR), lambda i: (0, 0))
    vkv = pl.BlockSpec((1, MLA_KVR), lambda i: (0, 0))
    return pl.pallas_call(
        body, name="mla_mid_bwd",
        out_shape=[jax.ShapeDtypeStruct((S, MLA_A_PAD), BF16), jax.ShapeDtypeStruct((1, MLA_QR), F32),
                   jax.ShapeDtypeStruct((1, MLA_KVR), F32)],
        grid=(S // t,),
        in_specs=[pl.BlockSpec((t, MLA_A_PAD), lambda i: (i, 0)), pl.BlockSpec((t, MLA_QR), lambda i: (i, 0)),
                  pl.BlockSpec((t, MLA_KV_IN), lambda i: (i, 0)), vq, vkv, tab, tab, tab],
        out_specs=[pl.BlockSpec((t, MLA_A_PAD), lambda i: (i, 0)), vq, vkv],
        compiler_params=_cparams(),
    )(a, dq, dkv, gq, gkv, *tables)


FFN_ROWS = 256
FFN_COLS = 1408
HALO = 8


def _shift_down(cur, halo, s, first):
    rolled = pltpu.roll(cur, s, 0)
    hrolled = jnp.where(first, 0.0, pltpu.roll(halo, s, 0))
    row = lax.broadcasted_iota(jnp.int32, (HALO, cur.shape[1]), 0)
    top = jnp.where(row < s, hrolled, rolled[:HALO])
    return jnp.concatenate([top, rolled[HALO:]], axis=0)


def _shift_up(cur, halo, s, last):
    rows = cur.shape[0]
    rolled = pltpu.roll(cur, rows - s, 0)
    hrolled = jnp.where(last, 0.0, pltpu.roll(halo, HALO - s, 0))
    row = lax.broadcasted_iota(jnp.int32, (HALO, cur.shape[1]), 0)
    bottom = jnp.where(row >= HALO - s, hrolled, rolled[rows - HALO:])
    return jnp.concatenate([rolled[:rows - HALO], bottom], axis=0)


def _conv(u, halo, w_ref, b, first):
    u1 = _shift_down(u, halo, 1, first)
    u2 = _shift_down(u, halo, 2, first)
    y = b + w_ref[0:1, :] * u2
    y = y + w_ref[1:2, :] * u1
    y = y + w_ref[2:3, :] * u
    return y, u1, u2


def _ffn_specs(S, tr, tn):
    per = tr // HALO
    tile = pl.BlockSpec((tr, tn), lambda j, i: (i, j))
    before = pl.BlockSpec((HALO, tn), lambda j, i: (jnp.maximum(i * per - 1, 0), j))
    after = pl.BlockSpec((HALO, tn), lambda j, i: (jnp.minimum((i + 1) * per, S // HALO - 1), j))
    w3 = pl.BlockSpec((3, tn), lambda j, i: (0, j))
    w1 = pl.BlockSpec((1, tn), lambda j, i: (0, j))
    return tile, before, after, w3, w1


def _ffn_act_fwd(ug, uv, wg, wv, bg, bv, name):
    S, F = ug.shape
    tr, tn = _tile(S, FFN_ROWS), _tile(F, FFN_COLS)
    tile, before, _, w3, w1 = _ffn_specs(S, tr, tn)

    def body(ug_ref, uv_ref, hg_ref, hv_ref, wg_ref, wv_ref, bg_ref, bv_ref, o_ref):
        first = pl.program_id(1) == 0
        g, _, _ = _conv(ug_ref[...], hg_ref[...], wg_ref, bg_ref[...], first)
        v, _, _ = _conv(uv_ref[...], hv_ref[...], wv_ref, bv_ref[...], first)
        o_ref[...] = (g * jax.nn.sigmoid(g) * v).astype(BF16)

    return pl.pallas_call(
        body, name=name, out_shape=jax.ShapeDtypeStruct((S, F), BF16), grid=(F // tn, S // tr),
        in_specs=[tile, tile, before, before, w3, w3, w1, w1], out_specs=tile, compiler_params=_cparams(),
    )(ug, uv, ug, uv, wg, wv, bg, bv)


def _ffn_bwd_gate(ug, uv, dact, wg, wv, bg, bv, name):
    S, F = ug.shape
    tr, tn = _tile(S, FFN_ROWS), _tile(F, FFN_COLS)
    tile, before, _, w3, w1 = _ffn_specs(S, tr, tn)

    def body(ug_ref, uv_ref, hg_ref, hv_ref, da_ref, wg_ref, wv_ref, bg_ref, bv_ref,
             dyg_ref, dyv_ref, dwg_ref, dwv_ref, dbg_ref, dbv_ref):
        first = pl.program_id(1) == 0

        @pl.when(first)
        def _():
            for r in (dwg_ref, dwv_ref, dbg_ref, dbv_ref):
                r[...] = jnp.zeros_like(r)

        ugv, uvv = ug_ref[...], uv_ref[...]
        g, ug1, ug2 = _conv(ugv, hg_ref[...], wg_ref, bg_ref[...], first)
        v, uv1, uv2 = _conv(uvv, hv_ref[...], wv_ref, bv_ref[...], first)
        da = da_ref[...]
        sg = jax.nn.sigmoid(g)
        dyg = da * v * (sg * (1.0 + g * (1.0 - sg)))
        dyv = da * (g * sg)
        dyg_ref[...] = dyg
        dyv_ref[...] = dyv

        def add_taps(dw_ref, dy, taps):
            for k, u in enumerate(taps):
                dw_ref[k:k + 1, :] += jnp.sum(dy * u, axis=0, keepdims=True)

        add_taps(dwg_ref, dyg, (ug2, ug1, ugv))
        add_taps(dwv_ref, dyv, (uv2, uv1, uvv))
        dbg_ref[...] += jnp.sum(dyg, axis=0, keepdims=True)
        dbv_ref[...] += jnp.sum(dyv, axis=0, keepdims=True)

    big = jax.ShapeDtypeStruct((S, F), F32)
    s3, s1 = jax.ShapeDtypeStruct((3, F), F32), jax.ShapeDtypeStruct((1, F), F32)
    return pl.pallas_call(
        body, name=name, out_shape=[big, big, s3, s3, s1, s1], grid=(F // tn, S // tr),
        in_specs=[tile, tile, before, before, tile, w3, w3, w1, w1],
        out_specs=[tile, tile, w3, w3, w1, w1], compiler_params=_cparams(),
    )(ug, uv, ug, uv, dact, wg, wv, bg, bv)


def _ffn_bwd_conv(dy, w, name):
    S, F = dy.shape
    tr, tn = _tile(S, FFN_ROWS), _tile(F, FFN_COLS)
    tile, _, after, w3, _ = _ffn_specs(S, tr, tn)
    n = S // tr

    def body(dy_ref, h_ref, w_ref, o_ref):
        last = pl.program_id(1) == n - 1
        dyv = dy_ref[...]
        d1 = _shift_up(dyv, h_ref[...], 1, last)
        d2 = _shift_up(dyv, h_ref[...], 2, last)
        o_ref[...] = (w_ref[2:3, :] * dyv + w_ref[1:2, :] * d1 + w_ref[0:1, :] * d2).astype(BF16)

    return pl.pallas_call(
        body, name=name, out_shape=jax.ShapeDtypeStruct((S, F), BF16), grid=(F // tn, n),
        in_specs=[tile, after, w3], out_specs=tile, compiler_params=_cparams(),
    )(dy, dy, w)


def _dot3(a, b):
    a_hi = a.astype(BF16)
    a_lo = (a - a_hi.astype(F32)).astype(BF16)
    b_hi = b.astype(BF16)
    b_lo = (b - b_hi.astype(F32)).astype(BF16)
    d = lambda p, q: lax.dot_general(p, q, NN, preferred_element_type=F32)
    return d(a_hi, b_hi) + (d(a_hi, b_lo) + d(a_lo, b_hi))


def _ada_mod(c_all, w, b):
    nmod, D, n = w.shape

    def body(c_ref, w_ref, b_ref, o_ref):
        cv = c_ref[...]
        o_ref[0] = _dot3(cv * jax.nn.sigmoid(cv), w_ref[0]) + b_ref[0]

    return pl.pallas_call(
        body, name="ada_mod", out_shape=jax.ShapeDtypeStruct((nmod, 8, n), F32), grid=(nmod,),
        in_specs=[pl.BlockSpec((8, D), lambda m: (0, 0)), pl.BlockSpec((1, D, n), lambda m: (m, 0, 0)),
                  pl.BlockSpec((1, 1, n), lambda m: (m, 0, 0))],
        out_specs=pl.BlockSpec((1, 8, n), lambda m: (m, 0, 0)), compiler_params=_cparams(),
    )(c_all, w, b)


def _ada_grad(c_all_t, dmod):
    D = c_all_t.shape[0]
    nmod, _, n = dmod.shape
    tr = 256

    def body(c_ref, d_ref, dw_ref, db_ref):
        cv = c_ref[...]
        sc = cv * jax.nn.sigmoid(cv)
        dv = d_ref[0]
        acc = sc[:, 0:1] * dv[0:1, :]
        tot = dv[0:1, :]
        for k in range(1, 8):
            acc = acc + sc[:, k:k + 1] * dv[k:k + 1, :]
            tot = tot + dv[k:k + 1, :]
        dw_ref[0] = acc
        db_ref[0] = tot

    return pl.pallas_call(
        body, name="ada_grad",
        out_shape=[jax.ShapeDtypeStruct((nmod, D, n), F32), jax.ShapeDtypeStruct((nmod, 1, n), F32)],
        grid=(nmod, D // tr),
        in_specs=[pl.BlockSpec((tr, 8), lambda m, i: (i, 0)), pl.BlockSpec((1, 8, n), lambda m, i: (m, 0, 0))],
        out_specs=[pl.BlockSpec((1, tr, n), lambda m, i: (m, i, 0)), pl.BlockSpec((1, 1, n), lambda m, i: (m, 0, 0))],
        compiler_params=_cparams(),
    )(c_all_t, dmod)


def _adamw(w, g, m, v, name):
    R, C = w.shape
    t = _row_tile(R, C)

    def body(w_ref, g_ref, m_ref, v_ref, d_ref, nm_ref, nv_ref):
        gv = g_ref[...]
        mn = ADAM_B1 * m_ref[...] + (1.0 - ADAM_B1) * gv
        vn = ADAM_B2 * v_ref[...] + (1.0 - ADAM_B2) * (gv * gv)
        m_hat = mn / (1.0 - ADAM_B1 ** ADAM_STEP)
        v_hat = vn / (1.0 - ADAM_B2 ** ADAM_STEP)
        d_ref[...] = -ADAM_LR * (m_hat / (jnp.sqrt(v_hat) + ADAM_EPS) + ADAM_WD * w_ref[...])
        nm_ref[...] = mn
        nv_ref[...] = vn

    blk = pl.BlockSpec((t, C), lambda i: (i, 0))
    shp = jax.ShapeDtypeStruct((R, C), F32)
    return pl.pallas_call(
        body, name=name, out_shape=[shp, shp, shp], grid=(R // t,), in_specs=[blk] * 4, out_specs=[blk] * 3,
        compiler_params=_cparams(),
    )(w, g, m, v)


def _cat_cols(g4):
    return jnp.concatenate([g4[s] for s in range(4)], axis=-1)


def _cat_rows(g4):
    return jnp.concatenate([g4[s] for s in range(4)], axis=-2)


def _pad_head_cols(w, width):
    K = w.shape[0]
    w = w.reshape(K, HEADS, width)
    return jnp.pad(w, ((0, 0), (0, 0), (0, HEAD_PAD - width))).reshape(K, HEADS * HEAD_PAD)


def _unpad_head_cols(w, width):
    K = w.shape[0]
    return w.reshape(K, HEADS, HEAD_PAD)[:, :, :width].reshape(K, HEADS * width)


def _pad_head_rows(w, width):
    N = w.shape[1]
    w = w.reshape(HEADS, width, N)
    return jnp.pad(w, ((0, 0), (0, HEAD_PAD - width), (0, 0))).reshape(HEADS * HEAD_PAD, N)


def _unpad_head_rows(w, width):
    N = w.shape[1]
    return w.reshape(HEADS, HEAD_PAD, N)[:, :width, :].reshape(HEADS * width, N)


def _prepare_fox_in(fox_in):
    W = {}
    w = _cat_cols(fox_in[:, 0])
    hw = HEADS * FOX_HEAD
    qkv = [_pad_head_cols(w[:, p * hw:(p + 1) * hw], FOX_HEAD) for p in range(3)]
    qkv[0] = qkv[0] * FOX_SCALE
    f = jnp.pad(w[:, 3 * hw:], ((0, 0), (0, LANE - HEADS)))
    W["fox_qkv"] = jnp.concatenate(qkv, axis=1)
    W["fox_f"] = f
    W["fox_all"] = jnp.concatenate(qkv + [f], axis=1)
    return W


def _prepare_weights(gathered):
    fox_o, mla_a, mla_uq, mla_ukv, mla_o, ffn_in, ffn_out = gathered
    W = {}
    W["fox_o"] = _pad_head_rows(_cat_rows(fox_o[:, 0]), FOX_HEAD)
    w = _cat_rows(mla_a[:, 0])
    lat = MLA_QR + MLA_KVR
    W["mla_a"] = jnp.concatenate(
        [w[:, :lat], jnp.zeros((D_MODEL, MLA_NOPE), BF16), w[:, lat:],
         jnp.zeros((D_MODEL, HEAD_PAD - MLA_NOPE - MLA_ROPE), BF16)], axis=1)
    W["mla_uq"] = _pad_head_cols(_cat_cols(mla_uq[:, 0]), MLA_NOPE + MLA_ROPE)
    w = _cat_cols(mla_ukv[:, 0]).reshape(MLA_KVR, HEADS, MLA_NOPE + MLA_V)
    kn = _pad_head_cols(w[:, :, :MLA_NOPE].reshape(MLA_KVR, -1), MLA_NOPE)
    vv = _pad_head_cols(w[:, :, MLA_NOPE:].reshape(MLA_KVR, -1), MLA_V)
    eye = np.zeros((HEAD_PAD, HEADS, HEAD_PAD), np.float32)
    for j in range(MLA_NOPE, MLA_NOPE + MLA_ROPE):
        eye[j, :, j] = 1.0
    eye = jnp.asarray(eye.reshape(HEAD_PAD, HEADS * HEAD_PAD), BF16)
    bottom = jnp.concatenate([eye, jnp.zeros((HEAD_PAD, HEADS * HEAD_PAD), BF16)], axis=1)
    W["mla_kv"] = jnp.concatenate([jnp.concatenate([kn, vv], axis=1), bottom], axis=0)
    W["mla_o"] = _pad_head_rows(_cat_rows(mla_o[:, 0]), MLA_V)
    w = _cat_cols(ffn_in)
    W["ffn_g"] = [w[i, :, :D_FF] for i in range(2)]
    W["ffn_v"] = [w[i, :, D_FF:] for i in range(2)]
    w = _cat_rows(ffn_out)
    W["ffn_out"] = [w[i] for i in range(2)]
    return W


def _ffn_forward(xin, mod, W, i, conv_w, conv_b):
    shift, scale, gate = mod
    h = _adaln_fwd(xin, scale, shift, name=f"ffn{i}_adaln")
    ug = _mm(h, W["ffn_g"][i], name=f"ffn{i}_up_gate")
    uv = _mm(h, W["ffn_v"][i], name=f"ffn{i}_up_val")
    wg, wv = conv_w[i][:, :D_FF], conv_w[i][:, D_FF:]
    bg, bv = conv_b[i][None, :D_FF], conv_b[i][None, D_FF:]
    act = _ffn_act_fwd(ug, uv, wg, wv, bg, bv, name=f"ffn{i}_act")
    y = _mm(act, W["ffn_out"][i], name=f"ffn{i}_down")
    xout = _residual(xin, gate, y, name=f"ffn{i}_residual")
    return xout, (xin, h, ug, uv, act, y, wg, wv, bg, bv)


def _ffn_backward(dx_out, saved, mod, W, i):
    xin, h, ug, uv, act, y, wg, wv, bg, bv = saved
    shift, scale, gate = mod
    dy, dgate = _gate_bwd(dx_out, y, gate, name=f"ffn{i}_gate_bwd")
    d_out = _mm(act, dy, ta=True, name=f"ffn{i}_dw_out")
    dact = _mm(dy, W["ffn_out"][i], tb=True, name=f"ffn{i}_dact")
    dyg, dyv, dwg, dwv, dbg, dbv = _ffn_bwd_gate(ug, uv, dact, wg, wv, bg, bv, name=f"ffn{i}_bwd_gate")
    dug = _ffn_bwd_conv(dyg, wg, name=f"ffn{i}_bwd_conv_g")
    duv = _ffn_bwd_conv(dyv, wv, name=f"ffn{i}_bwd_conv_v")
    d_in = jnp.concatenate([_mm(h, dug, ta=True, name=f"ffn{i}_dw_gate"),
                            _mm(h, duv, ta=True, name=f"ffn{i}_dw_val")], axis=1)
    dh = _mm(dug, W["ffn_g"][i], tb=True, name=f"ffn{i}_dh_gate")
    dh = _mm(duv, W["ffn_v"][i], tb=True, res=dh, name=f"ffn{i}_dh_val")
    dx, dscale, dshift = _adaln_bwd(xin, dh, dx_out, scale, name=f"ffn{i}_adaln_bwd")
    grads = dict(ffn_w_in=d_in, ffn_w_out=d_out, ffn_conv_w=jnp.concatenate([dwg, dwv], axis=1),
                 ffn_conv_b=jnp.concatenate([dbg, dbv], axis=1)[0])
    return dx, jnp.concatenate([dshift, dscale, dgate], axis=1)[0], grads


def _local_step(x, target, mods, W, small):
    S = x.shape[0]
    tables = _rope_tables(S)
    wide = HEADS * HEAD_PAD
    bf = jnp.pad(small["fox_b_f"], ((0, 0), (0, LANE - HEADS)))
    gq, gkv = small["mla_g_q"], small["mla_g_kv"]

    shift, scale, gate = mods[0]
    h0 = _adaln_fwd(x, scale, shift, name="fox_adaln")
    qkv = _mm(h0, W["fox_qkv"], out_dtype=BF16, name="fox_qkv")
    fl = _mm(h0, W["fox_f"], name="fox_gate_logits")
    cum = _fox_gate_fwd(fl, bf)
    q_fox, k_fox = _fox_bias_lanes(qkv, cum, name="fox_bias_lanes")
    o0, lse0, gathered = _attn_fwd(q_fox, 0, k_fox, 0, qkv, 2 * HEADS, name="fox_attn_fwd", gather=W["pending"])
    W = {**W, **_prepare_weights(gathered)}
    y0 = _mm(o0, W["fox_o"], name="fox_out")
    x1 = _residual(x, gate, y0, name="fox_residual")
    x2, ffn0 = _ffn_forward(x1, mods[1], W, 0, small["ffn_conv_w"], small["ffn_conv_b"])

    shift, scale, gate = mods[2]
    h2 = _adaln_fwd(x2, scale, shift, name="mla_adaln")
    a = _mm(h2, W["mla_a"], name="mla_down")
    cqn, ckv_in = _mla_mid_fwd(a, gq, gkv, tables)
    q_raw = _mm(cqn, W["mla_uq"], name="mla_up_q")
    q_cat = _rope_heads(q_raw, tables, False, MLA_SCALE, name="mla_rope_q")
    kv = _mm(ckv_in, W["mla_kv"], out_dtype=BF16, name="mla_up_kv")
    o1, lse1, _ = _attn_fwd(q_cat, 0, kv, 0, kv, HEADS, name="mla_attn_fwd")
    y2 = _mm(o1, W["mla_o"], name="mla_out")
    x3 = _residual(x2, gate, y2, name="mla_residual")
    x4, ffn1 = _ffn_forward(x3, mods[3], W, 1, small["ffn_conv_w"], small["ffn_conv_b"])

    loss, dx4, d_final_g = _final_loss(x4, small["final_g"], target)

    dx3, dmod3, g_ffn1 = _ffn_backward(dx4, ffn1, mods[3], W, 1)

    shift, scale, gate = mods[2]
    dy, dgate = _gate_bwd(dx3, y2, gate, name="mla_gate_bwd")
    d_mla_o = _mm(o1, dy, ta=True, name="mla_dw_out")
    do = _mm(dy, W["mla_o"], tb=True, out_dtype=BF16, name="mla_do")
    delta = _attn_delta(do, o1, name="mla_attn_delta")
    dq, dk, dv = _attn_bwd(q_cat, 0, kv, 0, kv, HEADS, do, lse1.reshape(HEADS, 1, S),
                           delta, name="mla_attn_bwd")
    dq_raw = _rope_heads(dq, tables, True, MLA_SCALE, name="mla_rope_q_bwd")
    d_mla_uq = _mm(cqn, dq_raw, ta=True, name="mla_dw_uq")
    dcqn = _mm(dq_raw, W["mla_uq"], tb=True, name="mla_dcq")
    dkv = jnp.concatenate([dk.astype(BF16), dv], axis=1)
    d_mla_kv = _mm(ckv_in, dkv, ta=True, name="mla_dw_kv")
    dckv_in = _mm(dkv, W["mla_kv"], tb=True, name="mla_dckv")
    da, dgq, dgkv = _mla_mid_bwd(a, dcqn, dckv_in, gq, gkv, tables)
    d_mla_a = _mm(h2, da, ta=True, name="mla_dw_a")
    dh = _mm(da, W["mla_a"], tb=True, name="mla_dh")
    dx2, dscale, dshift = _adaln_bwd(x2, dh, dx3, scale, name="mla_adaln_bwd")
    dmod2 = jnp.concatenate([dshift, dscale, dgate], axis=1)[0]

    dx1, dmod1, g_ffn0 = _ffn_backward(dx2, ffn0, mods[1], W, 0)

    shift, scale, gate = mods[0]
    dy, dgate = _gate_bwd(dx1, y0, gate, name="fox_gate_bwd")
    d_fox_o = _mm(o0, dy, ta=True, name="fox_dw_out")
    do = _mm(dy, W["fox_o"], tb=True, out_dtype=BF16, name="fox_do")
    delta = _attn_delta(do, o0, name="fox_attn_delta")
    dq, dk, dv = _attn_bwd(q_fox, 0, k_fox, 0, qkv, 2 * HEADS, do, lse0.reshape(HEADS, 1, S),
                           delta, name="fox_attn_bwd")
    dcq = dq.reshape(S, HEADS, HEAD_PAD)[:, :, BIAS_LANE + 3]
    dck = dk.reshape(S, HEADS, HEAD_PAD)[:, :, BIAS_LANE]
    dcum = jnp.pad(dcq - dck, ((0, 0), (0, LANE - HEADS)))
    dfl, dbf = _fox_gate_bwd(dcum, fl, bf)
    dproj = jnp.concatenate([dq.astype(BF16), dk.astype(BF16), dv, dfl.astype(BF16)], axis=1)
    d_fox_all = _mm(h0, dproj, ta=True, name="fox_dw_in")
    dh = _mm(dproj, W["fox_all"], tb=True, name="fox_dh")
    dx0, dscale, dshift = _adaln_bwd(x, dh, dx1, scale, name="fox_adaln_bwd")
    dmod0 = jnp.concatenate([dshift, dscale, dgate], axis=1)[0]

    G = {}
    d_qkv = [_unpad_head_cols(d_fox_all[:, p * wide:(p + 1) * wide], FOX_HEAD) for p in range(3)]
    d_qkv[0] = d_qkv[0] * FOX_SCALE
    G["fox_w_in"] = jnp.concatenate(d_qkv + [d_fox_all[:, 3 * wide:3 * wide + HEADS]], axis=1)[None]
    G["fox_b_f"] = dbf[:, :HEADS]
    G["fox_w_o"] = _unpad_head_rows(d_fox_o, FOX_HEAD)[None]
    lat = MLA_QR + MLA_KVR
    G["mla_w_a"] = jnp.concatenate([d_mla_a[:, :lat], d_mla_a[:, lat + MLA_NOPE:lat + MLA_NOPE + MLA_ROPE]],
                                   axis=1)[None]
    G["mla_g_q"] = dgq
    G["mla_g_kv"] = dgkv
    G["mla_w_uq"] = _unpad_head_cols(d_mla_uq, MLA_NOPE + MLA_ROPE)[None]
    dkn = d_mla_kv[:MLA_KVR, :wide].reshape(MLA_KVR, HEADS, HEAD_PAD)[:, :, :MLA_NOPE]
    dvv = d_mla_kv[:MLA_KVR, wide:].reshape(MLA_KVR, HEADS, HEAD_PAD)[:, :, :MLA_V]
    G["mla_w_ukv"] = jnp.concatenate([dkn, dvv], axis=2).reshape(MLA_KVR, -1)[None]
    G["mla_w_o"] = _unpad_head_rows(d_mla_o, MLA_V)[None]
    for name in ("ffn_w_in", "ffn_w_out", "ffn_conv_w", "ffn_conv_b"):
        G[name] = jnp.stack([g_ffn0[name], g_ffn1[name]])
    G["final_g"] = d_final_g[0]
    dmods = jnp.stack([dmod0, dmod1, dmod2, dmod3])
    return loss, dx0, dmods, G


PACKED = [("fox_w_in", 2), ("fox_w_o", 1), ("mla_w_a", 1), ("mla_w_uq", 2), ("mla_w_ukv", 2), ("mla_w_o", 1),
          ("ffn_w_in", 2), ("ffn_w_out", 1), ("fox_b_f", None), ("mla_g_q", 1), ("mla_g_kv", 1),
          ("ffn_conv_w", 2), ("ffn_conv_b", None), ("final_g", None)]


def _shard_of(g, axis, s):
    if axis is None:
        return g
    n = g.shape[axis] // 4
    return lax.slice_in_dim(g, s * n, (s + 1) * n, axis=axis)


def _pack_plan(G):
    shard_size = lambda name, axis: math.prod(_shard_of(G[name], axis, 0).shape)
    big = [(name, axis) for name, axis in PACKED if shard_size(name, axis) % PACK_ROW == 0]
    small = [(name, axis) for name, axis in PACKED if shard_size(name, axis) % PACK_ROW != 0]
    whole_tiles = lambda size: -(-size // (8 * PACK_ROW)) * 8
    runs = [([item], whole_tiles(shard_size(*item))) for item in big]
    runs.append((small, whole_tiles(sum(shard_size(*item) for item in small))))
    used = sum(rows for _, rows in runs)
    unit = 2 * PACK_ROWS_ALIGN
    return runs, -(-used // unit) * unit


def _pack(G):
    runs, total = _pack_plan(G)
    pieces = []
    for items, rows in runs:
        per_chip = [jnp.concatenate([_shard_of(G[name], axis, s).reshape(-1) for name, axis in items])
                    for s in range(4)]
        flat = jnp.stack(per_chip)
        flat = jnp.pad(flat, ((0, 0), (0, rows * PACK_ROW - flat.shape[1])))
        pieces.append(flat.reshape(4, rows, PACK_ROW))
    used = sum(rows for _, rows in runs)
    if total > used:
        pieces.append(jnp.zeros((4, total - used, PACK_ROW), F32))
    return jnp.concatenate(pieces, axis=1).reshape(4, 2, total // 2, PACK_ROW)


def _unpack(both, G_like):
    runs, total = _pack_plan(G_like)
    table = both.reshape(total, PACK_ROW)
    out, row = {}, 0
    for items, rows in runs:
        flat, off = table[row:row + rows].reshape(-1), 0
        for name, axis in items:
            shape = _shard_of(G_like[name], axis, 0).shape
            out[name] = flat[off:off + math.prod(shape)].reshape(shape)
            off += math.prod(shape)
        row += rows
    return out


WEIGHTS = ['ada_w', 'ada_b', 'fox_w_in', 'fox_b_f', 'fox_w_o', 'mla_w_a', 'mla_g_q', 'mla_g_kv', 'mla_w_uq',
           'mla_w_ukv', 'mla_w_o', 'ffn_w_in', 'ffn_conv_w', 'ffn_conv_b', 'ffn_w_out', 'final_g']
BIG = ['fox_w_in', 'fox_w_o', 'mla_w_a', 'mla_w_uq', 'mla_w_ukv', 'mla_w_o', 'ffn_w_in', 'ffn_w_out']
SMALL = ['ada_b', 'fox_b_f', 'mla_g_q', 'mla_g_kv', 'ffn_conv_w', 'ffn_conv_b', 'final_g']


def _as2d(a):
    return a.reshape(-1, a.shape[-1])


def kernel(x, c, ada_w, ada_b, fox_w_in, fox_b_f, fox_w_o, mla_w_a, mla_g_q, mla_g_kv, mla_w_uq, mla_w_ukv, mla_w_o, ffn_w_in, ffn_conv_w, ffn_conv_b, ffn_w_out, final_g, loss_target, m_ada_w, m_ada_b, m_fox_w_in, m_fox_b_f, m_fox_w_o, m_mla_w_a, m_mla_g_q, m_mla_g_kv, m_mla_w_uq, m_mla_w_ukv, m_mla_w_o, m_ffn_w_in, m_ffn_conv_w, m_ffn_conv_b, m_ffn_w_out, m_final_g, v_ada_w, v_ada_b, v_fox_w_in, v_fox_b_f, v_fox_w_o, v_mla_w_a, v_mla_g_q, v_mla_g_kv, v_mla_w_uq, v_mla_w_ukv, v_mla_w_o, v_ffn_w_in, v_ffn_conv_w, v_ffn_conv_b, v_ffn_w_out, v_final_g):
    w = dict(ada_w=ada_w, ada_b=ada_b, fox_w_in=fox_w_in, fox_b_f=fox_b_f, fox_w_o=fox_w_o, mla_w_a=mla_w_a,
             mla_g_q=mla_g_q, mla_g_kv=mla_g_kv, mla_w_uq=mla_w_uq, mla_w_ukv=mla_w_ukv, mla_w_o=mla_w_o,
             ffn_w_in=ffn_w_in, ffn_conv_w=ffn_conv_w, ffn_conv_b=ffn_conv_b, ffn_w_out=ffn_w_out, final_g=final_g)
    m = dict(ada_w=m_ada_w, ada_b=m_ada_b, fox_w_in=m_fox_w_in, fox_b_f=m_fox_b_f, fox_w_o=m_fox_w_o,
             mla_w_a=m_mla_w_a, mla_g_q=m_mla_g_q, mla_g_kv=m_mla_g_kv, mla_w_uq=m_mla_w_uq,
             mla_w_ukv=m_mla_w_ukv, mla_w_o=m_mla_w_o, ffn_w_in=m_ffn_w_in, ffn_conv_w=m_ffn_conv_w,
             ffn_conv_b=m_ffn_conv_b, ffn_w_out=m_ffn_w_out, final_g=m_final_g)
    v = dict(ada_w=v_ada_w, ada_b=v_ada_b, fox_w_in=v_fox_w_in, fox_b_f=v_fox_b_f, fox_w_o=v_fox_w_o,
             mla_w_a=v_mla_w_a, mla_g_q=v_mla_g_q, mla_g_kv=v_mla_g_kv, mla_w_uq=v_mla_w_uq,
             mla_w_ukv=v_mla_w_ukv, mla_w_o=v_mla_w_o, ffn_w_in=v_ffn_w_in, ffn_conv_w=v_ffn_conv_w,
             ffn_conv_b=v_ffn_conv_b, ffn_w_out=v_ffn_w_out, final_g=v_final_g)
    D = D_MODEL
    xi, yi, ci = lax.axis_index("x"), lax.axis_index("y"), lax.axis_index("c")
    dev = 4 * xi + 2 * yi + ci

    n_ada = ada_w.shape[-1]
    small_parts = [c.reshape(-1), ffn_conv_w.reshape(-1), mla_g_q.reshape(-1), mla_g_kv.reshape(-1)]
    sizes = [p.shape[0] for p in small_parts]
    flat = jnp.concatenate(small_parts)
    flat = jnp.pad(flat, (0, -flat.shape[0] % LANE))[None]
    got = _all_gather8(flat, name="gather_cond")[:, 0]
    offs = np.cumsum([0] + sizes)
    c_all = got[:, offs[0]:offs[1]]
    chips = got[0::2]
    conv_w_full = jnp.concatenate(
        [chips[s, offs[1]:offs[2]].reshape(ffn_conv_w.shape) for s in range(4)], axis=2)
    gq_full = jnp.concatenate([chips[s, offs[2]:offs[3]] for s in range(4)])[None]
    gkv_full = jnp.concatenate([chips[s, offs[3]:offs[4]] for s in range(4)])[None]

    mod_shard = _ada_mod(c_all, ada_w.reshape(4, D, n_ada), ada_b.reshape(4, 1, n_ada))
    mod_all = _all_gather8(mod_shard.reshape(4 * 8, n_ada), name="gather_mod")
    mod_all = mod_all[0::2].reshape(4, 4, 8, n_ada)
    mine = lax.dynamic_index_in_dim(mod_all, dev, axis=2, keepdims=False)
    mod_rows = jnp.transpose(mine, (1, 0, 2)).reshape(4, 4 * n_ada)
    mods = [(mod_rows[k:k + 1, :D], mod_rows[k:k + 1, D:2 * D], mod_rows[k:k + 1, 2 * D:]) for k in range(4)]

    shards = [w[name].astype(BF16) for name in BIG]
    W = _prepare_fox_in(_all_gather_chips(shards[:1], name="gather_weights")[0])
    W["pending"] = shards[1:]
    small = dict(fox_b_f=fox_b_f, mla_g_q=gq_full, mla_g_kv=gkv_full, ffn_conv_w=conv_w_full,
                 ffn_conv_b=ffn_conv_b, final_g=final_g[None])

    loss, dx, dmods, G = _local_step(x[0], loss_target[0], mods, W, small)

    loss_row = jnp.pad(loss[:, :1], ((0, 0), (0, dmods.shape[1] - 1)))
    gathered_rows = _all_gather8(jnp.concatenate([dmods, loss_row]), name="gather_dmod")
    dmod_all = gathered_rows[:, :4]
    loss_total = jnp.sum(gathered_rows[:, 4, 0])
    chip = 2 * xi + yi
    dmod_cols = lax.dynamic_slice_in_dim(dmod_all, chip * n_ada, n_ada, axis=2)
    g_ada_w, g_ada_b = _ada_grad(c_all.T, jnp.transpose(dmod_cols, (1, 0, 2)))
    grads = dict(ada_w=g_ada_w.reshape(ada_w.shape), ada_b=g_ada_b.reshape(ada_b.shape))

    packed = _pack(G)
    recv = _exchange_halves(packed)
    part = _add_halves(packed, recv, ci.reshape(1).astype(jnp.int32))
    from_chips = _exchange_chips(part)
    half = _add_chips(from_chips)
    other = _share_halves(half)
    both = jnp.stack([jnp.where(ci == 0, half, other), jnp.where(ci == 0, other, half)])
    grads.update(_unpack(both, G))

    delta, new_m, new_v = {}, {}, {}
    for name in BIG + ["ada_w"]:
        shape = w[name].shape
        d_, m_, v_ = _adamw(_as2d(w[name]), _as2d(grads[name]), _as2d(m[name]), _as2d(v[name]), name=f"adamw_{name}")
        delta[name], new_m[name], new_v[name] = d_.reshape(shape), m_.reshape(shape), v_.reshape(shape)
        grads[name] = grads[name].reshape(shape)
    sizes = [math.prod(w[name].shape) for name in SMALL]
    total = sum(sizes)
    rows = -(-total // LANE)
    rows += -rows % 8

    def pack_small(d):
        flat = jnp.concatenate([d[name].reshape(-1) for name in SMALL])
        return jnp.pad(flat, (0, rows * LANE - total)).reshape(rows, LANE)

    outs = _adamw(pack_small(w), pack_small(grads), pack_small(m), pack_small(v), name="adamw_small")
    off = 0
    for name, size in zip(SMALL, sizes):
        shape = w[name].shape
        for dst, src in zip((delta, new_m, new_v), outs):
            dst[name] = src.reshape(-1)[off:off + size].reshape(shape)
        grads[name] = grads[name].reshape(shape)
        off += size

    return (loss_total, dx[None], *[grads[n] for n in WEIGHTS], *[delta[n] for n in WEIGHTS],
            *[new_m[n] for n in WEIGHTS], *[new_v[n] for n in WEIGHTS])
```

```python
import functools
import math

import numpy as np
import jax
import jax.numpy as jnp
from jax import lax
from jax.experimental import pallas as pl
from jax.experimental.pallas import tpu as pltpu

F32 = jnp.float32
BF16 = jnp.bfloat16
MESH = pl.DeviceIdType.MESH

D_MODEL = 1024
HEADS = 16
HEAD_PAD = 128
FOX_HEAD = 64
MLA_NOPE = 64
MLA_ROPE = 32
MLA_V = 64
MLA_QR = 384
MLA_KVR = 256
D_FF = 2816
NORM_EPS = 1e-6
ROPE_BASE = 10000.0
ADAM_LR = 0.001
ADAM_B1 = 0.9
ADAM_B2 = 0.999
ADAM_EPS = 1e-08
ADAM_WD = 0.01
ADAM_STEP = 10
FOX_SCALE = FOX_HEAD ** -0.5
MLA_SCALE = (MLA_NOPE + MLA_ROPE) ** -0.5
NEG_BIG = -1e30
VMEM_LIMIT = 56 * 1024 * 1024
LANE = 128
MM_ROWS, MM_COLS, MM_DEPTH = 1024, 1408, 1408
PACK_ROW = 1024
PACK_ROWS_ALIGN = 256

NT = (((1,), (1,)), ((), ()))
TN = (((0,), (0,)), ((), ()))
NN = (((1,), (0,)), ((), ()))


def _cparams():
    return pltpu.CompilerParams(vmem_limit_bytes=VMEM_LIMIT)


def _tile(n, cap):
    if n <= cap:
        return n
    for t in range(cap - cap % LANE, 0, -LANE):
        if n % t == 0:
            return t
    raise ValueError((n, cap))


def _row_tile(rows, cols, limit_bytes=1 << 20):
    best = None
    for t in range(8, rows + 1, 8):
        if rows % t == 0 and t * cols * 4 <= limit_bytes:
            best = t
    return best if best is not None else rows


def _me():
    return lax.axis_index("x"), lax.axis_index("y"), lax.axis_index("c")


def _all_gather8(v, name):
    R, N = v.shape

    def body(v_ref, o_ref, ssem, rsem):
        x, y, c = _me()
        me = 4 * x + 2 * y + c
        o_ref[me] = v_ref[...]
        copies = []
        for k in range(1, 8):
            peer = (1 - x if k & 4 else x, 1 - y if k & 2 else y, 1 - c if k & 1 else c)
            cp = pltpu.make_async_remote_copy(
                src_ref=v_ref, dst_ref=o_ref.at[me], send_sem=ssem.at[k - 1], recv_sem=rsem.at[k - 1],
                device_id=peer, device_id_type=MESH)
            cp.start()
            copies.append(cp)
        for cp in copies:
            cp.wait()

    return pl.pallas_call(
        body, name=name,
        out_shape=jax.ShapeDtypeStruct((8, R, N), v.dtype),
        in_specs=[pl.BlockSpec(memory_space=pltpu.VMEM)],
        out_specs=pl.BlockSpec(memory_space=pltpu.VMEM),
        scratch_shapes=[pltpu.SemaphoreType.DMA((7,)), pltpu.SemaphoreType.DMA((7,))],
    )(v)


def _all_gather_chips(shards, name):
    n = len(shards)

    def body(*refs):
        copies = _chip_gather_copies(refs[:n], refs[n:2 * n], *refs[2 * n:])
        for cp in copies:
            cp.start()
        for cp in copies:
            cp.wait()

    return pl.pallas_call(
        body, name=name,
        out_shape=[jax.ShapeDtypeStruct((4,) + s.shape, s.dtype) for s in shards],
        in_specs=[pl.BlockSpec(memory_space=pl.ANY)] * n,
        out_specs=[pl.BlockSpec(memory_space=pl.ANY)] * n,
        scratch_shapes=[pltpu.SemaphoreType.DMA((3 * n,)), pltpu.SemaphoreType.DMA((3 * n,)),
                        pltpu.SemaphoreType.DMA((n,))],
    )(*shards)


def _exchange_halves(g):
    _, _, R, P = g.shape

    def body(g_ref, o_ref, ssem, rsem):
        x, y, c = _me()
        copies = []
        for s in range(4):
            cp = pltpu.make_async_remote_copy(
                src_ref=g_ref.at[s, 1 - c], dst_ref=o_ref.at[s], send_sem=ssem.at[s], recv_sem=rsem.at[s],
                device_id=(x, y, 1 - c), device_id_type=MESH)
            cp.start()
            copies.append(cp)
        for cp in copies:
            cp.wait()

    return pl.pallas_call(
        body, name="grad_exchange_sibling",
        out_shape=jax.ShapeDtypeStruct((4, R, P), g.dtype),
        in_specs=[pl.BlockSpec(memory_space=pl.ANY)],
        out_specs=pl.BlockSpec(memory_space=pl.ANY),
        scratch_shapes=[pltpu.SemaphoreType.DMA((4,)), pltpu.SemaphoreType.DMA((4,))],
    )(g)


def _exchange_chips(a):
    _, R, P = a.shape

    def body(a_ref, o_ref, ssem, rsem, lsem):
        x, y, c = _me()
        me = 2 * x + y
        own = pltpu.make_async_copy(a_ref.at[me], o_ref.at[me], lsem)
        own.start()
        copies = []
        for k in (1, 2, 3):
            px, py = (1 - x if k & 2 else x), (1 - y if k & 1 else y)
            cp = pltpu.make_async_remote_copy(
                src_ref=a_ref.at[2 * px + py], dst_ref=o_ref.at[me], send_sem=ssem.at[k - 1],
                recv_sem=rsem.at[k - 1], device_id=(px, py, c), device_id_type=MESH)
            cp.start()
            copies.append(cp)
        own.wait()
        for cp in copies:
            cp.wait()

    return pl.pallas_call(
        body, name="grad_exchange_chips",
        out_shape=jax.ShapeDtypeStruct((4, R, P), a.dtype),
        in_specs=[pl.BlockSpec(memory_space=pl.ANY)],
        out_specs=pl.BlockSpec(memory_space=pl.ANY),
        scratch_shapes=[pltpu.SemaphoreType.DMA((3,)), pltpu.SemaphoreType.DMA((3,)), pltpu.SemaphoreType.DMA],
    )(a)


def _share_halves(f):
    R, P = f.shape

    def body(f_ref, o_ref, ssem, rsem):
        x, y, c = _me()
        cp = pltpu.make_async_remote_copy(
            src_ref=f_ref, dst_ref=o_ref, send_sem=ssem, recv_sem=rsem,
            device_id=(x, y, 1 - c), device_id_type=MESH)
        cp.start()
        cp.wait()

    return pl.pallas_call(
        body, name="grad_share_sibling",
        out_shape=jax.ShapeDtypeStruct((R, P), f.dtype),
        in_specs=[pl.BlockSpec(memory_space=pl.ANY)],
        out_specs=pl.BlockSpec(memory_space=pl.ANY),
        scratch_shapes=[pltpu.SemaphoreType.DMA, pltpu.SemaphoreType.DMA],
    )(f)


def _add_halves(g, r, c_idx):
    _, _, R, P = g.shape
    t = PACK_ROWS_ALIGN

    def body(c_ref, g_ref, r_ref, o_ref):
        o_ref[...] = (g_ref[0] + r_ref[...]).astype(BF16)

    return pl.pallas_call(
        body, name="grad_add_halves",
        out_shape=jax.ShapeDtypeStruct((4, R, P), BF16),
        grid_spec=pltpu.PrefetchScalarGridSpec(
            num_scalar_prefetch=1, grid=(4, R // t),
            in_specs=[pl.BlockSpec((1, 1, t, P), lambda s, i, c_ref: (s, c_ref[0], i, 0)),
                      pl.BlockSpec((1, t, P), lambda s, i, c_ref: (s, i, 0))],
            out_specs=pl.BlockSpec((1, t, P), lambda s, i, c_ref: (s, i, 0))),
        compiler_params=_cparams(),
    )(c_idx, g, r)


def _add_chips(b):
    _, R, P = b.shape
    t = PACK_ROWS_ALIGN

    def body(b_ref, o_ref):
        b0, b1, b2, b3 = (b_ref[k].astype(F32) for k in range(4))
        o_ref[...] = ((b0 + b1) + b2) + b3

    return pl.pallas_call(
        body, name="grad_add_chips",
        out_shape=jax.ShapeDtypeStruct((R, P), F32),
        grid=(R // t,),
        in_specs=[pl.BlockSpec((4, t, P), lambda i: (0, i, 0))],
        out_specs=pl.BlockSpec((t, P), lambda i: (i, 0)),
        compiler_params=_cparams(),
    )(b)


def _mm(a, b, *, ta=False, tb=False, out_dtype=F32, res=None, name):
    if ta:
        K, M = a.shape
    else:
        M, K = a.shape
    N = b.shape[0] if tb else b.shape[1]
    assert (b.shape[1] if tb else b.shape[0]) == K, (a.shape, b.shape, ta, tb)
    tm, tn, tk = _tile(M, MM_ROWS), _tile(N, MM_COLS), _tile(K, MM_DEPTH)
    nk = K // tk
    has_res = res is not None
    dims = (((0 if ta else 1,), (1 if tb else 0,)), ((), ()))

    def body(*refs):
        a_ref, b_ref = refs[:2]
        r_ref = refs[2] if has_res else None
        o_ref = refs[3] if has_res else refs[2]

        def finish(r):
            if has_res:
                r = r + r_ref[...]
            o_ref[...] = r.astype(out_dtype)

        prod = lax.dot_general(a_ref[...].astype(BF16), b_ref[...].astype(BF16), dims,
                               preferred_element_type=F32)
        if nk == 1:
            finish(prod)
            return
        acc_ref = refs[-1]
        k = pl.program_id(2)

        @pl.when(k == 0)
        def _():
            acc_ref[...] = prod

        @pl.when(k > 0)
        def _():
            acc_ref[...] += prod

        @pl.when(k == nk - 1)
        def _():
            finish(acc_ref[...])

    a_spec = (pl.BlockSpec((tk, tm), lambda j, i, k: (k, i)) if ta
              else pl.BlockSpec((tm, tk), lambda j, i, k: (i, k)))
    b_spec = (pl.BlockSpec((tn, tk), lambda j, i, k: (j, k)) if tb
              else pl.BlockSpec((tk, tn), lambda j, i, k: (k, j)))
    in_specs = [a_spec, b_spec]
    args = [a, b]
    if has_res:
        in_specs.append(pl.BlockSpec((tm, tn), lambda j, i, k: (i, j)))
        args.append(res)
    return pl.pallas_call(
        body, name=name,
        out_shape=jax.ShapeDtypeStruct((M, N), out_dtype),
        grid=(N // tn, M // tm, nk),
        in_specs=in_specs,
        out_specs=pl.BlockSpec((tm, tn), lambda j, i, k: (i, j)),
        scratch_shapes=[pltpu.VMEM((tm, tn), F32)] if nk > 1 else [],
        compiler_params=_cparams(),
    )(*args)


def _rstd(xv):
    return lax.rsqrt(jnp.mean(xv * xv, axis=-1, keepdims=True) + NORM_EPS)


def _norm_bwd(n, r, dn):
    return r * (dn - n * jnp.mean(dn * n, axis=-1, keepdims=True))


def _adaln_fwd(x, scale, shift, name):
    S, D = x.shape
    t = _tile(S, 512)

    def body(x_ref, sc_ref, sh_ref, h_ref):
        xv = x_ref[...]
        h_ref[...] = (xv * _rstd(xv) * (1.0 + sc_ref[...]) + sh_ref[...]).astype(BF16)

    row = pl.BlockSpec((t, D), lambda i: (i, 0))
    vec = pl.BlockSpec((1, D), lambda i: (0, 0))
    return pl.pallas_call(
        body, name=name, out_shape=jax.ShapeDtypeStruct((S, D), BF16), grid=(S // t,),
        in_specs=[row, vec, vec], out_specs=row, compiler_params=_cparams(),
    )(x, scale, shift)


def _adaln_bwd(x, dh, dx_next, scale, name):
    S, D = x.shape
    t = _tile(S, 512)

    def body(x_ref, dh_ref, dxn_ref, sc_ref, dx_ref, dsc_ref, dsh_ref):
        @pl.when(pl.program_id(0) == 0)
        def _():
            dsc_ref[...] = jnp.zeros_like(dsc_ref)
            dsh_ref[...] = jnp.zeros_like(dsh_ref)

        xv = x_ref[...]
        r = _rstd(xv)
        n = xv * r
        dh = dh_ref[...]
        dx_ref[...] = dxn_ref[...] + _norm_bwd(n, r, dh * (1.0 + sc_ref[...]))
        dsc_ref[...] += jnp.sum(dh * n, axis=0, keepdims=True)
        dsh_ref[...] += jnp.sum(dh, axis=0, keepdims=True)

    row = pl.BlockSpec((t, D), lambda i: (i, 0))
    vec = pl.BlockSpec((1, D), lambda i: (0, 0))
    return pl.pallas_call(
        body, name=name,
        out_shape=[jax.ShapeDtypeStruct((S, D), F32), jax.ShapeDtypeStruct((1, D), F32),
                   jax.ShapeDtypeStruct((1, D), F32)],
        grid=(S // t,), in_specs=[row, row, row, vec], out_specs=[row, vec, vec],
        compiler_params=_cparams(),
    )(x, dh, dx_next, scale)


def _residual(x, gate, y, name):
    S, D = x.shape
    t = _tile(S, 512)

    def body(x_ref, g_ref, y_ref, o_ref):
        o_ref[...] = x_ref[...] + g_ref[...] * y_ref[...]

    row = pl.BlockSpec((t, D), lambda i: (i, 0))
    vec = pl.BlockSpec((1, D), lambda i: (0, 0))
    return pl.pallas_call(
        body, name=name, out_shape=jax.ShapeDtypeStruct((S, D), F32), grid=(S // t,),
        in_specs=[row, vec, row], out_specs=row, compiler_params=_cparams(),
    )(x, gate, y)


def _gate_bwd(dx, y, gate, name):
    S, D = dx.shape
    t = _tile(S, 512)

    def body(dx_ref, y_ref, g_ref, dy_ref, dg_ref):
        @pl.when(pl.program_id(0) == 0)
        def _():
            dg_ref[...] = jnp.zeros_like(dg_ref)

        dxv = dx_ref[...]
        dy_ref[...] = (dxv * g_ref[...]).astype(BF16)
        dg_ref[...] += jnp.sum(dxv * y_ref[...], axis=0, keepdims=True)

    row = pl.BlockSpec((t, D), lambda i: (i, 0))
    vec = pl.BlockSpec((1, D), lambda i: (0, 0))
    return pl.pallas_call(
        body, name=name,
        out_shape=[jax.ShapeDtypeStruct((S, D), BF16), jax.ShapeDtypeStruct((1, D), F32)],
        grid=(S // t,), in_specs=[row, row, vec], out_specs=[row, vec], compiler_params=_cparams(),
    )(dx, y, gate)


def _final_loss(x, g, target):
    S, D = x.shape
    t = _tile(S, 512)

    def body(x_ref, g_ref, t_ref, loss_ref, dx_ref, dg_ref):
        @pl.when(pl.program_id(0) == 0)
        def _():
            loss_ref[...] = jnp.zeros_like(loss_ref)
            dg_ref[...] = jnp.zeros_like(dg_ref)

        xv = x_ref[...]
        gv = g_ref[...]
        r = _rstd(xv)
        n = xv * r
        err = n * gv - t_ref[...]
        part = jnp.sum(jnp.mean(err * err, axis=-1, keepdims=True), axis=0, keepdims=True)
        loss_ref[...] += jnp.broadcast_to(0.5 * part, loss_ref.shape)
        dy = err * (1.0 / D)
        dg_ref[...] += jnp.sum(dy * n, axis=0, keepdims=True)
        dx_ref[...] = _norm_bwd(n, r, dy * gv)

    row = pl.BlockSpec((t, D), lambda i: (i, 0))
    vec = pl.BlockSpec((1, D), lambda i: (0, 0))
    one = pl.BlockSpec((1, LANE), lambda i: (0, 0))
    return pl.pallas_call(
        body, name="final_loss",
        out_shape=[jax.ShapeDtypeStruct((1, LANE), F32), jax.ShapeDtypeStruct((S, D), F32),
                   jax.ShapeDtypeStruct((1, D), F32)],
        grid=(S // t,), in_specs=[row, vec, row], out_specs=[one, row, vec], compiler_params=_cparams(),
    )(x, g, target)


ATTN_TILE = 1024
ATTN_ROWS = 16
ONES_LANE = 127


def _causal_pairs(n, k_major):
    if k_major:
        pairs = [(i, j) for j in range(n) for i in range(j, n)]
    else:
        pairs = [(i, j) for i in range(n) for j in range(i + 1)]
    return (jnp.asarray(np.array([p[0] for p in pairs], np.int32)),
            jnp.asarray(np.array([p[1] for p in pairs], np.int32)))


def _chip_gather_copies(ins, outs, ssem, rsem, lsem):
    x, y, c = _me()
    me = 2 * x + y
    copies = []
    for i, (src, dst) in enumerate(zip(ins, outs)):
        copies.append(pltpu.make_async_copy(src, dst.at[me], lsem.at[i]))
        for k in (1, 2, 3):
            peer = (1 - x if k & 2 else x, 1 - y if k & 1 else y, c)
            copies.append(pltpu.make_async_remote_copy(
                src_ref=src, dst_ref=dst.at[me], send_sem=ssem.at[3 * i + k - 1],
                recv_sem=rsem.at[3 * i + k - 1], device_id=peer, device_id_type=MESH))
    return copies


def _attn_fwd(q_arr, q_off, k_arr, k_off, v_arr, v_off, name, gather=()):
    S = q_arr.shape[0]
    t = _tile(S, ATTN_TILE)
    n = S // t
    rc = min(ATTN_ROWS, t)
    ii, jj = _causal_pairs(n, k_major=False)
    ng = len(gather)
    last_step = ii.shape[0] - 1

    def body(ii_ref, jj_ref, q_ref, k_ref, v_ref, *rest):
        g_in, (o_ref, lse_ref), g_out = rest[:ng], rest[ng:ng + 2], rest[ng + 2:2 * ng + 2]
        s_s, m_s, a_s, acc_s = rest[2 * ng + 2:2 * ng + 6]
        step = pl.program_id(1)
        i, j = ii_ref[step], jj_ref[step]
        lane = lax.broadcasted_iota(jnp.int32, (t, HEAD_PAD), 1)
        if ng:
            copies = _chip_gather_copies(g_in, g_out, *rest[2 * ng + 6:])

            @pl.when(jnp.logical_and(pl.program_id(0) == 0, step == 0))
            def _():
                for cp in copies:
                    cp.start()

            @pl.when(jnp.logical_and(pl.program_id(0) == HEADS - 1, step == last_step))
            def _():
                for cp in copies:
                    cp.wait()

        @pl.when(j == 0)
        def _():
            m_s[...] = jnp.full_like(m_s, NEG_BIG)
            acc_s[...] = jnp.zeros_like(acc_s)

        def tile(diagonal):
            s_s[...] = lax.dot_general(q_ref[...], k_ref[...], NT, preferred_element_type=F32)
            for r in range(t // rc):
                rows = pl.ds(r * rc, rc)
                sc = s_s[rows, :]
                if diagonal:
                    qpos = r * rc + lax.broadcasted_iota(jnp.int32, (rc, t), 0)
                    kpos = lax.broadcasted_iota(jnp.int32, (rc, t), 1)
                    sc = jnp.where(qpos >= kpos, sc, NEG_BIG)
                m_old = m_s[rows, :]
                m_new = jnp.maximum(m_old, jnp.max(sc, axis=1, keepdims=True))
                s_s[rows, :] = jnp.exp(sc - m_new)
                a_s[rows, :] = jnp.exp(m_old - m_new)
                m_s[rows, :] = m_new
            v_ones = jnp.where(lane == ONES_LANE, 1.0, v_ref[...].astype(F32)).astype(BF16)
            acc_s[...] = a_s[...] * acc_s[...] + lax.dot_general(s_s[...].astype(BF16), v_ones, NN,
                                                                 preferred_element_type=F32)

        @pl.when(j < i)
        def _():
            tile(False)

        @pl.when(j == i)
        def _():
            tile(True)
            acc = acc_s[...]
            l = jnp.sum(jnp.where(lane == ONES_LANE, acc, 0.0), axis=1, keepdims=True)
            o_ref[...] = jnp.where(lane == ONES_LANE, 0.0, acc / l).astype(BF16)
            lse_ref[0] = m_s[...] + jnp.log(l)

    hbm = pl.BlockSpec(memory_space=pl.ANY)
    sems = [pltpu.SemaphoreType.DMA((3 * ng,)), pltpu.SemaphoreType.DMA((3 * ng,)),
            pltpu.SemaphoreType.DMA((ng,))] if ng else []
    grid_spec = pltpu.PrefetchScalarGridSpec(
        num_scalar_prefetch=2, grid=(HEADS, ii.shape[0]),
        in_specs=[pl.BlockSpec((t, HEAD_PAD), lambda h, s, ii, jj: (ii[s], q_off + h)),
                  pl.BlockSpec((t, HEAD_PAD), lambda h, s, ii, jj: (jj[s], k_off + h)),
                  pl.BlockSpec((t, HEAD_PAD), lambda h, s, ii, jj: (jj[s], v_off + h))] + [hbm] * ng,
        out_specs=[pl.BlockSpec((t, HEAD_PAD), lambda h, s, ii, jj: (ii[s], h)),
                   pl.BlockSpec((1, t, 1), lambda h, s, ii, jj: (h, ii[s], 0))] + [hbm] * ng,
        scratch_shapes=[pltpu.VMEM((t, t), F32), pltpu.VMEM((t, 1), F32), pltpu.VMEM((t, 1), F32),
                        pltpu.VMEM((t, HEAD_PAD), F32)] + sems)
    outs = pl.pallas_call(
        body, name=name, grid_spec=grid_spec,
        out_shape=[jax.ShapeDtypeStruct((S, HEADS * HEAD_PAD), BF16), jax.ShapeDtypeStruct((HEADS, S, 1), F32)]
        + [jax.ShapeDtypeStruct((4,) + g.shape, g.dtype) for g in gather],
        compiler_params=_cparams(),
    )(ii, jj, q_arr, k_arr, v_arr, *gather)
    return outs[0], outs[1], list(outs[2:])


def _attn_delta(do, o, name):
    S = do.shape[0]
    t = _tile(S, 512)

    def body(do_ref, o_ref, d_ref):
        lane = lax.broadcasted_iota(jnp.int32, (t, LANE), 1)
        acc = jnp.zeros((t, LANE), F32)
        for h in range(HEADS):
            cols = pl.ds(h * HEAD_PAD, HEAD_PAD)
            d = jnp.sum(do_ref[:, cols].astype(F32) * o_ref[:, cols].astype(F32), axis=1, keepdims=True)
            acc = jnp.where(lane == h, d, acc)
        d_ref[...] = acc

    blk = pl.BlockSpec((t, HEADS * HEAD_PAD), lambda i: (i, 0))
    by_lane = pl.pallas_call(
        body, name=name, out_shape=jax.ShapeDtypeStruct((S, LANE), F32), grid=(S // t,),
        in_specs=[blk, blk], out_specs=pl.BlockSpec((t, LANE), lambda i: (i, 0)),
        compiler_params=_cparams(),
    )(do, o)
    return by_lane[:, :HEADS].T.reshape(HEADS, 1, S)


def _attn_bwd(q_arr, q_off, k_arr, k_off, v_arr, v_off, do, lse, delta, name):
    S = q_arr.shape[0]
    t = _tile(S, ATTN_TILE)
    n = S // t
    rc = min(ATTN_ROWS, t)
    ii, jj = _causal_pairs(n, k_major=True)

    def body(ii_ref, jj_ref, q_ref, k_ref, v_ref, do_ref, lse_ref, dl_ref, dq_ref, dk_ref, dv_ref,
             s_s, dp_s, dk_s, dv_s):
        step = pl.program_id(1)
        i, j = ii_ref[step], jj_ref[step]

        @pl.when(step == 0)
        def _():
            dq_ref[...] = jnp.zeros_like(dq_ref)

        @pl.when(i == j)
        def _():
            dk_s[...] = jnp.zeros_like(dk_s)
            dv_s[...] = jnp.zeros_like(dv_s)

        def tile(diagonal):
            q, k, dov = q_ref[...], k_ref[...], do_ref[...]
            s_s[...] = lax.dot_general(k, q, NT, preferred_element_type=F32)
            dp_s[...] = lax.dot_general(v_ref[...], dov, NT, preferred_element_type=F32)
            lse_row, dl_row = lse_ref[0], dl_ref[0]
            for r in range(t // rc):
                rows = pl.ds(r * rc, rc)
                sc = s_s[rows, :]
                if diagonal:
                    kpos = r * rc + lax.broadcasted_iota(jnp.int32, (rc, t), 0)
                    qpos = lax.broadcasted_iota(jnp.int32, (rc, t), 1)
                    sc = jnp.where(qpos >= kpos, sc, NEG_BIG)
                p = jnp.exp(sc - lse_row)
                s_s[rows, :] = p
                dp_s[rows, :] = p * (dp_s[rows, :] - dl_row)
            ds = dp_s[...].astype(BF16)
            dv_s[...] += lax.dot_general(s_s[...].astype(BF16), dov, NN, preferred_element_type=F32)
            dk_s[...] += lax.dot_general(ds, q, NN, preferred_element_type=F32)
            rows = pl.ds(pl.multiple_of(i * t, t), t)
            dq_ref[rows, :] += lax.dot_general(ds, k, TN, preferred_element_type=F32)

        @pl.when(i > j)
        def _():
            tile(False)

        @pl.when(i == j)
        def _():
            tile(True)

        @pl.when(i == n - 1)
        def _():
            dk_ref[...] = dk_s[...]
            dv_ref[...] = dv_s[...].astype(BF16)

    qblk = lambda off: pl.BlockSpec((t, HEAD_PAD), lambda h, s, ii, jj: (ii[s], off + h))
    kblk = lambda off: pl.BlockSpec((t, HEAD_PAD), lambda h, s, ii, jj: (jj[s], off + h))
    stat = pl.BlockSpec((1, 1, t), lambda h, s, ii, jj: (h, 0, ii[s]))
    wide = (S, HEADS * HEAD_PAD)
    grid_spec = pltpu.PrefetchScalarGridSpec(
        num_scalar_prefetch=2, grid=(HEADS, ii.shape[0]),
        in_specs=[qblk(q_off), kblk(k_off), kblk(v_off), qblk(0), stat, stat],
        out_specs=[pl.BlockSpec((S, HEAD_PAD), lambda h, s, ii, jj: (0, h)), kblk(0), kblk(0)],
        scratch_shapes=[pltpu.VMEM((t, t), F32), pltpu.VMEM((t, t), F32), pltpu.VMEM((t, HEAD_PAD), F32),
                        pltpu.VMEM((t, HEAD_PAD), F32)])
    return pl.pallas_call(
        body, name=name, grid_spec=grid_spec,
        out_shape=[jax.ShapeDtypeStruct(wide, F32), jax.ShapeDtypeStruct(wide, F32),
                   jax.ShapeDtypeStruct(wide, BF16)],
        compiler_params=_cparams(),
    )(ii, jj, q_arr, k_arr, v_arr, do, lse, delta)


BIAS_LANE = FOX_HEAD


def _fox_bias_lanes(qkv, cum, name):
    S = qkv.shape[0]
    t = _tile(S, 256)
    wide = HEADS * HEAD_PAD

    def body(q_ref, k_ref, c_ref, qo_ref, ko_ref):
        lane = lax.broadcasted_iota(jnp.int32, (t, HEAD_PAD), 1)
        cum_all = c_ref[...]

        def fill(x, first, second):
            out = x.astype(F32)
            for n_, (a, b) in enumerate(zip(first, second)):
                out = jnp.where(lane == BIAS_LANE + n_, a, out)
                out = jnp.where(lane == BIAS_LANE + 3 + n_, b, out)
            return out.astype(BF16)

        ones = [1.0, 1.0, 1.0]
        for h in range(HEADS):
            cols = pl.ds(h * HEAD_PAD, HEAD_PAD)
            c = jnp.sum(jnp.where(lane == h, cum_all, 0.0), axis=1, keepdims=True)
            pieces = [x.astype(F32) for x in _split3(c)]
            qo_ref[:, cols] = fill(q_ref[:, cols], ones, pieces)
            ko_ref[:, cols] = fill(k_ref[:, cols], [-x for x in pieces], ones)

    blk = lambda off: pl.BlockSpec((t, wide), lambda i: (i, off))
    shp = jax.ShapeDtypeStruct((S, wide), BF16)
    return pl.pallas_call(
        body, name=name, out_shape=[shp, shp], grid=(S // t,),
        in_specs=[blk(0), blk(1), pl.BlockSpec((t, LANE), lambda i: (i, 0))],
        out_specs=[blk(0), blk(0)], compiler_params=_cparams(),
    )(qkv, qkv, cum)


def _split3(x):
    hi = x.astype(BF16)
    r1 = x - hi.astype(F32)
    mid = r1.astype(BF16)
    lo = (r1 - mid.astype(F32)).astype(BF16)
    return hi, mid, lo


def _tri_matmul(tri, x):
    acc = None
    for part in _split3(x):
        d = lax.dot_general(tri, part, NN, preferred_element_type=F32)
        acc = d if acc is None else acc + d
    return acc


def _log_sigmoid(z):
    return jnp.minimum(z, 0.0) - jnp.log1p(jnp.exp(-jnp.abs(z)))


def _fox_gate_fwd(fl, bf):
    S = fl.shape[0]
    t = _tile(S, 256)

    def body(fl_ref, bf_ref, cum_ref, carry):
        @pl.when(pl.program_id(0) == 0)
        def _():
            carry[...] = jnp.zeros_like(carry)

        logf = _log_sigmoid(fl_ref[...] + bf_ref[...])
        tri = (lax.broadcasted_iota(jnp.int32, (t, t), 0) >= lax.broadcasted_iota(jnp.int32, (t, t), 1)).astype(BF16)
        cum_ref[...] = _tri_matmul(tri, logf) + carry[...]
        carry[...] += jnp.sum(logf, axis=0, keepdims=True)

    row = pl.BlockSpec((t, LANE), lambda i: (i, 0))
    return pl.pallas_call(
        body, name="fox_gate_fwd", out_shape=jax.ShapeDtypeStruct((S, LANE), F32), grid=(S // t,),
        in_specs=[row, pl.BlockSpec((1, LANE), lambda i: (0, 0))], out_specs=row,
        scratch_shapes=[pltpu.VMEM((1, LANE), F32)], compiler_params=_cparams(),
    )(fl, bf)


def _fox_gate_bwd(dcum, fl, bf):
    S = fl.shape[0]
    t = _tile(S, 256)
    n = S // t

    def body(dc_ref, fl_ref, bf_ref, df_ref, db_ref, carry):
        @pl.when(pl.program_id(0) == 0)
        def _():
            carry[...] = jnp.zeros_like(carry)
            db_ref[...] = jnp.zeros_like(db_ref)

        dc = dc_ref[...]
        tri = (lax.broadcasted_iota(jnp.int32, (t, t), 1) >= lax.broadcasted_iota(jnp.int32, (t, t), 0)).astype(BF16)
        dlogf = _tri_matmul(tri, dc) + carry[...]
        carry[...] += jnp.sum(dc, axis=0, keepdims=True)
        z = fl_ref[...] + bf_ref[...]
        e = jnp.exp(-jnp.abs(z))
        sig_neg = jnp.where(z >= 0, e, 1.0) / (1.0 + e)
        lanes = lax.broadcasted_iota(jnp.int32, (t, LANE), 1)
        df = jnp.where(lanes < HEADS, dlogf * sig_neg, 0.0)
        df_ref[...] = df
        db_ref[...] += jnp.sum(df, axis=0, keepdims=True)

    row = pl.BlockSpec((t, LANE), lambda i: (n - 1 - i, 0))
    vec = pl.BlockSpec((1, LANE), lambda i: (0, 0))
    return pl.pallas_call(
        body, name="fox_gate_bwd",
        out_shape=[jax.ShapeDtypeStruct((S, LANE), F32), jax.ShapeDtypeStruct((1, LANE), F32)],
        grid=(n,), in_specs=[row, row, vec], out_specs=[row, vec],
        scratch_shapes=[pltpu.VMEM((1, LANE), F32)], compiler_params=_cparams(),
    )(dcum, fl, bf)


def _rope_tables(S):
    pos = jnp.arange(S, dtype=F32)
    inv_freq = ROPE_BASE ** (-jnp.arange(0, MLA_ROPE, 2, dtype=F32) / MLA_ROPE)
    ang = pos[:, None] * inv_freq[None, :]
    cos, sin = jnp.cos(ang), jnp.sin(ang)
    half = MLA_ROPE // 2
    ones = jnp.ones((S, MLA_NOPE), F32)
    z = lambda w: jnp.zeros((S, w), F32)
    keep = jnp.concatenate([ones, cos, cos, z(HEAD_PAD - MLA_NOPE - MLA_ROPE)], axis=1)
    from_above = jnp.concatenate([z(MLA_NOPE), -sin, z(HEAD_PAD - MLA_NOPE - half)], axis=1)
    from_below = jnp.concatenate([z(MLA_NOPE + half), sin, z(HEAD_PAD - MLA_NOPE - MLA_ROPE)], axis=1)
    return keep, from_above, from_below


def _rope(x, keep, up, down, transpose):
    half = MLA_ROPE // 2
    if transpose:
        return x * keep + pltpu.roll(x * up, half, 1) + pltpu.roll(x * down, HEAD_PAD - half, 1)
    return x * keep + pltpu.roll(x, HEAD_PAD - half, 1) * up + pltpu.roll(x, half, 1) * down


def _rope_heads(x, tables, transpose, scale, name):
    S = x.shape[0]
    t = _tile(S, 256)

    def body(x_ref, a_ref, b_ref, c_ref, o_ref):
        keep, up, down = a_ref[...], b_ref[...], c_ref[...]
        for h in range(HEADS):
            cols = pl.ds(h * HEAD_PAD, HEAD_PAD)
            o_ref[:, cols] = (_rope(x_ref[:, cols], keep, up, down, transpose) * scale).astype(BF16)

    blk = pl.BlockSpec((t, HEADS * HEAD_PAD), lambda i: (i, 0))
    tab = pl.BlockSpec((t, HEAD_PAD), lambda i: (i, 0))
    return pl.pallas_call(
        body, name=name, out_shape=jax.ShapeDtypeStruct(x.shape, BF16), grid=(S // t,),
        in_specs=[blk, tab, tab, tab], out_specs=blk, compiler_params=_cparams(),
    )(x, *tables)


MLA_A_PAD = MLA_QR + MLA_KVR + HEAD_PAD
MLA_KV_IN = MLA_KVR + HEAD_PAD


def _mla_mid_fwd(a, gq, gkv, tables):
    S = a.shape[0]
    t = _tile(S, 512)

    def body(a_ref, gq_ref, gkv_ref, ta_ref, tb_ref, tc_ref, q_ref, kv_ref):
        cq = a_ref[:, :MLA_QR]
        ckv = a_ref[:, MLA_QR:MLA_QR + MLA_KVR]
        kr = a_ref[:, MLA_QR + MLA_KVR:]
        q_ref[...] = (cq * _rstd(cq) * gq_ref[...]).astype(BF16)
        kv_ref[:, :MLA_KVR] = (ckv * _rstd(ckv) * gkv_ref[...]).astype(BF16)
        kv_ref[:, MLA_KVR:] = _rope(kr, ta_ref[...], tb_ref[...], tc_ref[...], False).astype(BF16)

    tab = pl.BlockSpec((t, HEAD_PAD), lambda i: (i, 0))
    return pl.pallas_call(
        body, name="mla_mid_fwd",
        out_shape=[jax.ShapeDtypeStruct((S, MLA_QR), BF16), jax.ShapeDtypeStruct((S, MLA_KV_IN), BF16)],
        grid=(S // t,),
        in_specs=[pl.BlockSpec((t, MLA_A_PAD), lambda i: (i, 0)), pl.BlockSpec((1, MLA_QR), lambda i: (0, 0)),
                  pl.BlockSpec((1, MLA_KVR), lambda i: (0, 0)), tab, tab, tab],
        out_specs=[pl.BlockSpec((t, MLA_QR), lambda i: (i, 0)), pl.BlockSpec((t, MLA_KV_IN), lambda i: (i, 0))],
        compiler_params=_cparams(),
    )(a, gq, gkv, *tables)


def _mla_mid_bwd(a, dq, dkv, gq, gkv, tables):
    S = a.shape[0]
    t = _tile(S, 512)

    def body(a_ref, dq_ref, dkv_ref, gq_ref, gkv_ref, ta_ref, tb_ref, tc_ref, da_ref, dgq_ref, dgkv_ref):
        @pl.when(pl.program_id(0) == 0)
        def _():
            dgq_ref[...] = jnp.zeros_like(dgq_ref)
            dgkv_ref[...] = jnp.zeros_like(dgkv_ref)

        def one(c, dout, g):
            r = _rstd(c)
            n = c * r
            return _norm_bwd(n, r, dout * g), jnp.sum(dout * n, axis=0, keepdims=True)

        dcq, dgq = one(a_ref[:, :MLA_QR], dq_ref[...], gq_ref[...])
        dckv, dgkv = one(a_ref[:, MLA_QR:MLA_QR + MLA_KVR], dkv_ref[:, :MLA_KVR], gkv_ref[...])
        da_ref[:, :MLA_QR] = dcq.astype(BF16)
        da_ref[:, MLA_QR:MLA_QR + MLA_KVR] = dckv.astype(BF16)
        da_ref[:, MLA_QR + MLA_KVR:] = _rope(dkv_ref[:, MLA_KVR:], ta_ref[...], tb_ref[...], tc_ref[...],
                                             True).astype(BF16)
        dgq_ref[...] += dgq
        dgkv_ref[...] += dgkv

    tab = pl.BlockSpec((t, HEAD_PAD), lambda i: (i, 0))
    vq = pl.BlockSpec((1, MLA_QR), lambda i: (0, 0))
    vkv = pl.BlockSpec((1, MLA_KVR), lambda i: (0, 0))
    return pl.pallas_call(
        body, name="mla_mid_bwd",
        out_shape=[jax.ShapeDtypeStruct((S, MLA_A_PAD), BF16), jax.ShapeDtypeStruct((1, MLA_QR), F32),
                   jax.ShapeDtypeStruct((1, MLA_KVR), F32)],
        grid=(S // t,),
        in_specs=[pl.BlockSpec((t, MLA_A_PAD), lambda i: (i, 0)), pl.BlockSpec((t, MLA_QR), lambda i: (i, 0)),
                  pl.BlockSpec((t, MLA_KV_IN), lambda i: (i, 0)), vq, vkv, tab, tab, tab],
        out_specs=[pl.BlockSpec((t, MLA_A_PAD), lambda i: (i, 0)), vq, vkv],
        compiler_params=_cparams(),
    )(a, dq, dkv, gq, gkv, *tables)


FFN_ROWS = 256
FFN_COLS = 1408
HALO = 8


def _shift_down(cur, halo, s, first):
    rolled = pltpu.roll(cur, s, 0)
    hrolled = jnp.where(first, 0.0, pltpu.roll(halo, s, 0))
    row = lax.broadcasted_iota(jnp.int32, (HALO, cur.shape[1]), 0)
    top = jnp.where(row < s, hrolled, rolled[:HALO])
    return jnp.concatenate([top, rolled[HALO:]], axis=0)


def _shift_up(cur, halo, s, last):
    rows = cur.shape[0]
    rolled = pltpu.roll(cur, rows - s, 0)
    hrolled = jnp.where(last, 0.0, pltpu.roll(halo, HALO - s, 0))
    row = lax.broadcasted_iota(jnp.int32, (HALO, cur.shape[1]), 0)
    bottom = jnp.where(row >= HALO - s, hrolled, rolled[rows - HALO:])
    return jnp.concatenate([rolled[:rows - HALO], bottom], axis=0)


def _conv(u, halo, w_ref, b, first):
    u1 = _shift_down(u, halo, 1, first)
    u2 = _shift_down(u, halo, 2, first)
    y = b + w_ref[0:1, :] * u2
    y = y + w_ref[1:2, :] * u1
    y = y + w_ref[2:3, :] * u
    return y, u1, u2


def _ffn_specs(S, tr, tn):
    per = tr // HALO
    tile = pl.BlockSpec((tr, tn), lambda j, i: (i, j))
    before = pl.BlockSpec((HALO, tn), lambda j, i: (jnp.maximum(i * per - 1, 0), j))
    after = pl.BlockSpec((HALO, tn), lambda j, i: (jnp.minimum((i + 1) * per, S // HALO - 1), j))
    w3 = pl.BlockSpec((3, tn), lambda j, i: (0, j))
    w1 = pl.BlockSpec((1, tn), lambda j, i: (0, j))
    return tile, before, after, w3, w1


def _ffn_act_fwd(ug, uv, wg, wv, bg, bv, name):
    S, F = ug.shape
    tr, tn = _tile(S, FFN_ROWS), _tile(F, FFN_COLS)
    tile, before, _, w3, w1 = _ffn_specs(S, tr, tn)

    def body(ug_ref, uv_ref, hg_ref, hv_ref, wg_ref, wv_ref, bg_ref, bv_ref, o_ref):
        first = pl.program_id(1) == 0
        g, _, _ = _conv(ug_ref[...], hg_ref[...], wg_ref, bg_ref[...], first)
        v, _, _ = _conv(uv_ref[...], hv_ref[...], wv_ref, bv_ref[...], first)
        o_ref[...] = (g * jax.nn.sigmoid(g) * v).astype(BF16)

    return pl.pallas_call(
        body, name=name, out_shape=jax.ShapeDtypeStruct((S, F), BF16), grid=(F // tn, S // tr),
        in_specs=[tile, tile, before, before, w3, w3, w1, w1], out_specs=tile, compiler_params=_cparams(),
    )(ug, uv, ug, uv, wg, wv, bg, bv)


def _gate_grads(g, v, da):
    sg = jax.nn.sigmoid(g)
    return da * v * (sg * (1.0 + g * (1.0 - sg))), da * (g * sg)


def _conv_rows_after(u_after, u_tile_end, w_ref, b):
    row = lax.broadcasted_iota(jnp.int32, u_after.shape, 0)
    shifted = lambda s: jnp.where(row < s, pltpu.roll(u_tile_end, s, 0), pltpu.roll(u_after, s, 0))
    return b + w_ref[0:1, :] * shifted(2) + w_ref[1:2, :] * shifted(1) + w_ref[2:3, :] * u_after


def _ffn_bwd_mid(ug, uv, dact, wg, wv, bg, bv, name):
    S, F = ug.shape
    tr, tn = _tile(S, FFN_ROWS), _tile(F, FFN_COLS)
    tile, before, after, w3, w1 = _ffn_specs(S, tr, tn)
    n = S // tr

    def body(ug_ref, uv_ref, hg_ref, hv_ref, ag_ref, av_ref, da_ref, ada_ref, wg_ref, wv_ref, bg_ref, bv_ref,
             dug_ref, duv_ref, dwg_ref, dwv_ref, dbg_ref, dbv_ref):
        first = pl.program_id(1) == 0
        last = pl.program_id(1) == n - 1

        @pl.when(first)
        def _():
            for r in (dwg_ref, dwv_ref, dbg_ref, dbv_ref):
                r[...] = jnp.zeros_like(r)

        ugv, uvv = ug_ref[...], uv_ref[...]
        g, ug1, ug2 = _conv(ugv, hg_ref[...], wg_ref, bg_ref[...], first)
        v, uv1, uv2 = _conv(uvv, hv_ref[...], wv_ref, bv_ref[...], first)
        dyg, dyv = _gate_grads(g, v, da_ref[...])
        g_after = _conv_rows_after(ag_ref[...], ugv[tr - HALO:], wg_ref, bg_ref[...])
        v_after = _conv_rows_after(av_ref[...], uvv[tr - HALO:], wv_ref, bv_ref[...])
        dyg_after, dyv_after = _gate_grads(g_after, v_after, ada_ref[...])

        def back(dy, dy_after, w_ref):
            d1 = _shift_up(dy, dy_after, 1, last)
            d2 = _shift_up(dy, dy_after, 2, last)
            return (w_ref[2:3, :] * dy + w_ref[1:2, :] * d1 + w_ref[0:1, :] * d2).astype(BF16)

        dug_ref[...] = back(dyg, dyg_after, wg_ref)
        duv_ref[...] = back(dyv, dyv_after, wv_ref)

        def add_taps(dw_ref, dy, taps):
            for k, u in enumerate(taps):
                dw_ref[k:k + 1, :] += jnp.sum(dy * u, axis=0, keepdims=True)

        add_taps(dwg_ref, dyg, (ug2, ug1, ugv))
        add_taps(dwv_ref, dyv, (uv2, uv1, uvv))
        dbg_ref[...] += jnp.sum(dyg, axis=0, keepdims=True)
        dbv_ref[...] += jnp.sum(dyv, axis=0, keepdims=True)

    big = jax.ShapeDtypeStruct((S, F), BF16)
    s3, s1 = jax.ShapeDtypeStruct((3, F), F32), jax.ShapeDtypeStruct((1, F), F32)
    return pl.pallas_call(
        body, name=name, out_shape=[big, big, s3, s3, s1, s1], grid=(F // tn, n),
        in_specs=[tile, tile, before, before, after, after, tile, after, w3, w3, w1, w1],
        out_specs=[tile, tile, w3, w3, w1, w1], compiler_params=_cparams(),
    )(ug, uv, ug, uv, ug, uv, dact, dact, wg, wv, bg, bv)


def _dot3(a, b):
    a_hi = a.astype(BF16)
    a_lo = (a - a_hi.astype(F32)).astype(BF16)
    b_hi = b.astype(BF16)
    b_lo = (b - b_hi.astype(F32)).astype(BF16)
    d = lambda p, q: lax.dot_general(p, q, NN, preferred_element_type=F32)
    return d(a_hi, b_hi) + (d(a_hi, b_lo) + d(a_lo, b_hi))


def _ada_mod(c_all, w, b):
    nmod, D, n = w.shape

    def body(c_ref, w_ref, b_ref, o_ref):
        cv = c_ref[...]
        o_ref[0] = _dot3(cv * jax.nn.sigmoid(cv), w_ref[0]) + b_ref[0]

    return pl.pallas_call(
        body, name="ada_mod", out_shape=jax.ShapeDtypeStruct((nmod, 8, n), F32), grid=(nmod,),
        in_specs=[pl.BlockSpec((8, D), lambda m: (0, 0)), pl.BlockSpec((1, D, n), lambda m: (m, 0, 0)),
                  pl.BlockSpec((1, 1, n), lambda m: (m, 0, 0))],
        out_specs=pl.BlockSpec((1, 8, n), lambda m: (m, 0, 0)), compiler_params=_cparams(),
    )(c_all, w, b)


def _ada_grad(c_all_t, dmod):
    D = c_all_t.shape[0]
    nmod, _, n = dmod.shape
    tr = 256

    def body(c_ref, d_ref, dw_ref, db_ref):
        cv = c_ref[...]
        sc = cv * jax.nn.sigmoid(cv)
        dv = d_ref[0]
        acc = sc[:, 0:1] * dv[0:1, :]
        tot = dv[0:1, :]
        for k in range(1, 8):
            acc = acc + sc[:, k:k + 1] * dv[k:k + 1, :]
            tot = tot + dv[k:k + 1, :]
        dw_ref[0] = acc
        db_ref[0] = tot

    return pl.pallas_call(
        body, name="ada_grad",
        out_shape=[jax.ShapeDtypeStruct((nmod, D, n), F32), jax.ShapeDtypeStruct((nmod, 1, n), F32)],
        grid=(nmod, D // tr),
        in_specs=[pl.BlockSpec((tr, 8), lambda m, i: (i, 0)), pl.BlockSpec((1, 8, n), lambda m, i: (m, 0, 0))],
        out_specs=[pl.BlockSpec((1, tr, n), lambda m, i: (m, i, 0)), pl.BlockSpec((1, 1, n), lambda m, i: (m, 0, 0))],
        compiler_params=_cparams(),
    )(c_all_t, dmod)


def _adamw(w, g, m, v, name):
    R, C = w.shape
    t = _row_tile(R, C)

    def body(w_ref, g_ref, m_ref, v_ref, d_ref, nm_ref, nv_ref):
        gv = g_ref[...]
        mn = ADAM_B1 * m_ref[...] + (1.0 - ADAM_B1) * gv
        vn = ADAM_B2 * v_ref[...] + (1.0 - ADAM_B2) * (gv * gv)
        m_hat = mn / (1.0 - ADAM_B1 ** ADAM_STEP)
        v_hat = vn / (1.0 - ADAM_B2 ** ADAM_STEP)
        d_ref[...] = -ADAM_LR * (m_hat / (jnp.sqrt(v_hat) + ADAM_EPS) + ADAM_WD * w_ref[...])
        nm_ref[...] = mn
        nv_ref[...] = vn

    blk = pl.BlockSpec((t, C), lambda i: (i, 0))
    shp = jax.ShapeDtypeStruct((R, C), F32)
    return pl.pallas_call(
        body, name=name, out_shape=[shp, shp, shp], grid=(R // t,), in_specs=[blk] * 4, out_specs=[blk] * 3,
        compiler_params=_cparams(),
    )(w, g, m, v)


def _cat_cols(g4):
    return jnp.concatenate([g4[s] for s in range(4)], axis=-1)


def _cat_rows(g4):
    return jnp.concatenate([g4[s] for s in range(4)], axis=-2)


def _pad_head_cols(w, width):
    K = w.shape[0]
    w = w.reshape(K, HEADS, width)
    return jnp.pad(w, ((0, 0), (0, 0), (0, HEAD_PAD - width))).reshape(K, HEADS * HEAD_PAD)


def _unpad_head_cols(w, width):
    K = w.shape[0]
    return w.reshape(K, HEADS, HEAD_PAD)[:, :, :width].reshape(K, HEADS * width)


def _pad_head_rows(w, width):
    N = w.shape[1]
    w = w.reshape(HEADS, width, N)
    return jnp.pad(w, ((0, 0), (0, HEAD_PAD - width), (0, 0))).reshape(HEADS * HEAD_PAD, N)


def _unpad_head_rows(w, width):
    N = w.shape[1]
    return w.reshape(HEADS, HEAD_PAD, N)[:, :width, :].reshape(HEADS * width, N)


def _prepare_fox_in(fox_in):
    W = {}
    w = _cat_cols(fox_in[:, 0])
    hw = HEADS * FOX_HEAD
    qkv = [_pad_head_cols(w[:, p * hw:(p + 1) * hw], FOX_HEAD) for p in range(3)]
    qkv[0] = qkv[0] * FOX_SCALE
    f = jnp.pad(w[:, 3 * hw:], ((0, 0), (0, LANE - HEADS)))
    W["fox_qkv"] = jnp.concatenate(qkv, axis=1)
    W["fox_f"] = f
    W["fox_all"] = jnp.concatenate(qkv + [f], axis=1)
    return W


def _prepare_weights(gathered):
    fox_o, mla_a, mla_uq, mla_ukv, mla_o, ffn_in, ffn_out = gathered
    W = {}
    W["fox_o"] = _pad_head_rows(_cat_rows(fox_o[:, 0]), FOX_HEAD)
    w = _cat_rows(mla_a[:, 0])
    lat = MLA_QR + MLA_KVR
    W["mla_a"] = jnp.concatenate(
        [w[:, :lat], jnp.zeros((D_MODEL, MLA_NOPE), BF16), w[:, lat:],
         jnp.zeros((D_MODEL, HEAD_PAD - MLA_NOPE - MLA_ROPE), BF16)], axis=1)
    W["mla_uq"] = _pad_head_cols(_cat_cols(mla_uq[:, 0]), MLA_NOPE + MLA_ROPE)
    w = _cat_cols(mla_ukv[:, 0]).reshape(MLA_KVR, HEADS, MLA_NOPE + MLA_V)
    kn = _pad_head_cols(w[:, :, :MLA_NOPE].reshape(MLA_KVR, -1), MLA_NOPE)
    vv = _pad_head_cols(w[:, :, MLA_NOPE:].reshape(MLA_KVR, -1), MLA_V)
    eye = np.zeros((HEAD_PAD, HEADS, HEAD_PAD), np.float32)
    for j in range(MLA_NOPE, MLA_NOPE + MLA_ROPE):
        eye[j, :, j] = 1.0
    eye = jnp.asarray(eye.reshape(HEAD_PAD, HEADS * HEAD_PAD), BF16)
    bottom = jnp.concatenate([eye, jnp.zeros((HEAD_PAD, HEADS * HEAD_PAD), BF16)], axis=1)
    W["mla_kv"] = jnp.concatenate([jnp.concatenate([kn, vv], axis=1), bottom], axis=0)
    W["mla_o"] = _pad_head_rows(_cat_rows(mla_o[:, 0]), MLA_V)
    w = _cat_cols(ffn_in)
    W["ffn_g"] = [w[i, :, :D_FF] for i in range(2)]
    W["ffn_v"] = [w[i, :, D_FF:] for i in range(2)]
    w = _cat_rows(ffn_out)
    W["ffn_out"] = [w[i] for i in range(2)]
    return W


def _ffn_forward(xin, mod, W, i, conv_w, conv_b):
    shift, scale, gate = mod
    h = _adaln_fwd(xin, scale, shift, name=f"ffn{i}_adaln")
    ug = _mm(h, W["ffn_g"][i], name=f"ffn{i}_up_gate")
    uv = _mm(h, W["ffn_v"][i], name=f"ffn{i}_up_val")
    wg, wv = conv_w[i][:, :D_FF], conv_w[i][:, D_FF:]
    bg, bv = conv_b[i][None, :D_FF], conv_b[i][None, D_FF:]
    act = _ffn_act_fwd(ug, uv, wg, wv, bg, bv, name=f"ffn{i}_act")
    y = _mm(act, W["ffn_out"][i], name=f"ffn{i}_down")
    xout = _residual(xin, gate, y, name=f"ffn{i}_residual")
    return xout, (xin, h, ug, uv, act, y, wg, wv, bg, bv)


def _ffn_backward(dx_out, saved, mod, W, i):
    xin, h, ug, uv, act, y, wg, wv, bg, bv = saved
    shift, scale, gate = mod
    dy, dgate = _gate_bwd(dx_out, y, gate, name=f"ffn{i}_gate_bwd")
    d_out = _mm(act, dy, ta=True, name=f"ffn{i}_dw_out")
    dact = _mm(dy, W["ffn_out"][i], tb=True, name=f"ffn{i}_dact")
    dug, duv, dwg, dwv, dbg, dbv = _ffn_bwd_mid(ug, uv, dact, wg, wv, bg, bv, name=f"ffn{i}_bwd_mid")
    d_in = jnp.concatenate([_mm(h, dug, ta=True, name=f"ffn{i}_dw_gate"),
                            _mm(h, duv, ta=True, name=f"ffn{i}_dw_val")], axis=1)
    dh = _mm(dug, W["ffn_g"][i], tb=True, name=f"ffn{i}_dh_gate")
    dh = _mm(duv, W["ffn_v"][i], tb=True, res=dh, name=f"ffn{i}_dh_val")
    dx, dscale, dshift = _adaln_bwd(xin, dh, dx_out, scale, name=f"ffn{i}_adaln_bwd")
    grads = dict(ffn_w_in=d_in, ffn_w_out=d_out, ffn_conv_w=jnp.concatenate([dwg, dwv], axis=1),
                 ffn_conv_b=jnp.concatenate([dbg, dbv], axis=1)[0])
    return dx, jnp.concatenate([dshift, dscale, dgate], axis=1)[0], grads


def _local_step(x, target, mods, W, small):
    S = x.shape[0]
    tables = _rope_tables(S)
    wide = HEADS * HEAD_PAD
    bf = jnp.pad(small["fox_b_f"], ((0, 0), (0, LANE - HEADS)))
    gq, gkv = small["mla_g_q"], small["mla_g_kv"]

    shift, scale, gate = mods[0]
    h0 = _adaln_fwd(x, scale, shift, name="fox_adaln")
    qkv = _mm(h0, W["fox_qkv"], out_dtype=BF16, name="fox_qkv")
    fl = _mm(h0, W["fox_f"], name="fox_gate_logits")
    cum = _fox_gate_fwd(fl, bf)
    q_fox, k_fox = _fox_bias_lanes(qkv, cum, name="fox_bias_lanes")
    o0, lse0, gathered = _attn_fwd(q_fox, 0, k_fox, 0, qkv, 2 * HEADS, name="fox_attn_fwd", gather=W["pending"])
    W = {**W, **_prepare_weights(gathered)}
    y0 = _mm(o0, W["fox_o"], name="fox_out")
    x1 = _residual(x, gate, y0, name="fox_residual")
    x2, ffn0 = _ffn_forward(x1, mods[1], W, 0, small["ffn_conv_w"], small["ffn_conv_b"])

    shift, scale, gate = mods[2]
    h2 = _adaln_fwd(x2, scale, shift, name="mla_adaln")
    a = _mm(h2, W["mla_a"], name="mla_down")
    cqn, ckv_in = _mla_mid_fwd(a, gq, gkv, tables)
    q_raw = _mm(cqn, W["mla_uq"], name="mla_up_q")
    q_cat = _rope_heads(q_raw, tables, False, MLA_SCALE, name="mla_rope_q")
    kv = _mm(ckv_in, W["mla_kv"], out_dtype=BF16, name="mla_up_kv")
    o1, lse1, _ = _attn_fwd(q_cat, 0, kv, 0, kv, HEADS, name="mla_attn_fwd")
    y2 = _mm(o1, W["mla_o"], name="mla_out")
    x3 = _residual(x2, gate, y2, name="mla_residual")
    x4, ffn1 = _ffn_forward(x3, mods[3], W, 1, small["ffn_conv_w"], small["ffn_conv_b"])

    loss, dx4, d_final_g = _final_loss(x4, small["final_g"], target)

    dx3, dmod3, g_ffn1 = _ffn_backward(dx4, ffn1, mods[3], W, 1)

    shift, scale, gate = mods[2]
    dy, dgate = _gate_bwd(dx3, y2, gate, name="mla_gate_bwd")
    d_mla_o = _mm(o1, dy, ta=True, name="mla_dw_out")
    do = _mm(dy, W["mla_o"], tb=True, out_dtype=BF16, name="mla_do")
    delta = _attn_delta(do, o1, name="mla_attn_delta")
    dq, dk, dv = _attn_bwd(q_cat, 0, kv, 0, kv, HEADS, do, lse1.reshape(HEADS, 1, S),
                           delta, name="mla_attn_bwd")
    dq_raw = _rope_heads(dq, tables, True, MLA_SCALE, name="mla_rope_q_bwd")
    d_mla_uq = _mm(cqn, dq_raw, ta=True, name="mla_dw_uq")
    dcqn = _mm(dq_raw, W["mla_uq"], tb=True, name="mla_dcq")
    dkv = jnp.concatenate([dk.astype(BF16), dv], axis=1)
    d_mla_kv = _mm(ckv_in, dkv, ta=True, name="mla_dw_kv")
    dckv_in = _mm(dkv, W["mla_kv"], tb=True, name="mla_dckv")
    da, dgq, dgkv = _mla_mid_bwd(a, dcqn, dckv_in, gq, gkv, tables)
    d_mla_a = _mm(h2, da, ta=True, name="mla_dw_a")
    dh = _mm(da, W["mla_a"], tb=True, name="mla_dh")
    dx2, dscale, dshift = _adaln_bwd(x2, dh, dx3, scale, name="mla_adaln_bwd")
    dmod2 = jnp.concatenate([dshift, dscale, dgate], axis=1)[0]

    dx1, dmod1, g_ffn0 = _ffn_backward(dx2, ffn0, mods[1], W, 0)

    shift, scale, gate = mods[0]
    dy, dgate = _gate_bwd(dx1, y0, gate, name="fox_gate_bwd")
    d_fox_o = _mm(o0, dy, ta=True, name="fox_dw_out")
    do = _mm(dy, W["fox_o"], tb=True, out_dtype=BF16, name="fox_do")
    delta = _attn_delta(do, o0, name="fox_attn_delta")
    dq, dk, dv = _attn_bwd(q_fox, 0, k_fox, 0, qkv, 2 * HEADS, do, lse0.reshape(HEADS, 1, S),
                           delta, name="fox_attn_bwd")
    dcq = dq.reshape(S, HEADS, HEAD_PAD)[:, :, BIAS_LANE + 3]
    dck = dk.reshape(S, HEADS, HEAD_PAD)[:, :, BIAS_LANE]
    dcum = jnp.pad(dcq - dck, ((0, 0), (0, LANE - HEADS)))
    dfl, dbf = _fox_gate_bwd(dcum, fl, bf)
    dproj = jnp.concatenate([dq.astype(BF16), dk.astype(BF16), dv, dfl.astype(BF16)], axis=1)
    d_fox_all = _mm(h0, dproj, ta=True, name="fox_dw_in")
    dh = _mm(dproj, W["fox_all"], tb=True, name="fox_dh")
    dx0, dscale, dshift = _adaln_bwd(x, dh, dx1, scale, name="fox_adaln_bwd")
    dmod0 = jnp.concatenate([dshift, dscale, dgate], axis=1)[0]

    G = {}
    d_qkv = [_unpad_head_cols(d_fox_all[:, p * wide:(p + 1) * wide], FOX_HEAD) for p in range(3)]
    d_qkv[0] = d_qkv[0] * FOX_SCALE
    G["fox_w_in"] = jnp.concatenate(d_qkv + [d_fox_all[:, 3 * wide:3 * wide + HEADS]], axis=1)[None]
    G["fox_b_f"] = dbf[:, :HEADS]
    G["fox_w_o"] = _unpad_head_rows(d_fox_o, FOX_HEAD)[None]
    lat = MLA_QR + MLA_KVR
    G["mla_w_a"] = jnp.concatenate([d_mla_a[:, :lat], d_mla_a[:, lat + MLA_NOPE:lat + MLA_NOPE + MLA_ROPE]],
                                   axis=1)[None]
    G["mla_g_q"] = dgq
    G["mla_g_kv"] = dgkv
    G["mla_w_uq"] = _unpad_head_cols(d_mla_uq, MLA_NOPE + MLA_ROPE)[None]
    dkn = d_mla_kv[:MLA_KVR, :wide].reshape(MLA_KVR, HEADS, HEAD_PAD)[:, :, :MLA_NOPE]
    dvv = d_mla_kv[:MLA_KVR, wide:].reshape(MLA_KVR, HEADS, HEAD_PAD)[:, :, :MLA_V]
    G["mla_w_ukv"] = jnp.concatenate([dkn, dvv], axis=2).reshape(MLA_KVR, -1)[None]
    G["mla_w_o"] = _unpad_head_rows(d_mla_o, MLA_V)[None]
    for name in ("ffn_w_in", "ffn_w_out", "ffn_conv_w", "ffn_conv_b"):
        G[name] = jnp.stack([g_ffn0[name], g_ffn1[name]])
    G["final_g"] = d_final_g[0]
    dmods = jnp.stack([dmod0, dmod1, dmod2, dmod3])
    return loss, dx0, dmods, G


PACKED = [("fox_w_in", 2), ("fox_w_o", 1), ("mla_w_a", 1), ("mla_w_uq", 2), ("mla_w_ukv", 2), ("mla_w_o", 1),
          ("ffn_w_in", 2), ("ffn_w_out", 1), ("fox_b_f", None), ("mla_g_q", 1), ("mla_g_kv", 1),
          ("ffn_conv_w", 2), ("ffn_conv_b", None), ("final_g", None)]


def _shard_of(g, axis, s):
    if axis is None:
        return g
    n = g.shape[axis] // 4
    return lax.slice_in_dim(g, s * n, (s + 1) * n, axis=axis)


def _pack_plan(G):
    shard_size = lambda name, axis: math.prod(_shard_of(G[name], axis, 0).shape)
    big = [(name, axis) for name, axis in PACKED if shard_size(name, axis) % PACK_ROW == 0]
    small = [(name, axis) for name, axis in PACKED if shard_size(name, axis) % PACK_ROW != 0]
    whole_tiles = lambda size: -(-size // (8 * PACK_ROW)) * 8
    runs = [([item], whole_tiles(shard_size(*item))) for item in big]
    runs.append((small, whole_tiles(sum(shard_size(*item) for item in small))))
    used = sum(rows for _, rows in runs)
    unit = 2 * PACK_ROWS_ALIGN
    return runs, -(-used // unit) * unit


def _pack(G):
    runs, total = _pack_plan(G)
    pieces = []
    for items, rows in runs:
        per_chip = [jnp.concatenate([_shard_of(G[name], axis, s).reshape(-1) for name, axis in items])
                    for s in range(4)]
        flat = jnp.stack(per_chip)
        flat = jnp.pad(flat, ((0, 0), (0, rows * PACK_ROW - flat.shape[1])))
        pieces.append(flat.reshape(4, rows, PACK_ROW))
    used = sum(rows for _, rows in runs)
    if total > used:
        pieces.append(jnp.zeros((4, total - used, PACK_ROW), F32))
    return jnp.concatenate(pieces, axis=1).reshape(4, 2, total // 2, PACK_ROW)


def _unpack(both, G_like):
    runs, total = _pack_plan(G_like)
    table = both.reshape(total, PACK_ROW)
    out, row = {}, 0
    for items, rows in runs:
        flat, off = table[row:row + rows].reshape(-1), 0
        for name, axis in items:
            shape = _shard_of(G_like[name], axis, 0).shape
            out[name] = flat[off:off + math.prod(shape)].reshape(shape)
            off += math.prod(shape)
        row += rows
    return out


WEIGHTS = ['ada_w', 'ada_b', 'fox_w_in', 'fox_b_f', 'fox_w_o', 'mla_w_a', 'mla_g_q', 'mla_g_kv', 'mla_w_uq',
           'mla_w_ukv', 'mla_w_o', 'ffn_w_in', 'ffn_conv_w', 'ffn_conv_b', 'ffn_w_out', 'final_g']
BIG = ['fox_w_in', 'fox_w_o', 'mla_w_a', 'mla_w_uq', 'mla_w_ukv', 'mla_w_o', 'ffn_w_in', 'ffn_w_out']
SMALL = ['ada_b', 'fox_b_f', 'mla_g_q', 'mla_g_kv', 'ffn_conv_w', 'ffn_conv_b', 'final_g']


def _as2d(a):
    return a.reshape(-1, a.shape[-1])


def kernel(x, c, ada_w, ada_b, fox_w_in, fox_b_f, fox_w_o, mla_w_a, mla_g_q, mla_g_kv, mla_w_uq, mla_w_ukv, mla_w_o, ffn_w_in, ffn_conv_w, ffn_conv_b, ffn_w_out, final_g, loss_target, m_ada_w, m_ada_b, m_fox_w_in, m_fox_b_f, m_fox_w_o, m_mla_w_a, m_mla_g_q, m_mla_g_kv, m_mla_w_uq, m_mla_w_ukv, m_mla_w_o, m_ffn_w_in, m_ffn_conv_w, m_ffn_conv_b, m_ffn_w_out, m_final_g, v_ada_w, v_ada_b, v_fox_w_in, v_fox_b_f, v_fox_w_o, v_mla_w_a, v_mla_g_q, v_mla_g_kv, v_mla_w_uq, v_mla_w_ukv, v_mla_w_o, v_ffn_w_in, v_ffn_conv_w, v_ffn_conv_b, v_ffn_w_out, v_final_g):
    w = dict(ada_w=ada_w, ada_b=ada_b, fox_w_in=fox_w_in, fox_b_f=fox_b_f, fox_w_o=fox_w_o, mla_w_a=mla_w_a,
             mla_g_q=mla_g_q, mla_g_kv=mla_g_kv, mla_w_uq=mla_w_uq, mla_w_ukv=mla_w_ukv, mla_w_o=mla_w_o,
             ffn_w_in=ffn_w_in, ffn_conv_w=ffn_conv_w, ffn_conv_b=ffn_conv_b, ffn_w_out=ffn_w_out, final_g=final_g)
    m = dict(ada_w=m_ada_w, ada_b=m_ada_b, fox_w_in=m_fox_w_in, fox_b_f=m_fox_b_f, fox_w_o=m_fox_w_o,
             mla_w_a=m_mla_w_a, mla_g_q=m_mla_g_q, mla_g_kv=m_mla_g_kv, mla_w_uq=m_mla_w_uq,
             mla_w_ukv=m_mla_w_ukv, mla_w_o=m_mla_w_o, ffn_w_in=m_ffn_w_in, ffn_conv_w=m_ffn_conv_w,
             ffn_conv_b=m_ffn_conv_b, ffn_w_out=m_ffn_w_out, final_g=m_final_g)
    v = dict(ada_w=v_ada_w, ada_b=v_ada_b, fox_w_in=v_fox_w_in, fox_b_f=v_fox_b_f, fox_w_o=v_fox_w_o,
             mla_w_a=v_mla_w_a, mla_g_q=v_mla_g_q, mla_g_kv=v_mla_g_kv, mla_w_uq=v_mla_w_uq,
             mla_w_ukv=v_mla_w_ukv, mla_w_o=v_mla_w_o, ffn_w_in=v_ffn_w_in, ffn_conv_w=v_ffn_conv_w,
             ffn_conv_b=v_ffn_conv_b, ffn_w_out=v_ffn_w_out, final_g=v_final_g)
    D = D_MODEL
    xi, yi, ci = lax.axis_index("x"), lax.axis_index("y"), lax.axis_index("c")
    dev = 4 * xi + 2 * yi + ci

    n_ada = ada_w.shape[-1]
    small_parts = [c.reshape(-1), ffn_conv_w.reshape(-1), mla_g_q.reshape(-1), mla_g_kv.reshape(-1)]
    sizes = [p.shape[0] for p in small_parts]
    flat = jnp.concatenate(small_parts)
    flat = jnp.pad(flat, (0, -flat.shape[0] % LANE))[None]
    got = _all_gather8(flat, name="gather_cond")[:, 0]
    offs = np.cumsum([0] + sizes)
    c_all = got[:, offs[0]:offs[1]]
    chips = got[0::2]
    conv_w_full = jnp.concatenate(
        [chips[s, offs[1]:offs[2]].reshape(ffn_conv_w.shape) for s in range(4)], axis=2)
    gq_full = jnp.concatenate([chips[s, offs[2]:offs[3]] for s in range(4)])[None]
    gkv_full = jnp.concatenate([chips[s, offs[3]:offs[4]] for s in range(4)])[None]

    mod_shard = _ada_mod(c_all, ada_w.reshape(4, D, n_ada), ada_b.reshape(4, 1, n_ada))
    mod_all = _all_gather8(mod_shard.reshape(4 * 8, n_ada), name="gather_mod")
    mod_all = mod_all[0::2].reshape(4, 4, 8, n_ada)
    mine = lax.dynamic_index_in_dim(mod_all, dev, axis=2, keepdims=False)
    mod_rows = jnp.transpose(mine, (1, 0, 2)).reshape(4, 4 * n_ada)
    mods = [(mod_rows[k:k + 1, :D], mod_rows[k:k + 1, D:2 * D], mod_rows[k:k + 1, 2 * D:]) for k in range(4)]

    shards = [w[name].astype(BF16) for name in BIG]
    W = _prepare_fox_in(_all_gather_chips(shards[:1], name="gather_weights")[0])
    W["pending"] = shards[1:]
    small = dict(fox_b_f=fox_b_f, mla_g_q=gq_full, mla_g_kv=gkv_full, ffn_conv_w=conv_w_full,
                 ffn_conv_b=ffn_conv_b, final_g=final_g[None])

    loss, dx, dmods, G = _local_step(x[0], loss_target[0], mods, W, small)

    loss_row = jnp.pad(loss[:, :1], ((0, 0), (0, dmods.shape[1] - 1)))
    gathered_rows = _all_gather8(jnp.concatenate([dmods, loss_row]), name="gather_dmod")
    dmod_all = gathered_rows[:, :4]
    loss_total = jnp.sum(gathered_rows[:, 4, 0])
    chip = 2 * xi + yi
    dmod_cols = lax.dynamic_slice_in_dim(dmod_all, chip * n_ada, n_ada, axis=2)
    g_ada_w, g_ada_b = _ada_grad(c_all.T, jnp.transpose(dmod_cols, (1, 0, 2)))
    grads = dict(ada_w=g_ada_w.reshape(ada_w.shape), ada_b=g_ada_b.reshape(ada_b.shape))

    packed = _pack(G)
    recv = _exchange_halves(packed)
    part = _add_halves(packed, recv, ci.reshape(1).astype(jnp.int32))
    from_chips = _exchange_chips(part)
    half = _add_chips(from_chips)
    other = _share_halves(half)
    both = jnp.stack([jnp.where(ci == 0, half, other), jnp.where(ci == 0, other, half)])
    grads.update(_unpack(both, G))

    delta, new_m, new_v = {}, {}, {}
    for name in BIG + ["ada_w"]:
        shape = w[name].shape
        d_, m_, v_ = _adamw(_as2d(w[name]), _as2d(grads[name]), _as2d(m[name]), _as2d(v[name]), name=f"adamw_{name}")
        delta[name], new_m[name], new_v[name] = d_.reshape(shape), m_.reshape(shape), v_.reshape(shape)
        grads[name] = grads[name].reshape(shape)
    sizes = [math.prod(w[name].shape) for name in SMALL]
    total = sum(sizes)
    rows = -(-total // LANE)
    rows += -rows % 8

    def pack_small(d):
        flat = jnp.concatenate([d[name].reshape(-1) for name in SMALL])
        return jnp.pad(flat, (0, rows * LANE - total)).reshape(rows, LANE)

    outs = _adamw(pack_small(w), pack_small(grads), pack_small(m), pack_small(v), name="adamw_small")
    off = 0
    for name, size in zip(SMALL, sizes):
        shape = w[name].shape
        for dst, src in zip((delta, new_m, new_v), outs):
            dst[name] = src.reshape(-1)[off:off + size].reshape(shape)
        grads[name] = grads[name].reshape(shape)
        off += size

    return (loss_total, dx[None], *[grads[n] for n in WEIGHTS], *[delta[n] for n in WEIGHTS],
            *[new_m[n] for n in WEIGHTS], *[new_v[n] for n in WEIGHTS])
```

```python
import functools
import math

import numpy as np
import jax
import jax.numpy as jnp
from jax import lax
from jax.experimental import pallas as pl
from jax.experimental.pallas import tpu as pltpu

F32 = jnp.float32
BF16 = jnp.bfloat16
MESH = pl.DeviceIdType.MESH

D_MODEL = 1024
HEADS = 16
HEAD_PAD = 128
FOX_HEAD = 64
MLA_NOPE = 64
MLA_ROPE = 32
MLA_V = 64
MLA_QR = 384
MLA_KVR = 256
D_FF = 2816
NORM_EPS = 1e-6
ROPE_BASE = 10000.0
ADAM_LR = 0.001
ADAM_B1 = 0.9
ADAM_B2 = 0.999
ADAM_EPS = 1e-08
ADAM_WD = 0.01
ADAM_STEP = 10
FOX_SCALE = FOX_HEAD ** -0.5
MLA_SCALE = (MLA_NOPE + MLA_ROPE) ** -0.5
NEG_BIG = -1e30
VMEM_LIMIT = 56 * 1024 * 1024
LANE = 128
MM_ROWS, MM_COLS, MM_DEPTH = 1408, 1408, 1408
PACK_ROW = 1024
PACK_ROWS_ALIGN = 256

NT = (((1,), (1,)), ((), ()))
TN = (((0,), (0,)), ((), ()))
NN = (((1,), (0,)), ((), ()))


def _cparams():
    return pltpu.CompilerParams(vmem_limit_bytes=VMEM_LIMIT)


def _tile(n, cap):
    if n <= cap:
        return n
    for t in range(cap - cap % LANE, 0, -LANE):
        if n % t == 0:
            return t
    raise ValueError((n, cap))


def _row_tile(rows, cols, limit_bytes=1 << 20):
    best = None
    for t in range(8, rows + 1, 8):
        if rows % t == 0 and t * cols * 4 <= limit_bytes:
            best = t
    return best if best is not None else rows


def _me():
    return lax.axis_index("x"), lax.axis_index("y"), lax.axis_index("c")


def _all_gather8(v, name):
    R, N = v.shape

    def body(v_ref, o_ref, ssem, rsem):
        x, y, c = _me()
        me = 4 * x + 2 * y + c
        o_ref[me] = v_ref[...]
        copies = []
        for k in range(1, 8):
            peer = (1 - x if k & 4 else x, 1 - y if k & 2 else y, 1 - c if k & 1 else c)
            cp = pltpu.make_async_remote_copy(
                src_ref=v_ref, dst_ref=o_ref.at[me], send_sem=ssem.at[k - 1], recv_sem=rsem.at[k - 1],
                device_id=peer, device_id_type=MESH)
            cp.start()
            copies.append(cp)
        for cp in copies:
            cp.wait()

    return pl.pallas_call(
        body, name=name,
        out_shape=jax.ShapeDtypeStruct((8, R, N), v.dtype),
        in_specs=[pl.BlockSpec(memory_space=pltpu.VMEM)],
        out_specs=pl.BlockSpec(memory_space=pltpu.VMEM),
        scratch_shapes=[pltpu.SemaphoreType.DMA((7,)), pltpu.SemaphoreType.DMA((7,))],
    )(v)


def _all_gather_chips(shards, name):
    n = len(shards)

    def body(*refs):
        copies = _chip_gather_copies(refs[:n], refs[n:2 * n], *refs[2 * n:])
        for cp in copies:
            cp.start()
        for cp in copies:
            cp.wait()

    return pl.pallas_call(
        body, name=name,
        out_shape=[jax.ShapeDtypeStruct((4,) + s.shape, s.dtype) for s in shards],
        in_specs=[pl.BlockSpec(memory_space=pl.ANY)] * n,
        out_specs=[pl.BlockSpec(memory_space=pl.ANY)] * n,
        scratch_shapes=[pltpu.SemaphoreType.DMA((3 * n,)), pltpu.SemaphoreType.DMA((3 * n,)),
                        pltpu.SemaphoreType.DMA((n,))],
    )(*shards)


def _exchange_halves(g):
    _, _, R, P = g.shape

    def body(g_ref, o_ref, ssem, rsem):
        x, y, c = _me()
        copies = []
        for s in range(4):
            cp = pltpu.make_async_remote_copy(
                src_ref=g_ref.at[s, 1 - c], dst_ref=o_ref.at[s], send_sem=ssem.at[s], recv_sem=rsem.at[s],
                device_id=(x, y, 1 - c), device_id_type=MESH)
            cp.start()
            copies.append(cp)
        for cp in copies:
            cp.wait()

    return pl.pallas_call(
        body, name="grad_exchange_sibling",
        out_shape=jax.ShapeDtypeStruct((4, R, P), g.dtype),
        in_specs=[pl.BlockSpec(memory_space=pl.ANY)],
        out_specs=pl.BlockSpec(memory_space=pl.ANY),
        scratch_shapes=[pltpu.SemaphoreType.DMA((4,)), pltpu.SemaphoreType.DMA((4,))],
    )(g)


def _exchange_chips(a):
    _, R, P = a.shape

    def body(a_ref, o_ref, ssem, rsem, lsem):
        x, y, c = _me()
        me = 2 * x + y
        own = pltpu.make_async_copy(a_ref.at[me], o_ref.at[me], lsem)
        own.start()
        copies = []
        for k in (1, 2, 3):
            px, py = (1 - x if k & 2 else x), (1 - y if k & 1 else y)
            cp = pltpu.make_async_remote_copy(
                src_ref=a_ref.at[2 * px + py], dst_ref=o_ref.at[me], send_sem=ssem.at[k - 1],
                recv_sem=rsem.at[k - 1], device_id=(px, py, c), device_id_type=MESH)
            cp.start()
            copies.append(cp)
        own.wait()
        for cp in copies:
            cp.wait()

    return pl.pallas_call(
        body, name="grad_exchange_chips",
        out_shape=jax.ShapeDtypeStruct((4, R, P), a.dtype),
        in_specs=[pl.BlockSpec(memory_space=pl.ANY)],
        out_specs=pl.BlockSpec(memory_space=pl.ANY),
        scratch_shapes=[pltpu.SemaphoreType.DMA((3,)), pltpu.SemaphoreType.DMA((3,)), pltpu.SemaphoreType.DMA],
    )(a)


def _share_halves(f):
    R, P = f.shape

    def body(f_ref, o_ref, ssem, rsem):
        x, y, c = _me()
        cp = pltpu.make_async_remote_copy(
            src_ref=f_ref, dst_ref=o_ref, send_sem=ssem, recv_sem=rsem,
            device_id=(x, y, 1 - c), device_id_type=MESH)
        cp.start()
        cp.wait()

    return pl.pallas_call(
        body, name="grad_share_sibling",
        out_shape=jax.ShapeDtypeStruct((R, P), f.dtype),
        in_specs=[pl.BlockSpec(memory_space=pl.ANY)],
        out_specs=pl.BlockSpec(memory_space=pl.ANY),
        scratch_shapes=[pltpu.SemaphoreType.DMA, pltpu.SemaphoreType.DMA],
    )(f)


def _add_halves(g, r, c_idx):
    _, _, R, P = g.shape
    t = PACK_ROWS_ALIGN

    def body(c_ref, g_ref, r_ref, o_ref):
        o_ref[...] = (g_ref[0] + r_ref[...]).astype(BF16)

    return pl.pallas_call(
        body, name="grad_add_halves",
        out_shape=jax.ShapeDtypeStruct((4, R, P), BF16),
        grid_spec=pltpu.PrefetchScalarGridSpec(
            num_scalar_prefetch=1, grid=(4, R // t),
            in_specs=[pl.BlockSpec((1, 1, t, P), lambda s, i, c_ref: (s, c_ref[0], i, 0)),
                      pl.BlockSpec((1, t, P), lambda s, i, c_ref: (s, i, 0))],
            out_specs=pl.BlockSpec((1, t, P), lambda s, i, c_ref: (s, i, 0))),
        compiler_params=_cparams(),
    )(c_idx, g, r)


def _add_chips(b):
    _, R, P = b.shape
    t = PACK_ROWS_ALIGN

    def body(b_ref, o_ref):
        b0, b1, b2, b3 = (b_ref[k].astype(F32) for k in range(4))
        o_ref[...] = ((b0 + b1) + b2) + b3

    return pl.pallas_call(
        body, name="grad_add_chips",
        out_shape=jax.ShapeDtypeStruct((R, P), F32),
        grid=(R // t,),
        in_specs=[pl.BlockSpec((4, t, P), lambda i: (0, i, 0))],
        out_specs=pl.BlockSpec((t, P), lambda i: (i, 0)),
        compiler_params=_cparams(),
    )(b)


def _mm(a, b, *, ta=False, tb=False, out_dtype=F32, res=None, name):
    if ta:
        K, M = a.shape
    else:
        M, K = a.shape
    N = b.shape[0] if tb else b.shape[1]
    assert (b.shape[1] if tb else b.shape[0]) == K, (a.shape, b.shape, ta, tb)
    tm, tn, tk = _tile(M, MM_ROWS), _tile(N, MM_COLS), _tile(K, MM_DEPTH)
    nk = K // tk
    has_res = res is not None
    dims = (((0 if ta else 1,), (1 if tb else 0,)), ((), ()))

    def body(*refs):
        a_ref, b_ref = refs[:2]
        r_ref = refs[2] if has_res else None
        o_ref = refs[3] if has_res else refs[2]

        def finish(r):
            if has_res:
                r = r + r_ref[...]
            o_ref[...] = r.astype(out_dtype)

        prod = lax.dot_general(a_ref[...].astype(BF16), b_ref[...].astype(BF16), dims,
                               preferred_element_type=F32)
        if nk == 1:
            finish(prod)
            return
        acc_ref = refs[-1]
        k = pl.program_id(2)

        @pl.when(k == 0)
        def _():
            acc_ref[...] = prod

        @pl.when(k > 0)
        def _():
            acc_ref[...] += prod

        @pl.when(k == nk - 1)
        def _():
            finish(acc_ref[...])

    a_spec = (pl.BlockSpec((tk, tm), lambda j, i, k: (k, i)) if ta
              else pl.BlockSpec((tm, tk), lambda j, i, k: (i, k)))
    b_spec = (pl.BlockSpec((tn, tk), lambda j, i, k: (j, k)) if tb
              else pl.BlockSpec((tk, tn), lambda j, i, k: (k, j)))
    in_specs = [a_spec, b_spec]
    args = [a, b]
    if has_res:
        in_specs.append(pl.BlockSpec((tm, tn), lambda j, i, k: (i, j)))
        args.append(res)
    return pl.pallas_call(
        body, name=name,
        out_shape=jax.ShapeDtypeStruct((M, N), out_dtype),
        grid=(N // tn, M // tm, nk),
        in_specs=in_specs,
        out_specs=pl.BlockSpec((tm, tn), lambda j, i, k: (i, j)),
        scratch_shapes=[pltpu.VMEM((tm, tn), F32)] if nk > 1 else [],
        compiler_params=_cparams(),
    )(*args)


def _rstd(xv):
    return lax.rsqrt(jnp.mean(xv * xv, axis=-1, keepdims=True) + NORM_EPS)


def _norm_bwd(n, r, dn):
    return r * (dn - n * jnp.mean(dn * n, axis=-1, keepdims=True))


def _adaln_fwd(x, scale, shift, name):
    S, D = x.shape
    t = _tile(S, 512)

    def body(x_ref, sc_ref, sh_ref, h_ref):
        xv = x_ref[...]
        h_ref[...] = (xv * _rstd(xv) * (1.0 + sc_ref[...]) + sh_ref[...]).astype(BF16)

    row = pl.BlockSpec((t, D), lambda i: (i, 0))
    vec = pl.BlockSpec((1, D), lambda i: (0, 0))
    return pl.pallas_call(
        body, name=name, out_shape=jax.ShapeDtypeStruct((S, D), BF16), grid=(S // t,),
        in_specs=[row, vec, vec], out_specs=row, compiler_params=_cparams(),
    )(x, scale, shift)


def _adaln_bwd(x, dh, dx_next, scale, name):
    S, D = x.shape
    t = _tile(S, 512)

    def body(x_ref, dh_ref, dxn_ref, sc_ref, dx_ref, dsc_ref, dsh_ref):
        @pl.when(pl.program_id(0) == 0)
        def _():
            dsc_ref[...] = jnp.zeros_like(dsc_ref)
            dsh_ref[...] = jnp.zeros_like(dsh_ref)

        xv = x_ref[...]
        r = _rstd(xv)
        n = xv * r
        dh = dh_ref[...]
        dx_ref[...] = dxn_ref[...] + _norm_bwd(n, r, dh * (1.0 + sc_ref[...]))
        dsc_ref[...] += jnp.sum(dh * n, axis=0, keepdims=True)
        dsh_ref[...] += jnp.sum(dh, axis=0, keepdims=True)

    row = pl.BlockSpec((t, D), lambda i: (i, 0))
    vec = pl.BlockSpec((1, D), lambda i: (0, 0))
    return pl.pallas_call(
        body, name=name,
        out_shape=[jax.ShapeDtypeStruct((S, D), F32), jax.ShapeDtypeStruct((1, D), F32),
                   jax.ShapeDtypeStruct((1, D), F32)],
        grid=(S // t,), in_specs=[row, row, row, vec], out_specs=[row, vec, vec],
        compiler_params=_cparams(),
    )(x, dh, dx_next, scale)


def _residual(x, gate, y, name):
    S, D = x.shape
    t = _tile(S, 512)

    def body(x_ref, g_ref, y_ref, o_ref):
        o_ref[...] = x_ref[...] + g_ref[...] * y_ref[...]

    row = pl.BlockSpec((t, D), lambda i: (i, 0))
    vec = pl.BlockSpec((1, D), lambda i: (0, 0))
    return pl.pallas_call(
        body, name=name, out_shape=jax.ShapeDtypeStruct((S, D), F32), grid=(S // t,),
        in_specs=[row, vec, row], out_specs=row, compiler_params=_cparams(),
    )(x, gate, y)


def _gate_bwd(dx, y, gate, name):
    S, D = dx.shape
    t = _tile(S, 512)

    def body(dx_ref, y_ref, g_ref, dy_ref, dg_ref):
        @pl.when(pl.program_id(0) == 0)
        def _():
            dg_ref[...] = jnp.zeros_like(dg_ref)

        dxv = dx_ref[...]
        dy_ref[...] = (dxv * g_ref[...]).astype(BF16)
        dg_ref[...] += jnp.sum(dxv * y_ref[...], axis=0, keepdims=True)

    row = pl.BlockSpec((t, D), lambda i: (i, 0))
    vec = pl.BlockSpec((1, D), lambda i: (0, 0))
    return pl.pallas_call(
        body, name=name,
        out_shape=[jax.ShapeDtypeStruct((S, D), BF16), jax.ShapeDtypeStruct((1, D), F32)],
        grid=(S // t,), in_specs=[row, row, vec], out_specs=[row, vec], compiler_params=_cparams(),
    )(dx, y, gate)


def _final_loss(x, g, target):
    S, D = x.shape
    t = _tile(S, 512)

    def body(x_ref, g_ref, t_ref, loss_ref, dx_ref, dg_ref):
        @pl.when(pl.program_id(0) == 0)
        def _():
            loss_ref[...] = jnp.zeros_like(loss_ref)
            dg_ref[...] = jnp.zeros_like(dg_ref)

        xv = x_ref[...]
        gv = g_ref[...]
        r = _rstd(xv)
        n = xv * r
        err = n * gv - t_ref[...]
        part = jnp.sum(jnp.mean(err * err, axis=-1, keepdims=True), axis=0, keepdims=True)
        loss_ref[...] += jnp.broadcast_to(0.5 * part, loss_ref.shape)
        dy = err * (1.0 / D)
        dg_ref[...] += jnp.sum(dy * n, axis=0, keepdims=True)
        dx_ref[...] = _norm_bwd(n, r, dy * gv)

    row = pl.BlockSpec((t, D), lambda i: (i, 0))
    vec = pl.BlockSpec((1, D), lambda i: (0, 0))
    one = pl.BlockSpec((1, LANE), lambda i: (0, 0))
    return pl.pallas_call(
        body, name="final_loss",
        out_shape=[jax.ShapeDtypeStruct((1, LANE), F32), jax.ShapeDtypeStruct((S, D), F32),
                   jax.ShapeDtypeStruct((1, D), F32)],
        grid=(S // t,), in_specs=[row, vec, row], out_specs=[one, row, vec], compiler_params=_cparams(),
    )(x, g, target)


ATTN_TILE = 1024
ATTN_ROWS = 16
ONES_LANE = 127


def _causal_pairs(n, k_major):
    if k_major:
        pairs = [(i, j) for j in range(n) for i in range(j, n)]
    else:
        pairs = [(i, j) for i in range(n) for j in range(i + 1)]
    return (jnp.asarray(np.array([p[0] for p in pairs], np.int32)),
            jnp.asarray(np.array([p[1] for p in pairs], np.int32)))


def _chip_gather_copies(ins, outs, ssem, rsem, lsem):
    x, y, c = _me()
    me = 2 * x + y
    copies = []
    for i, (src, dst) in enumerate(zip(ins, outs)):
        copies.append(pltpu.make_async_copy(src, dst.at[me], lsem.at[i]))
        for k in (1, 2, 3):
            peer = (1 - x if k & 2 else x, 1 - y if k & 1 else y, c)
            copies.append(pltpu.make_async_remote_copy(
                src_ref=src, dst_ref=dst.at[me], send_sem=ssem.at[3 * i + k - 1],
                recv_sem=rsem.at[3 * i + k - 1], device_id=peer, device_id_type=MESH))
    return copies


def _attn_fwd(q_arr, q_off, k_arr, k_off, v_arr, v_off, name, gather=()):
    S = q_arr.shape[0]
    t = _tile(S, ATTN_TILE)
    n = S // t
    rc = min(ATTN_ROWS, t)
    ii, jj = _causal_pairs(n, k_major=False)
    ng = len(gather)
    last_step = ii.shape[0] - 1

    def body(ii_ref, jj_ref, q_ref, k_ref, v_ref, *rest):
        g_in, (o_ref, lse_ref), g_out = rest[:ng], rest[ng:ng + 2], rest[ng + 2:2 * ng + 2]
        s_s, m_s, a_s, acc_s = rest[2 * ng + 2:2 * ng + 6]
        step = pl.program_id(1)
        i, j = ii_ref[step], jj_ref[step]
        lane = lax.broadcasted_iota(jnp.int32, (t, HEAD_PAD), 1)
        if ng:
            copies = _chip_gather_copies(g_in, g_out, *rest[2 * ng + 6:])

            @pl.when(jnp.logical_and(pl.program_id(0) == 0, step == 0))
            def _():
                for cp in copies:
                    cp.start()

            @pl.when(jnp.logical_and(pl.program_id(0) == HEADS - 1, step == last_step))
            def _():
                for cp in copies:
                    cp.wait()

        @pl.when(j == 0)
        def _():
            m_s[...] = jnp.full_like(m_s, NEG_BIG)
            acc_s[...] = jnp.zeros_like(acc_s)

        def tile(diagonal):
            s_s[...] = lax.dot_general(q_ref[...], k_ref[...], NT, preferred_element_type=F32)
            for r in range(t // rc):
                rows = pl.ds(r * rc, rc)
                sc = s_s[rows, :]
                if diagonal:
                    qpos = r * rc + lax.broadcasted_iota(jnp.int32, (rc, t), 0)
                    kpos = lax.broadcasted_iota(jnp.int32, (rc, t), 1)
                    sc = jnp.where(qpos >= kpos, sc, NEG_BIG)
                m_old = m_s[rows, :]
                m_new = jnp.maximum(m_old, jnp.max(sc, axis=1, keepdims=True))
                s_s[rows, :] = jnp.exp(sc - m_new)
                a_s[rows, :] = jnp.exp(m_old - m_new)
                m_s[rows, :] = m_new
            v_ones = jnp.where(lane == ONES_LANE, 1.0, v_ref[...].astype(F32)).astype(BF16)
            acc_s[...] = a_s[...] * acc_s[...] + lax.dot_general(s_s[...].astype(BF16), v_ones, NN,
                                                                 preferred_element_type=F32)

        @pl.when(j < i)
        def _():
            tile(False)

        @pl.when(j == i)
        def _():
            tile(True)
            acc = acc_s[...]
            l = jnp.sum(jnp.where(lane == ONES_LANE, acc, 0.0), axis=1, keepdims=True)
            o_ref[...] = jnp.where(lane == ONES_LANE, 0.0, acc / l).astype(BF16)
            lse_ref[0] = m_s[...] + jnp.log(l)

    hbm = pl.BlockSpec(memory_space=pl.ANY)
    sems = [pltpu.SemaphoreType.DMA((3 * ng,)), pltpu.SemaphoreType.DMA((3 * ng,)),
            pltpu.SemaphoreType.DMA((ng,))] if ng else []
    grid_spec = pltpu.PrefetchScalarGridSpec(
        num_scalar_prefetch=2, grid=(HEADS, ii.shape[0]),
        in_specs=[pl.BlockSpec((t, HEAD_PAD), lambda h, s, ii, jj: (ii[s], q_off + h)),
                  pl.BlockSpec((t, HEAD_PAD), lambda h, s, ii, jj: (jj[s], k_off + h)),
                  pl.BlockSpec((t, HEAD_PAD), lambda h, s, ii, jj: (jj[s], v_off + h))] + [hbm] * ng,
        out_specs=[pl.BlockSpec((t, HEAD_PAD), lambda h, s, ii, jj: (ii[s], h)),
                   pl.BlockSpec((1, t, 1), lambda h, s, ii, jj: (h, ii[s], 0))] + [hbm] * ng,
        scratch_shapes=[pltpu.VMEM((t, t), F32), pltpu.VMEM((t, 1), F32), pltpu.VMEM((t, 1), F32),
                        pltpu.VMEM((t, HEAD_PAD), F32)] + sems)
    outs = pl.pallas_call(
        body, name=name, grid_spec=grid_spec,
        out_shape=[jax.ShapeDtypeStruct((S, HEADS * HEAD_PAD), BF16), jax.ShapeDtypeStruct((HEADS, S, 1), F32)]
        + [jax.ShapeDtypeStruct((4,) + g.shape, g.dtype) for g in gather],
        compiler_params=_cparams(),
    )(ii, jj, q_arr, k_arr, v_arr, *gather)
    return outs[0], outs[1], list(outs[2:])


def _attn_delta(do, o, name):
    S = do.shape[0]
    t = _tile(S, 512)

    def body(do_ref, o_ref, d_ref):
        lane = lax.broadcasted_iota(jnp.int32, (t, LANE), 1)
        acc = jnp.zeros((t, LANE), F32)
        for h in range(HEADS):
            cols = pl.ds(h * HEAD_PAD, HEAD_PAD)
            d = jnp.sum(do_ref[:, cols].astype(F32) * o_ref[:, cols].astype(F32), axis=1, keepdims=True)
            acc = jnp.where(lane == h, d, acc)
        d_ref[...] = acc

    blk = pl.BlockSpec((t, HEADS * HEAD_PAD), lambda i: (i, 0))
    by_lane = pl.pallas_call(
        body, name=name, out_shape=jax.ShapeDtypeStruct((S, LANE), F32), grid=(S // t,),
        in_specs=[blk, blk], out_specs=pl.BlockSpec((t, LANE), lambda i: (i, 0)),
        compiler_params=_cparams(),
    )(do, o)
    return by_lane[:, :HEADS].T.reshape(HEADS, 1, S)


def _lane_as_row(x, lane):
    pick = (lax.broadcasted_iota(jnp.int32, (8, HEAD_PAD), 1) == lane).astype(BF16)
    acc = None
    for part in _split3(x):
        d = lax.dot_general(pick, part, NT, preferred_element_type=F32)
        acc = d if acc is None else acc + d
    return jnp.max(acc, axis=0, keepdims=True)


def _attn_bwd(q_arr, q_off, k_arr, k_off, v_arr, v_off, do, lse, delta, name, bias_grads=False, dq_dtype=F32):
    S = q_arr.shape[0]
    t = _tile(S, ATTN_TILE)
    n = S // t
    rc = min(ATTN_ROWS, t)
    ii, jj = _causal_pairs(n, k_major=True)
    last_step = ii.shape[0] - 1

    def body(ii_ref, jj_ref, q_ref, k_ref, v_ref, do_ref, lse_ref, dl_ref, dq_ref, dk_ref, dv_ref, *rest):
        if bias_grads:
            dcq_ref, dck_ref = rest[:2]
        s_s, dp_s, dk_s, dv_s, dq_s = rest[-5:]
        step = pl.program_id(1)
        i, j = ii_ref[step], jj_ref[step]

        @pl.when(step == 0)
        def _():
            dq_s[...] = jnp.zeros_like(dq_s)

        @pl.when(i == j)
        def _():
            dk_s[...] = jnp.zeros_like(dk_s)
            dv_s[...] = jnp.zeros_like(dv_s)

        def tile(diagonal):
            q, k, dov = q_ref[...], k_ref[...], do_ref[...]
            s_s[...] = lax.dot_general(k, q, NT, preferred_element_type=F32)
            dp_s[...] = lax.dot_general(v_ref[...], dov, NT, preferred_element_type=F32)
            lse_row, dl_row = lse_ref[0], dl_ref[0]
            for r in range(t // rc):
                rows = pl.ds(r * rc, rc)
                sc = s_s[rows, :]
                if diagonal:
                    kpos = r * rc + lax.broadcasted_iota(jnp.int32, (rc, t), 0)
                    qpos = lax.broadcasted_iota(jnp.int32, (rc, t), 1)
                    sc = jnp.where(qpos >= kpos, sc, NEG_BIG)
                p = jnp.exp(sc - lse_row)
                s_s[rows, :] = p
                dp_s[rows, :] = p * (dp_s[rows, :] - dl_row)
            ds = dp_s[...].astype(BF16)
            dv_s[...] += lax.dot_general(s_s[...].astype(BF16), dov, NN, preferred_element_type=F32)
            dk_s[...] += lax.dot_general(ds, q, NN, preferred_element_type=F32)
            rows = pl.ds(pl.multiple_of(i * t, t), t)
            dq_s[rows, :] += lax.dot_general(ds, k, TN, preferred_element_type=F32)

        @pl.when(i > j)
        def _():
            tile(False)

        @pl.when(i == j)
        def _():
            tile(True)

        @pl.when(i == n - 1)
        def _():
            dk = dk_s[...]
            dk_ref[...] = dk.astype(BF16)
            dv_ref[...] = dv_s[...].astype(BF16)
            if bias_grads:
                dck_ref[0] = _lane_as_row(dk, BIAS_LANE)

        @pl.when(step == last_step)
        def _():
            dq = dq_s[...]
            dq_ref[...] = dq.astype(dq_dtype)
            if bias_grads:
                dcq_ref[0] = _lane_as_row(dq, BIAS_LANE + 3)

    qblk = lambda off: pl.BlockSpec((t, HEAD_PAD), lambda h, s, ii, jj: (ii[s], off + h))
    kblk = lambda off: pl.BlockSpec((t, HEAD_PAD), lambda h, s, ii, jj: (jj[s], off + h))
    stat = pl.BlockSpec((1, 1, t), lambda h, s, ii, jj: (h, 0, ii[s]))
    wide = (S, HEADS * HEAD_PAD)
    out_specs = [pl.BlockSpec((S, HEAD_PAD), lambda h, s, ii, jj: (0, h)), kblk(0), kblk(0)]
    out_shape = [jax.ShapeDtypeStruct(wide, dq_dtype), jax.ShapeDtypeStruct(wide, BF16),
                 jax.ShapeDtypeStruct(wide, BF16)]
    if bias_grads:
        out_specs += [pl.BlockSpec((1, 1, S), lambda h, s, ii, jj: (h, 0, 0)),
                      pl.BlockSpec((1, 1, t), lambda h, s, ii, jj: (h, 0, jj[s]))]
        out_shape += [jax.ShapeDtypeStruct((HEADS, 1, S), F32)] * 2
    grid_spec = pltpu.PrefetchScalarGridSpec(
        num_scalar_prefetch=2, grid=(HEADS, ii.shape[0]),
        in_specs=[qblk(q_off), kblk(k_off), kblk(v_off), qblk(0), stat, stat],
        out_specs=out_specs,
        scratch_shapes=[pltpu.VMEM((t, t), F32), pltpu.VMEM((t, t), F32), pltpu.VMEM((t, HEAD_PAD), F32),
                        pltpu.VMEM((t, HEAD_PAD), F32), pltpu.VMEM((S, HEAD_PAD), F32)])
    return pl.pallas_call(
        body, name=name, grid_spec=grid_spec, out_shape=out_shape, compiler_params=_cparams(),
    )(ii, jj, q_arr, k_arr, v_arr, do, lse, delta)


BIAS_LANE = FOX_HEAD


def _fox_bias_lanes(qkv, cum, name):
    S = qkv.shape[0]
    t = _tile(S, 256)
    wide = HEADS * HEAD_PAD

    def body(q_ref, k_ref, c_ref, qo_ref, ko_ref):
        lane = lax.broadcasted_iota(jnp.int32, (t, HEAD_PAD), 1)
        cum_all = c_ref[...]

        def fill(x, first, second):
            out = x.astype(F32)
            for n_, (a, b) in enumerate(zip(first, second)):
                out = jnp.where(lane == BIAS_LANE + n_, a, out)
                out = jnp.where(lane == BIAS_LANE + 3 + n_, b, out)
            return out.astype(BF16)

        ones = [1.0, 1.0, 1.0]
        for h in range(HEADS):
            cols = pl.ds(h * HEAD_PAD, HEAD_PAD)
            c = jnp.sum(jnp.where(lane == h, cum_all, 0.0), axis=1, keepdims=True)
            pieces = [x.astype(F32) for x in _split3(c)]
            qo_ref[:, cols] = fill(q_ref[:, cols], ones, pieces)
            ko_ref[:, cols] = fill(k_ref[:, cols], [-x for x in pieces], ones)

    blk = lambda off: pl.BlockSpec((t, wide), lambda i: (i, off))
    shp = jax.ShapeDtypeStruct((S, wide), BF16)
    return pl.pallas_call(
        body, name=name, out_shape=[shp, shp], grid=(S // t,),
        in_specs=[blk(0), blk(1), pl.BlockSpec((t, LANE), lambda i: (i, 0))],
        out_specs=[blk(0), blk(0)], compiler_params=_cparams(),
    )(qkv, qkv, cum)


def _split3(x):
    hi = x.astype(BF16)
    r1 = x - hi.astype(F32)
    mid = r1.astype(BF16)
    lo = (r1 - mid.astype(F32)).astype(BF16)
    return hi, mid, lo


def _tri_matmul(tri, x):
    acc = None
    for part in _split3(x):
        d = lax.dot_general(tri, part, NN, preferred_element_type=F32)
        acc = d if acc is None else acc + d
    return acc


def _log_sigmoid(z):
    return jnp.minimum(z, 0.0) - jnp.log1p(jnp.exp(-jnp.abs(z)))


def _fox_gate_fwd(fl, bf):
    S = fl.shape[0]
    t = _tile(S, 256)

    def body(fl_ref, bf_ref, cum_ref, carry):
        @pl.when(pl.program_id(0) == 0)
        def _():
            carry[...] = jnp.zeros_like(carry)

        logf = _log_sigmoid(fl_ref[...] + bf_ref[...])
        tri = (lax.broadcasted_iota(jnp.int32, (t, t), 0) >= lax.broadcasted_iota(jnp.int32, (t, t), 1)).astype(BF16)
        cum_ref[...] = _tri_matmul(tri, logf) + carry[...]
        carry[...] += jnp.sum(logf, axis=0, keepdims=True)

    row = pl.BlockSpec((t, LANE), lambda i: (i, 0))
    return pl.pallas_call(
        body, name="fox_gate_fwd", out_shape=jax.ShapeDtypeStruct((S, LANE), F32), grid=(S // t,),
        in_specs=[row, pl.BlockSpec((1, LANE), lambda i: (0, 0))], out_specs=row,
        scratch_shapes=[pltpu.VMEM((1, LANE), F32)], compiler_params=_cparams(),
    )(fl, bf)


def _fox_gate_bwd(dcum, fl, bf):
    S = fl.shape[0]
    t = _tile(S, 256)
    n = S // t

    def body(dc_ref, fl_ref, bf_ref, df_ref, db_ref, carry):
        @pl.when(pl.program_id(0) == 0)
        def _():
            carry[...] = jnp.zeros_like(carry)
            db_ref[...] = jnp.zeros_like(db_ref)

        dc = dc_ref[...]
        tri = (lax.broadcasted_iota(jnp.int32, (t, t), 1) >= lax.broadcasted_iota(jnp.int32, (t, t), 0)).astype(BF16)
        dlogf = _tri_matmul(tri, dc) + carry[...]
        carry[...] += jnp.sum(dc, axis=0, keepdims=True)
        z = fl_ref[...] + bf_ref[...]
        e = jnp.exp(-jnp.abs(z))
        sig_neg = jnp.where(z >= 0, e, 1.0) / (1.0 + e)
        lanes = lax.broadcasted_iota(jnp.int32, (t, LANE), 1)
        df = jnp.where(lanes < HEADS, dlogf * sig_neg, 0.0)
        df_ref[...] = df
        db_ref[...] += jnp.sum(df, axis=0, keepdims=True)

    row = pl.BlockSpec((t, LANE), lambda i: (n - 1 - i, 0))
    vec = pl.BlockSpec((1, LANE), lambda i: (0, 0))
    return pl.pallas_call(
        body, name="fox_gate_bwd",
        out_shape=[jax.ShapeDtypeStruct((S, LANE), F32), jax.ShapeDtypeStruct((1, LANE), F32)],
        grid=(n,), in_specs=[row, row, vec], out_specs=[row, vec],
        scratch_shapes=[pltpu.VMEM((1, LANE), F32)], compiler_params=_cparams(),
    )(dcum, fl, bf)


def _rope_tables(S):
    pos = jnp.arange(S, dtype=F32)
    inv_freq = ROPE_BASE ** (-jnp.arange(0, MLA_ROPE, 2, dtype=F32) / MLA_ROPE)
    ang = pos[:, None] * inv_freq[None, :]
    cos, sin = jnp.cos(ang), jnp.sin(ang)
    half = MLA_ROPE // 2
    ones = jnp.ones((S, MLA_NOPE), F32)
    z = lambda w: jnp.zeros((S, w), F32)
    keep = jnp.concatenate([ones, cos, cos, z(HEAD_PAD - MLA_NOPE - MLA_ROPE)], axis=1)
    from_above = jnp.concatenate([z(MLA_NOPE), -sin, z(HEAD_PAD - MLA_NOPE - half)], axis=1)
    from_below = jnp.concatenate([z(MLA_NOPE + half), sin, z(HEAD_PAD - MLA_NOPE - MLA_ROPE)], axis=1)
    return keep, from_above, from_below


def _rope(x, keep, up, down, transpose):
    half = MLA_ROPE // 2
    if transpose:
        return x * keep + pltpu.roll(x * up, half, 1) + pltpu.roll(x * down, HEAD_PAD - half, 1)
    return x * keep + pltpu.roll(x, HEAD_PAD - half, 1) * up + pltpu.roll(x, half, 1) * down


def _rope_heads(x, tables, transpose, scale, name):
    S = x.shape[0]
    t = _tile(S, 256)

    def body(x_ref, a_ref, b_ref, c_ref, o_ref):
        keep, up, down = a_ref[...], b_ref[...], c_ref[...]
        for h in range(HEADS):
            cols = pl.ds(h * HEAD_PAD, HEAD_PAD)
            o_ref[:, cols] = (_rope(x_ref[:, cols], keep, up, down, transpose) * scale).astype(BF16)

    blk = pl.BlockSpec((t, HEADS * HEAD_PAD), lambda i: (i, 0))
    tab = pl.BlockSpec((t, HEAD_PAD), lambda i: (i, 0))
    return pl.pallas_call(
        body, name=name, out_shape=jax.ShapeDtypeStruct(x.shape, BF16), grid=(S // t,),
        in_specs=[blk, tab, tab, tab], out_specs=blk, compiler_params=_cparams(),
    )(x, *tables)


MLA_A_PAD = MLA_QR + MLA_KVR + HEAD_PAD
MLA_KV_IN = MLA_KVR + HEAD_PAD


def _mla_mid_fwd(a, gq, gkv, tables):
    S = a.shape[0]
    t = _tile(S, 512)

    def body(a_ref, gq_ref, gkv_ref, ta_ref, tb_ref, tc_ref, q_ref, kv_ref):
        cq = a_ref[:, :MLA_QR]
        ckv = a_ref[:, MLA_QR:MLA_QR + MLA_KVR]
        kr = a_ref[:, MLA_QR + MLA_KVR:]
        q_ref[...] = (cq * _rstd(cq) * gq_ref[...]).astype(BF16)
        kv_ref[:, :MLA_KVR] = (ckv * _rstd(ckv) * gkv_ref[...]).astype(BF16)
        kv_ref[:, MLA_KVR:] = _rope(kr, ta_ref[...], tb_ref[...], tc_ref[...], False).astype(BF16)

    tab = pl.BlockSpec((t, HEAD_PAD), lambda i: (i, 0))
    return pl.pallas_call(
        body, name="mla_mid_fwd",
        out_shape=[jax.ShapeDtypeStruct((S, MLA_QR), BF16), jax.ShapeDtypeStruct((S, MLA_KV_IN), BF16)],
        grid=(S // t,),
        in_specs=[pl.BlockSpec((t, MLA_A_PAD), lambda i: (i, 0)), pl.BlockSpec((1, MLA_QR), lambda i: (0, 0)),
                  pl.BlockSpec((1, MLA_KVR), lambda i: (0, 0)), tab, tab, tab],
        out_specs=[pl.BlockSpec((t, MLA_QR), lambda i: (i, 0)), pl.BlockSpec((t, MLA_KV_IN), lambda i: (i, 0))],
        compiler_params=_cparams(),
    )(a, gq, gkv, *tables)


def _mla_mid_bwd(a, dq, dkv, gq, gkv, tables):
    S = a.shape[0]
    t = _tile(S, 512)

    def body(a_ref, dq_ref, dkv_ref, gq_ref, gkv_ref, ta_ref, tb_ref, tc_ref, da_ref, dgq_ref, dgkv_ref):
        @pl.when(pl.program_id(0) == 0)
        def _():
            dgq_ref[...] = jnp.zeros_like(dgq_ref)
            dgkv_ref[...] = jnp.zeros_like(dgkv_ref)

        def one(c, dout, g):
            r = _rstd(c)
            n = c * r
            return _norm_bwd(n, r, dout * g), jnp.sum(dout * n, axis=0, keepdims=True)

        dcq, dgq = one(a_ref[:, :MLA_QR], dq_ref[...], gq_ref[...])
        dckv, dgkv = one(a_ref[:, MLA_QR:MLA_QR + MLA_KVR], dkv_ref[:, :MLA_KVR], gkv_ref[...])
        da_ref[:, :MLA_QR] = dcq.astype(BF16)
        da_ref[:, MLA_QR:MLA_QR + MLA_KVR] = dckv.astype(BF16)
        da_ref[:, MLA_QR + MLA_KVR:] = _rope(dkv_ref[:, MLA_KVR:], ta_ref[...], tb_ref[...], tc_ref[...],
                                             True).astype(BF16)
        dgq_ref[...] += dgq
        dgkv_ref[...] += dgkv

    tab = pl.BlockSpec((t, HEAD_PAD), lambda i: (i, 0))
    vq = pl.BlockSpec((1, MLA_QR), lambda i: (0, 0))
    vkv = pl.BlockSpec((1, MLA_KVR), lambda i: (0, 0))
    return pl.pallas_call(
        body, name="mla_mid_bwd",
        out_shape=[jax.ShapeDtypeStruct((S, MLA_A_PAD), BF16), jax.ShapeDtypeStruct((1, MLA_QR), F32),
                   jax.ShapeDtypeStruct((1, MLA_KVR), F32)],
        grid=(S // t,),
        in_specs=[pl.BlockSpec((t, MLA_A_PAD), lambda i: (i, 0)), pl.BlockSpec((t, MLA_QR), lambda i: (i, 0)),
                  pl.BlockSpec((t, MLA_KV_IN), lambda i: (i, 0)), vq, vkv, tab, tab, tab],
        out_specs=[pl.BlockSpec((t, MLA_A_PAD), lambda i: (i, 0)), vq, vkv],
        compiler_params=_cparams(),
    )(a, dq, dkv, gq, gkv, *tables)


FFN_ROWS = 256
FFN_COLS = 1408
HALO = 8


def _shift_down(cur, halo, s, first):
    rolled = pltpu.roll(cur, s, 0)
    hrolled = jnp.where(first, 0.0, pltpu.roll(halo, s, 0))
    row = lax.broadcasted_iota(jnp.int32, (HALO, cur.shape[1]), 0)
    top = jnp.where(row < s, hrolled, rolled[:HALO])
    return jnp.concatenate([top, rolled[HALO:]], axis=0)


def _shift_up(cur, halo, s, last):
    rows = cur.shape[0]
    rolled = pltpu.roll(cur, rows - s, 0)
    hrolled = jnp.where(last, 0.0, pltpu.roll(halo, HALO - s, 0))
    row = lax.broadcasted_iota(jnp.int32, (HALO, cur.shape[1]), 0)
    bottom = jnp.where(row >= HALO - s, hrolled, rolled[rows - HALO:])
    return jnp.concatenate([rolled[:rows - HALO], bottom], axis=0)


def _conv(u, halo, w_ref, b, first):
    u1 = _shift_down(u, halo, 1, first)
    u2 = _shift_down(u, halo, 2, first)
    y = b + w_ref[0:1, :] * u2
    y = y + w_ref[1:2, :] * u1
    y = y + w_ref[2:3, :] * u
    return y, u1, u2


def _ffn_specs(S, tr, tn):
    per = tr // HALO
    tile = pl.BlockSpec((tr, tn), lambda j, i: (i, j))
    before = pl.BlockSpec((HALO, tn), lambda j, i: (jnp.maximum(i * per - 1, 0), j))
    after = pl.BlockSpec((HALO, tn), lambda j, i: (jnp.minimum((i + 1) * per, S // HALO - 1), j))
    w3 = pl.BlockSpec((3, tn), lambda j, i: (0, j))
    w1 = pl.BlockSpec((1, tn), lambda j, i: (0, j))
    return tile, before, after, w3, w1


def _ffn_act_fwd(ug, uv, wg, wv, bg, bv, name):
    S, F = ug.shape
    tr, tn = _tile(S, FFN_ROWS), _tile(F, FFN_COLS)
    tile, before, _, w3, w1 = _ffn_specs(S, tr, tn)

    def body(ug_ref, uv_ref, hg_ref, hv_ref, wg_ref, wv_ref, bg_ref, bv_ref, o_ref):
        first = pl.program_id(1) == 0
        g, _, _ = _conv(ug_ref[...], hg_ref[...], wg_ref, bg_ref[...], first)
        v, _, _ = _conv(uv_ref[...], hv_ref[...], wv_ref, bv_ref[...], first)
        o_ref[...] = (g * jax.nn.sigmoid(g) * v).astype(BF16)

    return pl.pallas_call(
        body, name=name, out_shape=jax.ShapeDtypeStruct((S, F), BF16), grid=(F // tn, S // tr),
        in_specs=[tile, tile, before, before, w3, w3, w1, w1], out_specs=tile, compiler_params=_cparams(),
    )(ug, uv, ug, uv, wg, wv, bg, bv)


def _gate_grads(g, v, da):
    sg = jax.nn.sigmoid(g)
    return da * v * (sg * (1.0 + g * (1.0 - sg))), da * (g * sg)


def _conv_rows_after(u_after, u_tile_end, w_ref, b):
    row = lax.broadcasted_iota(jnp.int32, u_after.shape, 0)
    shifted = lambda s: jnp.where(row < s, pltpu.roll(u_tile_end, s, 0), pltpu.roll(u_after, s, 0))
    return b + w_ref[0:1, :] * shifted(2) + w_ref[1:2, :] * shifted(1) + w_ref[2:3, :] * u_after


def _ffn_bwd_mid(ug, uv, dact, wg, wv, bg, bv, name):
    S, F = ug.shape
    tr, tn = _tile(S, FFN_ROWS), _tile(F, FFN_COLS)
    tile, before, after, w3, w1 = _ffn_specs(S, tr, tn)
    n = S // tr

    def body(ug_ref, uv_ref, hg_ref, hv_ref, ag_ref, av_ref, da_ref, ada_ref, wg_ref, wv_ref, bg_ref, bv_ref,
             dug_ref, duv_ref, dwg_ref, dwv_ref, dbg_ref, dbv_ref):
        first = pl.program_id(1) == 0
        last = pl.program_id(1) == n - 1

        @pl.when(first)
        def _():
            for r in (dwg_ref, dwv_ref, dbg_ref, dbv_ref):
                r[...] = jnp.zeros_like(r)

        ugv, uvv = ug_ref[...], uv_ref[...]
        g, ug1, ug2 = _conv(ugv, hg_ref[...], wg_ref, bg_ref[...], first)
        v, uv1, uv2 = _conv(uvv, hv_ref[...], wv_ref, bv_ref[...], first)
        dyg, dyv = _gate_grads(g, v, da_ref[...])
        g_after = _conv_rows_after(ag_ref[...], ugv[tr - HALO:], wg_ref, bg_ref[...])
        v_after = _conv_rows_after(av_ref[...], uvv[tr - HALO:], wv_ref, bv_ref[...])
        dyg_after, dyv_after = _gate_grads(g_after, v_after, ada_ref[...])

        def back(dy, dy_after, w_ref):
            d1 = _shift_up(dy, dy_after, 1, last)
            d2 = _shift_up(dy, dy_after, 2, last)
            return (w_ref[2:3, :] * dy + w_ref[1:2, :] * d1 + w_ref[0:1, :] * d2).astype(BF16)

        dug_ref[...] = back(dyg, dyg_after, wg_ref)
        duv_ref[...] = back(dyv, dyv_after, wv_ref)

        def add_taps(dw_ref, dy, taps):
            for k, u in enumerate(taps):
                dw_ref[k:k + 1, :] += jnp.sum(dy * u, axis=0, keepdims=True)

        add_taps(dwg_ref, dyg, (ug2, ug1, ugv))
        add_taps(dwv_ref, dyv, (uv2, uv1, uvv))
        dbg_ref[...] += jnp.sum(dyg, axis=0, keepdims=True)
        dbv_ref[...] += jnp.sum(dyv, axis=0, keepdims=True)

    big = jax.ShapeDtypeStruct((S, F), BF16)
    s3, s1 = jax.ShapeDtypeStruct((3, F), F32), jax.ShapeDtypeStruct((1, F), F32)
    return pl.pallas_call(
        body, name=name, out_shape=[big, big, s3, s3, s1, s1], grid=(F // tn, n),
        in_specs=[tile, tile, before, before, after, after, tile, after, w3, w3, w1, w1],
        out_specs=[tile, tile, w3, w3, w1, w1], compiler_params=_cparams(),
    )(ug, uv, ug, uv, ug, uv, dact, dact, wg, wv, bg, bv)


def _dot3(a, b):
    a_hi = a.astype(BF16)
    a_lo = (a - a_hi.astype(F32)).astype(BF16)
    b_hi = b.astype(BF16)
    b_lo = (b - b_hi.astype(F32)).astype(BF16)
    d = lambda p, q: lax.dot_general(p, q, NN, preferred_element_type=F32)
    return d(a_hi, b_hi) + (d(a_hi, b_lo) + d(a_lo, b_hi))


def _ada_mod(c_all, w, b):
    nmod, D, n = w.shape

    def body(c_ref, w_ref, b_ref, o_ref):
        cv = c_ref[...]
        o_ref[0] = _dot3(cv * jax.nn.sigmoid(cv), w_ref[0]) + b_ref[0]

    return pl.pallas_call(
        body, name="ada_mod", out_shape=jax.ShapeDtypeStruct((nmod, 8, n), F32), grid=(nmod,),
        in_specs=[pl.BlockSpec((8, D), lambda m: (0, 0)), pl.BlockSpec((1, D, n), lambda m: (m, 0, 0)),
                  pl.BlockSpec((1, 1, n), lambda m: (m, 0, 0))],
        out_specs=pl.BlockSpec((1, 8, n), lambda m: (m, 0, 0)), compiler_params=_cparams(),
    )(c_all, w, b)


def _ada_grad(c_all_t, dmod):
    D = c_all_t.shape[0]
    nmod, _, n = dmod.shape
    tr = 256

    def body(c_ref, d_ref, dw_ref, db_ref):
        cv = c_ref[...]
        sc = cv * jax.nn.sigmoid(cv)
        dv = d_ref[0]
        acc = sc[:, 0:1] * dv[0:1, :]
        tot = dv[0:1, :]
        for k in range(1, 8):
            acc = acc + sc[:, k:k + 1] * dv[k:k + 1, :]
            tot = tot + dv[k:k + 1, :]
        dw_ref[0] = acc
        db_ref[0] = tot

    return pl.pallas_call(
        body, name="ada_grad",
        out_shape=[jax.ShapeDtypeStruct((nmod, D, n), F32), jax.ShapeDtypeStruct((nmod, 1, n), F32)],
        grid=(nmod, D // tr),
        in_specs=[pl.BlockSpec((tr, 8), lambda m, i: (i, 0)), pl.BlockSpec((1, 8, n), lambda m, i: (m, 0, 0))],
        out_specs=[pl.BlockSpec((1, tr, n), lambda m, i: (m, i, 0)), pl.BlockSpec((1, 1, n), lambda m, i: (m, 0, 0))],
        compiler_params=_cparams(),
    )(c_all_t, dmod)


def _adamw(w, g, m, v, name):
    R, C = w.shape
    t = _row_tile(R, C)

    def body(w_ref, g_ref, m_ref, v_ref, d_ref, nm_ref, nv_ref):
        gv = g_ref[...]
        mn = ADAM_B1 * m_ref[...] + (1.0 - ADAM_B1) * gv
        vn = ADAM_B2 * v_ref[...] + (1.0 - ADAM_B2) * (gv * gv)
        m_hat = mn / (1.0 - ADAM_B1 ** ADAM_STEP)
        v_hat = vn / (1.0 - ADAM_B2 ** ADAM_STEP)
        d_ref[...] = -ADAM_LR * (m_hat / (jnp.sqrt(v_hat) + ADAM_EPS) + ADAM_WD * w_ref[...])
        nm_ref[...] = mn
        nv_ref[...] = vn

    blk = pl.BlockSpec((t, C), lambda i: (i, 0))
    shp = jax.ShapeDtypeStruct((R, C), F32)
    return pl.pallas_call(
        body, name=name, out_shape=[shp, shp, shp], grid=(R // t,), in_specs=[blk] * 4, out_specs=[blk] * 3,
        compiler_params=_cparams(),
    )(w, g, m, v)


def _cat_cols(g4):
    return jnp.concatenate([g4[s] for s in range(4)], axis=-1)


def _cat_rows(g4):
    return jnp.concatenate([g4[s] for s in range(4)], axis=-2)


def _pad_head_cols(w, width):
    K = w.shape[0]
    w = w.reshape(K, HEADS, width)
    return jnp.pad(w, ((0, 0), (0, 0), (0, HEAD_PAD - width))).reshape(K, HEADS * HEAD_PAD)


def _unpad_head_cols(w, width):
    K = w.shape[0]
    return w.reshape(K, HEADS, HEAD_PAD)[:, :, :width].reshape(K, HEADS * width)


def _pad_head_rows(w, width):
    N = w.shape[1]
    w = w.reshape(HEADS, width, N)
    return jnp.pad(w, ((0, 0), (0, HEAD_PAD - width), (0, 0))).reshape(HEADS * HEAD_PAD, N)


def _unpad_head_rows(w, width):
    N = w.shape[1]
    return w.reshape(HEADS, HEAD_PAD, N)[:, :width, :].reshape(HEADS * width, N)


def _prepare_fox_in(fox_in):
    W = {}
    w = _cat_cols(fox_in[:, 0])
    hw = HEADS * FOX_HEAD
    qkv = [_pad_head_cols(w[:, p * hw:(p + 1) * hw], FOX_HEAD) for p in range(3)]
    qkv[0] = qkv[0] * FOX_SCALE
    f = jnp.pad(w[:, 3 * hw:], ((0, 0), (0, LANE - HEADS)))
    W["fox_qkv"] = jnp.concatenate(qkv, axis=1)
    W["fox_f"] = f
    W["fox_all"] = jnp.concatenate(qkv + [f], axis=1)
    return W


def _prepare_weights(gathered):
    fox_o, mla_a, mla_uq, mla_ukv, mla_o, ffn_in, ffn_out = gathered
    W = {}
    W["fox_o"] = _pad_head_rows(_cat_rows(fox_o[:, 0]), FOX_HEAD)
    w = _cat_rows(mla_a[:, 0])
    lat = MLA_QR + MLA_KVR
    W["mla_a"] = jnp.concatenate(
        [w[:, :lat], jnp.zeros((D_MODEL, MLA_NOPE), BF16), w[:, lat:],
         jnp.zeros((D_MODEL, HEAD_PAD - MLA_NOPE - MLA_ROPE), BF16)], axis=1)
    W["mla_uq"] = _pad_head_cols(_cat_cols(mla_uq[:, 0]), MLA_NOPE + MLA_ROPE)
    w = _cat_cols(mla_ukv[:, 0]).reshape(MLA_KVR, HEADS, MLA_NOPE + MLA_V)
    kn = _pad_head_cols(w[:, :, :MLA_NOPE].reshape(MLA_KVR, -1), MLA_NOPE)
    vv = _pad_head_cols(w[:, :, MLA_NOPE:].reshape(MLA_KVR, -1), MLA_V)
    eye = np.zeros((HEAD_PAD, HEADS, HEAD_PAD), np.float32)
    for j in range(MLA_NOPE, MLA_NOPE + MLA_ROPE):
        eye[j, :, j] = 1.0
    eye = jnp.asarray(eye.reshape(HEAD_PAD, HEADS * HEAD_PAD), BF16)
    bottom = jnp.concatenate([eye, jnp.zeros((HEAD_PAD, HEADS * HEAD_PAD), BF16)], axis=1)
    W["mla_kv"] = jnp.concatenate([jnp.concatenate([kn, vv], axis=1), bottom], axis=0)
    W["mla_o"] = _pad_head_rows(_cat_rows(mla_o[:, 0]), MLA_V)
    w = _cat_cols(ffn_in)
    W["ffn_g"] = [w[i, :, :D_FF] for i in range(2)]
    W["ffn_v"] = [w[i, :, D_FF:] for i in range(2)]
    w = _cat_rows(ffn_out)
    W["ffn_out"] = [w[i] for i in range(2)]
    return W


def _ffn_forward(xin, mod, W, i, conv_w, conv_b):
    shift, scale, gate = mod
    h = _adaln_fwd(xin, scale, shift, name=f"ffn{i}_adaln")
    ug = _mm(h, W["ffn_g"][i], name=f"ffn{i}_up_gate")
    uv = _mm(h, W["ffn_v"][i], name=f"ffn{i}_up_val")
    wg, wv = conv_w[i][:, :D_FF], conv_w[i][:, D_FF:]
    bg, bv = conv_b[i][None, :D_FF], conv_b[i][None, D_FF:]
    act = _ffn_act_fwd(ug, uv, wg, wv, bg, bv, name=f"ffn{i}_act")
    y = _mm(act, W["ffn_out"][i], name=f"ffn{i}_down")
    xout = _residual(xin, gate, y, name=f"ffn{i}_residual")
    return xout, (xin, h, ug, uv, act, y, wg, wv, bg, bv)


def _ffn_backward(dx_out, saved, mod, W, i):
    xin, h, ug, uv, act, y, wg, wv, bg, bv = saved
    shift, scale, gate = mod
    dy, dgate = _gate_bwd(dx_out, y, gate, name=f"ffn{i}_gate_bwd")
    d_out = _mm(act, dy, ta=True, name=f"ffn{i}_dw_out")
    dact = _mm(dy, W["ffn_out"][i], tb=True, name=f"ffn{i}_dact")
    dug, duv, dwg, dwv, dbg, dbv = _ffn_bwd_mid(ug, uv, dact, wg, wv, bg, bv, name=f"ffn{i}_bwd_mid")
    d_in = jnp.concatenate([_mm(h, dug, ta=True, name=f"ffn{i}_dw_gate"),
                            _mm(h, duv, ta=True, name=f"ffn{i}_dw_val")], axis=1)
    dh = _mm(dug, W["ffn_g"][i], tb=True, name=f"ffn{i}_dh_gate")
    dh = _mm(duv, W["ffn_v"][i], tb=True, res=dh, name=f"ffn{i}_dh_val")
    dx, dscale, dshift = _adaln_bwd(xin, dh, dx_out, scale, name=f"ffn{i}_adaln_bwd")
    grads = dict(ffn_w_in=d_in, ffn_w_out=d_out, ffn_conv_w=jnp.concatenate([dwg, dwv], axis=1),
                 ffn_conv_b=jnp.concatenate([dbg, dbv], axis=1)[0])
    return dx, jnp.concatenate([dshift, dscale, dgate], axis=1)[0], grads


def _local_step(x, target, mods, W, small):
    S = x.shape[0]
    tables = _rope_tables(S)
    wide = HEADS * HEAD_PAD
    bf = jnp.pad(small["fox_b_f"], ((0, 0), (0, LANE - HEADS)))
    gq, gkv = small["mla_g_q"], small["mla_g_kv"]

    shift, scale, gate = mods[0]
    h0 = _adaln_fwd(x, scale, shift, name="fox_adaln")
    qkv = _mm(h0, W["fox_qkv"], out_dtype=BF16, name="fox_qkv")
    fl = _mm(h0, W["fox_f"], name="fox_gate_logits")
    cum = _fox_gate_fwd(fl, bf)
    q_fox, k_fox = _fox_bias_lanes(qkv, cum, name="fox_bias_lanes")
    o0, lse0, gathered = _attn_fwd(q_fox, 0, k_fox, 0, qkv, 2 * HEADS, name="fox_attn_fwd", gather=W["pending"])
    W = {**W, **_prepare_weights(gathered)}
    y0 = _mm(o0, W["fox_o"], name="fox_out")
    x1 = _residual(x, gate, y0, name="fox_residual")
    x2, ffn0 = _ffn_forward(x1, mods[1], W, 0, small["ffn_conv_w"], small["ffn_conv_b"])

    shift, scale, gate = mods[2]
    h2 = _adaln_fwd(x2, scale, shift, name="mla_adaln")
    a = _mm(h2, W["mla_a"], name="mla_down")
    cqn, ckv_in = _mla_mid_fwd(a, gq, gkv, tables)
    q_raw = _mm(cqn, W["mla_uq"], name="mla_up_q")
    q_cat = _rope_heads(q_raw, tables, False, MLA_SCALE, name="mla_rope_q")
    kv = _mm(ckv_in, W["mla_kv"], out_dtype=BF16, name="mla_up_kv")
    o1, lse1, _ = _attn_fwd(q_cat, 0, kv, 0, kv, HEADS, name="mla_attn_fwd")
    y2 = _mm(o1, W["mla_o"], name="mla_out")
    x3 = _residual(x2, gate, y2, name="mla_residual")
    x4, ffn1 = _ffn_forward(x3, mods[3], W, 1, small["ffn_conv_w"], small["ffn_conv_b"])

    loss, dx4, d_final_g = _final_loss(x4, small["final_g"], target)

    dx3, dmod3, g_ffn1 = _ffn_backward(dx4, ffn1, mods[3], W, 1)

    shift, scale, gate = mods[2]
    dy, dgate = _gate_bwd(dx3, y2, gate, name="mla_gate_bwd")
    d_mla_o = _mm(o1, dy, ta=True, name="mla_dw_out")
    do = _mm(dy, W["mla_o"], tb=True, out_dtype=BF16, name="mla_do")
    delta = _attn_delta(do, o1, name="mla_attn_delta")
    dq, dk, dv = _attn_bwd(q_cat, 0, kv, 0, kv, HEADS, do, lse1.reshape(HEADS, 1, S),
                           delta, name="mla_attn_bwd")
    dq_raw = _rope_heads(dq, tables, True, MLA_SCALE, name="mla_rope_q_bwd")
    d_mla_uq = _mm(cqn, dq_raw, ta=True, name="mla_dw_uq")
    dcqn = _mm(dq_raw, W["mla_uq"], tb=True, name="mla_dcq")
    dkv = jnp.concatenate([dk, dv], axis=1)
    d_mla_kv = _mm(ckv_in, dkv, ta=True, name="mla_dw_kv")
    dckv_in = _mm(dkv, W["mla_kv"], tb=True, name="mla_dckv")
    da, dgq, dgkv = _mla_mid_bwd(a, dcqn, dckv_in, gq, gkv, tables)
    d_mla_a = _mm(h2, da, ta=True, name="mla_dw_a")
    dh = _mm(da, W["mla_a"], tb=True, name="mla_dh")
    dx2, dscale, dshift = _adaln_bwd(x2, dh, dx3, scale, name="mla_adaln_bwd")
    dmod2 = jnp.concatenate([dshift, dscale, dgate], axis=1)[0]

    dx1, dmod1, g_ffn0 = _ffn_backward(dx2, ffn0, mods[1], W, 0)

    shift, scale, gate = mods[0]
    dy, dgate = _gate_bwd(dx1, y0, gate, name="fox_gate_bwd")
    d_fox_o = _mm(o0, dy, ta=True, name="fox_dw_out")
    do = _mm(dy, W["fox_o"], tb=True, out_dtype=BF16, name="fox_do")
    delta = _attn_delta(do, o0, name="fox_attn_delta")
    dq, dk, dv, dcq, dck = _attn_bwd(q_fox, 0, k_fox, 0, qkv, 2 * HEADS, do, lse0.reshape(HEADS, 1, S),
                                     delta, name="fox_attn_bwd", bias_grads=True, dq_dtype=BF16)
    dcum = jnp.pad((dcq[:, 0, :] - dck[:, 0, :]).T, ((0, 0), (0, LANE - HEADS)))
    dfl, dbf = _fox_gate_bwd(dcum, fl, bf)
    dproj = jnp.concatenate([dq, dk, dv, dfl.astype(BF16)], axis=1)
    d_fox_all = _mm(h0, dproj, ta=True, name="fox_dw_in")
    dh = _mm(dproj, W["fox_all"], tb=True, name="fox_dh")
    dx0, dscale, dshift = _adaln_bwd(x, dh, dx1, scale, name="fox_adaln_bwd")
    dmod0 = jnp.concatenate([dshift, dscale, dgate], axis=1)[0]

    G = {}
    d_qkv = [_unpad_head_cols(d_fox_all[:, p * wide:(p + 1) * wide], FOX_HEAD) for p in range(3)]
    d_qkv[0] = d_qkv[0] * FOX_SCALE
    G["fox_w_in"] = jnp.concatenate(d_qkv + [d_fox_all[:, 3 * wide:3 * wide + HEADS]], axis=1)[None]
    G["fox_b_f"] = dbf[:, :HEADS]
    G["fox_w_o"] = _unpad_head_rows(d_fox_o, FOX_HEAD)[None]
    lat = MLA_QR + MLA_KVR
    G["mla_w_a"] = jnp.concatenate([d_mla_a[:, :lat], d_mla_a[:, lat + MLA_NOPE:lat + MLA_NOPE + MLA_ROPE]],
                                   axis=1)[None]
    G["mla_g_q"] = dgq
    G["mla_g_kv"] = dgkv
    G["mla_w_uq"] = _unpad_head_cols(d_mla_uq, MLA_NOPE + MLA_ROPE)[None]
    dkn = d_mla_kv[:MLA_KVR, :wide].reshape(MLA_KVR, HEADS, HEAD_PAD)[:, :, :MLA_NOPE]
    dvv = d_mla_kv[:MLA_KVR, wide:].reshape(MLA_KVR, HEADS, HEAD_PAD)[:, :, :MLA_V]
    G["mla_w_ukv"] = jnp.concatenate([dkn, dvv], axis=2).reshape(MLA_KVR, -1)[None]
    G["mla_w_o"] = _unpad_head_rows(d_mla_o, MLA_V)[None]
    for name in ("ffn_w_in", "ffn_w_out", "ffn_conv_w", "ffn_conv_b"):
        G[name] = jnp.stack([g_ffn0[name], g_ffn1[name]])
    G["final_g"] = d_final_g[0]
    dmods = jnp.stack([dmod0, dmod1, dmod2, dmod3])
    return loss, dx0, dmods, G


PACKED = [("fox_w_in", 2), ("fox_w_o", 1), ("mla_w_a", 1), ("mla_w_uq", 2), ("mla_w_ukv", 2), ("mla_w_o", 1),
          ("ffn_w_in", 2), ("ffn_w_out", 1), ("fox_b_f", None), ("mla_g_q", 1), ("mla_g_kv", 1),
          ("ffn_conv_w", 2), ("ffn_conv_b", None), ("final_g", None)]


def _shard_of(g, axis, s):
    if axis is None:
        return g
    n = g.shape[axis] // 4
    return lax.slice_in_dim(g, s * n, (s + 1) * n, axis=axis)


def _pack_plan(G):
    shard_size = lambda name, axis: math.prod(_shard_of(G[name], axis, 0).shape)
    big = [(name, axis) for name, axis in PACKED if shard_size(name, axis) % PACK_ROW == 0]
    small = [(name, axis) for name, axis in PACKED if shard_size(name, axis) % PACK_ROW != 0]
    whole_tiles = lambda size: -(-size // (8 * PACK_ROW)) * 8
    runs = [([item], whole_tiles(shard_size(*item))) for item in big]
    runs.append((small, whole_tiles(sum(shard_size(*item) for item in small))))
    used = sum(rows for _, rows in runs)
    unit = 2 * PACK_ROWS_ALIGN
    return runs, -(-used // unit) * unit


def _pack(G):
    runs, total = _pack_plan(G)
    pieces = []
    for items, rows in runs:
        per_chip = [jnp.concatenate([_shard_of(G[name], axis, s).reshape(-1) for name, axis in items])
                    for s in range(4)]
        flat = jnp.stack(per_chip)
        flat = jnp.pad(flat, ((0, 0), (0, rows * PACK_ROW - flat.shape[1])))
        pieces.append(flat.reshape(4, rows, PACK_ROW))
    used = sum(rows for _, rows in runs)
    if total > used:
        pieces.append(jnp.zeros((4, total - used, PACK_ROW), F32))
    return jnp.concatenate(pieces, axis=1).reshape(4, 2, total // 2, PACK_ROW)


def _unpack(both, G_like):
    runs, total = _pack_plan(G_like)
    table = both.reshape(total, PACK_ROW)
    out, row = {}, 0
    for items, rows in runs:
        flat, off = table[row:row + rows].reshape(-1), 0
        for name, axis in items:
            shape = _shard_of(G_like[name], axis, 0).shape
            out[name] = flat[off:off + math.prod(shape)].reshape(shape)
            off += math.prod(shape)
        row += rows
    return out


WEIGHTS = ['ada_w', 'ada_b', 'fox_w_in', 'fox_b_f', 'fox_w_o', 'mla_w_a', 'mla_g_q', 'mla_g_kv', 'mla_w_uq',
           'mla_w_ukv', 'mla_w_o', 'ffn_w_in', 'ffn_conv_w', 'ffn_conv_b', 'ffn_w_out', 'final_g']
BIG = ['fox_w_in', 'fox_w_o', 'mla_w_a', 'mla_w_uq', 'mla_w_ukv', 'mla_w_o', 'ffn_w_in', 'ffn_w_out']
SMALL = ['ada_b', 'fox_b_f', 'mla_g_q', 'mla_g_kv', 'ffn_conv_w', 'ffn_conv_b', 'final_g']


def _as2d(a):
    return a.reshape(-1, a.shape[-1])


def kernel(x, c, ada_w, ada_b, fox_w_in, fox_b_f, fox_w_o, mla_w_a, mla_g_q, mla_g_kv, mla_w_uq, mla_w_ukv, mla_w_o, ffn_w_in, ffn_conv_w, ffn_conv_b, ffn_w_out, final_g, loss_target, m_ada_w, m_ada_b, m_fox_w_in, m_fox_b_f, m_fox_w_o, m_mla_w_a, m_mla_g_q, m_mla_g_kv, m_mla_w_uq, m_mla_w_ukv, m_mla_w_o, m_ffn_w_in, m_ffn_conv_w, m_ffn_conv_b, m_ffn_w_out, m_final_g, v_ada_w, v_ada_b, v_fox_w_in, v_fox_b_f, v_fox_w_o, v_mla_w_a, v_mla_g_q, v_mla_g_kv, v_mla_w_uq, v_mla_w_ukv, v_mla_w_o, v_ffn_w_in, v_ffn_conv_w, v_ffn_conv_b, v_ffn_w_out, v_final_g):
    w = dict(ada_w=ada_w, ada_b=ada_b, fox_w_in=fox_w_in, fox_b_f=fox_b_f, fox_w_o=fox_w_o, mla_w_a=mla_w_a,
             mla_g_q=mla_g_q, mla_g_kv=mla_g_kv, mla_w_uq=mla_w_uq, mla_w_ukv=mla_w_ukv, mla_w_o=mla_w_o,
             ffn_w_in=ffn_w_in, ffn_conv_w=ffn_conv_w, ffn_conv_b=ffn_conv_b, ffn_w_out=ffn_w_out, final_g=final_g)
    m = dict(ada_w=m_ada_w, ada_b=m_ada_b, fox_w_in=m_fox_w_in, fox_b_f=m_fox_b_f, fox_w_o=m_fox_w_o,
             mla_w_a=m_mla_w_a, mla_g_q=m_mla_g_q, mla_g_kv=m_mla_g_kv, mla_w_uq=m_mla_w_uq,
             mla_w_ukv=m_mla_w_ukv, mla_w_o=m_mla_w_o, ffn_w_in=m_ffn_w_in, ffn_conv_w=m_ffn_conv_w,
             ffn_conv_b=m_ffn_conv_b, ffn_w_out=m_ffn_w_out, final_g=m_final_g)
    v = dict(ada_w=v_ada_w, ada_b=v_ada_b, fox_w_in=v_fox_w_in, fox_b_f=v_fox_b_f, fox_w_o=v_fox_w_o,
             mla_w_a=v_mla_w_a, mla_g_q=v_mla_g_q, mla_g_kv=v_mla_g_kv, mla_w_uq=v_mla_w_uq,
             mla_w_ukv=v_mla_w_ukv, mla_w_o=v_mla_w_o, ffn_w_in=v_ffn_w_in, ffn_conv_w=v_ffn_conv_w,
             ffn_conv_b=v_ffn_conv_b, ffn_w_out=v_ffn_w_out, final_g=v_final_g)
    D = D_MODEL
    xi, yi, ci = lax.axis_index("x"), lax.axis_index("y"), lax.axis_index("c")
    dev = 4 * xi + 2 * yi + ci

    n_ada = ada_w.shape[-1]
    small_parts = [c.reshape(-1), ffn_conv_w.reshape(-1), mla_g_q.reshape(-1), mla_g_kv.reshape(-1)]
    sizes = [p.shape[0] for p in small_parts]
    flat = jnp.concatenate(small_parts)
    flat = jnp.pad(flat, (0, -flat.shape[0] % LANE))[None]
    got = _all_gather8(flat, name="gather_cond")[:, 0]
    offs = np.cumsum([0] + sizes)
    c_all = got[:, offs[0]:offs[1]]
    chips = got[0::2]
    conv_w_full = jnp.concatenate(
        [chips[s, offs[1]:offs[2]].reshape(ffn_conv_w.shape) for s in range(4)], axis=2)
    gq_full = jnp.concatenate([chips[s, offs[2]:offs[3]] for s in range(4)])[None]
    gkv_full = jnp.concatenate([chips[s, offs[3]:offs[4]] for s in range(4)])[None]

    mod_shard = _ada_mod(c_all, ada_w.reshape(4, D, n_ada), ada_b.reshape(4, 1, n_ada))
    mod_all = _all_gather8(mod_shard.reshape(4 * 8, n_ada), name="gather_mod")
    mod_all = mod_all[0::2].reshape(4, 4, 8, n_ada)
    mine = lax.dynamic_index_in_dim(mod_all, dev, axis=2, keepdims=False)
    mod_rows = jnp.transpose(mine, (1, 0, 2)).reshape(4, 4 * n_ada)
    mods = [(mod_rows[k:k + 1, :D], mod_rows[k:k + 1, D:2 * D], mod_rows[k:k + 1, 2 * D:]) for k in range(4)]

    shards = [w[name].astype(BF16) for name in BIG]
    W = _prepare_fox_in(_all_gather_chips(shards[:1], name="gather_weights")[0])
    W["pending"] = shards[1:]
    small = dict(fox_b_f=fox_b_f, mla_g_q=gq_full, mla_g_kv=gkv_full, ffn_conv_w=conv_w_full,
                 ffn_conv_b=ffn_conv_b, final_g=final_g[None])

    loss, dx, dmods, G = _local_step(x[0], loss_target[0], mods, W, small)

    loss_row = jnp.pad(loss[:, :1], ((0, 0), (0, dmods.shape[1] - 1)))
    gathered_rows = _all_gather8(jnp.concatenate([dmods, loss_row]), name="gather_dmod")
    dmod_all = gathered_rows[:, :4]
    loss_total = jnp.sum(gathered_rows[:, 4, 0])
    chip = 2 * xi + yi
    dmod_cols = lax.dynamic_slice_in_dim(dmod_all, chip * n_ada, n_ada, axis=2)
    g_ada_w, g_ada_b = _ada_grad(c_all.T, jnp.transpose(dmod_cols, (1, 0, 2)))
    grads = dict(ada_w=g_ada_w.reshape(ada_w.shape), ada_b=g_ada_b.reshape(ada_b.shape))

    packed = _pack(G)
    recv = _exchange_halves(packed)
    part = _add_halves(packed, recv, ci.reshape(1).astype(jnp.int32))
    from_chips = _exchange_chips(part)
    half = _add_chips(from_chips)
    other = _share_halves(half)
    both = jnp.stack([jnp.where(ci == 0, half, other), jnp.where(ci == 0, other, half)])
    grads.update(_unpack(both, G))

    delta, new_m, new_v = {}, {}, {}
    for name in BIG + ["ada_w"]:
        shape = w[name].shape
        d_, m_, v_ = _adamw(_as2d(w[name]), _as2d(grads[name]), _as2d(m[name]), _as2d(v[name]), name=f"adamw_{name}")
        delta[name], new_m[name], new_v[name] = d_.reshape(shape), m_.reshape(shape), v_.reshape(shape)
        grads[name] = grads[name].reshape(shape)
    sizes = [math.prod(w[name].shape) for name in SMALL]
    total = sum(sizes)
    rows = -(-total // LANE)
    rows += -rows % 8

    def pack_small(d):
        flat = jnp.concatenate([d[name].reshape(-1) for name in SMALL])
        return jnp.pad(flat, (0, rows * LANE - total)).reshape(rows, LANE)

    outs = _adamw(pack_small(w), pack_small(grads), pack_small(m), pack_small(v), name="adamw_small")
    off = 0
    for name, size in zip(SMALL, sizes):
        shape = w[name].shape
        for dst, src in zip((delta, new_m, new_v), outs):
            dst[name] = src.reshape(-1)[off:off + size].reshape(shape)
        grads[name] = grads[name].reshape(shape)
        off += size

    return (loss_total, dx[None], *[grads[n] for n in WEIGHTS], *[delta[n] for n in WEIGHTS],
            *[new_m[n] for n in WEIGHTS], *[new_v[n] for n in WEIGHTS])
```

```python
import functools
import math

import numpy as np
import jax
import jax.numpy as jnp
from jax import lax
from jax.experimental import pallas as pl
from jax.experimental.pallas import tpu as pltpu

F32 = jnp.float32
BF16 = jnp.bfloat16
MESH = pl.DeviceIdType.MESH

D_MODEL = 1024
HEADS = 16
HEAD_PAD = 128
FOX_HEAD = 64
MLA_NOPE = 64
MLA_ROPE = 32
MLA_V = 64
MLA_QR = 384
MLA_KVR = 256
D_FF = 2816
NORM_EPS = 1e-6
ROPE_BASE = 10000.0
ADAM_LR = 0.001
ADAM_B1 = 0.9
ADAM_B2 = 0.999
ADAM_EPS = 1e-08
ADAM_WD = 0.01
ADAM_STEP = 10
FOX_SCALE = FOX_HEAD ** -0.5
MLA_SCALE = (MLA_NOPE + MLA_ROPE) ** -0.5
NEG_BIG = -1e30
VMEM_LIMIT = 56 * 1024 * 1024
LANE = 128
MM_ROWS, MM_COLS, MM_DEPTH = 1408, 1408, 1408
PACK_ROW = 1024
PACK_ROWS_ALIGN = 256

NT = (((1,), (1,)), ((), ()))
TN = (((0,), (0,)), ((), ()))
NN = (((1,), (0,)), ((), ()))


def _cparams():
    return pltpu.CompilerParams(vmem_limit_bytes=VMEM_LIMIT)


def _tile(n, cap):
    if n <= cap:
        return n
    for t in range(cap - cap % LANE, 0, -LANE):
        if n % t == 0:
            return t
    raise ValueError((n, cap))


def _row_tile(rows, cols, limit_bytes=1 << 20):
    best = None
    for t in range(8, rows + 1, 8):
        if rows % t == 0 and t * cols * 4 <= limit_bytes:
            best = t
    return best if best is not None else rows


def _me():
    return lax.axis_index("x"), lax.axis_index("y"), lax.axis_index("c")


def _all_gather8(v, name):
    R, N = v.shape

    def body(v_ref, o_ref, ssem, rsem):
        x, y, c = _me()
        me = 4 * x + 2 * y + c
        o_ref[me] = v_ref[...]
        copies = []
        for k in range(1, 8):
            peer = (1 - x if k & 4 else x, 1 - y if k & 2 else y, 1 - c if k & 1 else c)
            cp = pltpu.make_async_remote_copy(
                src_ref=v_ref, dst_ref=o_ref.at[me], send_sem=ssem.at[k - 1], recv_sem=rsem.at[k - 1],
                device_id=peer, device_id_type=MESH)
            cp.start()
            copies.append(cp)
        for cp in copies:
            cp.wait()

    return pl.pallas_call(
        body, name=name,
        out_shape=jax.ShapeDtypeStruct((8, R, N), v.dtype),
        in_specs=[pl.BlockSpec(memory_space=pltpu.VMEM)],
        out_specs=pl.BlockSpec(memory_space=pltpu.VMEM),
        scratch_shapes=[pltpu.SemaphoreType.DMA((7,)), pltpu.SemaphoreType.DMA((7,))],
    )(v)


def _all_gather_chips(shards, name):
    n = len(shards)

    def body(*refs):
        copies = _chip_gather_copies(refs[:n], refs[n:2 * n], *refs[2 * n:])
        for cp in copies:
            cp.start()
        for cp in copies:
            cp.wait()

    return pl.pallas_call(
        body, name=name,
        out_shape=[jax.ShapeDtypeStruct((4,) + s.shape, s.dtype) for s in shards],
        in_specs=[pl.BlockSpec(memory_space=pl.ANY)] * n,
        out_specs=[pl.BlockSpec(memory_space=pl.ANY)] * n,
        scratch_shapes=[pltpu.SemaphoreType.DMA((3 * n,)), pltpu.SemaphoreType.DMA((3 * n,)),
                        pltpu.SemaphoreType.DMA((n,))],
    )(*shards)


def _exchange_halves(g):
    _, _, R, P = g.shape

    def body(g_ref, o_ref, ssem, rsem):
        x, y, c = _me()
        copies = []
        for s in range(4):
            cp = pltpu.make_async_remote_copy(
                src_ref=g_ref.at[s, 1 - c], dst_ref=o_ref.at[s], send_sem=ssem.at[s], recv_sem=rsem.at[s],
                device_id=(x, y, 1 - c), device_id_type=MESH)
            cp.start()
            copies.append(cp)
        for cp in copies:
            cp.wait()

    return pl.pallas_call(
        body, name="grad_exchange_sibling",
        out_shape=jax.ShapeDtypeStruct((4, R, P), g.dtype),
        in_specs=[pl.BlockSpec(memory_space=pl.ANY)],
        out_specs=pl.BlockSpec(memory_space=pl.ANY),
        scratch_shapes=[pltpu.SemaphoreType.DMA((4,)), pltpu.SemaphoreType.DMA((4,))],
    )(g)


def _exchange_chips(a):
    _, R, P = a.shape

    def body(a_ref, o_ref, ssem, rsem, lsem):
        x, y, c = _me()
        me = 2 * x + y
        own = pltpu.make_async_copy(a_ref.at[me], o_ref.at[me], lsem)
        own.start()
        copies = []
        for k in (1, 2, 3):
            px, py = (1 - x if k & 2 else x), (1 - y if k & 1 else y)
            cp = pltpu.make_async_remote_copy(
                src_ref=a_ref.at[2 * px + py], dst_ref=o_ref.at[me], send_sem=ssem.at[k - 1],
                recv_sem=rsem.at[k - 1], device_id=(px, py, c), device_id_type=MESH)
            cp.start()
            copies.append(cp)
        own.wait()
        for cp in copies:
            cp.wait()

    return pl.pallas_call(
        body, name="grad_exchange_chips",
        out_shape=jax.ShapeDtypeStruct((4, R, P), a.dtype),
        in_specs=[pl.BlockSpec(memory_space=pl.ANY)],
        out_specs=pl.BlockSpec(memory_space=pl.ANY),
        scratch_shapes=[pltpu.SemaphoreType.DMA((3,)), pltpu.SemaphoreType.DMA((3,)), pltpu.SemaphoreType.DMA],
    )(a)


def _share_halves(f):
    R, P = f.shape

    def body(f_ref, o_ref, ssem, rsem):
        x, y, c = _me()
        cp = pltpu.make_async_remote_copy(
            src_ref=f_ref, dst_ref=o_ref, send_sem=ssem, recv_sem=rsem,
            device_id=(x, y, 1 - c), device_id_type=MESH)
        cp.start()
        cp.wait()

    return pl.pallas_call(
        body, name="grad_share_sibling",
        out_shape=jax.ShapeDtypeStruct((R, P), f.dtype),
        in_specs=[pl.BlockSpec(memory_space=pl.ANY)],
        out_specs=pl.BlockSpec(memory_space=pl.ANY),
        scratch_shapes=[pltpu.SemaphoreType.DMA, pltpu.SemaphoreType.DMA],
    )(f)


def _add_halves(g, r, c_idx):
    _, _, R, P = g.shape
    t = PACK_ROWS_ALIGN

    def body(c_ref, g_ref, r_ref, o_ref):
        o_ref[...] = (g_ref[0] + r_ref[...]).astype(BF16)

    return pl.pallas_call(
        body, name="grad_add_halves",
        out_shape=jax.ShapeDtypeStruct((4, R, P), BF16),
        grid_spec=pltpu.PrefetchScalarGridSpec(
            num_scalar_prefetch=1, grid=(4, R // t),
            in_specs=[pl.BlockSpec((1, 1, t, P), lambda s, i, c_ref: (s, c_ref[0], i, 0)),
                      pl.BlockSpec((1, t, P), lambda s, i, c_ref: (s, i, 0))],
            out_specs=pl.BlockSpec((1, t, P), lambda s, i, c_ref: (s, i, 0))),
        compiler_params=_cparams(),
    )(c_idx, g, r)


def _add_chips(b):
    _, R, P = b.shape
    t = PACK_ROWS_ALIGN

    def body(b_ref, o_ref):
        b0, b1, b2, b3 = (b_ref[k].astype(F32) for k in range(4))
        o_ref[...] = ((b0 + b1) + b2) + b3

    return pl.pallas_call(
        body, name="grad_add_chips",
        out_shape=jax.ShapeDtypeStruct((R, P), F32),
        grid=(R // t,),
        in_specs=[pl.BlockSpec((4, t, P), lambda i: (0, i, 0))],
        out_specs=pl.BlockSpec((t, P), lambda i: (i, 0)),
        compiler_params=_cparams(),
    )(b)


def _mm(a, b, *, ta=False, tb=False, out_dtype=F32, res=None, name):
    if ta:
        K, M = a.shape
    else:
        M, K = a.shape
    N = b.shape[0] if tb else b.shape[1]
    assert (b.shape[1] if tb else b.shape[0]) == K, (a.shape, b.shape, ta, tb)
    tm, tn, tk = _tile(M, MM_ROWS), _tile(N, MM_COLS), _tile(K, MM_DEPTH)
    nk = K // tk
    has_res = res is not None
    dims = (((0 if ta else 1,), (1 if tb else 0,)), ((), ()))

    def body(*refs):
        a_ref, b_ref = refs[:2]
        r_ref = refs[2] if has_res else None
        o_ref = refs[3] if has_res else refs[2]

        def finish(r):
            if has_res:
                r = r + r_ref[...]
            o_ref[...] = r.astype(out_dtype)

        prod = lax.dot_general(a_ref[...].astype(BF16), b_ref[...].astype(BF16), dims,
                               preferred_element_type=F32)
        if nk == 1:
            finish(prod)
            return
        acc_ref = refs[-1]
        k = pl.program_id(2)

        @pl.when(k == 0)
        def _():
            acc_ref[...] = prod

        @pl.when(k > 0)
        def _():
            acc_ref[...] += prod

        @pl.when(k == nk - 1)
        def _():
            finish(acc_ref[...])

    a_spec = (pl.BlockSpec((tk, tm), lambda j, i, k: (k, i)) if ta
              else pl.BlockSpec((tm, tk), lambda j, i, k: (i, k)))
    b_spec = (pl.BlockSpec((tn, tk), lambda j, i, k: (j, k)) if tb
              else pl.BlockSpec((tk, tn), lambda j, i, k: (k, j)))
    in_specs = [a_spec, b_spec]
    args = [a, b]
    if has_res:
        in_specs.append(pl.BlockSpec((tm, tn), lambda j, i, k: (i, j)))
        args.append(res)
    return pl.pallas_call(
        body, name=name,
        out_shape=jax.ShapeDtypeStruct((M, N), out_dtype),
        grid=(N // tn, M // tm, nk),
        in_specs=in_specs,
        out_specs=pl.BlockSpec((tm, tn), lambda j, i, k: (i, j)),
        scratch_shapes=[pltpu.VMEM((tm, tn), F32)] if nk > 1 else [],
        compiler_params=_cparams(),
    )(*args)


def _rstd(xv):
    return lax.rsqrt(jnp.mean(xv * xv, axis=-1, keepdims=True) + NORM_EPS)


def _norm_bwd(n, r, dn):
    return r * (dn - n * jnp.mean(dn * n, axis=-1, keepdims=True))


def _adaln_fwd(x, scale, shift, name):
    S, D = x.shape
    t = _tile(S, 512)

    def body(x_ref, sc_ref, sh_ref, h_ref):
        xv = x_ref[...]
        h_ref[...] = (xv * _rstd(xv) * (1.0 + sc_ref[...]) + sh_ref[...]).astype(BF16)

    row = pl.BlockSpec((t, D), lambda i: (i, 0))
    vec = pl.BlockSpec((1, D), lambda i: (0, 0))
    return pl.pallas_call(
        body, name=name, out_shape=jax.ShapeDtypeStruct((S, D), BF16), grid=(S // t,),
        in_specs=[row, vec, vec], out_specs=row, compiler_params=_cparams(),
    )(x, scale, shift)


def _adaln_bwd(x, dh, dx_next, scale, name):
    S, D = x.shape
    t = _tile(S, 512)

    def body(x_ref, dh_ref, dxn_ref, sc_ref, dx_ref, dsc_ref, dsh_ref):
        @pl.when(pl.program_id(0) == 0)
        def _():
            dsc_ref[...] = jnp.zeros_like(dsc_ref)
            dsh_ref[...] = jnp.zeros_like(dsh_ref)

        xv = x_ref[...]
        r = _rstd(xv)
        n = xv * r
        dh = dh_ref[...]
        dx_ref[...] = dxn_ref[...] + _norm_bwd(n, r, dh * (1.0 + sc_ref[...]))
        dsc_ref[...] += jnp.sum(dh * n, axis=0, keepdims=True)
        dsh_ref[...] += jnp.sum(dh, axis=0, keepdims=True)

    row = pl.BlockSpec((t, D), lambda i: (i, 0))
    vec = pl.BlockSpec((1, D), lambda i: (0, 0))
    return pl.pallas_call(
        body, name=name,
        out_shape=[jax.ShapeDtypeStruct((S, D), F32), jax.ShapeDtypeStruct((1, D), F32),
                   jax.ShapeDtypeStruct((1, D), F32)],
        grid=(S // t,), in_specs=[row, row, row, vec], out_specs=[row, vec, vec],
        compiler_params=_cparams(),
    )(x, dh, dx_next, scale)


def _residual(x, gate, y, name):
    S, D = x.shape
    t = _tile(S, 512)

    def body(x_ref, g_ref, y_ref, o_ref):
        o_ref[...] = x_ref[...] + g_ref[...] * y_ref[...]

    row = pl.BlockSpec((t, D), lambda i: (i, 0))
    vec = pl.BlockSpec((1, D), lambda i: (0, 0))
    return pl.pallas_call(
        body, name=name, out_shape=jax.ShapeDtypeStruct((S, D), F32), grid=(S // t,),
        in_specs=[row, vec, row], out_specs=row, compiler_params=_cparams(),
    )(x, gate, y)


def _gate_bwd(dx, y, gate, name):
    S, D = dx.shape
    t = _tile(S, 512)

    def body(dx_ref, y_ref, g_ref, dy_ref, dg_ref):
        @pl.when(pl.program_id(0) == 0)
        def _():
            dg_ref[...] = jnp.zeros_like(dg_ref)

        dxv = dx_ref[...]
        dy_ref[...] = (dxv * g_ref[...]).astype(BF16)
        dg_ref[...] += jnp.sum(dxv * y_ref[...], axis=0, keepdims=True)

    row = pl.BlockSpec((t, D), lambda i: (i, 0))
    vec = pl.BlockSpec((1, D), lambda i: (0, 0))
    return pl.pallas_call(
        body, name=name,
        out_shape=[jax.ShapeDtypeStruct((S, D), BF16), jax.ShapeDtypeStruct((1, D), F32)],
        grid=(S // t,), in_specs=[row, row, vec], out_specs=[row, vec], compiler_params=_cparams(),
    )(dx, y, gate)


def _final_loss(x, g, target):
    S, D = x.shape
    t = _tile(S, 512)

    def body(x_ref, g_ref, t_ref, loss_ref, dx_ref, dg_ref):
        @pl.when(pl.program_id(0) == 0)
        def _():
            loss_ref[...] = jnp.zeros_like(loss_ref)
            dg_ref[...] = jnp.zeros_like(dg_ref)

        xv = x_ref[...]
        gv = g_ref[...]
        r = _rstd(xv)
        n = xv * r
        err = n * gv - t_ref[...]
        part = jnp.sum(jnp.mean(err * err, axis=-1, keepdims=True), axis=0, keepdims=True)
        loss_ref[...] += jnp.broadcast_to(0.5 * part, loss_ref.shape)
        dy = err * (1.0 / D)
        dg_ref[...] += jnp.sum(dy * n, axis=0, keepdims=True)
        dx_ref[...] = _norm_bwd(n, r, dy * gv)

    row = pl.BlockSpec((t, D), lambda i: (i, 0))
    vec = pl.BlockSpec((1, D), lambda i: (0, 0))
    one = pl.BlockSpec((1, LANE), lambda i: (0, 0))
    return pl.pallas_call(
        body, name="final_loss",
        out_shape=[jax.ShapeDtypeStruct((1, LANE), F32), jax.ShapeDtypeStruct((S, D), F32),
                   jax.ShapeDtypeStruct((1, D), F32)],
        grid=(S // t,), in_specs=[row, vec, row], out_specs=[one, row, vec], compiler_params=_cparams(),
    )(x, g, target)


ATTN_TILE = 1024
ATTN_ROWS = 16
ONES_LANE = 127


def _causal_pairs(n, k_major):
    if k_major:
        pairs = [(i, j) for j in range(n) for i in range(j, n)]
    else:
        pairs = [(i, j) for i in range(n) for j in range(i + 1)]
    return (jnp.asarray(np.array([p[0] for p in pairs], np.int32)),
            jnp.asarray(np.array([p[1] for p in pairs], np.int32)))


def _chip_gather_copies(ins, outs, ssem, rsem, lsem):
    x, y, c = _me()
    me = 2 * x + y
    copies = []
    for i, (src, dst) in enumerate(zip(ins, outs)):
        copies.append(pltpu.make_async_copy(src, dst.at[me], lsem.at[i]))
        for k in (1, 2, 3):
            peer = (1 - x if k & 2 else x, 1 - y if k & 1 else y, c)
            copies.append(pltpu.make_async_remote_copy(
                src_ref=src, dst_ref=dst.at[me], send_sem=ssem.at[3 * i + k - 1],
                recv_sem=rsem.at[3 * i + k - 1], device_id=peer, device_id_type=MESH))
    return copies


def _attn_fwd(q_arr, q_off, k_arr, k_off, v_arr, v_off, name, gather=()):
    S = q_arr.shape[0]
    t = _tile(S, ATTN_TILE)
    n = S // t
    rc = min(ATTN_ROWS, t)
    ii, jj = _causal_pairs(n, k_major=False)
    ng = len(gather)
    last_step = ii.shape[0] - 1

    def body(ii_ref, jj_ref, q_ref, k_ref, v_ref, *rest):
        g_in, (o_ref, lse_ref), g_out = rest[:ng], rest[ng:ng + 2], rest[ng + 2:2 * ng + 2]
        s_s, m_s, a_s, acc_s = rest[2 * ng + 2:2 * ng + 6]
        step = pl.program_id(1)
        i, j = ii_ref[step], jj_ref[step]
        lane = lax.broadcasted_iota(jnp.int32, (t, HEAD_PAD), 1)
        if ng:
            copies = _chip_gather_copies(g_in, g_out, *rest[2 * ng + 6:])

            @pl.when(jnp.logical_and(pl.program_id(0) == 0, step == 0))
            def _():
                for cp in copies:
                    cp.start()

            @pl.when(jnp.logical_and(pl.program_id(0) == HEADS - 1, step == last_step))
            def _():
                for cp in copies:
                    cp.wait()

        @pl.when(j == 0)
        def _():
            m_s[...] = jnp.full_like(m_s, NEG_BIG)
            acc_s[...] = jnp.zeros_like(acc_s)

        def tile(diagonal):
            s_s[...] = lax.dot_general(q_ref[...], k_ref[...], NT, preferred_element_type=F32)
            for r in range(t // rc):
                rows = pl.ds(r * rc, rc)
                sc = s_s[rows, :]
                if diagonal:
                    qpos = r * rc + lax.broadcasted_iota(jnp.int32, (rc, t), 0)
                    kpos = lax.broadcasted_iota(jnp.int32, (rc, t), 1)
                    sc = jnp.where(qpos >= kpos, sc, NEG_BIG)
                m_old = m_s[rows, :]
                m_new = jnp.maximum(m_old, jnp.max(sc, axis=1, keepdims=True))
                s_s[rows, :] = jnp.exp(sc - m_new)
                a_s[rows, :] = jnp.exp(m_old - m_new)
                m_s[rows, :] = m_new
            v_ones = jnp.where(lane == ONES_LANE, 1.0, v_ref[...].astype(F32)).astype(BF16)
            acc_s[...] = a_s[...] * acc_s[...] + lax.dot_general(s_s[...].astype(BF16), v_ones, NN,
                                                                 preferred_element_type=F32)

        @pl.when(j < i)
        def _():
            tile(False)

        @pl.when(j == i)
        def _():
            tile(True)
            acc = acc_s[...]
            l = jnp.sum(jnp.where(lane == ONES_LANE, acc, 0.0), axis=1, keepdims=True)
            o_ref[...] = jnp.where(lane == ONES_LANE, 0.0, acc / l).astype(BF16)
            lse_ref[0] = m_s[...] + jnp.log(l)

    hbm = pl.BlockSpec(memory_space=pl.ANY)
    sems = [pltpu.SemaphoreType.DMA((3 * ng,)), pltpu.SemaphoreType.DMA((3 * ng,)),
            pltpu.SemaphoreType.DMA((ng,))] if ng else []
    grid_spec = pltpu.PrefetchScalarGridSpec(
        num_scalar_prefetch=2, grid=(HEADS, ii.shape[0]),
        in_specs=[pl.BlockSpec((t, HEAD_PAD), lambda h, s, ii, jj: (ii[s], q_off + h)),
                  pl.BlockSpec((t, HEAD_PAD), lambda h, s, ii, jj: (jj[s], k_off + h)),
                  pl.BlockSpec((t, HEAD_PAD), lambda h, s, ii, jj: (jj[s], v_off + h))] + [hbm] * ng,
        out_specs=[pl.BlockSpec((t, HEAD_PAD), lambda h, s, ii, jj: (ii[s], h)),
                   pl.BlockSpec((1, t, 1), lambda h, s, ii, jj: (h, ii[s], 0))] + [hbm] * ng,
        scratch_shapes=[pltpu.VMEM((t, t), F32), pltpu.VMEM((t, 1), F32), pltpu.VMEM((t, 1), F32),
                        pltpu.VMEM((t, HEAD_PAD), F32)] + sems)
    outs = pl.pallas_call(
        body, name=name, grid_spec=grid_spec,
        out_shape=[jax.ShapeDtypeStruct((S, HEADS * HEAD_PAD), BF16), jax.ShapeDtypeStruct((HEADS, S, 1), F32)]
        + [jax.ShapeDtypeStruct((4,) + g.shape, g.dtype) for g in gather],
        compiler_params=_cparams(),
    )(ii, jj, q_arr, k_arr, v_arr, *gather)
    return outs[0], outs[1], list(outs[2:])


def _attn_delta(do, o, name):
    S = do.shape[0]
    t = _tile(S, 512)

    def body(do_ref, o_ref, d_ref):
        lane = lax.broadcasted_iota(jnp.int32, (t, LANE), 1)
        acc = jnp.zeros((t, LANE), F32)
        for h in range(HEADS):
            cols = pl.ds(h * HEAD_PAD, HEAD_PAD)
            d = jnp.sum(do_ref[:, cols].astype(F32) * o_ref[:, cols].astype(F32), axis=1, keepdims=True)
            acc = jnp.where(lane == h, d, acc)
        d_ref[...] = acc

    blk = pl.BlockSpec((t, HEADS * HEAD_PAD), lambda i: (i, 0))
    by_lane = pl.pallas_call(
        body, name=name, out_shape=jax.ShapeDtypeStruct((S, LANE), F32), grid=(S // t,),
        in_specs=[blk, blk], out_specs=pl.BlockSpec((t, LANE), lambda i: (i, 0)),
        compiler_params=_cparams(),
    )(do, o)
    return by_lane[:, :HEADS].T.reshape(HEADS, 1, S)


def _lane_as_row(x, lane):
    pick = (lax.broadcasted_iota(jnp.int32, (8, HEAD_PAD), 1) == lane).astype(BF16)
    acc = None
    for part in _split3(x):
        d = lax.dot_general(pick, part, NT, preferred_element_type=F32)
        acc = d if acc is None else acc + d
    return jnp.max(acc, axis=0, keepdims=True)


def _attn_bwd(q_arr, q_off, k_arr, k_off, v_arr, v_off, do, lse, delta, name, bias_grads=False, dq_dtype=F32):
    S = q_arr.shape[0]
    t = _tile(S, ATTN_TILE)
    n = S // t
    rc = min(ATTN_ROWS, t)
    ii, jj = _causal_pairs(n, k_major=True)
    last_step = ii.shape[0] - 1

    def body(ii_ref, jj_ref, q_ref, k_ref, v_ref, do_ref, lse_ref, dl_ref, dq_ref, dk_ref, dv_ref, *rest):
        if bias_grads:
            dcq_ref, dck_ref = rest[:2]
        s_s, dp_s, dk_s, dv_s, dq_s = rest[-5:]
        step = pl.program_id(1)
        i, j = ii_ref[step], jj_ref[step]

        @pl.when(step == 0)
        def _():
            dq_s[...] = jnp.zeros_like(dq_s)

        @pl.when(i == j)
        def _():
            dk_s[...] = jnp.zeros_like(dk_s)
            dv_s[...] = jnp.zeros_like(dv_s)

        def tile(diagonal):
            q, k, dov = q_ref[...], k_ref[...], do_ref[...]
            s_s[...] = lax.dot_general(k, q, NT, preferred_element_type=F32)
            dp_s[...] = lax.dot_general(v_ref[...], dov, NT, preferred_element_type=F32)
            lse_row, dl_row = lse_ref[0], dl_ref[0]
            for r in range(t // rc):
                rows = pl.ds(r * rc, rc)
                sc = s_s[rows, :]
                if diagonal:
                    kpos = r * rc + lax.broadcasted_iota(jnp.int32, (rc, t), 0)
                    qpos = lax.broadcasted_iota(jnp.int32, (rc, t), 1)
                    sc = jnp.where(qpos >= kpos, sc, NEG_BIG)
                p = jnp.exp(sc - lse_row)
                s_s[rows, :] = p
                dp_s[rows, :] = p * (dp_s[rows, :] - dl_row)
            ds = dp_s[...].astype(BF16)
            dv_s[...] += lax.dot_general(s_s[...].astype(BF16), dov, NN, preferred_element_type=F32)
            dk_s[...] += lax.dot_general(ds, q, NN, preferred_element_type=F32)
            rows = pl.ds(pl.multiple_of(i * t, t), t)
            dq_s[rows, :] += lax.dot_general(ds, k, TN, preferred_element_type=F32)

        @pl.when(i > j)
        def _():
            tile(False)

        @pl.when(i == j)
        def _():
            tile(True)

        @pl.when(i == n - 1)
        def _():
            dk = dk_s[...]
            dk_ref[...] = dk.astype(BF16)
            dv_ref[...] = dv_s[...].astype(BF16)
            if bias_grads:
                dck_ref[0] = _lane_as_row(dk, BIAS_LANE)

        @pl.when(step == last_step)
        def _():
            dq = dq_s[...]
            dq_ref[...] = dq.astype(dq_dtype)
            if bias_grads:
                dcq_ref[0] = _lane_as_row(dq, BIAS_LANE + 3)

    qblk = lambda off: pl.BlockSpec((t, HEAD_PAD), lambda h, s, ii, jj: (ii[s], off + h))
    kblk = lambda off: pl.BlockSpec((t, HEAD_PAD), lambda h, s, ii, jj: (jj[s], off + h))
    stat = pl.BlockSpec((1, 1, t), lambda h, s, ii, jj: (h, 0, ii[s]))
    wide = (S, HEADS * HEAD_PAD)
    out_specs = [pl.BlockSpec((S, HEAD_PAD), lambda h, s, ii, jj: (0, h)), kblk(0), kblk(0)]
    out_shape = [jax.ShapeDtypeStruct(wide, dq_dtype), jax.ShapeDtypeStruct(wide, BF16),
                 jax.ShapeDtypeStruct(wide, BF16)]
    if bias_grads:
        out_specs += [pl.BlockSpec((1, 1, S), lambda h, s, ii, jj: (h, 0, 0)),
                      pl.BlockSpec((1, 1, t), lambda h, s, ii, jj: (h, 0, jj[s]))]
        out_shape += [jax.ShapeDtypeStruct((HEADS, 1, S), F32)] * 2
    grid_spec = pltpu.PrefetchScalarGridSpec(
        num_scalar_prefetch=2, grid=(HEADS, ii.shape[0]),
        in_specs=[qblk(q_off), kblk(k_off), kblk(v_off), qblk(0), stat, stat],
        out_specs=out_specs,
        scratch_shapes=[pltpu.VMEM((t, t), F32), pltpu.VMEM((t, t), F32), pltpu.VMEM((t, HEAD_PAD), F32),
                        pltpu.VMEM((t, HEAD_PAD), F32), pltpu.VMEM((S, HEAD_PAD), F32)])
    return pl.pallas_call(
        body, name=name, grid_spec=grid_spec, out_shape=out_shape, compiler_params=_cparams(),
    )(ii, jj, q_arr, k_arr, v_arr, do, lse, delta)


BIAS_LANE = FOX_HEAD


def _fox_expand(qkv, cum, name):
    S = qkv.shape[0]
    t = _tile(S, 256)
    narrow, wide = HEADS * FOX_HEAD, HEADS * HEAD_PAD

    def body(q_ref, k_ref, v_ref, c_ref, qo_ref, ko_ref, vo_ref):
        lane = lax.broadcasted_iota(jnp.int32, (t, HEAD_PAD), 1)
        cum_all = c_ref[...]

        def head_tile(ref, h):
            x = ref[:, pl.ds((h // 2) * HEAD_PAD, HEAD_PAD)].astype(F32)
            if h % 2:
                x = pltpu.roll(x, FOX_HEAD, 1)
            return jnp.where(lane < FOX_HEAD, x, 0.0)

        def fill(out, first, second):
            for n_, (a, b) in enumerate(zip(first, second)):
                out = jnp.where(lane == BIAS_LANE + n_, a, out)
                out = jnp.where(lane == BIAS_LANE + 3 + n_, b, out)
            return out.astype(BF16)

        ones = [1.0, 1.0, 1.0]
        for h in range(HEADS):
            cols = pl.ds(h * HEAD_PAD, HEAD_PAD)
            c = jnp.sum(jnp.where(lane == h, cum_all, 0.0), axis=1, keepdims=True)
            pieces = [x.astype(F32) for x in _split3(c)]
            qo_ref[:, cols] = fill(head_tile(q_ref, h), ones, pieces)
            ko_ref[:, cols] = fill(head_tile(k_ref, h), [-x for x in pieces], ones)
            vo_ref[:, cols] = head_tile(v_ref, h).astype(BF16)

    part = lambda p: pl.BlockSpec((t, narrow), lambda i: (i, p))
    out = pl.BlockSpec((t, wide), lambda i: (i, 0))
    shp = jax.ShapeDtypeStruct((S, wide), BF16)
    return pl.pallas_call(
        body, name=name, out_shape=[shp, shp, shp], grid=(S // t,),
        in_specs=[part(0), part(1), part(2), pl.BlockSpec((t, LANE), lambda i: (i, 0))],
        out_specs=[out, out, out], compiler_params=_cparams(),
    )(qkv, qkv, qkv, cum)


def _fox_compress(dq, dk, dv, name):
    S = dq.shape[0]
    t = _tile(S, 256)
    narrow, wide = HEADS * FOX_HEAD, HEADS * HEAD_PAD

    def body(dq_ref, dk_ref, dv_ref, o_ref):
        lane = lax.broadcasted_iota(jnp.int32, (t, HEAD_PAD), 1)
        for p, ref in enumerate((dq_ref, dk_ref, dv_ref)):
            for g in range(HEADS // 2):
                even = ref[:, pl.ds(2 * g * HEAD_PAD, HEAD_PAD)].astype(F32)
                odd = ref[:, pl.ds((2 * g + 1) * HEAD_PAD, HEAD_PAD)].astype(F32)
                both = jnp.where(lane < FOX_HEAD, even, pltpu.roll(odd, FOX_HEAD, 1))
                o_ref[:, pl.ds(p * narrow + g * HEAD_PAD, HEAD_PAD)] = both.astype(BF16)

    blk = pl.BlockSpec((t, wide), lambda i: (i, 0))
    return pl.pallas_call(
        body, name=name, out_shape=jax.ShapeDtypeStruct((S, 3 * narrow), BF16), grid=(S // t,),
        in_specs=[blk, blk, blk], out_specs=pl.BlockSpec((t, 3 * narrow), lambda i: (i, 0)),
        compiler_params=_cparams(),
    )(dq, dk, dv)


def _split3(x):
    hi = x.astype(BF16)
    r1 = x - hi.astype(F32)
    mid = r1.astype(BF16)
    lo = (r1 - mid.astype(F32)).astype(BF16)
    return hi, mid, lo


def _tri_matmul(tri, x):
    acc = None
    for part in _split3(x):
        d = lax.dot_general(tri, part, NN, preferred_element_type=F32)
        acc = d if acc is None else acc + d
    return acc


def _log_sigmoid(z):
    return jnp.minimum(z, 0.0) - jnp.log1p(jnp.exp(-jnp.abs(z)))


def _fox_gate_fwd(fl, bf):
    S = fl.shape[0]
    t = _tile(S, 256)

    def body(fl_ref, bf_ref, cum_ref, carry):
        @pl.when(pl.program_id(0) == 0)
        def _():
            carry[...] = jnp.zeros_like(carry)

        logf = _log_sigmoid(fl_ref[...] + bf_ref[...])
        tri = (lax.broadcasted_iota(jnp.int32, (t, t), 0) >= lax.broadcasted_iota(jnp.int32, (t, t), 1)).astype(BF16)
        cum_ref[...] = _tri_matmul(tri, logf) + carry[...]
        carry[...] += jnp.sum(logf, axis=0, keepdims=True)

    row = pl.BlockSpec((t, LANE), lambda i: (i, 0))
    return pl.pallas_call(
        body, name="fox_gate_fwd", out_shape=jax.ShapeDtypeStruct((S, LANE), F32), grid=(S // t,),
        in_specs=[row, pl.BlockSpec((1, LANE), lambda i: (0, 0))], out_specs=row,
        scratch_shapes=[pltpu.VMEM((1, LANE), F32)], compiler_params=_cparams(),
    )(fl, bf)


def _fox_gate_bwd(dcum, fl, bf):
    S = fl.shape[0]
    t = _tile(S, 256)
    n = S // t

    def body(dc_ref, fl_ref, bf_ref, df_ref, db_ref, carry):
        @pl.when(pl.program_id(0) == 0)
        def _():
            carry[...] = jnp.zeros_like(carry)
            db_ref[...] = jnp.zeros_like(db_ref)

        dc = dc_ref[...]
        tri = (lax.broadcasted_iota(jnp.int32, (t, t), 1) >= lax.broadcasted_iota(jnp.int32, (t, t), 0)).astype(BF16)
        dlogf = _tri_matmul(tri, dc) + carry[...]
        carry[...] += jnp.sum(dc, axis=0, keepdims=True)
        z = fl_ref[...] + bf_ref[...]
        e = jnp.exp(-jnp.abs(z))
        sig_neg = jnp.where(z >= 0, e, 1.0) / (1.0 + e)
        lanes = lax.broadcasted_iota(jnp.int32, (t, LANE), 1)
        df = jnp.where(lanes < HEADS, dlogf * sig_neg, 0.0)
        df_ref[...] = df
        db_ref[...] += jnp.sum(df, axis=0, keepdims=True)

    row = pl.BlockSpec((t, LANE), lambda i: (n - 1 - i, 0))
    vec = pl.BlockSpec((1, LANE), lambda i: (0, 0))
    return pl.pallas_call(
        body, name="fox_gate_bwd",
        out_shape=[jax.ShapeDtypeStruct((S, LANE), F32), jax.ShapeDtypeStruct((1, LANE), F32)],
        grid=(n,), in_specs=[row, row, vec], out_specs=[row, vec],
        scratch_shapes=[pltpu.VMEM((1, LANE), F32)], compiler_params=_cparams(),
    )(dcum, fl, bf)


def _rope_tables(S):
    pos = jnp.arange(S, dtype=F32)
    inv_freq = ROPE_BASE ** (-jnp.arange(0, MLA_ROPE, 2, dtype=F32) / MLA_ROPE)
    ang = pos[:, None] * inv_freq[None, :]
    cos, sin = jnp.cos(ang), jnp.sin(ang)
    half = MLA_ROPE // 2
    ones = jnp.ones((S, MLA_NOPE), F32)
    z = lambda w: jnp.zeros((S, w), F32)
    keep = jnp.concatenate([ones, cos, cos, z(HEAD_PAD - MLA_NOPE - MLA_ROPE)], axis=1)
    from_above = jnp.concatenate([z(MLA_NOPE), -sin, z(HEAD_PAD - MLA_NOPE - half)], axis=1)
    from_below = jnp.concatenate([z(MLA_NOPE + half), sin, z(HEAD_PAD - MLA_NOPE - MLA_ROPE)], axis=1)
    return keep, from_above, from_below


def _rope(x, keep, up, down, transpose):
    half = MLA_ROPE // 2
    if transpose:
        return x * keep + pltpu.roll(x * up, half, 1) + pltpu.roll(x * down, HEAD_PAD - half, 1)
    return x * keep + pltpu.roll(x, HEAD_PAD - half, 1) * up + pltpu.roll(x, half, 1) * down


def _rope_heads(x, tables, transpose, scale, name):
    S = x.shape[0]
    t = _tile(S, 256)

    def body(x_ref, a_ref, b_ref, c_ref, o_ref):
        keep, up, down = a_ref[...], b_ref[...], c_ref[...]
        for h in range(HEADS):
            cols = pl.ds(h * HEAD_PAD, HEAD_PAD)
            o_ref[:, cols] = (_rope(x_ref[:, cols], keep, up, down, transpose) * scale).astype(BF16)

    blk = pl.BlockSpec((t, HEADS * HEAD_PAD), lambda i: (i, 0))
    tab = pl.BlockSpec((t, HEAD_PAD), lambda i: (i, 0))
    return pl.pallas_call(
        body, name=name, out_shape=jax.ShapeDtypeStruct(x.shape, BF16), grid=(S // t,),
        in_specs=[blk, tab, tab, tab], out_specs=blk, compiler_params=_cparams(),
    )(x, *tables)


MLA_A_PAD = MLA_QR + MLA_KVR + HEAD_PAD
MLA_KV_IN = MLA_KVR + HEAD_PAD


def _mla_mid_fwd(a, gq, gkv, tables):
    S = a.shape[0]
    t = _tile(S, 512)

    def body(a_ref, gq_ref, gkv_ref, ta_ref, tb_ref, tc_ref, q_ref, kv_ref):
        cq = a_ref[:, :MLA_QR]
        ckv = a_ref[:, MLA_QR:MLA_QR + MLA_KVR]
        kr = a_ref[:, MLA_QR + MLA_KVR:]
        q_ref[...] = (cq * _rstd(cq) * gq_ref[...]).astype(BF16)
        kv_ref[:, :MLA_KVR] = (ckv * _rstd(ckv) * gkv_ref[...]).astype(BF16)
        kv_ref[:, MLA_KVR:] = _rope(kr, ta_ref[...], tb_ref[...], tc_ref[...], False).astype(BF16)

    tab = pl.BlockSpec((t, HEAD_PAD), lambda i: (i, 0))
    return pl.pallas_call(
        body, name="mla_mid_fwd",
        out_shape=[jax.ShapeDtypeStruct((S, MLA_QR), BF16), jax.ShapeDtypeStruct((S, MLA_KV_IN), BF16)],
        grid=(S // t,),
        in_specs=[pl.BlockSpec((t, MLA_A_PAD), lambda i: (i, 0)), pl.BlockSpec((1, MLA_QR), lambda i: (0, 0)),
                  pl.BlockSpec((1, MLA_KVR), lambda i: (0, 0)), tab, tab, tab],
        out_specs=[pl.BlockSpec((t, MLA_QR), lambda i: (i, 0)), pl.BlockSpec((t, MLA_KV_IN), lambda i: (i, 0))],
        compiler_params=_cparams(),
    )(a, gq, gkv, *tables)


def _mla_mid_bwd(a, dq, dkv, gq, gkv, tables):
    S = a.shape[0]
    t = _tile(S, 512)

    def body(a_ref, dq_ref, dkv_ref, gq_ref, gkv_ref, ta_ref, tb_ref, tc_ref, da_ref, dgq_ref, dgkv_ref):
        @pl.when(pl.program_id(0) == 0)
        def _():
            dgq_ref[...] = jnp.zeros_like(dgq_ref)
            dgkv_ref[...] = jnp.zeros_like(dgkv_ref)

        def one(c, dout, g):
            r = _rstd(c)
            n = c * r
            return _norm_bwd(n, r, dout * g), jnp.sum(dout * n, axis=0, keepdims=True)

        dcq, dgq = one(a_ref[:, :MLA_QR], dq_ref[...], gq_ref[...])
        dckv, dgkv = one(a_ref[:, MLA_QR:MLA_QR + MLA_KVR], dkv_ref[:, :MLA_KVR], gkv_ref[...])
        da_ref[:, :MLA_QR] = dcq.astype(BF16)
        da_ref[:, MLA_QR:MLA_QR + MLA_KVR] = dckv.astype(BF16)
        da_ref[:, MLA_QR + MLA_KVR:] = _rope(dkv_ref[:, MLA_KVR:], ta_ref[...], tb_ref[...], tc_ref[...],
                                             True).astype(BF16)
        dgq_ref[...] += dgq
        dgkv_ref[...] += dgkv

    tab = pl.BlockSpec((t, HEAD_PAD), lambda i: (i, 0))
    vq = pl.BlockSpec((1, MLA_QR), lambda i: (0, 0))
    vkv = pl.BlockSpec((1, MLA_KVR), lambda i: (0, 0))
    return pl.pallas_call(
        body, name="mla_mid_bwd",
        out_shape=[jax.ShapeDtypeStruct((S, MLA_A_PAD), BF16), jax.ShapeDtypeStruct((1, MLA_QR), F32),
                   jax.ShapeDtypeStruct((1, MLA_KVR), F32)],
        grid=(S // t,),
        in_specs=[pl.BlockSpec((t, MLA_A_PAD), lambda i: (i, 0)), pl.BlockSpec((t, MLA_QR), lambda i: (i, 0)),
                  pl.BlockSpec((t, MLA_KV_IN), lambda i: (i, 0)), vq, vkv, tab, tab, tab],
        out_specs=[pl.BlockSpec((t, MLA_A_PAD), lambda i: (i, 0)), vq, vkv],
        compiler_params=_cparams(),
    )(a, dq, dkv, gq, gkv, *tables)


FFN_ROWS = 256
FFN_COLS = 1408
HALO = 8


def _shift_down(cur, halo, s, first):
    rolled = pltpu.roll(cur, s, 0)
    hrolled = jnp.where(first, 0.0, pltpu.roll(halo, s, 0))
    row = lax.broadcasted_iota(jnp.int32, (HALO, cur.shape[1]), 0)
    top = jnp.where(row < s, hrolled, rolled[:HALO])
    return jnp.concatenate([top, rolled[HALO:]], axis=0)


def _shift_up(cur, halo, s, last):
    rows = cur.shape[0]
    rolled = pltpu.roll(cur, rows - s, 0)
    hrolled = jnp.where(last, 0.0, pltpu.roll(halo, HALO - s, 0))
    row = lax.broadcasted_iota(jnp.int32, (HALO, cur.shape[1]), 0)
    bottom = jnp.where(row >= HALO - s, hrolled, rolled[rows - HALO:])
    return jnp.concatenate([rolled[:rows - HALO], bottom], axis=0)


def _conv(u, halo, w_ref, b, first):
    u1 = _shift_down(u, halo, 1, first)
    u2 = _shift_down(u, halo, 2, first)
    y = b + w_ref[0:1, :] * u2
    y = y + w_ref[1:2, :] * u1
    y = y + w_ref[2:3, :] * u
    return y, u1, u2


def _ffn_specs(S, tr, tn):
    per = tr // HALO
    tile = pl.BlockSpec((tr, tn), lambda j, i: (i, j))
    before = pl.BlockSpec((HALO, tn), lambda j, i: (jnp.maximum(i * per - 1, 0), j))
    after = pl.BlockSpec((HALO, tn), lambda j, i: (jnp.minimum((i + 1) * per, S // HALO - 1), j))
    w3 = pl.BlockSpec((3, tn), lambda j, i: (0, j))
    w1 = pl.BlockSpec((1, tn), lambda j, i: (0, j))
    return tile, before, after, w3, w1


def _ffn_act_fwd(ug, uv, wg, wv, bg, bv, name):
    S, F = ug.shape
    tr, tn = _tile(S, FFN_ROWS), _tile(F, FFN_COLS)
    tile, before, _, w3, w1 = _ffn_specs(S, tr, tn)

    def body(ug_ref, uv_ref, hg_ref, hv_ref, wg_ref, wv_ref, bg_ref, bv_ref, o_ref):
        first = pl.program_id(1) == 0
        g, _, _ = _conv(ug_ref[...], hg_ref[...], wg_ref, bg_ref[...], first)
        v, _, _ = _conv(uv_ref[...], hv_ref[...], wv_ref, bv_ref[...], first)
        o_ref[...] = (g * jax.nn.sigmoid(g) * v).astype(BF16)

    return pl.pallas_call(
        body, name=name, out_shape=jax.ShapeDtypeStruct((S, F), BF16), grid=(F // tn, S // tr),
        in_specs=[tile, tile, before, before, w3, w3, w1, w1], out_specs=tile, compiler_params=_cparams(),
    )(ug, uv, ug, uv, wg, wv, bg, bv)


def _gate_grads(g, v, da):
    sg = jax.nn.sigmoid(g)
    return da * v * (sg * (1.0 + g * (1.0 - sg))), da * (g * sg)


def _conv_rows_after(u_after, u_tile_end, w_ref, b):
    row = lax.broadcasted_iota(jnp.int32, u_after.shape, 0)
    shifted = lambda s: jnp.where(row < s, pltpu.roll(u_tile_end, s, 0), pltpu.roll(u_after, s, 0))
    return b + w_ref[0:1, :] * shifted(2) + w_ref[1:2, :] * shifted(1) + w_ref[2:3, :] * u_after


def _ffn_bwd_mid(ug, uv, dact, wg, wv, bg, bv, name):
    S, F = ug.shape
    tr, tn = _tile(S, FFN_ROWS), _tile(F, FFN_COLS)
    tile, before, after, w3, w1 = _ffn_specs(S, tr, tn)
    n = S // tr

    def body(ug_ref, uv_ref, hg_ref, hv_ref, ag_ref, av_ref, da_ref, ada_ref, wg_ref, wv_ref, bg_ref, bv_ref,
             dug_ref, duv_ref, dwg_ref, dwv_ref, dbg_ref, dbv_ref):
        first = pl.program_id(1) == 0
        last = pl.program_id(1) == n - 1

        @pl.when(first)
        def _():
            for r in (dwg_ref, dwv_ref, dbg_ref, dbv_ref):
                r[...] = jnp.zeros_like(r)

        ugv, uvv = ug_ref[...], uv_ref[...]
        g, ug1, ug2 = _conv(ugv, hg_ref[...], wg_ref, bg_ref[...], first)
        v, uv1, uv2 = _conv(uvv, hv_ref[...], wv_ref, bv_ref[...], first)
        dyg, dyv = _gate_grads(g, v, da_ref[...])
        g_after = _conv_rows_after(ag_ref[...], ugv[tr - HALO:], wg_ref, bg_ref[...])
        v_after = _conv_rows_after(av_ref[...], uvv[tr - HALO:], wv_ref, bv_ref[...])
        dyg_after, dyv_after = _gate_grads(g_after, v_after, ada_ref[...])

        def back(dy, dy_after, w_ref):
            d1 = _shift_up(dy, dy_after, 1, last)
            d2 = _shift_up(dy, dy_after, 2, last)
            return (w_ref[2:3, :] * dy + w_ref[1:2, :] * d1 + w_ref[0:1, :] * d2).astype(BF16)

        dug_ref[...] = back(dyg, dyg_after, wg_ref)
        duv_ref[...] = back(dyv, dyv_after, wv_ref)

        def add_taps(dw_ref, dy, taps):
            for k, u in enumerate(taps):
                dw_ref[k:k + 1, :] += jnp.sum(dy * u, axis=0, keepdims=True)

        add_taps(dwg_ref, dyg, (ug2, ug1, ugv))
        add_taps(dwv_ref, dyv, (uv2, uv1, uvv))
        dbg_ref[...] += jnp.sum(dyg, axis=0, keepdims=True)
        dbv_ref[...] += jnp.sum(dyv, axis=0, keepdims=True)

    big = jax.ShapeDtypeStruct((S, F), BF16)
    s3, s1 = jax.ShapeDtypeStruct((3, F), F32), jax.ShapeDtypeStruct((1, F), F32)
    return pl.pallas_call(
        body, name=name, out_shape=[big, big, s3, s3, s1, s1], grid=(F // tn, n),
        in_specs=[tile, tile, before, before, after, after, tile, after, w3, w3, w1, w1],
        out_specs=[tile, tile, w3, w3, w1, w1], compiler_params=_cparams(),
    )(ug, uv, ug, uv, ug, uv, dact, dact, wg, wv, bg, bv)


def _dot3(a, b):
    a_hi = a.astype(BF16)
    a_lo = (a - a_hi.astype(F32)).astype(BF16)
    b_hi = b.astype(BF16)
    b_lo = (b - b_hi.astype(F32)).astype(BF16)
    d = lambda p, q: lax.dot_general(p, q, NN, preferred_element_type=F32)
    return d(a_hi, b_hi) + (d(a_hi, b_lo) + d(a_lo, b_hi))


def _ada_mod(c_all, w, b):
    nmod, D, n = w.shape

    def body(c_ref, w_ref, b_ref, o_ref):
        cv = c_ref[...]
        o_ref[0] = _dot3(cv * jax.nn.sigmoid(cv), w_ref[0]) + b_ref[0]

    return pl.pallas_call(
        body, name="ada_mod", out_shape=jax.ShapeDtypeStruct((nmod, 8, n), F32), grid=(nmod,),
        in_specs=[pl.BlockSpec((8, D), lambda m: (0, 0)), pl.BlockSpec((1, D, n), lambda m: (m, 0, 0)),
                  pl.BlockSpec((1, 1, n), lambda m: (m, 0, 0))],
        out_specs=pl.BlockSpec((1, 8, n), lambda m: (m, 0, 0)), compiler_params=_cparams(),
    )(c_all, w, b)


def _ada_grad(c_all_t, dmod):
    D = c_all_t.shape[0]
    nmod, _, n = dmod.shape
    tr = 256

    def body(c_ref, d_ref, dw_ref, db_ref):
        cv = c_ref[...]
        sc = cv * jax.nn.sigmoid(cv)
        dv = d_ref[0]
        acc = sc[:, 0:1] * dv[0:1, :]
        tot = dv[0:1, :]
        for k in range(1, 8):
            acc = acc + sc[:, k:k + 1] * dv[k:k + 1, :]
            tot = tot + dv[k:k + 1, :]
        dw_ref[0] = acc
        db_ref[0] = tot

    return pl.pallas_call(
        body, name="ada_grad",
        out_shape=[jax.ShapeDtypeStruct((nmod, D, n), F32), jax.ShapeDtypeStruct((nmod, 1, n), F32)],
        grid=(nmod, D // tr),
        in_specs=[pl.BlockSpec((tr, 8), lambda m, i: (i, 0)), pl.BlockSpec((1, 8, n), lambda m, i: (m, 0, 0))],
        out_specs=[pl.BlockSpec((1, tr, n), lambda m, i: (m, i, 0)), pl.BlockSpec((1, 1, n), lambda m, i: (m, 0, 0))],
        compiler_params=_cparams(),
    )(c_all_t, dmod)


def _adamw(w, g, m, v, name):
    R, C = w.shape
    t = _row_tile(R, C)

    def body(w_ref, g_ref, m_ref, v_ref, d_ref, nm_ref, nv_ref):
        gv = g_ref[...]
        mn = ADAM_B1 * m_ref[...] + (1.0 - ADAM_B1) * gv
        vn = ADAM_B2 * v_ref[...] + (1.0 - ADAM_B2) * (gv * gv)
        m_hat = mn / (1.0 - ADAM_B1 ** ADAM_STEP)
        v_hat = vn / (1.0 - ADAM_B2 ** ADAM_STEP)
        d_ref[...] = -ADAM_LR * (m_hat / (jnp.sqrt(v_hat) + ADAM_EPS) + ADAM_WD * w_ref[...])
        nm_ref[...] = mn
        nv_ref[...] = vn

    blk = pl.BlockSpec((t, C), lambda i: (i, 0))
    shp = jax.ShapeDtypeStruct((R, C), F32)
    return pl.pallas_call(
        body, name=name, out_shape=[shp, shp, shp], grid=(R // t,), in_specs=[blk] * 4, out_specs=[blk] * 3,
        compiler_params=_cparams(),
    )(w, g, m, v)


def _cat_cols(g4):
    return jnp.concatenate([g4[s] for s in range(4)], axis=-1)


def _cat_rows(g4):
    return jnp.concatenate([g4[s] for s in range(4)], axis=-2)


def _pad_head_cols(w, width):
    K = w.shape[0]
    w = w.reshape(K, HEADS, width)
    return jnp.pad(w, ((0, 0), (0, 0), (0, HEAD_PAD - width))).reshape(K, HEADS * HEAD_PAD)


def _unpad_head_cols(w, width):
    K = w.shape[0]
    return w.reshape(K, HEADS, HEAD_PAD)[:, :, :width].reshape(K, HEADS * width)


def _pad_head_rows(w, width):
    N = w.shape[1]
    w = w.reshape(HEADS, width, N)
    return jnp.pad(w, ((0, 0), (0, HEAD_PAD - width), (0, 0))).reshape(HEADS * HEAD_PAD, N)


def _unpad_head_rows(w, width):
    N = w.shape[1]
    return w.reshape(HEADS, HEAD_PAD, N)[:, :width, :].reshape(HEADS * width, N)


def _prepare_fox_in(fox_in):
    W = {}
    w = _cat_cols(fox_in[:, 0])
    hw = HEADS * FOX_HEAD
    q = w[:, :hw] * FOX_SCALE
    W["fox_qkv"] = jnp.concatenate([q, w[:, hw:3 * hw]], axis=1)
    W["fox_f"] = jnp.pad(w[:, 3 * hw:], ((0, 0), (0, LANE - HEADS)))
    return W


def _prepare_weights(gathered):
    fox_o, mla_a, mla_uq, mla_ukv, mla_o, ffn_in, ffn_out = gathered
    W = {}
    W["fox_o"] = _pad_head_rows(_cat_rows(fox_o[:, 0]), FOX_HEAD)
    w = _cat_rows(mla_a[:, 0])
    lat = MLA_QR + MLA_KVR
    W["mla_a"] = jnp.concatenate(
        [w[:, :lat], jnp.zeros((D_MODEL, MLA_NOPE), BF16), w[:, lat:],
         jnp.zeros((D_MODEL, HEAD_PAD - MLA_NOPE - MLA_ROPE), BF16)], axis=1)
    W["mla_uq"] = _pad_head_cols(_cat_cols(mla_uq[:, 0]), MLA_NOPE + MLA_ROPE)
    w = _cat_cols(mla_ukv[:, 0]).reshape(MLA_KVR, HEADS, MLA_NOPE + MLA_V)
    kn = _pad_head_cols(w[:, :, :MLA_NOPE].reshape(MLA_KVR, -1), MLA_NOPE)
    vv = _pad_head_cols(w[:, :, MLA_NOPE:].reshape(MLA_KVR, -1), MLA_V)
    eye = np.zeros((HEAD_PAD, HEADS, HEAD_PAD), np.float32)
    for j in range(MLA_NOPE, MLA_NOPE + MLA_ROPE):
        eye[j, :, j] = 1.0
    eye = jnp.asarray(eye.reshape(HEAD_PAD, HEADS * HEAD_PAD), BF16)
    bottom = jnp.concatenate([eye, jnp.zeros((HEAD_PAD, HEADS * HEAD_PAD), BF16)], axis=1)
    W["mla_kv"] = jnp.concatenate([jnp.concatenate([kn, vv], axis=1), bottom], axis=0)
    W["mla_o"] = _pad_head_rows(_cat_rows(mla_o[:, 0]), MLA_V)
    w = _cat_cols(ffn_in)
    W["ffn_g"] = [w[i, :, :D_FF] for i in range(2)]
    W["ffn_v"] = [w[i, :, D_FF:] for i in range(2)]
    w = _cat_rows(ffn_out)
    W["ffn_out"] = [w[i] for i in range(2)]
    return W


def _ffn_forward(xin, mod, W, i, conv_w, conv_b):
    shift, scale, gate = mod
    h = _adaln_fwd(xin, scale, shift, name=f"ffn{i}_adaln")
    ug = _mm(h, W["ffn_g"][i], name=f"ffn{i}_up_gate")
    uv = _mm(h, W["ffn_v"][i], name=f"ffn{i}_up_val")
    wg, wv = conv_w[i][:, :D_FF], conv_w[i][:, D_FF:]
    bg, bv = conv_b[i][None, :D_FF], conv_b[i][None, D_FF:]
    act = _ffn_act_fwd(ug, uv, wg, wv, bg, bv, name=f"ffn{i}_act")
    y = _mm(act, W["ffn_out"][i], name=f"ffn{i}_down")
    xout = _residual(xin, gate, y, name=f"ffn{i}_residual")
    return xout, (xin, h, ug, uv, act, y, wg, wv, bg, bv)


def _ffn_backward(dx_out, saved, mod, W, i):
    xin, h, ug, uv, act, y, wg, wv, bg, bv = saved
    shift, scale, gate = mod
    dy, dgate = _gate_bwd(dx_out, y, gate, name=f"ffn{i}_gate_bwd")
    d_out = _mm(act, dy, ta=True, name=f"ffn{i}_dw_out")
    dact = _mm(dy, W["ffn_out"][i], tb=True, name=f"ffn{i}_dact")
    dug, duv, dwg, dwv, dbg, dbv = _ffn_bwd_mid(ug, uv, dact, wg, wv, bg, bv, name=f"ffn{i}_bwd_mid")
    d_in = jnp.concatenate([_mm(h, dug, ta=True, name=f"ffn{i}_dw_gate"),
                            _mm(h, duv, ta=True, name=f"ffn{i}_dw_val")], axis=1)
    dh = _mm(dug, W["ffn_g"][i], tb=True, name=f"ffn{i}_dh_gate")
    dh = _mm(duv, W["ffn_v"][i], tb=True, res=dh, name=f"ffn{i}_dh_val")
    dx, dscale, dshift = _adaln_bwd(xin, dh, dx_out, scale, name=f"ffn{i}_adaln_bwd")
    grads = dict(ffn_w_in=d_in, ffn_w_out=d_out, ffn_conv_w=jnp.concatenate([dwg, dwv], axis=1),
                 ffn_conv_b=jnp.concatenate([dbg, dbv], axis=1)[0])
    return dx, jnp.concatenate([dshift, dscale, dgate], axis=1)[0], grads


def _local_step(x, target, mods, W, small):
    S = x.shape[0]
    tables = _rope_tables(S)
    wide = HEADS * HEAD_PAD
    bf = jnp.pad(small["fox_b_f"], ((0, 0), (0, LANE - HEADS)))
    gq, gkv = small["mla_g_q"], small["mla_g_kv"]

    shift, scale, gate = mods[0]
    h0 = _adaln_fwd(x, scale, shift, name="fox_adaln")
    qkv = _mm(h0, W["fox_qkv"], out_dtype=BF16, name="fox_qkv")
    fl = _mm(h0, W["fox_f"], name="fox_gate_logits")
    cum = _fox_gate_fwd(fl, bf)
    q_fox, k_fox, v_fox = _fox_expand(qkv, cum, name="fox_expand")
    o0, lse0, gathered = _attn_fwd(q_fox, 0, k_fox, 0, v_fox, 0, name="fox_attn_fwd", gather=W["pending"])
    W = {**W, **_prepare_weights(gathered)}
    y0 = _mm(o0, W["fox_o"], name="fox_out")
    x1 = _residual(x, gate, y0, name="fox_residual")
    x2, ffn0 = _ffn_forward(x1, mods[1], W, 0, small["ffn_conv_w"], small["ffn_conv_b"])

    shift, scale, gate = mods[2]
    h2 = _adaln_fwd(x2, scale, shift, name="mla_adaln")
    a = _mm(h2, W["mla_a"], name="mla_down")
    cqn, ckv_in = _mla_mid_fwd(a, gq, gkv, tables)
    q_raw = _mm(cqn, W["mla_uq"], name="mla_up_q")
    q_cat = _rope_heads(q_raw, tables, False, MLA_SCALE, name="mla_rope_q")
    kv = _mm(ckv_in, W["mla_kv"], out_dtype=BF16, name="mla_up_kv")
    o1, lse1, _ = _attn_fwd(q_cat, 0, kv, 0, kv, HEADS, name="mla_attn_fwd")
    y2 = _mm(o1, W["mla_o"], name="mla_out")
    x3 = _residual(x2, gate, y2, name="mla_residual")
    x4, ffn1 = _ffn_forward(x3, mods[3], W, 1, small["ffn_conv_w"], small["ffn_conv_b"])

    loss, dx4, d_final_g = _final_loss(x4, small["final_g"], target)

    dx3, dmod3, g_ffn1 = _ffn_backward(dx4, ffn1, mods[3], W, 1)

    shift, scale, gate = mods[2]
    dy, dgate = _gate_bwd(dx3, y2, gate, name="mla_gate_bwd")
    d_mla_o = _mm(o1, dy, ta=True, name="mla_dw_out")
    do = _mm(dy, W["mla_o"], tb=True, out_dtype=BF16, name="mla_do")
    delta = _attn_delta(do, o1, name="mla_attn_delta")
    dq, dk, dv = _attn_bwd(q_cat, 0, kv, 0, kv, HEADS, do, lse1.reshape(HEADS, 1, S),
                           delta, name="mla_attn_bwd")
    dq_raw = _rope_heads(dq, tables, True, MLA_SCALE, name="mla_rope_q_bwd")
    d_mla_uq = _mm(cqn, dq_raw, ta=True, name="mla_dw_uq")
    dcqn = _mm(dq_raw, W["mla_uq"], tb=True, name="mla_dcq")
    dkv = jnp.concatenate([dk, dv], axis=1)
    d_mla_kv = _mm(ckv_in, dkv, ta=True, name="mla_dw_kv")
    dckv_in = _mm(dkv, W["mla_kv"], tb=True, name="mla_dckv")
    da, dgq, dgkv = _mla_mid_bwd(a, dcqn, dckv_in, gq, gkv, tables)
    d_mla_a = _mm(h2, da, ta=True, name="mla_dw_a")
    dh = _mm(da, W["mla_a"], tb=True, name="mla_dh")
    dx2, dscale, dshift = _adaln_bwd(x2, dh, dx3, scale, name="mla_adaln_bwd")
    dmod2 = jnp.concatenate([dshift, dscale, dgate], axis=1)[0]

    dx1, dmod1, g_ffn0 = _ffn_backward(dx2, ffn0, mods[1], W, 0)

    shift, scale, gate = mods[0]
    dy, dgate = _gate_bwd(dx1, y0, gate, name="fox_gate_bwd")
    d_fox_o = _mm(o0, dy, ta=True, name="fox_dw_out")
    do = _mm(dy, W["fox_o"], tb=True, out_dtype=BF16, name="fox_do")
    delta = _attn_delta(do, o0, name="fox_attn_delta")
    dq, dk, dv, dcq, dck = _attn_bwd(q_fox, 0, k_fox, 0, v_fox, 0, do, lse0.reshape(HEADS, 1, S),
                                     delta, name="fox_attn_bwd", bias_grads=True, dq_dtype=BF16)
    dcum = jnp.pad((dcq[:, 0, :] - dck[:, 0, :]).T, ((0, 0), (0, LANE - HEADS)))
    dfl, dbf = _fox_gate_bwd(dcum, fl, bf)
    dqkv = _fox_compress(dq, dk, dv, name="fox_compress")
    d_fox_qkv = _mm(h0, dqkv, ta=True, name="fox_dw_in")
    d_fox_f = _mm(h0, dfl, ta=True, name="fox_dw_gate")
    dh = _mm(dqkv, W["fox_qkv"], tb=True, name="fox_dh")
    dh = _mm(dfl, W["fox_f"], tb=True, res=dh, name="fox_dh_gate")
    dx0, dscale, dshift = _adaln_bwd(x, dh, dx1, scale, name="fox_adaln_bwd")
    dmod0 = jnp.concatenate([dshift, dscale, dgate], axis=1)[0]

    G = {}
    hw = HEADS * FOX_HEAD
    G["fox_w_in"] = jnp.concatenate([d_fox_qkv[:, :hw] * FOX_SCALE,
                                     d_fox_qkv[:, hw:], d_fox_f[:, :HEADS]], axis=1)[None]
    G["fox_b_f"] = dbf[:, :HEADS]
    G["fox_w_o"] = _unpad_head_rows(d_fox_o, FOX_HEAD)[None]
    lat = MLA_QR + MLA_KVR
    G["mla_w_a"] = jnp.concatenate([d_mla_a[:, :lat], d_mla_a[:, lat + MLA_NOPE:lat + MLA_NOPE + MLA_ROPE]],
                                   axis=1)[None]
    G["mla_g_q"] = dgq
    G["mla_g_kv"] = dgkv
    G["mla_w_uq"] = _unpad_head_cols(d_mla_uq, MLA_NOPE + MLA_ROPE)[None]
    dkn = d_mla_kv[:MLA_KVR, :wide].reshape(MLA_KVR, HEADS, HEAD_PAD)[:, :, :MLA_NOPE]
    dvv = d_mla_kv[:MLA_KVR, wide:].reshape(MLA_KVR, HEADS, HEAD_PAD)[:, :, :MLA_V]
    G["mla_w_ukv"] = jnp.concatenate([dkn, dvv], axis=2).reshape(MLA_KVR, -1)[None]
    G["mla_w_o"] = _unpad_head_rows(d_mla_o, MLA_V)[None]
    for name in ("ffn_w_in", "ffn_w_out", "ffn_conv_w", "ffn_conv_b"):
        G[name] = jnp.stack([g_ffn0[name], g_ffn1[name]])
    G["final_g"] = d_final_g[0]
    dmods = jnp.stack([dmod0, dmod1, dmod2, dmod3])
    return loss, dx0, dmods, G


PACKED = [("fox_w_in", 2), ("fox_w_o", 1), ("mla_w_a", 1), ("mla_w_uq", 2), ("mla_w_ukv", 2), ("mla_w_o", 1),
          ("ffn_w_in", 2), ("ffn_w_out", 1), ("fox_b_f", None), ("mla_g_q", 1), ("mla_g_kv", 1),
          ("ffn_conv_w", 2), ("ffn_conv_b", None), ("final_g", None)]


def _shard_of(g, axis, s):
    if axis is None:
        return g
    n = g.shape[axis] // 4
    return lax.slice_in_dim(g, s * n, (s + 1) * n, axis=axis)


def _pack_plan(G):
    shard_size = lambda name, axis: math.prod(_shard_of(G[name], axis, 0).shape)
    big = [(name, axis) for name, axis in PACKED if shard_size(name, axis) % PACK_ROW == 0]
    small = [(name, axis) for name, axis in PACKED if shard_size(name, axis) % PACK_ROW != 0]
    whole_tiles = lambda size: -(-size // (8 * PACK_ROW)) * 8
    runs = [([item], whole_tiles(shard_size(*item))) for item in big]
    runs.append((small, whole_tiles(sum(shard_size(*item) for item in small))))
    used = sum(rows for _, rows in runs)
    unit = 2 * PACK_ROWS_ALIGN
    return runs, -(-used // unit) * unit


def _pack(G):
    runs, total = _pack_plan(G)
    pieces = []
    for items, rows in runs:
        per_chip = [jnp.concatenate([_shard_of(G[name], axis, s).reshape(-1) for name, axis in items])
                    for s in range(4)]
        flat = jnp.stack(per_chip)
        flat = jnp.pad(flat, ((0, 0), (0, rows * PACK_ROW - flat.shape[1])))
        pieces.append(flat.reshape(4, rows, PACK_ROW))
    used = sum(rows for _, rows in runs)
    if total > used:
        pieces.append(jnp.zeros((4, total - used, PACK_ROW), F32))
    return jnp.concatenate(pieces, axis=1).reshape(4, 2, total // 2, PACK_ROW)


def _unpack(both, G_like):
    runs, total = _pack_plan(G_like)
    table = both.reshape(total, PACK_ROW)
    out, row = {}, 0
    for items, rows in runs:
        flat, off = table[row:row + rows].reshape(-1), 0
        for name, axis in items:
            shape = _shard_of(G_like[name], axis, 0).shape
            out[name] = flat[off:off + math.prod(shape)].reshape(shape)
            off += math.prod(shape)
        row += rows
    return out


WEIGHTS = ['ada_w', 'ada_b', 'fox_w_in', 'fox_b_f', 'fox_w_o', 'mla_w_a', 'mla_g_q', 'mla_g_kv', 'mla_w_uq',
           'mla_w_ukv', 'mla_w_o', 'ffn_w_in', 'ffn_conv_w', 'ffn_conv_b', 'ffn_w_out', 'final_g']
BIG = ['fox_w_in', 'fox_w_o', 'mla_w_a', 'mla_w_uq', 'mla_w_ukv', 'mla_w_o', 'ffn_w_in', 'ffn_w_out']
SMALL = ['ada_b', 'fox_b_f', 'mla_g_q', 'mla_g_kv', 'ffn_conv_w', 'ffn_conv_b', 'final_g']


def _as2d(a):
    return a.reshape(-1, a.shape[-1])


def kernel(x, c, ada_w, ada_b, fox_w_in, fox_b_f, fox_w_o, mla_w_a, mla_g_q, mla_g_kv, mla_w_uq, mla_w_ukv, mla_w_o, ffn_w_in, ffn_conv_w, ffn_conv_b, ffn_w_out, final_g, loss_target, m_ada_w, m_ada_b, m_fox_w_in, m_fox_b_f, m_fox_w_o, m_mla_w_a, m_mla_g_q, m_mla_g_kv, m_mla_w_uq, m_mla_w_ukv, m_mla_w_o, m_ffn_w_in, m_ffn_conv_w, m_ffn_conv_b, m_ffn_w_out, m_final_g, v_ada_w, v_ada_b, v_fox_w_in, v_fox_b_f, v_fox_w_o, v_mla_w_a, v_mla_g_q, v_mla_g_kv, v_mla_w_uq, v_mla_w_ukv, v_mla_w_o, v_ffn_w_in, v_ffn_conv_w, v_ffn_conv_b, v_ffn_w_out, v_final_g):
    w = dict(ada_w=ada_w, ada_b=ada_b, fox_w_in=fox_w_in, fox_b_f=fox_b_f, fox_w_o=fox_w_o, mla_w_a=mla_w_a,
             mla_g_q=mla_g_q, mla_g_kv=mla_g_kv, mla_w_uq=mla_w_uq, mla_w_ukv=mla_w_ukv, mla_w_o=mla_w_o,
             ffn_w_in=ffn_w_in, ffn_conv_w=ffn_conv_w, ffn_conv_b=ffn_conv_b, ffn_w_out=ffn_w_out, final_g=final_g)
    m = dict(ada_w=m_ada_w, ada_b=m_ada_b, fox_w_in=m_fox_w_in, fox_b_f=m_fox_b_f, fox_w_o=m_fox_w_o,
             mla_w_a=m_mla_w_a, mla_g_q=m_mla_g_q, mla_g_kv=m_mla_g_kv, mla_w_uq=m_mla_w_uq,
             mla_w_ukv=m_mla_w_ukv, mla_w_o=m_mla_w_o, ffn_w_in=m_ffn_w_in, ffn_conv_w=m_ffn_conv_w,
             ffn_conv_b=m_ffn_conv_b, ffn_w_out=m_ffn_w_out, final_g=m_final_g)
    v = dict(ada_w=v_ada_w, ada_b=v_ada_b, fox_w_in=v_fox_w_in, fox_b_f=v_fox_b_f, fox_w_o=v_fox_w_o,
             mla_w_a=v_mla_w_a, mla_g_q=v_mla_g_q, mla_g_kv=v_mla_g_kv, mla_w_uq=v_mla_w_uq,
             mla_w_ukv=v_mla_w_ukv, mla_w_o=v_mla_w_o, ffn_w_in=v_ffn_w_in, ffn_conv_w=v_ffn_conv_w,
             ffn_conv_b=v_ffn_conv_b, ffn_w_out=v_ffn_w_out, final_g=v_final_g)
    D = D_MODEL
    xi, yi, ci = lax.axis_index("x"), lax.axis_index("y"), lax.axis_index("c")
    dev = 4 * xi + 2 * yi + ci

    n_ada = ada_w.shape[-1]
    small_parts = [c.reshape(-1), ffn_conv_w.reshape(-1), mla_g_q.reshape(-1), mla_g_kv.reshape(-1)]
    sizes = [p.shape[0] for p in small_parts]
    flat = jnp.concatenate(small_parts)
    flat = jnp.pad(flat, (0, -flat.shape[0] % LANE))[None]
    got = _all_gather8(flat, name="gather_cond")[:, 0]
    offs = np.cumsum([0] + sizes)
    c_all = got[:, offs[0]:offs[1]]
    chips = got[0::2]
    conv_w_full = jnp.concatenate(
        [chips[s, offs[1]:offs[2]].reshape(ffn_conv_w.shape) for s in range(4)], axis=2)
    gq_full = jnp.concatenate([chips[s, offs[2]:offs[3]] for s in range(4)])[None]
    gkv_full = jnp.concatenate([chips[s, offs[3]:offs[4]] for s in range(4)])[None]

    mod_shard = _ada_mod(c_all, ada_w.reshape(4, D, n_ada), ada_b.reshape(4, 1, n_ada))
    mod_all = _all_gather8(mod_shard.reshape(4 * 8, n_ada), name="gather_mod")
    mod_all = mod_all[0::2].reshape(4, 4, 8, n_ada)
    mine = lax.dynamic_index_in_dim(mod_all, dev, axis=2, keepdims=False)
    mod_rows = jnp.transpose(mine, (1, 0, 2)).reshape(4, 4 * n_ada)
    mods = [(mod_rows[k:k + 1, :D], mod_rows[k:k + 1, D:2 * D], mod_rows[k:k + 1, 2 * D:]) for k in range(4)]

    shards = [w[name].astype(BF16) for name in BIG]
    W = _prepare_fox_in(_all_gather_chips(shards[:1], name="gather_weights")[0])
    W["pending"] = shards[1:]
    small = dict(fox_b_f=fox_b_f, mla_g_q=gq_full, mla_g_kv=gkv_full, ffn_conv_w=conv_w_full,
                 ffn_conv_b=ffn_conv_b, final_g=final_g[None])

    loss, dx, dmods, G = _local_step(x[0], loss_target[0], mods, W, small)

    loss_row = jnp.pad(loss[:, :1], ((0, 0), (0, dmods.shape[1] - 1)))
    gathered_rows = _all_gather8(jnp.concatenate([dmods, loss_row]), name="gather_dmod")
    dmod_all = gathered_rows[:, :4]
    loss_total = jnp.sum(gathered_rows[:, 4, 0])
    chip = 2 * xi + yi
    dmod_cols = lax.dynamic_slice_in_dim(dmod_all, chip * n_ada, n_ada, axis=2)
    g_ada_w, g_ada_b = _ada_grad(c_all.T, jnp.transpose(dmod_cols, (1, 0, 2)))
    grads = dict(ada_w=g_ada_w.reshape(ada_w.shape), ada_b=g_ada_b.reshape(ada_b.shape))

    packed = _pack(G)
    recv = _exchange_halves(packed)
    part = _add_halves(packed, recv, ci.reshape(1).astype(jnp.int32))
    from_chips = _exchange_chips(part)
    half = _add_chips(from_chips)
    other = _share_halves(half)
    both = jnp.stack([jnp.where(ci == 0, half, other), jnp.where(ci == 0, other, half)])
    grads.update(_unpack(both, G))

    delta, new_m, new_v = {}, {}, {}
    for name in BIG + ["ada_w"]:
        shape = w[name].shape
        d_, m_, v_ = _adamw(_as2d(w[name]), _as2d(grads[name]), _as2d(m[name]), _as2d(v[name]), name=f"adamw_{name}")
        delta[name], new_m[name], new_v[name] = d_.reshape(shape), m_.reshape(shape), v_.reshape(shape)
        grads[name] = grads[name].reshape(shape)
    sizes = [math.prod(w[name].shape) for name in SMALL]
    total = sum(sizes)
    rows = -(-total // LANE)
    rows += -rows % 8

    def pack_small(d):
        flat = jnp.concatenate([d[name].reshape(-1) for name in SMALL])
        return jnp.pad(flat, (0, rows * LANE - total)).reshape(rows, LANE)

    outs = _adamw(pack_small(w), pack_small(grads), pack_small(m), pack_small(v), name="adamw_small")
    off = 0
    for name, size in zip(SMALL, sizes):
        shape = w[name].shape
        for dst, src in zip((delta, new_m, new_v), outs):
            dst[name] = src.reshape(-1)[off:off + size].reshape(shape)
        grads[name] = grads[name].reshape(shape)
        off += size

    return (loss_total, dx[None], *[grads[n] for n in WEIGHTS], *[delta[n] for n in WEIGHTS],
            *[new_m[n] for n in WEIGHTS], *[new_v[n] for n in WEIGHTS])
```

```python
import functools
import math

import numpy as np
import jax
import jax.numpy as jnp
from jax import lax
from jax.experimental import pallas as pl
from jax.experimental.pallas import tpu as pltpu

F32 = jnp.float32
BF16 = jnp.bfloat16
MESH = pl.DeviceIdType.MESH

D_MODEL = 1024
HEADS = 16
HEAD_PAD = 128
FOX_HEAD = 64
MLA_NOPE = 64
MLA_ROPE = 32
MLA_V = 64
MLA_QR = 384
MLA_KVR = 256
D_FF = 2816
NORM_EPS = 1e-6
ROPE_BASE = 10000.0
ADAM_LR = 0.001
ADAM_B1 = 0.9
ADAM_B2 = 0.999
ADAM_EPS = 1e-08
ADAM_WD = 0.01
ADAM_STEP = 10
FOX_SCALE = FOX_HEAD ** -0.5
MLA_SCALE = (MLA_NOPE + MLA_ROPE) ** -0.5
NEG_BIG = -1e30
VMEM_LIMIT = 56 * 1024 * 1024
LANE = 128
MM_ROWS, MM_COLS, MM_DEPTH = 1408, 1408, 1408
PACK_ROW = 1024
PACK_ROWS_ALIGN = 256

NT = (((1,), (1,)), ((), ()))
TN = (((0,), (0,)), ((), ()))
NN = (((1,), (0,)), ((), ()))


def _cparams():
    return pltpu.CompilerParams(vmem_limit_bytes=VMEM_LIMIT)


def _tile(n, cap):
    if n <= cap:
        return n
    for t in range(cap - cap % LANE, 0, -LANE):
        if n % t == 0:
            return t
    raise ValueError((n, cap))


def _row_tile(rows, cols, limit_bytes=1 << 20):
    best = None
    for t in range(8, rows + 1, 8):
        if rows % t == 0 and t * cols * 4 <= limit_bytes:
            best = t
    return best if best is not None else rows


def _me():
    return lax.axis_index("x"), lax.axis_index("y"), lax.axis_index("c")


def _all_gather8(v, name):
    R, N = v.shape

    def body(v_ref, o_ref, ssem, rsem):
        x, y, c = _me()
        me = 4 * x + 2 * y + c
        o_ref[me] = v_ref[...]
        copies = []
        for k in range(1, 8):
            peer = (1 - x if k & 4 else x, 1 - y if k & 2 else y, 1 - c if k & 1 else c)
            cp = pltpu.make_async_remote_copy(
                src_ref=v_ref, dst_ref=o_ref.at[me], send_sem=ssem.at[k - 1], recv_sem=rsem.at[k - 1],
                device_id=peer, device_id_type=MESH)
            cp.start()
            copies.append(cp)
        for cp in copies:
            cp.wait()

    return pl.pallas_call(
        body, name=name,
        out_shape=jax.ShapeDtypeStruct((8, R, N), v.dtype),
        in_specs=[pl.BlockSpec(memory_space=pltpu.VMEM)],
        out_specs=pl.BlockSpec(memory_space=pltpu.VMEM),
        scratch_shapes=[pltpu.SemaphoreType.DMA((7,)), pltpu.SemaphoreType.DMA((7,))],
    )(v)


def _all_gather_chips(shards, name):
    n = len(shards)

    def body(*refs):
        copies = _chip_gather_copies(refs[:n], refs[n:2 * n], *refs[2 * n:])
        for cp in copies:
            cp.start()
        for cp in copies:
            cp.wait()

    return pl.pallas_call(
        body, name=name,
        out_shape=[jax.ShapeDtypeStruct((4,) + s.shape, s.dtype) for s in shards],
        in_specs=[pl.BlockSpec(memory_space=pl.ANY)] * n,
        out_specs=[pl.BlockSpec(memory_space=pl.ANY)] * n,
        scratch_shapes=[pltpu.SemaphoreType.DMA((3 * n,)), pltpu.SemaphoreType.DMA((3 * n,)),
                        pltpu.SemaphoreType.DMA((n,))],
    )(*shards)


def _exchange_halves(g):
    _, _, R, P = g.shape

    def body(g_ref, o_ref, ssem, rsem):
        x, y, c = _me()
        copies = []
        for s in range(4):
            cp = pltpu.make_async_remote_copy(
                src_ref=g_ref.at[s, 1 - c], dst_ref=o_ref.at[s], send_sem=ssem.at[s], recv_sem=rsem.at[s],
                device_id=(x, y, 1 - c), device_id_type=MESH)
            cp.start()
            copies.append(cp)
        for cp in copies:
            cp.wait()

    return pl.pallas_call(
        body, name="grad_exchange_sibling",
        out_shape=jax.ShapeDtypeStruct((4, R, P), g.dtype),
        in_specs=[pl.BlockSpec(memory_space=pl.ANY)],
        out_specs=pl.BlockSpec(memory_space=pl.ANY),
        scratch_shapes=[pltpu.SemaphoreType.DMA((4,)), pltpu.SemaphoreType.DMA((4,))],
    )(g)


def _exchange_chips(a):
    _, R, P = a.shape

    def body(a_ref, o_ref, ssem, rsem, lsem):
        x, y, c = _me()
        me = 2 * x + y
        own = pltpu.make_async_copy(a_ref.at[me], o_ref.at[me], lsem)
        own.start()
        copies = []
        for k in (1, 2, 3):
            px, py = (1 - x if k & 2 else x), (1 - y if k & 1 else y)
            cp = pltpu.make_async_remote_copy(
                src_ref=a_ref.at[2 * px + py], dst_ref=o_ref.at[me], send_sem=ssem.at[k - 1],
                recv_sem=rsem.at[k - 1], device_id=(px, py, c), device_id_type=MESH)
            cp.start()
            copies.append(cp)
        own.wait()
        for cp in copies:
            cp.wait()

    return pl.pallas_call(
        body, name="grad_exchange_chips",
        out_shape=jax.ShapeDtypeStruct((4, R, P), a.dtype),
        in_specs=[pl.BlockSpec(memory_space=pl.ANY)],
        out_specs=pl.BlockSpec(memory_space=pl.ANY),
        scratch_shapes=[pltpu.SemaphoreType.DMA((3,)), pltpu.SemaphoreType.DMA((3,)), pltpu.SemaphoreType.DMA],
    )(a)


def _share_halves(f):
    R, P = f.shape

    def body(f_ref, o_ref, ssem, rsem):
        x, y, c = _me()
        cp = pltpu.make_async_remote_copy(
            src_ref=f_ref, dst_ref=o_ref, send_sem=ssem, recv_sem=rsem,
            device_id=(x, y, 1 - c), device_id_type=MESH)
        cp.start()
        cp.wait()

    return pl.pallas_call(
        body, name="grad_share_sibling",
        out_shape=jax.ShapeDtypeStruct((R, P), f.dtype),
        in_specs=[pl.BlockSpec(memory_space=pl.ANY)],
        out_specs=pl.BlockSpec(memory_space=pl.ANY),
        scratch_shapes=[pltpu.SemaphoreType.DMA, pltpu.SemaphoreType.DMA],
    )(f)


def _add_halves(g, r, c_idx):
    _, _, R, P = g.shape
    t = PACK_ROWS_ALIGN

    def body(c_ref, g_ref, r_ref, o_ref):
        o_ref[...] = (g_ref[0] + r_ref[...]).astype(BF16)

    return pl.pallas_call(
        body, name="grad_add_halves",
        out_shape=jax.ShapeDtypeStruct((4, R, P), BF16),
        grid_spec=pltpu.PrefetchScalarGridSpec(
            num_scalar_prefetch=1, grid=(4, R // t),
            in_specs=[pl.BlockSpec((1, 1, t, P), lambda s, i, c_ref: (s, c_ref[0], i, 0)),
                      pl.BlockSpec((1, t, P), lambda s, i, c_ref: (s, i, 0))],
            out_specs=pl.BlockSpec((1, t, P), lambda s, i, c_ref: (s, i, 0))),
        compiler_params=_cparams(),
    )(c_idx, g, r)


def _add_chips(b):
    _, R, P = b.shape
    t = PACK_ROWS_ALIGN

    def body(b_ref, o_ref):
        b0, b1, b2, b3 = (b_ref[k].astype(F32) for k in range(4))
        o_ref[...] = ((b0 + b1) + b2) + b3

    return pl.pallas_call(
        body, name="grad_add_chips",
        out_shape=jax.ShapeDtypeStruct((R, P), F32),
        grid=(R // t,),
        in_specs=[pl.BlockSpec((4, t, P), lambda i: (0, i, 0))],
        out_specs=pl.BlockSpec((t, P), lambda i: (i, 0)),
        compiler_params=_cparams(),
    )(b)


def _mm(a, b, *, ta=False, tb=False, out_dtype=F32, res=None, name):
    if ta:
        K, M = a.shape
    else:
        M, K = a.shape
    N = b.shape[0] if tb else b.shape[1]
    assert (b.shape[1] if tb else b.shape[0]) == K, (a.shape, b.shape, ta, tb)
    tm, tn, tk = _tile(M, MM_ROWS), _tile(N, MM_COLS), _tile(K, MM_DEPTH)
    nk = K // tk
    has_res = res is not None
    dims = (((0 if ta else 1,), (1 if tb else 0,)), ((), ()))

    def body(*refs):
        a_ref, b_ref = refs[:2]
        r_ref = refs[2] if has_res else None
        o_ref = refs[3] if has_res else refs[2]

        def finish(r):
            if has_res:
                r = r + r_ref[...]
            o_ref[...] = r.astype(out_dtype)

        prod = lax.dot_general(a_ref[...].astype(BF16), b_ref[...].astype(BF16), dims,
                               preferred_element_type=F32)
        if nk == 1:
            finish(prod)
            return
        acc_ref = refs[-1]
        k = pl.program_id(2)

        @pl.when(k == 0)
        def _():
            acc_ref[...] = prod

        @pl.when(k > 0)
        def _():
            acc_ref[...] += prod

        @pl.when(k == nk - 1)
        def _():
            finish(acc_ref[...])

    a_spec = (pl.BlockSpec((tk, tm), lambda j, i, k: (k, i)) if ta
              else pl.BlockSpec((tm, tk), lambda j, i, k: (i, k)))
    b_spec = (pl.BlockSpec((tn, tk), lambda j, i, k: (j, k)) if tb
              else pl.BlockSpec((tk, tn), lambda j, i, k: (k, j)))
    in_specs = [a_spec, b_spec]
    args = [a, b]
    if has_res:
        in_specs.append(pl.BlockSpec((tm, tn), lambda j, i, k: (i, j)))
        args.append(res)
    return pl.pallas_call(
        body, name=name,
        out_shape=jax.ShapeDtypeStruct((M, N), out_dtype),
        grid=(N // tn, M // tm, nk),
        in_specs=in_specs,
        out_specs=pl.BlockSpec((tm, tn), lambda j, i, k: (i, j)),
        scratch_shapes=[pltpu.VMEM((tm, tn), F32)] if nk > 1 else [],
        compiler_params=_cparams(),
    )(*args)


def _rstd(xv):
    return lax.rsqrt(jnp.mean(xv * xv, axis=-1, keepdims=True) + NORM_EPS)


def _norm_bwd(n, r, dn):
    return r * (dn - n * jnp.mean(dn * n, axis=-1, keepdims=True))


def _adaln_fwd(x, scale, shift, name):
    S, D = x.shape
    t = _tile(S, 512)

    def body(x_ref, sc_ref, sh_ref, h_ref):
        xv = x_ref[...]
        h_ref[...] = (xv * _rstd(xv) * (1.0 + sc_ref[...]) + sh_ref[...]).astype(BF16)

    row = pl.BlockSpec((t, D), lambda i: (i, 0))
    vec = pl.BlockSpec((1, D), lambda i: (0, 0))
    return pl.pallas_call(
        body, name=name, out_shape=jax.ShapeDtypeStruct((S, D), BF16), grid=(S // t,),
        in_specs=[row, vec, vec], out_specs=row, compiler_params=_cparams(),
    )(x, scale, shift)


def _adaln_bwd(x, dh, dx_next, scale, name):
    S, D = x.shape
    t = _tile(S, 512)

    def body(x_ref, dh_ref, dxn_ref, sc_ref, dx_ref, dsc_ref, dsh_ref):
        @pl.when(pl.program_id(0) == 0)
        def _():
            dsc_ref[...] = jnp.zeros_like(dsc_ref)
            dsh_ref[...] = jnp.zeros_like(dsh_ref)

        xv = x_ref[...]
        r = _rstd(xv)
        n = xv * r
        dh = dh_ref[...]
        dx_ref[...] = dxn_ref[...] + _norm_bwd(n, r, dh * (1.0 + sc_ref[...]))
        dsc_ref[...] += jnp.sum(dh * n, axis=0, keepdims=True)
        dsh_ref[...] += jnp.sum(dh, axis=0, keepdims=True)

    row = pl.BlockSpec((t, D), lambda i: (i, 0))
    vec = pl.BlockSpec((1, D), lambda i: (0, 0))
    return pl.pallas_call(
        body, name=name,
        out_shape=[jax.ShapeDtypeStruct((S, D), F32), jax.ShapeDtypeStruct((1, D), F32),
                   jax.ShapeDtypeStruct((1, D), F32)],
        grid=(S // t,), in_specs=[row, row, row, vec], out_specs=[row, vec, vec],
        compiler_params=_cparams(),
    )(x, dh, dx_next, scale)


def _residual(x, gate, y, name):
    S, D = x.shape
    t = _tile(S, 512)

    def body(x_ref, g_ref, y_ref, o_ref):
        o_ref[...] = x_ref[...] + g_ref[...] * y_ref[...]

    row = pl.BlockSpec((t, D), lambda i: (i, 0))
    vec = pl.BlockSpec((1, D), lambda i: (0, 0))
    return pl.pallas_call(
        body, name=name, out_shape=jax.ShapeDtypeStruct((S, D), F32), grid=(S // t,),
        in_specs=[row, vec, row], out_specs=row, compiler_params=_cparams(),
    )(x, gate, y)


def _gate_bwd(dx, y, gate, name):
    S, D = dx.shape
    t = _tile(S, 512)

    def body(dx_ref, y_ref, g_ref, dy_ref, dg_ref):
        @pl.when(pl.program_id(0) == 0)
        def _():
            dg_ref[...] = jnp.zeros_like(dg_ref)

        dxv = dx_ref[...]
        dy_ref[...] = (dxv * g_ref[...]).astype(BF16)
        dg_ref[...] += jnp.sum(dxv * y_ref[...], axis=0, keepdims=True)

    row = pl.BlockSpec((t, D), lambda i: (i, 0))
    vec = pl.BlockSpec((1, D), lambda i: (0, 0))
    return pl.pallas_call(
        body, name=name,
        out_shape=[jax.ShapeDtypeStruct((S, D), BF16), jax.ShapeDtypeStruct((1, D), F32)],
        grid=(S // t,), in_specs=[row, row, vec], out_specs=[row, vec], compiler_params=_cparams(),
    )(dx, y, gate)


def _final_loss(x, g, target):
    S, D = x.shape
    t = _tile(S, 512)

    def body(x_ref, g_ref, t_ref, loss_ref, dx_ref, dg_ref):
        @pl.when(pl.program_id(0) == 0)
        def _():
            loss_ref[...] = jnp.zeros_like(loss_ref)
            dg_ref[...] = jnp.zeros_like(dg_ref)

        xv = x_ref[...]
        gv = g_ref[...]
        r = _rstd(xv)
        n = xv * r
        err = n * gv - t_ref[...]
        part = jnp.sum(jnp.mean(err * err, axis=-1, keepdims=True), axis=0, keepdims=True)
        loss_ref[...] += jnp.broadcast_to(0.5 * part, loss_ref.shape)
        dy = err * (1.0 / D)
        dg_ref[...] += jnp.sum(dy * n, axis=0, keepdims=True)
        dx_ref[...] = _norm_bwd(n, r, dy * gv)

    row = pl.BlockSpec((t, D), lambda i: (i, 0))
    vec = pl.BlockSpec((1, D), lambda i: (0, 0))
    one = pl.BlockSpec((1, LANE), lambda i: (0, 0))
    return pl.pallas_call(
        body, name="final_loss",
        out_shape=[jax.ShapeDtypeStruct((1, LANE), F32), jax.ShapeDtypeStruct((S, D), F32),
                   jax.ShapeDtypeStruct((1, D), F32)],
        grid=(S // t,), in_specs=[row, vec, row], out_specs=[one, row, vec], compiler_params=_cparams(),
    )(x, g, target)


ATTN_TILE = 1024
ATTN_ROWS = 16
ONES_LANE = 127


def _causal_pairs(n, k_major):
    if k_major:
        pairs = [(i, j) for j in range(n) for i in range(j, n)]
    else:
        pairs = [(i, j) for i in range(n) for j in range(i + 1)]
    return (jnp.asarray(np.array([p[0] for p in pairs], np.int32)),
            jnp.asarray(np.array([p[1] for p in pairs], np.int32)))


def _chip_gather_copies(ins, outs, ssem, rsem, lsem):
    x, y, c = _me()
    me = 2 * x + y
    copies = []
    for i, (src, dst) in enumerate(zip(ins, outs)):
        copies.append(pltpu.make_async_copy(src, dst.at[me], lsem.at[i]))
        for k in (1, 2, 3):
            peer = (1 - x if k & 2 else x, 1 - y if k & 1 else y, c)
            copies.append(pltpu.make_async_remote_copy(
                src_ref=src, dst_ref=dst.at[me], send_sem=ssem.at[3 * i + k - 1],
                recv_sem=rsem.at[3 * i + k - 1], device_id=peer, device_id_type=MESH))
    return copies


def _attn_fwd(q_arr, q_off, k_arr, k_off, v_arr, v_off, name, gather=()):
    S = q_arr.shape[0]
    t = _tile(S, 2048)
    n = S // t
    rc = min(ATTN_ROWS, t)
    ii, jj = _causal_pairs(n, k_major=False)
    ng = len(gather)
    last_step = ii.shape[0] - 1

    def body(ii_ref, jj_ref, q_ref, k_ref, v_ref, *rest):
        g_in, (o_ref, lse_ref), g_out = rest[:ng], rest[ng:ng + 2], rest[ng + 2:2 * ng + 2]
        s_s, m_s, a_s, acc_s = rest[2 * ng + 2:2 * ng + 6]
        step = pl.program_id(1)
        i, j = ii_ref[step], jj_ref[step]
        lane = lax.broadcasted_iota(jnp.int32, (t, HEAD_PAD), 1)
        if ng:
            copies = _chip_gather_copies(g_in, g_out, *rest[2 * ng + 6:])

            @pl.when(jnp.logical_and(pl.program_id(0) == 0, step == 0))
            def _():
                for cp in copies:
                    cp.start()

            @pl.when(jnp.logical_and(pl.program_id(0) == HEADS - 1, step == last_step))
            def _():
                for cp in copies:
                    cp.wait()

        @pl.when(j == 0)
        def _():
            m_s[...] = jnp.full_like(m_s, NEG_BIG)
            acc_s[...] = jnp.zeros_like(acc_s)

        def tile(diagonal):
            s_s[...] = lax.dot_general(q_ref[...], k_ref[...], NT, preferred_element_type=F32)
            for r in range(t // rc):
                rows = pl.ds(r * rc, rc)
                sc = s_s[rows, :]
                if diagonal:
                    qpos = r * rc + lax.broadcasted_iota(jnp.int32, (rc, t), 0)
                    kpos = lax.broadcasted_iota(jnp.int32, (rc, t), 1)
                    sc = jnp.where(qpos >= kpos, sc, NEG_BIG)
                m_old = m_s[rows, :]
                m_new = jnp.maximum(m_old, jnp.max(sc, axis=1, keepdims=True))
                s_s[rows, :] = jnp.exp(sc - m_new)
                a_s[rows, :] = jnp.exp(m_old - m_new)
                m_s[rows, :] = m_new
            v_ones = jnp.where(lane == ONES_LANE, 1.0, v_ref[...].astype(F32)).astype(BF16)
            acc_s[...] = a_s[...] * acc_s[...] + lax.dot_general(s_s[...].astype(BF16), v_ones, NN,
                                                                 preferred_element_type=F32)

        @pl.when(j < i)
        def _():
            tile(False)

        @pl.when(j == i)
        def _():
            tile(True)
            acc = acc_s[...]
            l = jnp.sum(jnp.where(lane == ONES_LANE, acc, 0.0), axis=1, keepdims=True)
            o_ref[...] = jnp.where(lane == ONES_LANE, 0.0, acc / l).astype(BF16)
            lse_ref[0] = m_s[...] + jnp.log(l)

    hbm = pl.BlockSpec(memory_space=pl.ANY)
    sems = [pltpu.SemaphoreType.DMA((3 * ng,)), pltpu.SemaphoreType.DMA((3 * ng,)),
            pltpu.SemaphoreType.DMA((ng,))] if ng else []
    grid_spec = pltpu.PrefetchScalarGridSpec(
        num_scalar_prefetch=2, grid=(HEADS, ii.shape[0]),
        in_specs=[pl.BlockSpec((t, HEAD_PAD), lambda h, s, ii, jj: (ii[s], q_off + h)),
                  pl.BlockSpec((t, HEAD_PAD), lambda h, s, ii, jj: (jj[s], k_off + h)),
                  pl.BlockSpec((t, HEAD_PAD), lambda h, s, ii, jj: (jj[s], v_off + h))] + [hbm] * ng,
        out_specs=[pl.BlockSpec((t, HEAD_PAD), lambda h, s, ii, jj: (ii[s], h)),
                   pl.BlockSpec((1, t, 1), lambda h, s, ii, jj: (h, ii[s], 0))] + [hbm] * ng,
        scratch_shapes=[pltpu.VMEM((t, t), F32), pltpu.VMEM((t, 1), F32), pltpu.VMEM((t, 1), F32),
                        pltpu.VMEM((t, HEAD_PAD), F32)] + sems)
    outs = pl.pallas_call(
        body, name=name, grid_spec=grid_spec,
        out_shape=[jax.ShapeDtypeStruct((S, HEADS * HEAD_PAD), BF16), jax.ShapeDtypeStruct((HEADS, S, 1), F32)]
        + [jax.ShapeDtypeStruct((4,) + g.shape, g.dtype) for g in gather],
        compiler_params=_cparams(),
    )(ii, jj, q_arr, k_arr, v_arr, *gather)
    return outs[0], outs[1], list(outs[2:])


def _attn_delta(do, o, name):
    S = do.shape[0]
    t = _tile(S, 512)

    def body(do_ref, o_ref, d_ref):
        lane = lax.broadcasted_iota(jnp.int32, (t, LANE), 1)
        acc = jnp.zeros((t, LANE), F32)
        for h in range(HEADS):
            cols = pl.ds(h * HEAD_PAD, HEAD_PAD)
            d = jnp.sum(do_ref[:, cols].astype(F32) * o_ref[:, cols].astype(F32), axis=1, keepdims=True)
            acc = jnp.where(lane == h, d, acc)
        d_ref[...] = acc

    blk = pl.BlockSpec((t, HEADS * HEAD_PAD), lambda i: (i, 0))
    by_lane = pl.pallas_call(
        body, name=name, out_shape=jax.ShapeDtypeStruct((S, LANE), F32), grid=(S // t,),
        in_specs=[blk, blk], out_specs=pl.BlockSpec((t, LANE), lambda i: (i, 0)),
        compiler_params=_cparams(),
    )(do, o)
    return by_lane[:, :HEADS].T.reshape(HEADS, 1, S)


def _lane_as_row(x, lane):
    pick = (lax.broadcasted_iota(jnp.int32, (8, HEAD_PAD), 1) == lane).astype(BF16)
    acc = None
    for part in _split3(x):
        d = lax.dot_general(pick, part, NT, preferred_element_type=F32)
        acc = d if acc is None else acc + d
    return jnp.max(acc, axis=0, keepdims=True)


def _attn_bwd(q_arr, q_off, k_arr, k_off, v_arr, v_off, do, lse, delta, name, bias_grads=False, dq_dtype=F32):
    S = q_arr.shape[0]
    t = _tile(S, ATTN_TILE)
    n = S // t
    rc = min(ATTN_ROWS, t)
    ii, jj = _causal_pairs(n, k_major=True)
    last_step = ii.shape[0] - 1

    def body(ii_ref, jj_ref, q_ref, k_ref, v_ref, do_ref, lse_ref, dl_ref, dq_ref, dk_ref, dv_ref, *rest):
        if bias_grads:
            dcq_ref, dck_ref = rest[:2]
        s_s, dp_s, dk_s, dv_s, dq_s = rest[-5:]
        step = pl.program_id(1)
        i, j = ii_ref[step], jj_ref[step]

        @pl.when(step == 0)
        def _():
            dq_s[...] = jnp.zeros_like(dq_s)

        @pl.when(i == j)
        def _():
            dk_s[...] = jnp.zeros_like(dk_s)
            dv_s[...] = jnp.zeros_like(dv_s)

        def tile(diagonal):
            q, k, dov = q_ref[...], k_ref[...], do_ref[...]
            s_s[...] = lax.dot_general(k, q, NT, preferred_element_type=F32)
            dp_s[...] = lax.dot_general(v_ref[...], dov, NT, preferred_element_type=F32)
            lse_row, dl_row = lse_ref[0], dl_ref[0]
            for r in range(t // rc):
                rows = pl.ds(r * rc, rc)
                sc = s_s[rows, :]
                if diagonal:
                    kpos = r * rc + lax.broadcasted_iota(jnp.int32, (rc, t), 0)
                    qpos = lax.broadcasted_iota(jnp.int32, (rc, t), 1)
                    sc = jnp.where(qpos >= kpos, sc, NEG_BIG)
                p = jnp.exp(sc - lse_row)
                s_s[rows, :] = p
                dp_s[rows, :] = p * (dp_s[rows, :] - dl_row)
            ds = dp_s[...].astype(BF16)
            dv_s[...] += lax.dot_general(s_s[...].astype(BF16), dov, NN, preferred_element_type=F32)
            dk_s[...] += lax.dot_general(ds, q, NN, preferred_element_type=F32)
            rows = pl.ds(pl.multiple_of(i * t, t), t)
            dq_s[rows, :] += lax.dot_general(ds, k, TN, preferred_element_type=F32)

        @pl.when(i > j)
        def _():
            tile(False)

        @pl.when(i == j)
        def _():
            tile(True)

        @pl.when(i == n - 1)
        def _():
            dk = dk_s[...]
            dk_ref[...] = dk.astype(BF16)
            dv_ref[...] = dv_s[...].astype(BF16)
            if bias_grads:
                dck_ref[0] = _lane_as_row(dk, BIAS_LANE)

        @pl.when(step == last_step)
        def _():
            dq = dq_s[...]
            dq_ref[...] = dq.astype(dq_dtype)
            if bias_grads:
                dcq_ref[0] = _lane_as_row(dq, BIAS_LANE + 3)

    qblk = lambda off: pl.BlockSpec((t, HEAD_PAD), lambda h, s, ii, jj: (ii[s], off + h))
    kblk = lambda off: pl.BlockSpec((t, HEAD_PAD), lambda h, s, ii, jj: (jj[s], off + h))
    stat = pl.BlockSpec((1, 1, t), lambda h, s, ii, jj: (h, 0, ii[s]))
    wide = (S, HEADS * HEAD_PAD)
    out_specs = [pl.BlockSpec((S, HEAD_PAD), lambda h, s, ii, jj: (0, h)), kblk(0), kblk(0)]
    out_shape = [jax.ShapeDtypeStruct(wide, dq_dtype), jax.ShapeDtypeStruct(wide, BF16),
                 jax.ShapeDtypeStruct(wide, BF16)]
    if bias_grads:
        out_specs += [pl.BlockSpec((1, 1, S), lambda h, s, ii, jj: (h, 0, 0)),
                      pl.BlockSpec((1, 1, t), lambda h, s, ii, jj: (h, 0, jj[s]))]
        out_shape += [jax.ShapeDtypeStruct((HEADS, 1, S), F32)] * 2
    grid_spec = pltpu.PrefetchScalarGridSpec(
        num_scalar_prefetch=2, grid=(HEADS, ii.shape[0]),
        in_specs=[qblk(q_off), kblk(k_off), kblk(v_off), qblk(0), stat, stat],
        out_specs=out_specs,
        scratch_shapes=[pltpu.VMEM((t, t), F32), pltpu.VMEM((t, t), F32), pltpu.VMEM((t, HEAD_PAD), F32),
                        pltpu.VMEM((t, HEAD_PAD), F32), pltpu.VMEM((S, HEAD_PAD), F32)])
    return pl.pallas_call(
        body, name=name, grid_spec=grid_spec, out_shape=out_shape, compiler_params=_cparams(),
    )(ii, jj, q_arr, k_arr, v_arr, do, lse, delta)


BIAS_LANE = FOX_HEAD


def _fox_expand(qkv, cum, name):
    S = qkv.shape[0]
    t = _tile(S, 256)
    narrow, wide = HEADS * FOX_HEAD, HEADS * HEAD_PAD

    def body(q_ref, k_ref, v_ref, c_ref, qo_ref, ko_ref, vo_ref):
        lane = lax.broadcasted_iota(jnp.int32, (t, HEAD_PAD), 1)
        cum_all = c_ref[...]

        def head_tile(ref, h):
            x = ref[:, pl.ds((h // 2) * HEAD_PAD, HEAD_PAD)].astype(F32)
            if h % 2:
                x = pltpu.roll(x, FOX_HEAD, 1)
            return jnp.where(lane < FOX_HEAD, x, 0.0)

        def fill(out, first, second):
            for n_, (a, b) in enumerate(zip(first, second)):
                out = jnp.where(lane == BIAS_LANE + n_, a, out)
                out = jnp.where(lane == BIAS_LANE + 3 + n_, b, out)
            return out.astype(BF16)

        ones = [1.0, 1.0, 1.0]
        for h in range(HEADS):
            cols = pl.ds(h * HEAD_PAD, HEAD_PAD)
            c = jnp.sum(jnp.where(lane == h, cum_all, 0.0), axis=1, keepdims=True)
            pieces = [x.astype(F32) for x in _split3(c)]
            qo_ref[:, cols] = fill(head_tile(q_ref, h), ones, pieces)
            ko_ref[:, cols] = fill(head_tile(k_ref, h), [-x for x in pieces], ones)
            vo_ref[:, cols] = head_tile(v_ref, h).astype(BF16)

    part = lambda p: pl.BlockSpec((t, narrow), lambda i: (i, p))
    out = pl.BlockSpec((t, wide), lambda i: (i, 0))
    shp = jax.ShapeDtypeStruct((S, wide), BF16)
    return pl.pallas_call(
        body, name=name, out_shape=[shp, shp, shp], grid=(S // t,),
        in_specs=[part(0), part(1), part(2), pl.BlockSpec((t, LANE), lambda i: (i, 0))],
        out_specs=[out, out, out], compiler_params=_cparams(),
    )(qkv, qkv, qkv, cum)


def _fox_compress(dq, dk, dv, name):
    S = dq.shape[0]
    t = _tile(S, 256)
    narrow, wide = HEADS * FOX_HEAD, HEADS * HEAD_PAD

    def body(dq_ref, dk_ref, dv_ref, o_ref):
        lane = lax.broadcasted_iota(jnp.int32, (t, HEAD_PAD), 1)
        for p, ref in enumerate((dq_ref, dk_ref, dv_ref)):
            for g in range(HEADS // 2):
                even = ref[:, pl.ds(2 * g * HEAD_PAD, HEAD_PAD)].astype(F32)
                odd = ref[:, pl.ds((2 * g + 1) * HEAD_PAD, HEAD_PAD)].astype(F32)
                both = jnp.where(lane < FOX_HEAD, even, pltpu.roll(odd, FOX_HEAD, 1))
                o_ref[:, pl.ds(p * narrow + g * HEAD_PAD, HEAD_PAD)] = both.astype(BF16)

    blk = pl.BlockSpec((t, wide), lambda i: (i, 0))
    return pl.pallas_call(
        body, name=name, out_shape=jax.ShapeDtypeStruct((S, 3 * narrow), BF16), grid=(S // t,),
        in_specs=[blk, blk, blk], out_specs=pl.BlockSpec((t, 3 * narrow), lambda i: (i, 0)),
        compiler_params=_cparams(),
    )(dq, dk, dv)


def _split3(x):
    hi = x.astype(BF16)
    r1 = x - hi.astype(F32)
    mid = r1.astype(BF16)
    lo = (r1 - mid.astype(F32)).astype(BF16)
    return hi, mid, lo


def _tri_matmul(tri, x):
    acc = None
    for part in _split3(x):
        d = lax.dot_general(tri, part, NN, preferred_element_type=F32)
        acc = d if acc is None else acc + d
    return acc


def _log_sigmoid(z):
    return jnp.minimum(z, 0.0) - jnp.log1p(jnp.exp(-jnp.abs(z)))


def _fox_gate_fwd(fl, bf):
    S = fl.shape[0]
    t = _tile(S, 256)

    def body(fl_ref, bf_ref, cum_ref, carry):
        @pl.when(pl.program_id(0) == 0)
        def _():
            carry[...] = jnp.zeros_like(carry)

        logf = _log_sigmoid(fl_ref[...] + bf_ref[...])
        tri = (lax.broadcasted_iota(jnp.int32, (t, t), 0) >= lax.broadcasted_iota(jnp.int32, (t, t), 1)).astype(BF16)
        cum_ref[...] = _tri_matmul(tri, logf) + carry[...]
        carry[...] += jnp.sum(logf, axis=0, keepdims=True)

    row = pl.BlockSpec((t, LANE), lambda i: (i, 0))
    return pl.pallas_call(
        body, name="fox_gate_fwd", out_shape=jax.ShapeDtypeStruct((S, LANE), F32), grid=(S // t,),
        in_specs=[row, pl.BlockSpec((1, LANE), lambda i: (0, 0))], out_specs=row,
        scratch_shapes=[pltpu.VMEM((1, LANE), F32)], compiler_params=_cparams(),
    )(fl, bf)


def _fox_gate_bwd(dcum, fl, bf):
    S = fl.shape[0]
    t = _tile(S, 256)
    n = S // t

    def body(dc_ref, fl_ref, bf_ref, df_ref, db_ref, carry):
        @pl.when(pl.program_id(0) == 0)
        def _():
            carry[...] = jnp.zeros_like(carry)
            db_ref[...] = jnp.zeros_like(db_ref)

        dc = dc_ref[...]
        tri = (lax.broadcasted_iota(jnp.int32, (t, t), 1) >= lax.broadcasted_iota(jnp.int32, (t, t), 0)).astype(BF16)
        dlogf = _tri_matmul(tri, dc) + carry[...]
        carry[...] += jnp.sum(dc, axis=0, keepdims=True)
        z = fl_ref[...] + bf_ref[...]
        e = jnp.exp(-jnp.abs(z))
        sig_neg = jnp.where(z >= 0, e, 1.0) / (1.0 + e)
        lanes = lax.broadcasted_iota(jnp.int32, (t, LANE), 1)
        df = jnp.where(lanes < HEADS, dlogf * sig_neg, 0.0)
        df_ref[...] = df
        db_ref[...] += jnp.sum(df, axis=0, keepdims=True)

    row = pl.BlockSpec((t, LANE), lambda i: (n - 1 - i, 0))
    vec = pl.BlockSpec((1, LANE), lambda i: (0, 0))
    return pl.pallas_call(
        body, name="fox_gate_bwd",
        out_shape=[jax.ShapeDtypeStruct((S, LANE), F32), jax.ShapeDtypeStruct((1, LANE), F32)],
        grid=(n,), in_specs=[row, row, vec], out_specs=[row, vec],
        scratch_shapes=[pltpu.VMEM((1, LANE), F32)], compiler_params=_cparams(),
    )(dcum, fl, bf)


def _rope_tables(S):
    pos = jnp.arange(S, dtype=F32)
    inv_freq = ROPE_BASE ** (-jnp.arange(0, MLA_ROPE, 2, dtype=F32) / MLA_ROPE)
    ang = pos[:, None] * inv_freq[None, :]
    cos, sin = jnp.cos(ang), jnp.sin(ang)
    half = MLA_ROPE // 2
    ones = jnp.ones((S, MLA_NOPE), F32)
    z = lambda w: jnp.zeros((S, w), F32)
    keep = jnp.concatenate([ones, cos, cos, z(HEAD_PAD - MLA_NOPE - MLA_ROPE)], axis=1)
    from_above = jnp.concatenate([z(MLA_NOPE), -sin, z(HEAD_PAD - MLA_NOPE - half)], axis=1)
    from_below = jnp.concatenate([z(MLA_NOPE + half), sin, z(HEAD_PAD - MLA_NOPE - MLA_ROPE)], axis=1)
    return keep, from_above, from_below


def _rope(x, keep, up, down, transpose):
    half = MLA_ROPE // 2
    if transpose:
        return x * keep + pltpu.roll(x * up, half, 1) + pltpu.roll(x * down, HEAD_PAD - half, 1)
    return x * keep + pltpu.roll(x, HEAD_PAD - half, 1) * up + pltpu.roll(x, half, 1) * down


def _rope_heads(x, tables, transpose, scale, name):
    S = x.shape[0]
    t = _tile(S, 256)

    def body(x_ref, a_ref, b_ref, c_ref, o_ref):
        keep, up, down = a_ref[...], b_ref[...], c_ref[...]
        for h in range(HEADS):
            cols = pl.ds(h * HEAD_PAD, HEAD_PAD)
            o_ref[:, cols] = (_rope(x_ref[:, cols], keep, up, down, transpose) * scale).astype(BF16)

    blk = pl.BlockSpec((t, HEADS * HEAD_PAD), lambda i: (i, 0))
    tab = pl.BlockSpec((t, HEAD_PAD), lambda i: (i, 0))
    return pl.pallas_call(
        body, name=name, out_shape=jax.ShapeDtypeStruct(x.shape, BF16), grid=(S // t,),
        in_specs=[blk, tab, tab, tab], out_specs=blk, compiler_params=_cparams(),
    )(x, *tables)


MLA_A_PAD = MLA_QR + MLA_KVR + HEAD_PAD
MLA_KV_IN = MLA_KVR + HEAD_PAD


def _mla_mid_fwd(a, gq, gkv, tables):
    S = a.shape[0]
    t = _tile(S, 512)

    def body(a_ref, gq_ref, gkv_ref, ta_ref, tb_ref, tc_ref, q_ref, kv_ref):
        cq = a_ref[:, :MLA_QR]
        ckv = a_ref[:, MLA_QR:MLA_QR + MLA_KVR]
        kr = a_ref[:, MLA_QR + MLA_KVR:]
        q_ref[...] = (cq * _rstd(cq) * gq_ref[...]).astype(BF16)
        kv_ref[:, :MLA_KVR] = (ckv * _rstd(ckv) * gkv_ref[...]).astype(BF16)
        kv_ref[:, MLA_KVR:] = _rope(kr, ta_ref[...], tb_ref[...], tc_ref[...], False).astype(BF16)

    tab = pl.BlockSpec((t, HEAD_PAD), lambda i: (i, 0))
    return pl.pallas_call(
        body, name="mla_mid_fwd",
        out_shape=[jax.ShapeDtypeStruct((S, MLA_QR), BF16), jax.ShapeDtypeStruct((S, MLA_KV_IN), BF16)],
        grid=(S // t,),
        in_specs=[pl.BlockSpec((t, MLA_A_PAD), lambda i: (i, 0)), pl.BlockSpec((1, MLA_QR), lambda i: (0, 0)),
                  pl.BlockSpec((1, MLA_KVR), lambda i: (0, 0)), tab, tab, tab],
        out_specs=[pl.BlockSpec((t, MLA_QR), lambda i: (i, 0)), pl.BlockSpec((t, MLA_KV_IN), lambda i: (i, 0))],
        compiler_params=_cparams(),
    )(a, gq, gkv, *tables)


def _mla_mid_bwd(a, dq, dkv, gq, gkv, tables):
    S = a.shape[0]
    t = _tile(S, 512)

    def body(a_ref, dq_ref, dkv_ref, gq_ref, gkv_ref, ta_ref, tb_ref, tc_ref, da_ref, dgq_ref, dgkv_ref):
        @pl.when(pl.program_id(0) == 0)
        def _():
            dgq_ref[...] = jnp.zeros_like(dgq_ref)
            dgkv_ref[...] = jnp.zeros_like(dgkv_ref)

        def one(c, dout, g):
            r = _rstd(c)
            n = c * r
            return _norm_bwd(n, r, dout * g), jnp.sum(dout * n, axis=0, keepdims=True)

        dcq, dgq = one(a_ref[:, :MLA_QR], dq_ref[...], gq_ref[...])
        dckv, dgkv = one(a_ref[:, MLA_QR:MLA_QR + MLA_KVR], dkv_ref[:, :MLA_KVR], gkv_ref[...])
        da_ref[:, :MLA_QR] = dcq.astype(BF16)
        da_ref[:, MLA_QR:MLA_QR + MLA_KVR] = dckv.astype(BF16)
        da_ref[:, MLA_QR + MLA_KVR:] = _rope(dkv_ref[:, MLA_KVR:], ta_ref[...], tb_ref[...], tc_ref[...],
                                             True).astype(BF16)
        dgq_ref[...] += dgq
        dgkv_ref[...] += dgkv

    tab = pl.BlockSpec((t, HEAD_PAD), lambda i: (i, 0))
    vq = pl.BlockSpec((1, MLA_QR), lambda i: (0, 0))
    vkv = pl.BlockSpec((1, MLA_KVR), lambda i: (0, 0))
    return pl.pallas_call(
        body, name="mla_mid_bwd",
        out_shape=[jax.ShapeDtypeStruct((S, MLA_A_PAD), BF16), jax.ShapeDtypeStruct((1, MLA_QR), F32),
                   jax.ShapeDtypeStruct((1, MLA_KVR), F32)],
        grid=(S // t,),
        in_specs=[pl.BlockSpec((t, MLA_A_PAD), lambda i: (i, 0)), pl.BlockSpec((t, MLA_QR), lambda i: (i, 0)),
                  pl.BlockSpec((t, MLA_KV_IN), lambda i: (i, 0)), vq, vkv, tab, tab, tab],
        out_specs=[pl.BlockSpec((t, MLA_A_PAD), lambda i: (i, 0)), vq, vkv],
        compiler_params=_cparams(),
    )(a, dq, dkv, gq, gkv, *tables)


FFN_ROWS = 256
FFN_COLS = 1408
HALO = 8


def _shift_down(cur, halo, s, first):
    rolled = pltpu.roll(cur, s, 0)
    hrolled = jnp.where(first, 0.0, pltpu.roll(halo, s, 0))
    row = lax.broadcasted_iota(jnp.int32, (HALO, cur.shape[1]), 0)
    top = jnp.where(row < s, hrolled, rolled[:HALO])
    return jnp.concatenate([top, rolled[HALO:]], axis=0)


def _shift_up(cur, halo, s, last):
    rows = cur.shape[0]
    rolled = pltpu.roll(cur, rows - s, 0)
    hrolled = jnp.where(last, 0.0, pltpu.roll(halo, HALO - s, 0))
    row = lax.broadcasted_iota(jnp.int32, (HALO, cur.shape[1]), 0)
    bottom = jnp.where(row >= HALO - s, hrolled, rolled[rows - HALO:])
    return jnp.concatenate([rolled[:rows - HALO], bottom], axis=0)


def _conv(u, halo, w_ref, b, first):
    u1 = _shift_down(u, halo, 1, first)
    u2 = _shift_down(u, halo, 2, first)
    y = b + w_ref[0:1, :] * u2
    y = y + w_ref[1:2, :] * u1
    y = y + w_ref[2:3, :] * u
    return y, u1, u2


def _ffn_specs(S, tr, tn):
    per = tr // HALO
    tile = pl.BlockSpec((tr, tn), lambda j, i: (i, j))
    before = pl.BlockSpec((HALO, tn), lambda j, i: (jnp.maximum(i * per - 1, 0), j))
    after = pl.BlockSpec((HALO, tn), lambda j, i: (jnp.minimum((i + 1) * per, S // HALO - 1), j))
    w3 = pl.BlockSpec((3, tn), lambda j, i: (0, j))
    w1 = pl.BlockSpec((1, tn), lambda j, i: (0, j))
    return tile, before, after, w3, w1


def _ffn_act_fwd(ug, uv, wg, wv, bg, bv, name):
    S, F = ug.shape
    tr, tn = _tile(S, FFN_ROWS), _tile(F, FFN_COLS)
    tile, before, _, w3, w1 = _ffn_specs(S, tr, tn)

    def body(ug_ref, uv_ref, hg_ref, hv_ref, wg_ref, wv_ref, bg_ref, bv_ref, g_ref, v_ref, o_ref):
        first = pl.program_id(1) == 0
        g, _, _ = _conv(ug_ref[...], hg_ref[...], wg_ref, bg_ref[...], first)
        v, _, _ = _conv(uv_ref[...], hv_ref[...], wv_ref, bv_ref[...], first)
        g_ref[...] = g
        v_ref[...] = v
        o_ref[...] = (g * jax.nn.sigmoid(g) * v).astype(BF16)

    f32 = jax.ShapeDtypeStruct((S, F), F32)
    return pl.pallas_call(
        body, name=name, out_shape=[f32, f32, jax.ShapeDtypeStruct((S, F), BF16)], grid=(F // tn, S // tr),
        in_specs=[tile, tile, before, before, w3, w3, w1, w1], out_specs=[tile, tile, tile],
        compiler_params=_cparams(),
    )(ug, uv, ug, uv, wg, wv, bg, bv)


def _gate_grads(g, v, da):
    sg = jax.nn.sigmoid(g)
    return da * v * (sg * (1.0 + g * (1.0 - sg))), da * (g * sg)


def _ffn_bwd_mid(ug, uv, g, v, dact, wg, wv, name):
    S, F = ug.shape
    tr, tn = _tile(S, FFN_ROWS), _tile(F, FFN_COLS)
    tile, _, after, w3, w1 = _ffn_specs(S, tr, tn)
    n = S // tr

    def body(ug_ref, uv_ref, g_ref, v_ref, ag_ref, av_ref, da_ref, ada_ref, wg_ref, wv_ref,
             dug_ref, duv_ref, dwg_ref, dwv_ref, dbg_ref, dbv_ref):
        last = pl.program_id(1) == n - 1

        @pl.when(pl.program_id(1) == 0)
        def _():
            for r in (dwg_ref, dwv_ref, dbg_ref, dbv_ref):
                r[...] = jnp.zeros_like(r)

        dyg, dyv = _gate_grads(g_ref[...], v_ref[...], da_ref[...])
        dyg_after, dyv_after = _gate_grads(ag_ref[...], av_ref[...], ada_ref[...])

        def back(dy, dy_after, u_ref, w_ref, du_ref, dw_ref, db_ref):
            u = u_ref[...]
            d1 = _shift_up(dy, dy_after, 1, last)
            d2 = _shift_up(dy, dy_after, 2, last)
            du_ref[...] = (w_ref[2:3, :] * dy + w_ref[1:2, :] * d1 + w_ref[0:1, :] * d2).astype(BF16)
            for k, d in enumerate((d2, d1, dy)):
                dw_ref[k:k + 1, :] += jnp.sum(d * u, axis=0, keepdims=True)
            db_ref[...] += jnp.sum(dy, axis=0, keepdims=True)

        back(dyg, dyg_after, ug_ref, wg_ref, dug_ref, dwg_ref, dbg_ref)
        back(dyv, dyv_after, uv_ref, wv_ref, duv_ref, dwv_ref, dbv_ref)

    big = jax.ShapeDtypeStruct((S, F), BF16)
    s3, s1 = jax.ShapeDtypeStruct((3, F), F32), jax.ShapeDtypeStruct((1, F), F32)
    return pl.pallas_call(
        body, name=name, out_shape=[big, big, s3, s3, s1, s1], grid=(F // tn, n),
        in_specs=[tile, tile, tile, tile, after, after, tile, after, w3, w3],
        out_specs=[tile, tile, w3, w3, w1, w1], compiler_params=_cparams(),
    )(ug, uv, g, v, g, v, dact, dact, wg, wv)


def _dot3(a, b):
    a_hi = a.astype(BF16)
    a_lo = (a - a_hi.astype(F32)).astype(BF16)
    b_hi = b.astype(BF16)
    b_lo = (b - b_hi.astype(F32)).astype(BF16)
    d = lambda p, q: lax.dot_general(p, q, NN, preferred_element_type=F32)
    return d(a_hi, b_hi) + (d(a_hi, b_lo) + d(a_lo, b_hi))


def _ada_mod(c_all, w, b):
    nmod, D, n = w.shape

    def body(c_ref, w_ref, b_ref, o_ref):
        cv = c_ref[...]
        o_ref[0] = _dot3(cv * jax.nn.sigmoid(cv), w_ref[0]) + b_ref[0]

    return pl.pallas_call(
        body, name="ada_mod", out_shape=jax.ShapeDtypeStruct((nmod, 8, n), F32), grid=(nmod,),
        in_specs=[pl.BlockSpec((8, D), lambda m: (0, 0)), pl.BlockSpec((1, D, n), lambda m: (m, 0, 0)),
                  pl.BlockSpec((1, 1, n), lambda m: (m, 0, 0))],
        out_specs=pl.BlockSpec((1, 8, n), lambda m: (m, 0, 0)), compiler_params=_cparams(),
    )(c_all, w, b)


def _ada_grad(c_all_t, dmod):
    D = c_all_t.shape[0]
    nmod, _, n = dmod.shape
    tr = 256

    def body(c_ref, d_ref, dw_ref, db_ref):
        cv = c_ref[...]
        sc = cv * jax.nn.sigmoid(cv)
        dv = d_ref[0]
        acc = sc[:, 0:1] * dv[0:1, :]
        tot = dv[0:1, :]
        for k in range(1, 8):
            acc = acc + sc[:, k:k + 1] * dv[k:k + 1, :]
            tot = tot + dv[k:k + 1, :]
        dw_ref[0] = acc
        db_ref[0] = tot

    return pl.pallas_call(
        body, name="ada_grad",
        out_shape=[jax.ShapeDtypeStruct((nmod, D, n), F32), jax.ShapeDtypeStruct((nmod, 1, n), F32)],
        grid=(nmod, D // tr),
        in_specs=[pl.BlockSpec((tr, 8), lambda m, i: (i, 0)), pl.BlockSpec((1, 8, n), lambda m, i: (m, 0, 0))],
        out_specs=[pl.BlockSpec((1, tr, n), lambda m, i: (m, i, 0)), pl.BlockSpec((1, 1, n), lambda m, i: (m, 0, 0))],
        compiler_params=_cparams(),
    )(c_all_t, dmod)


def _adamw(w, g, m, v, name):
    R, C = w.shape
    t = _row_tile(R, C)

    def body(w_ref, g_ref, m_ref, v_ref, d_ref, nm_ref, nv_ref):
        gv = g_ref[...]
        mn = ADAM_B1 * m_ref[...] + (1.0 - ADAM_B1) * gv
        vn = ADAM_B2 * v_ref[...] + (1.0 - ADAM_B2) * (gv * gv)
        m_hat = mn / (1.0 - ADAM_B1 ** ADAM_STEP)
        v_hat = vn / (1.0 - ADAM_B2 ** ADAM_STEP)
        d_ref[...] = -ADAM_LR * (m_hat / (jnp.sqrt(v_hat) + ADAM_EPS) + ADAM_WD * w_ref[...])
        nm_ref[...] = mn
        nv_ref[...] = vn

    blk = pl.BlockSpec((t, C), lambda i: (i, 0))
    shp = jax.ShapeDtypeStruct((R, C), F32)
    return pl.pallas_call(
        body, name=name, out_shape=[shp, shp, shp], grid=(R // t,), in_specs=[blk] * 4, out_specs=[blk] * 3,
        compiler_params=_cparams(),
    )(w, g, m, v)


def _cat_cols(g4):
    return jnp.concatenate([g4[s] for s in range(4)], axis=-1)


def _cat_rows(g4):
    return jnp.concatenate([g4[s] for s in range(4)], axis=-2)


def _pad_head_cols(w, width):
    K = w.shape[0]
    w = w.reshape(K, HEADS, width)
    return jnp.pad(w, ((0, 0), (0, 0), (0, HEAD_PAD - width))).reshape(K, HEADS * HEAD_PAD)


def _unpad_head_cols(w, width):
    K = w.shape[0]
    return w.reshape(K, HEADS, HEAD_PAD)[:, :, :width].reshape(K, HEADS * width)


def _pad_head_rows(w, width):
    N = w.shape[1]
    w = w.reshape(HEADS, width, N)
    return jnp.pad(w, ((0, 0), (0, HEAD_PAD - width), (0, 0))).reshape(HEADS * HEAD_PAD, N)


def _unpad_head_rows(w, width):
    N = w.shape[1]
    return w.reshape(HEADS, HEAD_PAD, N)[:, :width, :].reshape(HEADS * width, N)


def _prepare_fox_in(fox_in):
    W = {}
    w = _cat_cols(fox_in[:, 0])
    hw = HEADS * FOX_HEAD
    q = w[:, :hw] * FOX_SCALE
    W["fox_qkv"] = jnp.concatenate([q, w[:, hw:3 * hw]], axis=1)
    W["fox_f"] = jnp.pad(w[:, 3 * hw:], ((0, 0), (0, LANE - HEADS)))
    return W


def _prepare_weights(gathered):
    fox_o, mla_a, mla_uq, mla_ukv, mla_o, ffn_in, ffn_out = gathered
    W = {}
    W["fox_o"] = _pad_head_rows(_cat_rows(fox_o[:, 0]), FOX_HEAD)
    w = _cat_rows(mla_a[:, 0])
    lat = MLA_QR + MLA_KVR
    W["mla_a"] = jnp.concatenate(
        [w[:, :lat], jnp.zeros((D_MODEL, MLA_NOPE), BF16), w[:, lat:],
         jnp.zeros((D_MODEL, HEAD_PAD - MLA_NOPE - MLA_ROPE), BF16)], axis=1)
    W["mla_uq"] = _pad_head_cols(_cat_cols(mla_uq[:, 0]), MLA_NOPE + MLA_ROPE)
    w = _cat_cols(mla_ukv[:, 0]).reshape(MLA_KVR, HEADS, MLA_NOPE + MLA_V)
    kn = _pad_head_cols(w[:, :, :MLA_NOPE].reshape(MLA_KVR, -1), MLA_NOPE)
    vv = _pad_head_cols(w[:, :, MLA_NOPE:].reshape(MLA_KVR, -1), MLA_V)
    eye = np.zeros((HEAD_PAD, HEADS, HEAD_PAD), np.float32)
    for j in range(MLA_NOPE, MLA_NOPE + MLA_ROPE):
        eye[j, :, j] = 1.0
    eye = jnp.asarray(eye.reshape(HEAD_PAD, HEADS * HEAD_PAD), BF16)
    bottom = jnp.concatenate([eye, jnp.zeros((HEAD_PAD, HEADS * HEAD_PAD), BF16)], axis=1)
    W["mla_kv"] = jnp.concatenate([jnp.concatenate([kn, vv], axis=1), bottom], axis=0)
    W["mla_o"] = _pad_head_rows(_cat_rows(mla_o[:, 0]), MLA_V)
    w = _cat_cols(ffn_in)
    W["ffn_g"] = [w[i, :, :D_FF] for i in range(2)]
    W["ffn_v"] = [w[i, :, D_FF:] for i in range(2)]
    w = _cat_rows(ffn_out)
    W["ffn_out"] = [w[i] for i in range(2)]
    return W


def _ffn_forward(xin, mod, W, i, conv_w, conv_b):
    shift, scale, gate = mod
    h = _adaln_fwd(xin, scale, shift, name=f"ffn{i}_adaln")
    ug = _mm(h, W["ffn_g"][i], name=f"ffn{i}_up_gate")
    uv = _mm(h, W["ffn_v"][i], name=f"ffn{i}_up_val")
    wg, wv = conv_w[i][:, :D_FF], conv_w[i][:, D_FF:]
    bg, bv = conv_b[i][None, :D_FF], conv_b[i][None, D_FF:]
    g, v, act = _ffn_act_fwd(ug, uv, wg, wv, bg, bv, name=f"ffn{i}_act")
    y = _mm(act, W["ffn_out"][i], name=f"ffn{i}_down")
    xout = _residual(xin, gate, y, name=f"ffn{i}_residual")
    return xout, (xin, h, ug, uv, g, v, act, y, wg, wv)


def _ffn_backward(dx_out, saved, mod, W, i):
    xin, h, ug, uv, g, v, act, y, wg, wv = saved
    shift, scale, gate = mod
    dy, dgate = _gate_bwd(dx_out, y, gate, name=f"ffn{i}_gate_bwd")
    d_out = _mm(act, dy, ta=True, name=f"ffn{i}_dw_out")
    dact = _mm(dy, W["ffn_out"][i], tb=True, name=f"ffn{i}_dact")
    dug, duv, dwg, dwv, dbg, dbv = _ffn_bwd_mid(ug, uv, g, v, dact, wg, wv, name=f"ffn{i}_bwd_mid")
    d_in = jnp.concatenate([_mm(h, dug, ta=True, name=f"ffn{i}_dw_gate"),
                            _mm(h, duv, ta=True, name=f"ffn{i}_dw_val")], axis=1)
    dh = _mm(dug, W["ffn_g"][i], tb=True, name=f"ffn{i}_dh_gate")
    dh = _mm(duv, W["ffn_v"][i], tb=True, res=dh, name=f"ffn{i}_dh_val")
    dx, dscale, dshift = _adaln_bwd(xin, dh, dx_out, scale, name=f"ffn{i}_adaln_bwd")
    grads = dict(ffn_w_in=d_in, ffn_w_out=d_out, ffn_conv_w=jnp.concatenate([dwg, dwv], axis=1),
                 ffn_conv_b=jnp.concatenate([dbg, dbv], axis=1)[0])
    return dx, jnp.concatenate([dshift, dscale, dgate], axis=1)[0], grads


def _local_step(x, target, mods, W, small):
    S = x.shape[0]
    tables = _rope_tables(S)
    wide = HEADS * HEAD_PAD
    bf = jnp.pad(small["fox_b_f"], ((0, 0), (0, LANE - HEADS)))
    gq, gkv = small["mla_g_q"], small["mla_g_kv"]

    shift, scale, gate = mods[0]
    h0 = _adaln_fwd(x, scale, shift, name="fox_adaln")
    qkv = _mm(h0, W["fox_qkv"], out_dtype=BF16, name="fox_qkv")
    fl = _mm(h0, W["fox_f"], name="fox_gate_logits")
    cum = _fox_gate_fwd(fl, bf)
    q_fox, k_fox, v_fox = _fox_expand(qkv, cum, name="fox_expand")
    o0, lse0, gathered = _attn_fwd(q_fox, 0, k_fox, 0, v_fox, 0, name="fox_attn_fwd", gather=W["pending"])
    W = {**W, **_prepare_weights(gathered)}
    y0 = _mm(o0, W["fox_o"], name="fox_out")
    x1 = _residual(x, gate, y0, name="fox_residual")
    x2, ffn0 = _ffn_forward(x1, mods[1], W, 0, small["ffn_conv_w"], small["ffn_conv_b"])

    shift, scale, gate = mods[2]
    h2 = _adaln_fwd(x2, scale, shift, name="mla_adaln")
    a = _mm(h2, W["mla_a"], name="mla_down")
    cqn, ckv_in = _mla_mid_fwd(a, gq, gkv, tables)
    q_raw = _mm(cqn, W["mla_uq"], name="mla_up_q")
    q_cat = _rope_heads(q_raw, tables, False, MLA_SCALE, name="mla_rope_q")
    kv = _mm(ckv_in, W["mla_kv"], out_dtype=BF16, name="mla_up_kv")
    o1, lse1, _ = _attn_fwd(q_cat, 0, kv, 0, kv, HEADS, name="mla_attn_fwd")
    y2 = _mm(o1, W["mla_o"], name="mla_out")
    x3 = _residual(x2, gate, y2, name="mla_residual")
    x4, ffn1 = _ffn_forward(x3, mods[3], W, 1, small["ffn_conv_w"], small["ffn_conv_b"])

    loss, dx4, d_final_g = _final_loss(x4, small["final_g"], target)

    dx3, dmod3, g_ffn1 = _ffn_backward(dx4, ffn1, mods[3], W, 1)

    shift, scale, gate = mods[2]
    dy, dgate = _gate_bwd(dx3, y2, gate, name="mla_gate_bwd")
    d_mla_o = _mm(o1, dy, ta=True, name="mla_dw_out")
    do = _mm(dy, W["mla_o"], tb=True, out_dtype=BF16, name="mla_do")
    delta = _attn_delta(do, o1, name="mla_attn_delta")
    dq, dk, dv = _attn_bwd(q_cat, 0, kv, 0, kv, HEADS, do, lse1.reshape(HEADS, 1, S),
                           delta, name="mla_attn_bwd")
    dq_raw = _rope_heads(dq, tables, True, MLA_SCALE, name="mla_rope_q_bwd")
    d_mla_uq = _mm(cqn, dq_raw, ta=True, name="mla_dw_uq")
    dcqn = _mm(dq_raw, W["mla_uq"], tb=True, name="mla_dcq")
    dkv = jnp.concatenate([dk, dv], axis=1)
    d_mla_kv = _mm(ckv_in, dkv, ta=True, name="mla_dw_kv")
    dckv_in = _mm(dkv, W["mla_kv"], tb=True, name="mla_dckv")
    da, dgq, dgkv = _mla_mid_bwd(a, dcqn, dckv_in, gq, gkv, tables)
    d_mla_a = _mm(h2, da, ta=True, name="mla_dw_a")
    dh = _mm(da, W["mla_a"], tb=True, name="mla_dh")
    dx2, dscale, dshift = _adaln_bwd(x2, dh, dx3, scale, name="mla_adaln_bwd")
    dmod2 = jnp.concatenate([dshift, dscale, dgate], axis=1)[0]

    dx1, dmod1, g_ffn0 = _ffn_backward(dx2, ffn0, mods[1], W, 0)

    shift, scale, gate = mods[0]
    dy, dgate = _gate_bwd(dx1, y0, gate, name="fox_gate_bwd")
    d_fox_o = _mm(o0, dy, ta=True, name="fox_dw_out")
    do = _mm(dy, W["fox_o"], tb=True, out_dtype=BF16, name="fox_do")
    delta = _attn_delta(do, o0, name="fox_attn_delta")
    dq, dk, dv, dcq, dck = _attn_bwd(q_fox, 0, k_fox, 0, v_fox, 0, do, lse0.reshape(HEADS, 1, S),
                                     delta, name="fox_attn_bwd", bias_grads=True, dq_dtype=BF16)
    dcum = jnp.pad((dcq[:, 0, :] - dck[:, 0, :]).T, ((0, 0), (0, LANE - HEADS)))
    dfl, dbf = _fox_gate_bwd(dcum, fl, bf)
    dqkv = _fox_compress(dq, dk, dv, name="fox_compress")
    d_fox_qkv = _mm(h0, dqkv, ta=True, name="fox_dw_in")
    d_fox_f = _mm(h0, dfl, ta=True, name="fox_dw_gate")
    dh = _mm(dqkv, W["fox_qkv"], tb=True, name="fox_dh")
    dh = _mm(dfl, W["fox_f"], tb=True, res=dh, name="fox_dh_gate")
    dx0, dscale, dshift = _adaln_bwd(x, dh, dx1, scale, name="fox_adaln_bwd")
    dmod0 = jnp.concatenate([dshift, dscale, dgate], axis=1)[0]

    G = {}
    hw = HEADS * FOX_HEAD
    G["fox_w_in"] = jnp.concatenate([d_fox_qkv[:, :hw] * FOX_SCALE,
                                     d_fox_qkv[:, hw:], d_fox_f[:, :HEADS]], axis=1)[None]
    G["fox_b_f"] = dbf[:, :HEADS]
    G["fox_w_o"] = _unpad_head_rows(d_fox_o, FOX_HEAD)[None]
    lat = MLA_QR + MLA_KVR
    G["mla_w_a"] = jnp.concatenate([d_mla_a[:, :lat], d_mla_a[:, lat + MLA_NOPE:lat + MLA_NOPE + MLA_ROPE]],
                                   axis=1)[None]
    G["mla_g_q"] = dgq
    G["mla_g_kv"] = dgkv
    G["mla_w_uq"] = _unpad_head_cols(d_mla_uq, MLA_NOPE + MLA_ROPE)[None]
    dkn = d_mla_kv[:MLA_KVR, :wide].reshape(MLA_KVR, HEADS, HEAD_PAD)[:, :, :MLA_NOPE]
    dvv = d_mla_kv[:MLA_KVR, wide:].reshape(MLA_KVR, HEADS, HEAD_PAD)[:, :, :MLA_V]
    G["mla_w_ukv"] = jnp.concatenate([dkn, dvv], axis=2).reshape(MLA_KVR, -1)[None]
    G["mla_w_o"] = _unpad_head_rows(d_mla_o, MLA_V)[None]
    for name in ("ffn_w_in", "ffn_w_out", "ffn_conv_w", "ffn_conv_b"):
        G[name] = jnp.stack([g_ffn0[name], g_ffn1[name]])
    G["final_g"] = d_final_g[0]
    dmods = jnp.stack([dmod0, dmod1, dmod2, dmod3])
    return loss, dx0, dmods, G


PACKED = [("fox_w_in", 2), ("fox_w_o", 1), ("mla_w_a", 1), ("mla_w_uq", 2), ("mla_w_ukv", 2), ("mla_w_o", 1),
          ("ffn_w_in", 2), ("ffn_w_out", 1), ("fox_b_f", None), ("mla_g_q", 1), ("mla_g_kv", 1),
          ("ffn_conv_w", 2), ("ffn_conv_b", None), ("final_g", None)]


def _shard_of(g, axis, s):
    if axis is None:
        return g
    n = g.shape[axis] // 4
    return lax.slice_in_dim(g, s * n, (s + 1) * n, axis=axis)


def _pack_plan(G):
    shard_size = lambda name, axis: math.prod(_shard_of(G[name], axis, 0).shape)
    big = [(name, axis) for name, axis in PACKED if shard_size(name, axis) % PACK_ROW == 0]
    small = [(name, axis) for name, axis in PACKED if shard_size(name, axis) % PACK_ROW != 0]
    whole_tiles = lambda size: -(-size // (8 * PACK_ROW)) * 8
    runs = [([item], whole_tiles(shard_size(*item))) for item in big]
    runs.append((small, whole_tiles(sum(shard_size(*item) for item in small))))
    used = sum(rows for _, rows in runs)
    unit = 2 * PACK_ROWS_ALIGN
    return runs, -(-used // unit) * unit


def _pack(G):
    runs, total = _pack_plan(G)
    pieces = []
    for items, rows in runs:
        per_chip = [jnp.concatenate([_shard_of(G[name], axis, s).reshape(-1) for name, axis in items])
                    for s in range(4)]
        flat = jnp.stack(per_chip)
        flat = jnp.pad(flat, ((0, 0), (0, rows * PACK_ROW - flat.shape[1])))
        pieces.append(flat.reshape(4, rows, PACK_ROW))
    used = sum(rows for _, rows in runs)
    if total > used:
        pieces.append(jnp.zeros((4, total - used, PACK_ROW), F32))
    return jnp.concatenate(pieces, axis=1).reshape(4, 2, total // 2, PACK_ROW)


def _unpack(both, G_like):
    runs, total = _pack_plan(G_like)
    table = both.reshape(total, PACK_ROW)
    out, row = {}, 0
    for items, rows in runs:
        flat, off = table[row:row + rows].reshape(-1), 0
        for name, axis in items:
            shape = _shard_of(G_like[name], axis, 0).shape
            out[name] = flat[off:off + math.prod(shape)].reshape(shape)
            off += math.prod(shape)
        row += rows
    return out


WEIGHTS = ['ada_w', 'ada_b', 'fox_w_in', 'fox_b_f', 'fox_w_o', 'mla_w_a', 'mla_g_q', 'mla_g_kv', 'mla_w_uq',
           'mla_w_ukv', 'mla_w_o', 'ffn_w_in', 'ffn_conv_w', 'ffn_conv_b', 'ffn_w_out', 'final_g']
BIG = ['fox_w_in', 'fox_w_o', 'mla_w_a', 'mla_w_uq', 'mla_w_ukv', 'mla_w_o', 'ffn_w_in', 'ffn_w_out']
SMALL = ['ada_b', 'fox_b_f', 'mla_g_q', 'mla_g_kv', 'ffn_conv_w', 'ffn_conv_b', 'final_g']


def _as2d(a):
    return a.reshape(-1, a.shape[-1])


def kernel(x, c, ada_w, ada_b, fox_w_in, fox_b_f, fox_w_o, mla_w_a, mla_g_q, mla_g_kv, mla_w_uq, mla_w_ukv, mla_w_o, ffn_w_in, ffn_conv_w, ffn_conv_b, ffn_w_out, final_g, loss_target, m_ada_w, m_ada_b, m_fox_w_in, m_fox_b_f, m_fox_w_o, m_mla_w_a, m_mla_g_q, m_mla_g_kv, m_mla_w_uq, m_mla_w_ukv, m_mla_w_o, m_ffn_w_in, m_ffn_conv_w, m_ffn_conv_b, m_ffn_w_out, m_final_g, v_ada_w, v_ada_b, v_fox_w_in, v_fox_b_f, v_fox_w_o, v_mla_w_a, v_mla_g_q, v_mla_g_kv, v_mla_w_uq, v_mla_w_ukv, v_mla_w_o, v_ffn_w_in, v_ffn_conv_w, v_ffn_conv_b, v_ffn_w_out, v_final_g):
    w = dict(ada_w=ada_w, ada_b=ada_b, fox_w_in=fox_w_in, fox_b_f=fox_b_f, fox_w_o=fox_w_o, mla_w_a=mla_w_a,
             mla_g_q=mla_g_q, mla_g_kv=mla_g_kv, mla_w_uq=mla_w_uq, mla_w_ukv=mla_w_ukv, mla_w_o=mla_w_o,
             ffn_w_in=ffn_w_in, ffn_conv_w=ffn_conv_w, ffn_conv_b=ffn_conv_b, ffn_w_out=ffn_w_out, final_g=final_g)
    m = dict(ada_w=m_ada_w, ada_b=m_ada_b, fox_w_in=m_fox_w_in, fox_b_f=m_fox_b_f, fox_w_o=m_fox_w_o,
             mla_w_a=m_mla_w_a, mla_g_q=m_mla_g_q, mla_g_kv=m_mla_g_kv, mla_w_uq=m_mla_w_uq,
             mla_w_ukv=m_mla_w_ukv, mla_w_o=m_mla_w_o, ffn_w_in=m_ffn_w_in, ffn_conv_w=m_ffn_conv_w,
             ffn_conv_b=m_ffn_conv_b, ffn_w_out=m_ffn_w_out, final_g=m_final_g)
    v = dict(ada_w=v_ada_w, ada_b=v_ada_b, fox_w_in=v_fox_w_in, fox_b_f=v_fox_b_f, fox_w_o=v_fox_w_o,
             mla_w_a=v_mla_w_a, mla_g_q=v_mla_g_q, mla_g_kv=v_mla_g_kv, mla_w_uq=v_mla_w_uq,
             mla_w_ukv=v_mla_w_ukv, mla_w_o=v_mla_w_o, ffn_w_in=v_ffn_w_in, ffn_conv_w=v_ffn_conv_w,
             ffn_conv_b=v_ffn_conv_b, ffn_w_out=v_ffn_w_out, final_g=v_final_g)
    D = D_MODEL
    xi, yi, ci = lax.axis_index("x"), lax.axis_index("y"), lax.axis_index("c")
    dev = 4 * xi + 2 * yi + ci

    n_ada = ada_w.shape[-1]
    small_parts = [c.reshape(-1), ffn_conv_w.reshape(-1), mla_g_q.reshape(-1), mla_g_kv.reshape(-1)]
    sizes = [p.shape[0] for p in small_parts]
    flat = jnp.concatenate(small_parts)
    flat = jnp.pad(flat, (0, -flat.shape[0] % LANE))[None]
    got = _all_gather8(flat, name="gather_cond")[:, 0]
    offs = np.cumsum([0] + sizes)
    c_all = got[:, offs[0]:offs[1]]
    chips = got[0::2]
    conv_w_full = jnp.concatenate(
        [chips[s, offs[1]:offs[2]].reshape(ffn_conv_w.shape) for s in range(4)], axis=2)
    gq_full = jnp.concatenate([chips[s, offs[2]:offs[3]] for s in range(4)])[None]
    gkv_full = jnp.concatenate([chips[s, offs[3]:offs[4]] for s in range(4)])[None]

    mod_shard = _ada_mod(c_all, ada_w.reshape(4, D, n_ada), ada_b.reshape(4, 1, n_ada))
    mod_all = _all_gather8(mod_shard.reshape(4 * 8, n_ada), name="gather_mod")
    mod_all = mod_all[0::2].reshape(4, 4, 8, n_ada)
    mine = lax.dynamic_index_in_dim(mod_all, dev, axis=2, keepdims=False)
    mod_rows = jnp.transpose(mine, (1, 0, 2)).reshape(4, 4 * n_ada)
    mods = [(mod_rows[k:k + 1, :D], mod_rows[k:k + 1, D:2 * D], mod_rows[k:k + 1, 2 * D:]) for k in range(4)]

    shards = [w[name].astype(BF16) for name in BIG]
    W = _prepare_fox_in(_all_gather_chips(shards[:1], name="gather_weights")[0])
    W["pending"] = shards[1:]
    small = dict(fox_b_f=fox_b_f, mla_g_q=gq_full, mla_g_kv=gkv_full, ffn_conv_w=conv_w_full,
                 ffn_conv_b=ffn_conv_b, final_g=final_g[None])

    loss, dx, dmods, G = _local_step(x[0], loss_target[0], mods, W, small)

    loss_row = jnp.pad(loss[:, :1], ((0, 0), (0, dmods.shape[1] - 1)))
    gathered_rows = _all_gather8(jnp.concatenate([dmods, loss_row]), name="gather_dmod")
    dmod_all = gathered_rows[:, :4]
    loss_total = jnp.sum(gathered_rows[:, 4, 0])
    chip = 2 * xi + yi
    dmod_cols = lax.dynamic_slice_in_dim(dmod_all, chip * n_ada, n_ada, axis=2)
    g_ada_w, g_ada_b = _ada_grad(c_all.T, jnp.transpose(dmod_cols, (1, 0, 2)))
    grads = dict(ada_w=g_ada_w.reshape(ada_w.shape), ada_b=g_ada_b.reshape(ada_b.shape))

    packed = _pack(G)
    recv = _exchange_halves(packed)
    part = _add_halves(packed, recv, ci.reshape(1).astype(jnp.int32))
    from_chips = _exchange_chips(part)
    half = _add_chips(from_chips)
    other = _share_halves(half)
    both = jnp.stack([jnp.where(ci == 0, half, other), jnp.where(ci == 0, other, half)])
    grads.update(_unpack(both, G))

    delta, new_m, new_v = {}, {}, {}
    for name in BIG + ["ada_w"]:
        shape = w[name].shape
        d_, m_, v_ = _adamw(_as2d(w[name]), _as2d(grads[name]), _as2d(m[name]), _as2d(v[name]), name=f"adamw_{name}")
        delta[name], new_m[name], new_v[name] = d_.reshape(shape), m_.reshape(shape), v_.reshape(shape)
        grads[name] = grads[name].reshape(shape)
    sizes = [math.prod(w[name].shape) for name in SMALL]
    total = sum(sizes)
    rows = -(-total // LANE)
    rows += -rows % 8

    def pack_small(d):
        flat = jnp.concatenate([d[name].reshape(-1) for name in SMALL])
        return jnp.pad(flat, (0, rows * LANE - total)).reshape(rows, LANE)

    outs = _adamw(pack_small(w), pack_small(grads), pack_small(m), pack_small(v), name="adamw_small")
    off = 0
    for name, size in zip(SMALL, sizes):
        shape = w[name].shape
        for dst, src in zip((delta, new_m, new_v), outs):
            dst[name] = src.reshape(-1)[off:off + size].reshape(shape)
        grads[name] = grads[name].reshape(shape)
        off += size

    return (loss_total, dx[None], *[grads[n] for n in WEIGHTS], *[delta[n] for n in WEIGHTS],
            *[new_m[n] for n in WEIGHTS], *[new_v[n] for n in WEIGHTS])
```

```python
import functools
import math

import numpy as np
import jax
import jax.numpy as jnp
from jax import lax
from jax.experimental import pallas as pl
from jax.experimental.pallas import tpu as pltpu

F32 = jnp.float32
BF16 = jnp.bfloat16
MESH = pl.DeviceIdType.MESH

D_MODEL = 1024
HEADS = 16
HEAD_PAD = 128
FOX_HEAD = 64
MLA_NOPE = 64
MLA_ROPE = 32
MLA_V = 64
MLA_QR = 384
MLA_KVR = 256
D_FF = 2816
NORM_EPS = 1e-6
ROPE_BASE = 10000.0
ADAM_LR = 0.001
ADAM_B1 = 0.9
ADAM_B2 = 0.999
ADAM_EPS = 1e-08
ADAM_WD = 0.01
ADAM_STEP = 10
FOX_SCALE = FOX_HEAD ** -0.5
MLA_SCALE = (MLA_NOPE + MLA_ROPE) ** -0.5
NEG_BIG = -1e30
VMEM_LIMIT = 56 * 1024 * 1024
LANE = 128
MM_ROWS, MM_COLS, MM_DEPTH = 1408, 1408, 1408
PACK_ROW = 1024
PACK_ROWS_ALIGN = 256

NT = (((1,), (1,)), ((), ()))
TN = (((0,), (0,)), ((), ()))
NN = (((1,), (0,)), ((), ()))


def _cparams():
    return pltpu.CompilerParams(vmem_limit_bytes=VMEM_LIMIT)


def _tile(n, cap):
    if n <= cap:
        return n
    for t in range(cap - cap % LANE, 0, -LANE):
        if n % t == 0:
            return t
    raise ValueError((n, cap))


def _row_tile(rows, cols, limit_bytes=1 << 20):
    best = None
    for t in range(8, rows + 1, 8):
        if rows % t == 0 and t * cols * 4 <= limit_bytes:
            best = t
    return best if best is not None else rows


def _me():
    return lax.axis_index("x"), lax.axis_index("y"), lax.axis_index("c")


def _all_gather8(v, name):
    R, N = v.shape

    def body(v_ref, o_ref, ssem, rsem):
        x, y, c = _me()
        me = 4 * x + 2 * y + c
        o_ref[me] = v_ref[...]
        copies = []
        for k in range(1, 8):
            peer = (1 - x if k & 4 else x, 1 - y if k & 2 else y, 1 - c if k & 1 else c)
            cp = pltpu.make_async_remote_copy(
                src_ref=v_ref, dst_ref=o_ref.at[me], send_sem=ssem.at[k - 1], recv_sem=rsem.at[k - 1],
                device_id=peer, device_id_type=MESH)
            cp.start()
            copies.append(cp)
        for cp in copies:
            cp.wait()

    return pl.pallas_call(
        body, name=name,
        out_shape=jax.ShapeDtypeStruct((8, R, N), v.dtype),
        in_specs=[pl.BlockSpec(memory_space=pltpu.VMEM)],
        out_specs=pl.BlockSpec(memory_space=pltpu.VMEM),
        scratch_shapes=[pltpu.SemaphoreType.DMA((7,)), pltpu.SemaphoreType.DMA((7,))],
    )(v)


def _all_gather_chips(shards, name):
    n = len(shards)

    def body(*refs):
        copies = _chip_gather_copies(refs[:n], refs[n:2 * n], *refs[2 * n:])
        for cp in copies:
            cp.start()
        for cp in copies:
            cp.wait()

    return pl.pallas_call(
        body, name=name,
        out_shape=[jax.ShapeDtypeStruct((4,) + s.shape, s.dtype) for s in shards],
        in_specs=[pl.BlockSpec(memory_space=pl.ANY)] * n,
        out_specs=[pl.BlockSpec(memory_space=pl.ANY)] * n,
        scratch_shapes=[pltpu.SemaphoreType.DMA((3 * n,)), pltpu.SemaphoreType.DMA((3 * n,)),
                        pltpu.SemaphoreType.DMA((n,))],
    )(*shards)


def _exchange_halves(g):
    _, _, R, P = g.shape

    def body(g_ref, o_ref, ssem, rsem):
        x, y, c = _me()
        copies = []
        for s in range(4):
            cp = pltpu.make_async_remote_copy(
                src_ref=g_ref.at[s, 1 - c], dst_ref=o_ref.at[s], send_sem=ssem.at[s], recv_sem=rsem.at[s],
                device_id=(x, y, 1 - c), device_id_type=MESH)
            cp.start()
            copies.append(cp)
        for cp in copies:
            cp.wait()

    return pl.pallas_call(
        body, name="grad_exchange_sibling",
        out_shape=jax.ShapeDtypeStruct((4, R, P), g.dtype),
        in_specs=[pl.BlockSpec(memory_space=pl.ANY)],
        out_specs=pl.BlockSpec(memory_space=pl.ANY),
        scratch_shapes=[pltpu.SemaphoreType.DMA((4,)), pltpu.SemaphoreType.DMA((4,))],
    )(g)


def _exchange_chips(a):
    _, R, P = a.shape

    def body(a_ref, o_ref, ssem, rsem, lsem):
        x, y, c = _me()
        me = 2 * x + y
        own = pltpu.make_async_copy(a_ref.at[me], o_ref.at[me], lsem)
        own.start()
        copies = []
        for k in (1, 2, 3):
            px, py = (1 - x if k & 2 else x), (1 - y if k & 1 else y)
            cp = pltpu.make_async_remote_copy(
                src_ref=a_ref.at[2 * px + py], dst_ref=o_ref.at[me], send_sem=ssem.at[k - 1],
                recv_sem=rsem.at[k - 1], device_id=(px, py, c), device_id_type=MESH)
            cp.start()
            copies.append(cp)
        own.wait()
        for cp in copies:
            cp.wait()

    return pl.pallas_call(
        body, name="grad_exchange_chips",
        out_shape=jax.ShapeDtypeStruct((4, R, P), a.dtype),
        in_specs=[pl.BlockSpec(memory_space=pl.ANY)],
        out_specs=pl.BlockSpec(memory_space=pl.ANY),
        scratch_shapes=[pltpu.SemaphoreType.DMA((3,)), pltpu.SemaphoreType.DMA((3,)), pltpu.SemaphoreType.DMA],
    )(a)


def _share_halves(f):
    R, P = f.shape

    def body(f_ref, o_ref, ssem, rsem):
        x, y, c = _me()
        cp = pltpu.make_async_remote_copy(
            src_ref=f_ref, dst_ref=o_ref, send_sem=ssem, recv_sem=rsem,
            device_id=(x, y, 1 - c), device_id_type=MESH)
        cp.start()
        cp.wait()

    return pl.pallas_call(
        body, name="grad_share_sibling",
        out_shape=jax.ShapeDtypeStruct((R, P), f.dtype),
        in_specs=[pl.BlockSpec(memory_space=pl.ANY)],
        out_specs=pl.BlockSpec(memory_space=pl.ANY),
        scratch_shapes=[pltpu.SemaphoreType.DMA, pltpu.SemaphoreType.DMA],
    )(f)


def _add_halves(g, r, c_idx):
    _, _, R, P = g.shape
    t = PACK_ROWS_ALIGN

    def body(c_ref, g_ref, r_ref, o_ref):
        o_ref[...] = (g_ref[0] + r_ref[...]).astype(BF16)

    return pl.pallas_call(
        body, name="grad_add_halves",
        out_shape=jax.ShapeDtypeStruct((4, R, P), BF16),
        grid_spec=pltpu.PrefetchScalarGridSpec(
            num_scalar_prefetch=1, grid=(4, R // t),
            in_specs=[pl.BlockSpec((1, 1, t, P), lambda s, i, c_ref: (s, c_ref[0], i, 0)),
                      pl.BlockSpec((1, t, P), lambda s, i, c_ref: (s, i, 0))],
            out_specs=pl.BlockSpec((1, t, P), lambda s, i, c_ref: (s, i, 0))),
        compiler_params=_cparams(),
    )(c_idx, g, r)


def _add_chips(b):
    _, R, P = b.shape
    t = PACK_ROWS_ALIGN

    def body(b_ref, o_ref):
        b0, b1, b2, b3 = (b_ref[k].astype(F32) for k in range(4))
        o_ref[...] = ((b0 + b1) + b2) + b3

    return pl.pallas_call(
        body, name="grad_add_chips",
        out_shape=jax.ShapeDtypeStruct((R, P), F32),
        grid=(R // t,),
        in_specs=[pl.BlockSpec((4, t, P), lambda i: (0, i, 0))],
        out_specs=pl.BlockSpec((t, P), lambda i: (i, 0)),
        compiler_params=_cparams(),
    )(b)


def _mm(a, b, *, ta=False, tb=False, out_dtype=F32, res=None, gated=None, name):
    if ta:
        K, M = a.shape
    else:
        M, K = a.shape
    N = b.shape[0] if tb else b.shape[1]
    assert (b.shape[1] if tb else b.shape[0]) == K, (a.shape, b.shape, ta, tb)
    tm, tn, tk = _tile(M, MM_ROWS), _tile(N, MM_COLS), _tile(K, MM_DEPTH)
    nk = K // tk
    has_res = res is not None
    has_gate = gated is not None
    dims = (((0 if ta else 1,), (1 if tb else 0,)), ((), ()))

    def body(*refs):
        refs = iter(refs)
        a_ref, b_ref = next(refs), next(refs)
        r_ref = next(refs) if has_res else None
        x_ref, g_ref = (next(refs), next(refs)) if has_gate else (None, None)
        o_ref = next(refs)
        xo_ref = next(refs) if has_gate else None
        acc_ref = next(refs) if nk > 1 else None

        def finish(r):
            if has_res:
                r = r + r_ref[...]
            o_ref[...] = r.astype(out_dtype)
            if has_gate:
                xo_ref[...] = x_ref[...] + g_ref[...] * r

        prod = lax.dot_general(a_ref[...].astype(BF16), b_ref[...].astype(BF16), dims,
                               preferred_element_type=F32)
        if nk == 1:
            finish(prod)
            return
        k = pl.program_id(2)

        @pl.when(k == 0)
        def _():
            acc_ref[...] = prod

        @pl.when(k > 0)
        def _():
            acc_ref[...] += prod

        @pl.when(k == nk - 1)
        def _():
            finish(acc_ref[...])

    a_spec = (pl.BlockSpec((tk, tm), lambda j, i, k: (k, i)) if ta
              else pl.BlockSpec((tm, tk), lambda j, i, k: (i, k)))
    b_spec = (pl.BlockSpec((tn, tk), lambda j, i, k: (j, k)) if tb
              else pl.BlockSpec((tk, tn), lambda j, i, k: (k, j)))
    in_specs = [a_spec, b_spec]
    args = [a, b]
    out_tile = pl.BlockSpec((tm, tn), lambda j, i, k: (i, j))
    if has_res:
        in_specs.append(out_tile)
        args.append(res)
    out_shape, out_specs = jax.ShapeDtypeStruct((M, N), out_dtype), out_tile
    if has_gate:
        in_specs += [out_tile, pl.BlockSpec((1, tn), lambda j, i, k: (0, j))]
        args += list(gated)
        out_shape, out_specs = [out_shape, jax.ShapeDtypeStruct((M, N), F32)], [out_tile, out_tile]
    return pl.pallas_call(
        body, name=name, out_shape=out_shape, grid=(N // tn, M // tm, nk), in_specs=in_specs,
        out_specs=out_specs, scratch_shapes=[pltpu.VMEM((tm, tn), F32)] if nk > 1 else [],
        compiler_params=_cparams(),
    )(*args)


def _rstd(xv):
    return lax.rsqrt(jnp.mean(xv * xv, axis=-1, keepdims=True) + NORM_EPS)


def _norm_bwd(n, r, dn):
    return r * (dn - n * jnp.mean(dn * n, axis=-1, keepdims=True))


def _adaln_fwd(x, scale, shift, name):
    S, D = x.shape
    t = _tile(S, 512)

    def body(x_ref, sc_ref, sh_ref, h_ref):
        xv = x_ref[...]
        h_ref[...] = (xv * _rstd(xv) * (1.0 + sc_ref[...]) + sh_ref[...]).astype(BF16)

    row = pl.BlockSpec((t, D), lambda i: (i, 0))
    vec = pl.BlockSpec((1, D), lambda i: (0, 0))
    return pl.pallas_call(
        body, name=name, out_shape=jax.ShapeDtypeStruct((S, D), BF16), grid=(S // t,),
        in_specs=[row, vec, vec], out_specs=row, compiler_params=_cparams(),
    )(x, scale, shift)


def _adaln_bwd(x, dh, dx_next, scale, name):
    S, D = x.shape
    t = _tile(S, 512)

    def body(x_ref, dh_ref, dxn_ref, sc_ref, dx_ref, dsc_ref, dsh_ref):
        @pl.when(pl.program_id(0) == 0)
        def _():
            dsc_ref[...] = jnp.zeros_like(dsc_ref)
            dsh_ref[...] = jnp.zeros_like(dsh_ref)

        xv = x_ref[...]
        r = _rstd(xv)
        n = xv * r
        dh = dh_ref[...]
        dx_ref[...] = dxn_ref[...] + _norm_bwd(n, r, dh * (1.0 + sc_ref[...]))
        dsc_ref[...] += jnp.sum(dh * n, axis=0, keepdims=True)
        dsh_ref[...] += jnp.sum(dh, axis=0, keepdims=True)

    row = pl.BlockSpec((t, D), lambda i: (i, 0))
    vec = pl.BlockSpec((1, D), lambda i: (0, 0))
    return pl.pallas_call(
        body, name=name,
        out_shape=[jax.ShapeDtypeStruct((S, D), F32), jax.ShapeDtypeStruct((1, D), F32),
                   jax.ShapeDtypeStruct((1, D), F32)],
        grid=(S // t,), in_specs=[row, row, row, vec], out_specs=[row, vec, vec],
        compiler_params=_cparams(),
    )(x, dh, dx_next, scale)


def _gate_bwd(dx, y, gate, name):
    S, D = dx.shape
    t = _tile(S, 512)

    def body(dx_ref, y_ref, g_ref, dy_ref, dg_ref):
        @pl.when(pl.program_id(0) == 0)
        def _():
            dg_ref[...] = jnp.zeros_like(dg_ref)

        dxv = dx_ref[...]
        dy_ref[...] = (dxv * g_ref[...]).astype(BF16)
        dg_ref[...] += jnp.sum(dxv * y_ref[...], axis=0, keepdims=True)

    row = pl.BlockSpec((t, D), lambda i: (i, 0))
    vec = pl.BlockSpec((1, D), lambda i: (0, 0))
    return pl.pallas_call(
        body, name=name,
        out_shape=[jax.ShapeDtypeStruct((S, D), BF16), jax.ShapeDtypeStruct((1, D), F32)],
        grid=(S // t,), in_specs=[row, row, vec], out_specs=[row, vec], compiler_params=_cparams(),
    )(dx, y, gate)


def _final_loss(x, g, target):
    S, D = x.shape
    t = _tile(S, 512)

    def body(x_ref, g_ref, t_ref, loss_ref, dx_ref, dg_ref):
        @pl.when(pl.program_id(0) == 0)
        def _():
            loss_ref[...] = jnp.zeros_like(loss_ref)
            dg_ref[...] = jnp.zeros_like(dg_ref)

        xv = x_ref[...]
        gv = g_ref[...]
        r = _rstd(xv)
        n = xv * r
        err = n * gv - t_ref[...]
        part = jnp.sum(jnp.mean(err * err, axis=-1, keepdims=True), axis=0, keepdims=True)
        loss_ref[...] += jnp.broadcast_to(0.5 * part, loss_ref.shape)
        dy = err * (1.0 / D)
        dg_ref[...] += jnp.sum(dy * n, axis=0, keepdims=True)
        dx_ref[...] = _norm_bwd(n, r, dy * gv)

    row = pl.BlockSpec((t, D), lambda i: (i, 0))
    vec = pl.BlockSpec((1, D), lambda i: (0, 0))
    one = pl.BlockSpec((1, LANE), lambda i: (0, 0))
    return pl.pallas_call(
        body, name="final_loss",
        out_shape=[jax.ShapeDtypeStruct((1, LANE), F32), jax.ShapeDtypeStruct((S, D), F32),
                   jax.ShapeDtypeStruct((1, D), F32)],
        grid=(S // t,), in_specs=[row, vec, row], out_specs=[one, row, vec], compiler_params=_cparams(),
    )(x, g, target)


ATTN_TILE = 1024
ATTN_ROWS = 16
ONES_LANE = 127


def _causal_pairs(n, k_major):
    if k_major:
        pairs = [(i, j) for j in range(n) for i in range(j, n)]
    else:
        pairs = [(i, j) for i in range(n) for j in range(i + 1)]
    return (jnp.asarray(np.array([p[0] for p in pairs], np.int32)),
            jnp.asarray(np.array([p[1] for p in pairs], np.int32)))


def _chip_gather_copies(ins, outs, ssem, rsem, lsem):
    x, y, c = _me()
    me = 2 * x + y
    copies = []
    for i, (src, dst) in enumerate(zip(ins, outs)):
        copies.append(pltpu.make_async_copy(src, dst.at[me], lsem.at[i]))
        for k in (1, 2, 3):
            peer = (1 - x if k & 2 else x, 1 - y if k & 1 else y, c)
            copies.append(pltpu.make_async_remote_copy(
                src_ref=src, dst_ref=dst.at[me], send_sem=ssem.at[3 * i + k - 1],
                recv_sem=rsem.at[3 * i + k - 1], device_id=peer, device_id_type=MESH))
    return copies


def _attn_fwd(q_arr, q_off, k_arr, k_off, v_arr, v_off, name, gather=()):
    S = q_arr.shape[0]
    t = _tile(S, 2048)
    n = S // t
    rc = min(ATTN_ROWS, t)
    ii, jj = _causal_pairs(n, k_major=False)
    ng = len(gather)
    last_step = ii.shape[0] - 1

    def body(ii_ref, jj_ref, q_ref, k_ref, v_ref, *rest):
        g_in, (o_ref, lse_ref), g_out = rest[:ng], rest[ng:ng + 2], rest[ng + 2:2 * ng + 2]
        s_s, m_s, a_s, acc_s = rest[2 * ng + 2:2 * ng + 6]
        step = pl.program_id(1)
        i, j = ii_ref[step], jj_ref[step]
        lane = lax.broadcasted_iota(jnp.int32, (t, HEAD_PAD), 1)
        if ng:
            copies = _chip_gather_copies(g_in, g_out, *rest[2 * ng + 6:])

            @pl.when(jnp.logical_and(pl.program_id(0) == 0, step == 0))
            def _():
                for cp in copies:
                    cp.start()

            @pl.when(jnp.logical_and(pl.program_id(0) == HEADS - 1, step == last_step))
            def _():
                for cp in copies:
                    cp.wait()

        @pl.when(j == 0)
        def _():
            m_s[...] = jnp.full_like(m_s, NEG_BIG)
            acc_s[...] = jnp.zeros_like(acc_s)

        def tile(diagonal):
            s_s[...] = lax.dot_general(q_ref[...], k_ref[...], NT, preferred_element_type=F32)
            for r in range(t // rc):
                rows = pl.ds(r * rc, rc)
                sc = s_s[rows, :]
                if diagonal:
                    qpos = r * rc + lax.broadcasted_iota(jnp.int32, (rc, t), 0)
                    kpos = lax.broadcasted_iota(jnp.int32, (rc, t), 1)
                    sc = jnp.where(qpos >= kpos, sc, NEG_BIG)
                m_old = m_s[rows, :]
                m_new = jnp.maximum(m_old, jnp.max(sc, axis=1, keepdims=True))
                s_s[rows, :] = jnp.exp(sc - m_new)
                a_s[rows, :] = jnp.exp(m_old - m_new)
                m_s[rows, :] = m_new
            v_ones = jnp.where(lane == ONES_LANE, 1.0, v_ref[...].astype(F32)).astype(BF16)
            acc_s[...] = a_s[...] * acc_s[...] + lax.dot_general(s_s[...].astype(BF16), v_ones, NN,
                                                                 preferred_element_type=F32)

        @pl.when(j < i)
        def _():
            tile(False)

        @pl.when(j == i)
        def _():
            tile(True)
            acc = acc_s[...]
            l = jnp.sum(jnp.where(lane == ONES_LANE, acc, 0.0), axis=1, keepdims=True)
            o_ref[...] = jnp.where(lane == ONES_LANE, 0.0, acc / l).astype(BF16)
            lse_ref[0] = m_s[...] + jnp.log(l)

    hbm = pl.BlockSpec(memory_space=pl.ANY)
    sems = [pltpu.SemaphoreType.DMA((3 * ng,)), pltpu.SemaphoreType.DMA((3 * ng,)),
            pltpu.SemaphoreType.DMA((ng,))] if ng else []
    grid_spec = pltpu.PrefetchScalarGridSpec(
        num_scalar_prefetch=2, grid=(HEADS, ii.shape[0]),
        in_specs=[pl.BlockSpec((t, HEAD_PAD), lambda h, s, ii, jj: (ii[s], q_off + h)),
                  pl.BlockSpec((t, HEAD_PAD), lambda h, s, ii, jj: (jj[s], k_off + h)),
                  pl.BlockSpec((t, HEAD_PAD), lambda h, s, ii, jj: (jj[s], v_off + h))] + [hbm] * ng,
        out_specs=[pl.BlockSpec((t, HEAD_PAD), lambda h, s, ii, jj: (ii[s], h)),
                   pl.BlockSpec((1, t, 1), lambda h, s, ii, jj: (h, ii[s], 0))] + [hbm] * ng,
        scratch_shapes=[pltpu.VMEM((t, t), F32), pltpu.VMEM((t, 1), F32), pltpu.VMEM((t, 1), F32),
                        pltpu.VMEM((t, HEAD_PAD), F32)] + sems)
    outs = pl.pallas_call(
        body, name=name, grid_spec=grid_spec,
        out_shape=[jax.ShapeDtypeStruct((S, HEADS * HEAD_PAD), BF16), jax.ShapeDtypeStruct((HEADS, S, 1), F32)]
        + [jax.ShapeDtypeStruct((4,) + g.shape, g.dtype) for g in gather],
        compiler_params=_cparams(),
    )(ii, jj, q_arr, k_arr, v_arr, *gather)
    return outs[0], outs[1], list(outs[2:])


def _attn_delta(do, o, name):
    S = do.shape[0]
    t = _tile(S, 512)

    def body(do_ref, o_ref, d_ref):
        lane = lax.broadcasted_iota(jnp.int32, (t, LANE), 1)
        acc = jnp.zeros((t, LANE), F32)
        for h in range(HEADS):
            cols = pl.ds(h * HEAD_PAD, HEAD_PAD)
            d = jnp.sum(do_ref[:, cols].astype(F32) * o_ref[:, cols].astype(F32), axis=1, keepdims=True)
            acc = jnp.where(lane == h, d, acc)
        d_ref[...] = acc

    blk = pl.BlockSpec((t, HEADS * HEAD_PAD), lambda i: (i, 0))
    by_lane = pl.pallas_call(
        body, name=name, out_shape=jax.ShapeDtypeStruct((S, LANE), F32), grid=(S // t,),
        in_specs=[blk, blk], out_specs=pl.BlockSpec((t, LANE), lambda i: (i, 0)),
        compiler_params=_cparams(),
    )(do, o)
    return by_lane[:, :HEADS].T.reshape(HEADS, 1, S)


def _lane_as_row(x, lane):
    pick = (lax.broadcasted_iota(jnp.int32, (8, HEAD_PAD), 1) == lane).astype(BF16)
    acc = None
    for part in _split3(x):
        d = lax.dot_general(pick, part, NT, preferred_element_type=F32)
        acc = d if acc is None else acc + d
    return jnp.max(acc, axis=0, keepdims=True)


def _attn_bwd(q_arr, q_off, k_arr, k_off, v_arr, v_off, do, lse, delta, name, bias_grads=False, dq_dtype=F32):
    S = q_arr.shape[0]
    t = _tile(S, ATTN_TILE)
    n = S // t
    rc = min(ATTN_ROWS, t)
    ii, jj = _causal_pairs(n, k_major=True)
    last_step = ii.shape[0] - 1

    def body(ii_ref, jj_ref, q_ref, k_ref, v_ref, do_ref, lse_ref, dl_ref, dq_ref, dk_ref, dv_ref, *rest):
        if bias_grads:
            dcq_ref, dck_ref = rest[:2]
        s_s, dp_s, dk_s, dv_s, dq_s = rest[-5:]
        step = pl.program_id(1)
        i, j = ii_ref[step], jj_ref[step]

        @pl.when(step == 0)
        def _():
            dq_s[...] = jnp.zeros_like(dq_s)

        @pl.when(i == j)
        def _():
            dk_s[...] = jnp.zeros_like(dk_s)
            dv_s[...] = jnp.zeros_like(dv_s)

        def tile(diagonal):
            q, k, dov = q_ref[...], k_ref[...], do_ref[...]
            s_s[...] = lax.dot_general(k, q, NT, preferred_element_type=F32)
            dp_s[...] = lax.dot_general(v_ref[...], dov, NT, preferred_element_type=F32)
            lse_row, dl_row = lse_ref[0], dl_ref[0]
            for r in range(t // rc):
                rows = pl.ds(r * rc, rc)
                sc = s_s[rows, :]
                if diagonal:
                    kpos = r * rc + lax.broadcasted_iota(jnp.int32, (rc, t), 0)
                    qpos = lax.broadcasted_iota(jnp.int32, (rc, t), 1)
                    sc = jnp.where(qpos >= kpos, sc, NEG_BIG)
                p = jnp.exp(sc - lse_row)
                s_s[rows, :] = p
                dp_s[rows, :] = p * (dp_s[rows, :] - dl_row)
            ds = dp_s[...].astype(BF16)
            dv_s[...] += lax.dot_general(s_s[...].astype(BF16), dov, NN, preferred_element_type=F32)
            dk_s[...] += lax.dot_general(ds, q, NN, preferred_element_type=F32)
            rows = pl.ds(pl.multiple_of(i * t, t), t)
            dq_s[rows, :] += lax.dot_general(ds, k, TN, preferred_element_type=F32)

        @pl.when(i > j)
        def _():
            tile(False)

        @pl.when(i == j)
        def _():
            tile(True)

        @pl.when(i == n - 1)
        def _():
            dk = dk_s[...]
            dk_ref[...] = dk.astype(BF16)
            dv_ref[...] = dv_s[...].astype(BF16)
            if bias_grads:
                dck_ref[0] = _lane_as_row(dk, BIAS_LANE)

        @pl.when(step == last_step)
        def _():
            dq = dq_s[...]
            dq_ref[...] = dq.astype(dq_dtype)
            if bias_grads:
                dcq_ref[0] = _lane_as_row(dq, BIAS_LANE + 3)

    qblk = lambda off: pl.BlockSpec((t, HEAD_PAD), lambda h, s, ii, jj: (ii[s], off + h))
    kblk = lambda off: pl.BlockSpec((t, HEAD_PAD), lambda h, s, ii, jj: (jj[s], off + h))
    stat = pl.BlockSpec((1, 1, t), lambda h, s, ii, jj: (h, 0, ii[s]))
    wide = (S, HEADS * HEAD_PAD)
    out_specs = [pl.BlockSpec((S, HEAD_PAD), lambda h, s, ii, jj: (0, h)), kblk(0), kblk(0)]
    out_shape = [jax.ShapeDtypeStruct(wide, dq_dtype), jax.ShapeDtypeStruct(wide, BF16),
                 jax.ShapeDtypeStruct(wide, BF16)]
    if bias_grads:
        out_specs += [pl.BlockSpec((1, 1, S), lambda h, s, ii, jj: (h, 0, 0)),
                      pl.BlockSpec((1, 1, t), lambda h, s, ii, jj: (h, 0, jj[s]))]
        out_shape += [jax.ShapeDtypeStruct((HEADS, 1, S), F32)] * 2
    grid_spec = pltpu.PrefetchScalarGridSpec(
        num_scalar_prefetch=2, grid=(HEADS, ii.shape[0]),
        in_specs=[qblk(q_off), kblk(k_off), kblk(v_off), qblk(0), stat, stat],
        out_specs=out_specs,
        scratch_shapes=[pltpu.VMEM((t, t), F32), pltpu.VMEM((t, t), F32), pltpu.VMEM((t, HEAD_PAD), F32),
                        pltpu.VMEM((t, HEAD_PAD), F32), pltpu.VMEM((S, HEAD_PAD), F32)])
    return pl.pallas_call(
        body, name=name, grid_spec=grid_spec, out_shape=out_shape, compiler_params=_cparams(),
    )(ii, jj, q_arr, k_arr, v_arr, do, lse, delta)


BIAS_LANE = FOX_HEAD


def _fox_expand(qkv, cum, name):
    S = qkv.shape[0]
    t = _tile(S, 256)
    narrow, wide = HEADS * FOX_HEAD, HEADS * HEAD_PAD

    def body(q_ref, k_ref, v_ref, c_ref, qo_ref, ko_ref, vo_ref):
        lane = lax.broadcasted_iota(jnp.int32, (t, HEAD_PAD), 1)
        cum_all = c_ref[...]

        def head_tile(ref, h):
            x = ref[:, pl.ds((h // 2) * HEAD_PAD, HEAD_PAD)].astype(F32)
            if h % 2:
                x = pltpu.roll(x, FOX_HEAD, 1)
            return jnp.where(lane < FOX_HEAD, x, 0.0)

        def fill(out, first, second):
            for n_, (a, b) in enumerate(zip(first, second)):
                out = jnp.where(lane == BIAS_LANE + n_, a, out)
                out = jnp.where(lane == BIAS_LANE + 3 + n_, b, out)
            return out.astype(BF16)

        ones = [1.0, 1.0, 1.0]
        for h in range(HEADS):
            cols = pl.ds(h * HEAD_PAD, HEAD_PAD)
            c = jnp.sum(jnp.where(lane == h, cum_all, 0.0), axis=1, keepdims=True)
            pieces = [x.astype(F32) for x in _split3(c)]
            qo_ref[:, cols] = fill(head_tile(q_ref, h), ones, pieces)
            ko_ref[:, cols] = fill(head_tile(k_ref, h), [-x for x in pieces], ones)
            vo_ref[:, cols] = head_tile(v_ref, h).astype(BF16)

    part = lambda p: pl.BlockSpec((t, narrow), lambda i: (i, p))
    out = pl.BlockSpec((t, wide), lambda i: (i, 0))
    shp = jax.ShapeDtypeStruct((S, wide), BF16)
    return pl.pallas_call(
        body, name=name, out_shape=[shp, shp, shp], grid=(S // t,),
        in_specs=[part(0), part(1), part(2), pl.BlockSpec((t, LANE), lambda i: (i, 0))],
        out_specs=[out, out, out], compiler_params=_cparams(),
    )(qkv, qkv, qkv, cum)


def _fox_compress(dq, dk, dv, name):
    S = dq.shape[0]
    t = _tile(S, 256)
    narrow, wide = HEADS * FOX_HEAD, HEADS * HEAD_PAD

    def body(dq_ref, dk_ref, dv_ref, o_ref):
        lane = lax.broadcasted_iota(jnp.int32, (t, HEAD_PAD), 1)
        for p, ref in enumerate((dq_ref, dk_ref, dv_ref)):
            for g in range(HEADS // 2):
                even = ref[:, pl.ds(2 * g * HEAD_PAD, HEAD_PAD)].astype(F32)
                odd = ref[:, pl.ds((2 * g + 1) * HEAD_PAD, HEAD_PAD)].astype(F32)
                both = jnp.where(lane < FOX_HEAD, even, pltpu.roll(odd, FOX_HEAD, 1))
                o_ref[:, pl.ds(p * narrow + g * HEAD_PAD, HEAD_PAD)] = both.astype(BF16)

    blk = pl.BlockSpec((t, wide), lambda i: (i, 0))
    return pl.pallas_call(
        body, name=name, out_shape=jax.ShapeDtypeStruct((S, 3 * narrow), BF16), grid=(S // t,),
        in_specs=[blk, blk, blk], out_specs=pl.BlockSpec((t, 3 * narrow), lambda i: (i, 0)),
        compiler_params=_cparams(),
    )(dq, dk, dv)


def _split3(x):
    hi = x.astype(BF16)
    r1 = x - hi.astype(F32)
    mid = r1.astype(BF16)
    lo = (r1 - mid.astype(F32)).astype(BF16)
    return hi, mid, lo


def _tri_matmul(tri, x):
    acc = None
    for part in _split3(x):
        d = lax.dot_general(tri, part, NN, preferred_element_type=F32)
        acc = d if acc is None else acc + d
    return acc


def _log_sigmoid(z):
    return jnp.minimum(z, 0.0) - jnp.log1p(jnp.exp(-jnp.abs(z)))


def _fox_gate_fwd(fl, bf):
    S = fl.shape[0]
    t = _tile(S, 256)

    def body(fl_ref, bf_ref, cum_ref, carry):
        @pl.when(pl.program_id(0) == 0)
        def _():
            carry[...] = jnp.zeros_like(carry)

        logf = _log_sigmoid(fl_ref[...] + bf_ref[...])
        tri = (lax.broadcasted_iota(jnp.int32, (t, t), 0) >= lax.broadcasted_iota(jnp.int32, (t, t), 1)).astype(BF16)
        cum_ref[...] = _tri_matmul(tri, logf) + carry[...]
        carry[...] += jnp.sum(logf, axis=0, keepdims=True)

    row = pl.BlockSpec((t, LANE), lambda i: (i, 0))
    return pl.pallas_call(
        body, name="fox_gate_fwd", out_shape=jax.ShapeDtypeStruct((S, LANE), F32), grid=(S // t,),
        in_specs=[row, pl.BlockSpec((1, LANE), lambda i: (0, 0))], out_specs=row,
        scratch_shapes=[pltpu.VMEM((1, LANE), F32)], compiler_params=_cparams(),
    )(fl, bf)


def _fox_gate_bwd(dcum, fl, bf):
    S = fl.shape[0]
    t = _tile(S, 256)
    n = S // t

    def body(dc_ref, fl_ref, bf_ref, df_ref, db_ref, carry):
        @pl.when(pl.program_id(0) == 0)
        def _():
            carry[...] = jnp.zeros_like(carry)
            db_ref[...] = jnp.zeros_like(db_ref)

        dc = dc_ref[...]
        tri = (lax.broadcasted_iota(jnp.int32, (t, t), 1) >= lax.broadcasted_iota(jnp.int32, (t, t), 0)).astype(BF16)
        dlogf = _tri_matmul(tri, dc) + carry[...]
        carry[...] += jnp.sum(dc, axis=0, keepdims=True)
        z = fl_ref[...] + bf_ref[...]
        e = jnp.exp(-jnp.abs(z))
        sig_neg = jnp.where(z >= 0, e, 1.0) / (1.0 + e)
        lanes = lax.broadcasted_iota(jnp.int32, (t, LANE), 1)
        df = jnp.where(lanes < HEADS, dlogf * sig_neg, 0.0)
        df_ref[...] = df
        db_ref[...] += jnp.sum(df, axis=0, keepdims=True)

    row = pl.BlockSpec((t, LANE), lambda i: (n - 1 - i, 0))
    vec = pl.BlockSpec((1, LANE), lambda i: (0, 0))
    return pl.pallas_call(
        body, name="fox_gate_bwd",
        out_shape=[jax.ShapeDtypeStruct((S, LANE), F32), jax.ShapeDtypeStruct((1, LANE), F32)],
        grid=(n,), in_specs=[row, row, vec], out_specs=[row, vec],
        scratch_shapes=[pltpu.VMEM((1, LANE), F32)], compiler_params=_cparams(),
    )(dcum, fl, bf)


def _rope_tables(S):
    pos = jnp.arange(S, dtype=F32)
    inv_freq = ROPE_BASE ** (-jnp.arange(0, MLA_ROPE, 2, dtype=F32) / MLA_ROPE)
    ang = pos[:, None] * inv_freq[None, :]
    cos, sin = jnp.cos(ang), jnp.sin(ang)
    half = MLA_ROPE // 2
    ones = jnp.ones((S, MLA_NOPE), F32)
    z = lambda w: jnp.zeros((S, w), F32)
    keep = jnp.concatenate([ones, cos, cos, z(HEAD_PAD - MLA_NOPE - MLA_ROPE)], axis=1)
    from_above = jnp.concatenate([z(MLA_NOPE), -sin, z(HEAD_PAD - MLA_NOPE - half)], axis=1)
    from_below = jnp.concatenate([z(MLA_NOPE + half), sin, z(HEAD_PAD - MLA_NOPE - MLA_ROPE)], axis=1)
    return keep, from_above, from_below


def _rope(x, keep, up, down, transpose):
    half = MLA_ROPE // 2
    if transpose:
        return x * keep + pltpu.roll(x * up, half, 1) + pltpu.roll(x * down, HEAD_PAD - half, 1)
    return x * keep + pltpu.roll(x, HEAD_PAD - half, 1) * up + pltpu.roll(x, half, 1) * down


def _rope_heads(x, tables, transpose, scale, name):
    S = x.shape[0]
    t = _tile(S, 256)

    def body(x_ref, a_ref, b_ref, c_ref, o_ref):
        keep, up, down = a_ref[...], b_ref[...], c_ref[...]
        for h in range(HEADS):
            cols = pl.ds(h * HEAD_PAD, HEAD_PAD)
            o_ref[:, cols] = (_rope(x_ref[:, cols], keep, up, down, transpose) * scale).astype(BF16)

    blk = pl.BlockSpec((t, HEADS * HEAD_PAD), lambda i: (i, 0))
    tab = pl.BlockSpec((t, HEAD_PAD), lambda i: (i, 0))
    return pl.pallas_call(
        body, name=name, out_shape=jax.ShapeDtypeStruct(x.shape, BF16), grid=(S // t,),
        in_specs=[blk, tab, tab, tab], out_specs=blk, compiler_params=_cparams(),
    )(x, *tables)


MLA_A_PAD = MLA_QR + MLA_KVR + HEAD_PAD
MLA_KV_IN = MLA_KVR + HEAD_PAD


def _mla_mid_fwd(a, gq, gkv, tables):
    S = a.shape[0]
    t = _tile(S, 512)

    def body(a_ref, gq_ref, gkv_ref, ta_ref, tb_ref, tc_ref, q_ref, kv_ref):
        cq = a_ref[:, :MLA_QR]
        ckv = a_ref[:, MLA_QR:MLA_QR + MLA_KVR]
        kr = a_ref[:, MLA_QR + MLA_KVR:]
        q_ref[...] = (cq * _rstd(cq) * gq_ref[...]).astype(BF16)
        kv_ref[:, :MLA_KVR] = (ckv * _rstd(ckv) * gkv_ref[...]).astype(BF16)
        kv_ref[:, MLA_KVR:] = _rope(kr, ta_ref[...], tb_ref[...], tc_ref[...], False).astype(BF16)

    tab = pl.BlockSpec((t, HEAD_PAD), lambda i: (i, 0))
    return pl.pallas_call(
        body, name="mla_mid_fwd",
        out_shape=[jax.ShapeDtypeStruct((S, MLA_QR), BF16), jax.ShapeDtypeStruct((S, MLA_KV_IN), BF16)],
        grid=(S // t,),
        in_specs=[pl.BlockSpec((t, MLA_A_PAD), lambda i: (i, 0)), pl.BlockSpec((1, MLA_QR), lambda i: (0, 0)),
                  pl.BlockSpec((1, MLA_KVR), lambda i: (0, 0)), tab, tab, tab],
        out_specs=[pl.BlockSpec((t, MLA_QR), lambda i: (i, 0)), pl.BlockSpec((t, MLA_KV_IN), lambda i: (i, 0))],
        compiler_params=_cparams(),
    )(a, gq, gkv, *tables)


def _mla_mid_bwd(a, dq, dkv, gq, gkv, tables):
    S = a.shape[0]
    t = _tile(S, 512)

    def body(a_ref, dq_ref, dkv_ref, gq_ref, gkv_ref, ta_ref, tb_ref, tc_ref, da_ref, dgq_ref, dgkv_ref):
        @pl.when(pl.program_id(0) == 0)
        def _():
            dgq_ref[...] = jnp.zeros_like(dgq_ref)
            dgkv_ref[...] = jnp.zeros_like(dgkv_ref)

        def one(c, dout, g):
            r = _rstd(c)
            n = c * r
            return _norm_bwd(n, r, dout * g), jnp.sum(dout * n, axis=0, keepdims=True)

        dcq, dgq = one(a_ref[:, :MLA_QR], dq_ref[...], gq_ref[...])
        dckv, dgkv = one(a_ref[:, MLA_QR:MLA_QR + MLA_KVR], dkv_ref[:, :MLA_KVR], gkv_ref[...])
        da_ref[:, :MLA_QR] = dcq.astype(BF16)
        da_ref[:, MLA_QR:MLA_QR + MLA_KVR] = dckv.astype(BF16)
        da_ref[:, MLA_QR + MLA_KVR:] = _rope(dkv_ref[:, MLA_KVR:], ta_ref[...], tb_ref[...], tc_ref[...],
                                             True).astype(BF16)
        dgq_ref[...] += dgq
        dgkv_ref[...] += dgkv

    tab = pl.BlockSpec((t, HEAD_PAD), lambda i: (i, 0))
    vq = pl.BlockSpec((1, MLA_QR), lambda i: (0, 0))
    vkv = pl.BlockSpec((1, MLA_KVR), lambda i: (0, 0))
    return pl.pallas_call(
        body, name="mla_mid_bwd",
        out_shape=[jax.ShapeDtypeStruct((S, MLA_A_PAD), BF16), jax.ShapeDtypeStruct((1, MLA_QR), F32),
                   jax.ShapeDtypeStruct((1, MLA_KVR), F32)],
        grid=(S // t,),
        in_specs=[pl.BlockSpec((t, MLA_A_PAD), lambda i: (i, 0)), pl.BlockSpec((t, MLA_QR), lambda i: (i, 0)),
                  pl.BlockSpec((t, MLA_KV_IN), lambda i: (i, 0)), vq, vkv, tab, tab, tab],
        out_specs=[pl.BlockSpec((t, MLA_A_PAD), lambda i: (i, 0)), vq, vkv],
        compiler_params=_cparams(),
    )(a, dq, dkv, gq, gkv, *tables)


FFN_ROWS = 256
FFN_COLS = 1408
HALO = 8


def _shift_down(cur, halo, s, first):
    rolled = pltpu.roll(cur, s, 0)
    hrolled = jnp.where(first, 0.0, pltpu.roll(halo, s, 0))
    row = lax.broadcasted_iota(jnp.int32, (HALO, cur.shape[1]), 0)
    top = jnp.where(row < s, hrolled, rolled[:HALO])
    return jnp.concatenate([top, rolled[HALO:]], axis=0)


def _shift_up(cur, halo, s, last):
    rows = cur.shape[0]
    rolled = pltpu.roll(cur, rows - s, 0)
    hrolled = jnp.where(last, 0.0, pltpu.roll(halo, HALO - s, 0))
    row = lax.broadcasted_iota(jnp.int32, (HALO, cur.shape[1]), 0)
    bottom = jnp.where(row >= HALO - s, hrolled, rolled[rows - HALO:])
    return jnp.concatenate([rolled[:rows - HALO], bottom], axis=0)


def _conv(u, halo, w_ref, b, first):
    u1 = _shift_down(u, halo, 1, first)
    u2 = _shift_down(u, halo, 2, first)
    y = b + w_ref[0:1, :] * u2
    y = y + w_ref[1:2, :] * u1
    y = y + w_ref[2:3, :] * u
    return y, u1, u2


def _ffn_specs(S, tr, tn):
    per = tr // HALO
    tile = pl.BlockSpec((tr, tn), lambda j, i: (i, j))
    before = pl.BlockSpec((HALO, tn), lambda j, i: (jnp.maximum(i * per - 1, 0), j))
    after = pl.BlockSpec((HALO, tn), lambda j, i: (jnp.minimum((i + 1) * per, S // HALO - 1), j))
    w3 = pl.BlockSpec((3, tn), lambda j, i: (0, j))
    w1 = pl.BlockSpec((1, tn), lambda j, i: (0, j))
    return tile, before, after, w3, w1


def _ffn_act_fwd(ug, uv, wg, wv, bg, bv, name):
    S, F = ug.shape
    tr, tn = _tile(S, FFN_ROWS), _tile(F, FFN_COLS)
    tile, before, _, w3, w1 = _ffn_specs(S, tr, tn)

    def body(ug_ref, uv_ref, hg_ref, hv_ref, wg_ref, wv_ref, bg_ref, bv_ref, g_ref, v_ref, o_ref):
        first = pl.program_id(1) == 0
        g, _, _ = _conv(ug_ref[...], hg_ref[...], wg_ref, bg_ref[...], first)
        v, _, _ = _conv(uv_ref[...], hv_ref[...], wv_ref, bv_ref[...], first)
        g_ref[...] = g
        v_ref[...] = v
        o_ref[...] = (g * jax.nn.sigmoid(g) * v).astype(BF16)

    f32 = jax.ShapeDtypeStruct((S, F), F32)
    return pl.pallas_call(
        body, name=name, out_shape=[f32, f32, jax.ShapeDtypeStruct((S, F), BF16)], grid=(F // tn, S // tr),
        in_specs=[tile, tile, before, before, w3, w3, w1, w1], out_specs=[tile, tile, tile],
        compiler_params=_cparams(),
    )(ug, uv, ug, uv, wg, wv, bg, bv)


def _gate_grads(g, v, da):
    sg = jax.nn.sigmoid(g)
    return da * v * (sg * (1.0 + g * (1.0 - sg))), da * (g * sg)


def _ffn_bwd_mid(ug, uv, g, v, dact, wg, wv, name):
    S, F = ug.shape
    tr, tn = _tile(S, FFN_ROWS), _tile(F, FFN_COLS)
    tile, _, after, w3, w1 = _ffn_specs(S, tr, tn)
    n = S // tr

    def body(ug_ref, uv_ref, g_ref, v_ref, ag_ref, av_ref, da_ref, ada_ref, wg_ref, wv_ref,
             dug_ref, duv_ref, dwg_ref, dwv_ref, dbg_ref, dbv_ref):
        last = pl.program_id(1) == n - 1

        @pl.when(pl.program_id(1) == 0)
        def _():
            for r in (dwg_ref, dwv_ref, dbg_ref, dbv_ref):
                r[...] = jnp.zeros_like(r)

        dyg, dyv = _gate_grads(g_ref[...], v_ref[...], da_ref[...])
        dyg_after, dyv_after = _gate_grads(ag_ref[...], av_ref[...], ada_ref[...])

        def back(dy, dy_after, u_ref, w_ref, du_ref, dw_ref, db_ref):
            u = u_ref[...]
            d1 = _shift_up(dy, dy_after, 1, last)
            d2 = _shift_up(dy, dy_after, 2, last)
            du_ref[...] = (w_ref[2:3, :] * dy + w_ref[1:2, :] * d1 + w_ref[0:1, :] * d2).astype(BF16)
            for k, d in enumerate((d2, d1, dy)):
                dw_ref[k:k + 1, :] += jnp.sum(d * u, axis=0, keepdims=True)
            db_ref[...] += jnp.sum(dy, axis=0, keepdims=True)

        back(dyg, dyg_after, ug_ref, wg_ref, dug_ref, dwg_ref, dbg_ref)
        back(dyv, dyv_after, uv_ref, wv_ref, duv_ref, dwv_ref, dbv_ref)

    big = jax.ShapeDtypeStruct((S, F), BF16)
    s3, s1 = jax.ShapeDtypeStruct((3, F), F32), jax.ShapeDtypeStruct((1, F), F32)
    return pl.pallas_call(
        body, name=name, out_shape=[big, big, s3, s3, s1, s1], grid=(F // tn, n),
        in_specs=[tile, tile, tile, tile, after, after, tile, after, w3, w3],
        out_specs=[tile, tile, w3, w3, w1, w1], compiler_params=_cparams(),
    )(ug, uv, g, v, g, v, dact, dact, wg, wv)


def _dot3(a, b):
    a_hi = a.astype(BF16)
    a_lo = (a - a_hi.astype(F32)).astype(BF16)
    b_hi = b.astype(BF16)
    b_lo = (b - b_hi.astype(F32)).astype(BF16)
    d = lambda p, q: lax.dot_general(p, q, NN, preferred_element_type=F32)
    return d(a_hi, b_hi) + (d(a_hi, b_lo) + d(a_lo, b_hi))


def _ada_mod(c_all, w, b):
    nmod, D, n = w.shape

    def body(c_ref, w_ref, b_ref, o_ref):
        cv = c_ref[...]
        o_ref[0] = _dot3(cv * jax.nn.sigmoid(cv), w_ref[0]) + b_ref[0]

    return pl.pallas_call(
        body, name="ada_mod", out_shape=jax.ShapeDtypeStruct((nmod, 8, n), F32), grid=(nmod,),
        in_specs=[pl.BlockSpec((8, D), lambda m: (0, 0)), pl.BlockSpec((1, D, n), lambda m: (m, 0, 0)),
                  pl.BlockSpec((1, 1, n), lambda m: (m, 0, 0))],
        out_specs=pl.BlockSpec((1, 8, n), lambda m: (m, 0, 0)), compiler_params=_cparams(),
    )(c_all, w, b)


def _ada_grad(c_all_t, dmod):
    D = c_all_t.shape[0]
    nmod, _, n = dmod.shape
    tr = 256

    def body(c_ref, d_ref, dw_ref, db_ref):
        cv = c_ref[...]
        sc = cv * jax.nn.sigmoid(cv)
        dv = d_ref[0]
        acc = sc[:, 0:1] * dv[0:1, :]
        tot = dv[0:1, :]
        for k in range(1, 8):
            acc = acc + sc[:, k:k + 1] * dv[k:k + 1, :]
            tot = tot + dv[k:k + 1, :]
        dw_ref[0] = acc
        db_ref[0] = tot

    return pl.pallas_call(
        body, name="ada_grad",
        out_shape=[jax.ShapeDtypeStruct((nmod, D, n), F32), jax.ShapeDtypeStruct((nmod, 1, n), F32)],
        grid=(nmod, D // tr),
        in_specs=[pl.BlockSpec((tr, 8), lambda m, i: (i, 0)), pl.BlockSpec((1, 8, n), lambda m, i: (m, 0, 0))],
        out_specs=[pl.BlockSpec((1, tr, n), lambda m, i: (m, i, 0)), pl.BlockSpec((1, 1, n), lambda m, i: (m, 0, 0))],
        compiler_params=_cparams(),
    )(c_all_t, dmod)


def _adamw(w, g, m, v, name):
    R, C = w.shape
    t = _row_tile(R, C)

    def body(w_ref, g_ref, m_ref, v_ref, d_ref, nm_ref, nv_ref):
        gv = g_ref[...]
        mn = ADAM_B1 * m_ref[...] + (1.0 - ADAM_B1) * gv
        vn = ADAM_B2 * v_ref[...] + (1.0 - ADAM_B2) * (gv * gv)
        m_hat = mn / (1.0 - ADAM_B1 ** ADAM_STEP)
        v_hat = vn / (1.0 - ADAM_B2 ** ADAM_STEP)
        d_ref[...] = -ADAM_LR * (m_hat / (jnp.sqrt(v_hat) + ADAM_EPS) + ADAM_WD * w_ref[...])
        nm_ref[...] = mn
        nv_ref[...] = vn

    blk = pl.BlockSpec((t, C), lambda i: (i, 0))
    shp = jax.ShapeDtypeStruct((R, C), F32)
    return pl.pallas_call(
        body, name=name, out_shape=[shp, shp, shp], grid=(R // t,), in_specs=[blk] * 4, out_specs=[blk] * 3,
        compiler_params=_cparams(),
    )(w, g, m, v)


def _cat_cols(g4):
    return jnp.concatenate([g4[s] for s in range(4)], axis=-1)


def _cat_rows(g4):
    return jnp.concatenate([g4[s] for s in range(4)], axis=-2)


def _pad_head_cols(w, width):
    K = w.shape[0]
    w = w.reshape(K, HEADS, width)
    return jnp.pad(w, ((0, 0), (0, 0), (0, HEAD_PAD - width))).reshape(K, HEADS * HEAD_PAD)


def _unpad_head_cols(w, width):
    K = w.shape[0]
    return w.reshape(K, HEADS, HEAD_PAD)[:, :, :width].reshape(K, HEADS * width)


def _pad_head_rows(w, width):
    N = w.shape[1]
    w = w.reshape(HEADS, width, N)
    return jnp.pad(w, ((0, 0), (0, HEAD_PAD - width), (0, 0))).reshape(HEADS * HEAD_PAD, N)


def _unpad_head_rows(w, width):
    N = w.shape[1]
    return w.reshape(HEADS, HEAD_PAD, N)[:, :width, :].reshape(HEADS * width, N)


def _prepare_fox_in(fox_in):
    W = {}
    w = _cat_cols(fox_in[:, 0])
    hw = HEADS * FOX_HEAD
    q = w[:, :hw] * FOX_SCALE
    W["fox_qkv"] = jnp.concatenate([q, w[:, hw:3 * hw]], axis=1)
    W["fox_f"] = jnp.pad(w[:, 3 * hw:], ((0, 0), (0, LANE - HEADS)))
    return W


def _prepare_weights(gathered):
    fox_o, mla_a, mla_uq, mla_ukv, mla_o, ffn_in, ffn_out = gathered
    W = {}
    W["fox_o"] = _pad_head_rows(_cat_rows(fox_o[:, 0]), FOX_HEAD)
    w = _cat_rows(mla_a[:, 0])
    lat = MLA_QR + MLA_KVR
    W["mla_a"] = jnp.concatenate(
        [w[:, :lat], jnp.zeros((D_MODEL, MLA_NOPE), BF16), w[:, lat:],
         jnp.zeros((D_MODEL, HEAD_PAD - MLA_NOPE - MLA_ROPE), BF16)], axis=1)
    W["mla_uq"] = _pad_head_cols(_cat_cols(mla_uq[:, 0]), MLA_NOPE + MLA_ROPE)
    w = _cat_cols(mla_ukv[:, 0]).reshape(MLA_KVR, HEADS, MLA_NOPE + MLA_V)
    kn = _pad_head_cols(w[:, :, :MLA_NOPE].reshape(MLA_KVR, -1), MLA_NOPE)
    vv = _pad_head_cols(w[:, :, MLA_NOPE:].reshape(MLA_KVR, -1), MLA_V)
    eye = np.zeros((HEAD_PAD, HEADS, HEAD_PAD), np.float32)
    for j in range(MLA_NOPE, MLA_NOPE + MLA_ROPE):
        eye[j, :, j] = 1.0
    eye = jnp.asarray(eye.reshape(HEAD_PAD, HEADS * HEAD_PAD), BF16)
    bottom = jnp.concatenate([eye, jnp.zeros((HEAD_PAD, HEADS * HEAD_PAD), BF16)], axis=1)
    W["mla_kv"] = jnp.concatenate([jnp.concatenate([kn, vv], axis=1), bottom], axis=0)
    W["mla_o"] = _pad_head_rows(_cat_rows(mla_o[:, 0]), MLA_V)
    w = _cat_cols(ffn_in)
    W["ffn_g"] = [w[i, :, :D_FF] for i in range(2)]
    W["ffn_v"] = [w[i, :, D_FF:] for i in range(2)]
    w = _cat_rows(ffn_out)
    W["ffn_out"] = [w[i] for i in range(2)]
    return W


def _ffn_forward(xin, mod, W, i, conv_w, conv_b):
    shift, scale, gate = mod
    h = _adaln_fwd(xin, scale, shift, name=f"ffn{i}_adaln")
    ug = _mm(h, W["ffn_g"][i], name=f"ffn{i}_up_gate")
    uv = _mm(h, W["ffn_v"][i], name=f"ffn{i}_up_val")
    wg, wv = conv_w[i][:, :D_FF], conv_w[i][:, D_FF:]
    bg, bv = conv_b[i][None, :D_FF], conv_b[i][None, D_FF:]
    g, v, act = _ffn_act_fwd(ug, uv, wg, wv, bg, bv, name=f"ffn{i}_act")
    y, xout = _mm(act, W["ffn_out"][i], gated=(xin, gate), name=f"ffn{i}_down")
    return xout, (xin, h, ug, uv, g, v, act, y, wg, wv)


def _ffn_backward(dx_out, saved, mod, W, i):
    xin, h, ug, uv, g, v, act, y, wg, wv = saved
    shift, scale, gate = mod
    dy, dgate = _gate_bwd(dx_out, y, gate, name=f"ffn{i}_gate_bwd")
    d_out = _mm(act, dy, ta=True, name=f"ffn{i}_dw_out")
    dact = _mm(dy, W["ffn_out"][i], tb=True, name=f"ffn{i}_dact")
    dug, duv, dwg, dwv, dbg, dbv = _ffn_bwd_mid(ug, uv, g, v, dact, wg, wv, name=f"ffn{i}_bwd_mid")
    d_in = jnp.concatenate([_mm(h, dug, ta=True, name=f"ffn{i}_dw_gate"),
                            _mm(h, duv, ta=True, name=f"ffn{i}_dw_val")], axis=1)
    dh = _mm(dug, W["ffn_g"][i], tb=True, name=f"ffn{i}_dh_gate")
    dh = _mm(duv, W["ffn_v"][i], tb=True, res=dh, name=f"ffn{i}_dh_val")
    dx, dscale, dshift = _adaln_bwd(xin, dh, dx_out, scale, name=f"ffn{i}_adaln_bwd")
    grads = dict(ffn_w_in=d_in, ffn_w_out=d_out, ffn_conv_w=jnp.concatenate([dwg, dwv], axis=1),
                 ffn_conv_b=jnp.concatenate([dbg, dbv], axis=1)[0])
    return dx, jnp.concatenate([dshift, dscale, dgate], axis=1)[0], grads


def _local_step(x, target, mods, W, small):
    S = x.shape[0]
    tables = _rope_tables(S)
    wide = HEADS * HEAD_PAD
    bf = jnp.pad(small["fox_b_f"], ((0, 0), (0, LANE - HEADS)))
    gq, gkv = small["mla_g_q"], small["mla_g_kv"]

    shift, scale, gate = mods[0]
    h0 = _adaln_fwd(x, scale, shift, name="fox_adaln")
    qkv = _mm(h0, W["fox_qkv"], out_dtype=BF16, name="fox_qkv")
    fl = _mm(h0, W["fox_f"], name="fox_gate_logits")
    cum = _fox_gate_fwd(fl, bf)
    q_fox, k_fox, v_fox = _fox_expand(qkv, cum, name="fox_expand")
    o0, lse0, gathered = _attn_fwd(q_fox, 0, k_fox, 0, v_fox, 0, name="fox_attn_fwd", gather=W["pending"])
    W = {**W, **_prepare_weights(gathered)}
    y0, x1 = _mm(o0, W["fox_o"], gated=(x, gate), name="fox_out")
    x2, ffn0 = _ffn_forward(x1, mods[1], W, 0, small["ffn_conv_w"], small["ffn_conv_b"])

    shift, scale, gate = mods[2]
    h2 = _adaln_fwd(x2, scale, shift, name="mla_adaln")
    a = _mm(h2, W["mla_a"], name="mla_down")
    cqn, ckv_in = _mla_mid_fwd(a, gq, gkv, tables)
    q_raw = _mm(cqn, W["mla_uq"], name="mla_up_q")
    q_cat = _rope_heads(q_raw, tables, False, MLA_SCALE, name="mla_rope_q")
    kv = _mm(ckv_in, W["mla_kv"], out_dtype=BF16, name="mla_up_kv")
    o1, lse1, _ = _attn_fwd(q_cat, 0, kv, 0, kv, HEADS, name="mla_attn_fwd")
    y2, x3 = _mm(o1, W["mla_o"], gated=(x2, gate), name="mla_out")
    x4, ffn1 = _ffn_forward(x3, mods[3], W, 1, small["ffn_conv_w"], small["ffn_conv_b"])

    loss, dx4, d_final_g = _final_loss(x4, small["final_g"], target)

    dx3, dmod3, g_ffn1 = _ffn_backward(dx4, ffn1, mods[3], W, 1)

    shift, scale, gate = mods[2]
    dy, dgate = _gate_bwd(dx3, y2, gate, name="mla_gate_bwd")
    d_mla_o = _mm(o1, dy, ta=True, name="mla_dw_out")
    do = _mm(dy, W["mla_o"], tb=True, out_dtype=BF16, name="mla_do")
    delta = _attn_delta(do, o1, name="mla_attn_delta")
    dq, dk, dv = _attn_bwd(q_cat, 0, kv, 0, kv, HEADS, do, lse1.reshape(HEADS, 1, S),
                           delta, name="mla_attn_bwd")
    dq_raw = _rope_heads(dq, tables, True, MLA_SCALE, name="mla_rope_q_bwd")
    d_mla_uq = _mm(cqn, dq_raw, ta=True, name="mla_dw_uq")
    dcqn = _mm(dq_raw, W["mla_uq"], tb=True, name="mla_dcq")
    dkv = jnp.concatenate([dk, dv], axis=1)
    d_mla_kv = _mm(ckv_in, dkv, ta=True, name="mla_dw_kv")
    dckv_in = _mm(dkv, W["mla_kv"], tb=True, name="mla_dckv")
    da, dgq, dgkv = _mla_mid_bwd(a, dcqn, dckv_in, gq, gkv, tables)
    d_mla_a = _mm(h2, da, ta=True, name="mla_dw_a")
    dh = _mm(da, W["mla_a"], tb=True, name="mla_dh")
    dx2, dscale, dshift = _adaln_bwd(x2, dh, dx3, scale, name="mla_adaln_bwd")
    dmod2 = jnp.concatenate([dshift, dscale, dgate], axis=1)[0]

    dx1, dmod1, g_ffn0 = _ffn_backward(dx2, ffn0, mods[1], W, 0)

    shift, scale, gate = mods[0]
    dy, dgate = _gate_bwd(dx1, y0, gate, name="fox_gate_bwd")
    d_fox_o = _mm(o0, dy, ta=True, name="fox_dw_out")
    do = _mm(dy, W["fox_o"], tb=True, out_dtype=BF16, name="fox_do")
    delta = _attn_delta(do, o0, name="fox_attn_delta")
    dq, dk, dv, dcq, dck = _attn_bwd(q_fox, 0, k_fox, 0, v_fox, 0, do, lse0.reshape(HEADS, 1, S),
                                     delta, name="fox_attn_bwd", bias_grads=True, dq_dtype=BF16)
    dcum = jnp.pad((dcq[:, 0, :] - dck[:, 0, :]).T, ((0, 0), (0, LANE - HEADS)))
    dfl, dbf = _fox_gate_bwd(dcum, fl, bf)
    dqkv = _fox_compress(dq, dk, dv, name="fox_compress")
    d_fox_qkv = _mm(h0, dqkv, ta=True, name="fox_dw_in")
    d_fox_f = _mm(h0, dfl, ta=True, name="fox_dw_gate")
    dh = _mm(dqkv, W["fox_qkv"], tb=True, name="fox_dh")
    dh = _mm(dfl, W["fox_f"], tb=True, res=dh, name="fox_dh_gate")
    dx0, dscale, dshift = _adaln_bwd(x, dh, dx1, scale, name="fox_adaln_bwd")
    dmod0 = jnp.concatenate([dshift, dscale, dgate], axis=1)[0]

    G = {}
    hw = HEADS * FOX_HEAD
    G["fox_w_in"] = jnp.concatenate([d_fox_qkv[:, :hw] * FOX_SCALE,
                                     d_fox_qkv[:, hw:], d_fox_f[:, :HEADS]], axis=1)[None]
    G["fox_b_f"] = dbf[:, :HEADS]
    G["fox_w_o"] = _unpad_head_rows(d_fox_o, FOX_HEAD)[None]
    lat = MLA_QR + MLA_KVR
    G["mla_w_a"] = jnp.concatenate([d_mla_a[:, :lat], d_mla_a[:, lat + MLA_NOPE:lat + MLA_NOPE + MLA_ROPE]],
                                   axis=1)[None]
    G["mla_g_q"] = dgq
    G["mla_g_kv"] = dgkv
    G["mla_w_uq"] = _unpad_head_cols(d_mla_uq, MLA_NOPE + MLA_ROPE)[None]
    dkn = d_mla_kv[:MLA_KVR, :wide].reshape(MLA_KVR, HEADS, HEAD_PAD)[:, :, :MLA_NOPE]
    dvv = d_mla_kv[:MLA_KVR, wide:].reshape(MLA_KVR, HEADS, HEAD_PAD)[:, :, :MLA_V]
    G["mla_w_ukv"] = jnp.concatenate([dkn, dvv], axis=2).reshape(MLA_KVR, -1)[None]
    G["mla_w_o"] = _unpad_head_rows(d_mla_o, MLA_V)[None]
    for name in ("ffn_w_in", "ffn_w_out", "ffn_conv_w", "ffn_conv_b"):
        G[name] = jnp.stack([g_ffn0[name], g_ffn1[name]])
    G["final_g"] = d_final_g[0]
    dmods = jnp.stack([dmod0, dmod1, dmod2, dmod3])
    return loss, dx0, dmods, G


PACKED = [("fox_w_in", 2), ("fox_w_o", 1), ("mla_w_a", 1), ("mla_w_uq", 2), ("mla_w_ukv", 2), ("mla_w_o", 1),
          ("ffn_w_in", 2), ("ffn_w_out", 1), ("fox_b_f", None), ("mla_g_q", 1), ("mla_g_kv", 1),
          ("ffn_conv_w", 2), ("ffn_conv_b", None), ("final_g", None)]


def _shard_of(g, axis, s):
    if axis is None:
        return g
    n = g.shape[axis] // 4
    return lax.slice_in_dim(g, s * n, (s + 1) * n, axis=axis)


def _pack_plan(G):
    shard_size = lambda name, axis: math.prod(_shard_of(G[name], axis, 0).shape)
    big = [(name, axis) for name, axis in PACKED if shard_size(name, axis) % PACK_ROW == 0]
    small = [(name, axis) for name, axis in PACKED if shard_size(name, axis) % PACK_ROW != 0]
    whole_tiles = lambda size: -(-size // (8 * PACK_ROW)) * 8
    runs = [([item], whole_tiles(shard_size(*item))) for item in big]
    runs.append((small, whole_tiles(sum(shard_size(*item) for item in small))))
    used = sum(rows for _, rows in runs)
    unit = 2 * PACK_ROWS_ALIGN
    return runs, -(-used // unit) * unit


def _pack(G):
    runs, total = _pack_plan(G)
    pieces = []
    for items, rows in runs:
        per_chip = [jnp.concatenate([_shard_of(G[name], axis, s).reshape(-1) for name, axis in items])
                    for s in range(4)]
        flat = jnp.stack(per_chip)
        flat = jnp.pad(flat, ((0, 0), (0, rows * PACK_ROW - flat.shape[1])))
        pieces.append(flat.reshape(4, rows, PACK_ROW))
    used = sum(rows for _, rows in runs)
    if total > used:
        pieces.append(jnp.zeros((4, total - used, PACK_ROW), F32))
    return jnp.concatenate(pieces, axis=1).reshape(4, 2, total // 2, PACK_ROW)


def _unpack(both, G_like):
    runs, total = _pack_plan(G_like)
    table = both.reshape(total, PACK_ROW)
    out, row = {}, 0
    for items, rows in runs:
        flat, off = table[row:row + rows].reshape(-1), 0
        for name, axis in items:
            shape = _shard_of(G_like[name], axis, 0).shape
            out[name] = flat[off:off + math.prod(shape)].reshape(shape)
            off += math.prod(shape)
        row += rows
    return out


WEIGHTS = ['ada_w', 'ada_b', 'fox_w_in', 'fox_b_f', 'fox_w_o', 'mla_w_a', 'mla_g_q', 'mla_g_kv', 'mla_w_uq',
           'mla_w_ukv', 'mla_w_o', 'ffn_w_in', 'ffn_conv_w', 'ffn_conv_b', 'ffn_w_out', 'final_g']
BIG = ['fox_w_in', 'fox_w_o', 'mla_w_a', 'mla_w_uq', 'mla_w_ukv', 'mla_w_o', 'ffn_w_in', 'ffn_w_out']
SMALL = ['ada_b', 'fox_b_f', 'mla_g_q', 'mla_g_kv', 'ffn_conv_w', 'ffn_conv_b', 'final_g']


def _as2d(a):
    return a.reshape(-1, a.shape[-1])


def kernel(x, c, ada_w, ada_b, fox_w_in, fox_b_f, fox_w_o, mla_w_a, mla_g_q, mla_g_kv, mla_w_uq, mla_w_ukv, mla_w_o, ffn_w_in, ffn_conv_w, ffn_conv_b, ffn_w_out, final_g, loss_target, m_ada_w, m_ada_b, m_fox_w_in, m_fox_b_f, m_fox_w_o, m_mla_w_a, m_mla_g_q, m_mla_g_kv, m_mla_w_uq, m_mla_w_ukv, m_mla_w_o, m_ffn_w_in, m_ffn_conv_w, m_ffn_conv_b, m_ffn_w_out, m_final_g, v_ada_w, v_ada_b, v_fox_w_in, v_fox_b_f, v_fox_w_o, v_mla_w_a, v_mla_g_q, v_mla_g_kv, v_mla_w_uq, v_mla_w_ukv, v_mla_w_o, v_ffn_w_in, v_ffn_conv_w, v_ffn_conv_b, v_ffn_w_out, v_final_g):
    w = dict(ada_w=ada_w, ada_b=ada_b, fox_w_in=fox_w_in, fox_b_f=fox_b_f, fox_w_o=fox_w_o, mla_w_a=mla_w_a,
             mla_g_q=mla_g_q, mla_g_kv=mla_g_kv, mla_w_uq=mla_w_uq, mla_w_ukv=mla_w_ukv, mla_w_o=mla_w_o,
             ffn_w_in=ffn_w_in, ffn_conv_w=ffn_conv_w, ffn_conv_b=ffn_conv_b, ffn_w_out=ffn_w_out, final_g=final_g)
    m = dict(ada_w=m_ada_w, ada_b=m_ada_b, fox_w_in=m_fox_w_in, fox_b_f=m_fox_b_f, fox_w_o=m_fox_w_o,
             mla_w_a=m_mla_w_a, mla_g_q=m_mla_g_q, mla_g_kv=m_mla_g_kv, mla_w_uq=m_mla_w_uq,
             mla_w_ukv=m_mla_w_ukv, mla_w_o=m_mla_w_o, ffn_w_in=m_ffn_w_in, ffn_conv_w=m_ffn_conv_w,
             ffn_conv_b=m_ffn_conv_b, ffn_w_out=m_ffn_w_out, final_g=m_final_g)
    v = dict(ada_w=v_ada_w, ada_b=v_ada_b, fox_w_in=v_fox_w_in, fox_b_f=v_fox_b_f, fox_w_o=v_fox_w_o,
             mla_w_a=v_mla_w_a, mla_g_q=v_mla_g_q, mla_g_kv=v_mla_g_kv, mla_w_uq=v_mla_w_uq,
             mla_w_ukv=v_mla_w_ukv, mla_w_o=v_mla_w_o, ffn_w_in=v_ffn_w_in, ffn_conv_w=v_ffn_conv_w,
             ffn_conv_b=v_ffn_conv_b, ffn_w_out=v_ffn_w_out, final_g=v_final_g)
    D = D_MODEL
    xi, yi, ci = lax.axis_index("x"), lax.axis_index("y"), lax.axis_index("c")
    dev = 4 * xi + 2 * yi + ci

    n_ada = ada_w.shape[-1]
    small_parts = [c.reshape(-1), ffn_conv_w.reshape(-1), mla_g_q.reshape(-1), mla_g_kv.reshape(-1)]
    sizes = [p.shape[0] for p in small_parts]
    flat = jnp.concatenate(small_parts)
    flat = jnp.pad(flat, (0, -flat.shape[0] % LANE))[None]
    got = _all_gather8(flat, name="gather_cond")[:, 0]
    offs = np.cumsum([0] + sizes)
    c_all = got[:, offs[0]:offs[1]]
    chips = got[0::2]
    conv_w_full = jnp.concatenate(
        [chips[s, offs[1]:offs[2]].reshape(ffn_conv_w.shape) for s in range(4)], axis=2)
    gq_full = jnp.concatenate([chips[s, offs[2]:offs[3]] for s in range(4)])[None]
    gkv_full = jnp.concatenate([chips[s, offs[3]:offs[4]] for s in range(4)])[None]

    mod_shard = _ada_mod(c_all, ada_w.reshape(4, D, n_ada), ada_b.reshape(4, 1, n_ada))
    mod_all = _all_gather8(mod_shard.reshape(4 * 8, n_ada), name="gather_mod")
    mod_all = mod_all[0::2].reshape(4, 4, 8, n_ada)
    mine = lax.dynamic_index_in_dim(mod_all, dev, axis=2, keepdims=False)
    mod_rows = jnp.transpose(mine, (1, 0, 2)).reshape(4, 4 * n_ada)
    mods = [(mod_rows[k:k + 1, :D], mod_rows[k:k + 1, D:2 * D], mod_rows[k:k + 1, 2 * D:]) for k in range(4)]

    shards = [w[name].astype(BF16) for name in BIG]
    W = _prepare_fox_in(_all_gather_chips(shards[:1], name="gather_weights")[0])
    W["pending"] = shards[1:]
    small = dict(fox_b_f=fox_b_f, mla_g_q=gq_full, mla_g_kv=gkv_full, ffn_conv_w=conv_w_full,
                 ffn_conv_b=ffn_conv_b, final_g=final_g[None])

    loss, dx, dmods, G = _local_step(x[0], loss_target[0], mods, W, small)

    loss_row = jnp.pad(loss[:, :1], ((0, 0), (0, dmods.shape[1] - 1)))
    gathered_rows = _all_gather8(jnp.concatenate([dmods, loss_row]), name="gather_dmod")
    dmod_all = gathered_rows[:, :4]
    loss_total = jnp.sum(gathered_rows[:, 4, 0])
    chip = 2 * xi + yi
    dmod_cols = lax.dynamic_slice_in_dim(dmod_all, chip * n_ada, n_ada, axis=2)
    g_ada_w, g_ada_b = _ada_grad(c_all.T, jnp.transpose(dmod_cols, (1, 0, 2)))
    grads = dict(ada_w=g_ada_w.reshape(ada_w.shape), ada_b=g_ada_b.reshape(ada_b.shape))

    packed = _pack(G)
    recv = _exchange_halves(packed)
    part = _add_halves(packed, recv, ci.reshape(1).astype(jnp.int32))
    from_chips = _exchange_chips(part)
    half = _add_chips(from_chips)
    other = _share_halves(half)
    both = jnp.stack([jnp.where(ci == 0, half, other), jnp.where(ci == 0, other, half)])
    grads.update(_unpack(both, G))

    delta, new_m, new_v = {}, {}, {}
    for name in BIG + ["ada_w"]:
        shape = w[name].shape
        d_, m_, v_ = _adamw(_as2d(w[name]), _as2d(grads[name]), _as2d(m[name]), _as2d(v[name]), name=f"adamw_{name}")
        delta[name], new_m[name], new_v[name] = d_.reshape(shape), m_.reshape(shape), v_.reshape(shape)
        grads[name] = grads[name].reshape(shape)
    sizes = [math.prod(w[name].shape) for name in SMALL]
    total = sum(sizes)
    rows = -(-total // LANE)
    rows += -rows % 8

    def pack_small(d):
        flat = jnp.concatenate([d[name].reshape(-1) for name in SMALL])
        return jnp.pad(flat, (0, rows * LANE - total)).reshape(rows, LANE)

    outs = _adamw(pack_small(w), pack_small(grads), pack_small(m), pack_small(v), name="adamw_small")
    off = 0
    for name, size in zip(SMALL, sizes):
        shape = w[name].shape
        for dst, src in zip((delta, new_m, new_v), outs):
            dst[name] = src.reshape(-1)[off:off + size].reshape(shape)
        grads[name] = grads[name].reshape(shape)
        off += size

    return (loss_total, dx[None], *[grads[n] for n in WEIGHTS], *[delta[n] for n in WEIGHTS],
            *[new_m[n] for n in WEIGHTS], *[new_v[n] for n in WEIGHTS])
```

```python
import functools
import math

import numpy as np
import jax
import jax.numpy as jnp
from jax import lax
from jax.experimental import pallas as pl
from jax.experimental.pallas import tpu as pltpu

F32 = jnp.float32
BF16 = jnp.bfloat16
MESH = pl.DeviceIdType.MESH

D_MODEL = 1024
HEADS = 16
HEAD_PAD = 128
FOX_HEAD = 64
MLA_NOPE = 64
MLA_ROPE = 32
MLA_V = 64
MLA_QR = 384
MLA_KVR = 256
D_FF = 2816
NORM_EPS = 1e-6
ROPE_BASE = 10000.0
ADAM_LR = 0.001
ADAM_B1 = 0.9
ADAM_B2 = 0.999
ADAM_EPS = 1e-08
ADAM_WD = 0.01
ADAM_STEP = 10
FOX_SCALE = FOX_HEAD ** -0.5
MLA_SCALE = (MLA_NOPE + MLA_ROPE) ** -0.5
NEG_BIG = -1e30
VMEM_LIMIT = 56 * 1024 * 1024
LANE = 128
MM_ROWS, MM_COLS, MM_DEPTH = 1408, 1408, 1408
PACK_ROW = 1024
PACK_ROWS_ALIGN = 256

NT = (((1,), (1,)), ((), ()))
TN = (((0,), (0,)), ((), ()))
NN = (((1,), (0,)), ((), ()))


def _cparams():
    return pltpu.CompilerParams(vmem_limit_bytes=VMEM_LIMIT)


def _tile(n, cap):
    if n <= cap:
        return n
    for t in range(cap - cap % LANE, 0, -LANE):
        if n % t == 0:
            return t
    raise ValueError((n, cap))


def _row_tile(rows, cols, limit_bytes=1 << 20):
    best = None
    for t in range(8, rows + 1, 8):
        if rows % t == 0 and t * cols * 4 <= limit_bytes:
            best = t
    return best if best is not None else rows


def _me():
    return lax.axis_index("x"), lax.axis_index("y"), lax.axis_index("c")


def _all_gather8(v, name):
    R, N = v.shape

    def body(v_ref, o_ref, ssem, rsem):
        x, y, c = _me()
        me = 4 * x + 2 * y + c
        o_ref[me] = v_ref[...]
        copies = []
        for k in range(1, 8):
            peer = (1 - x if k & 4 else x, 1 - y if k & 2 else y, 1 - c if k & 1 else c)
            cp = pltpu.make_async_remote_copy(
                src_ref=v_ref, dst_ref=o_ref.at[me], send_sem=ssem.at[k - 1], recv_sem=rsem.at[k - 1],
                device_id=peer, device_id_type=MESH)
            cp.start()
            copies.append(cp)
        for cp in copies:
            cp.wait()

    return pl.pallas_call(
        body, name=name,
        out_shape=jax.ShapeDtypeStruct((8, R, N), v.dtype),
        in_specs=[pl.BlockSpec(memory_space=pltpu.VMEM)],
        out_specs=pl.BlockSpec(memory_space=pltpu.VMEM),
        scratch_shapes=[pltpu.SemaphoreType.DMA((7,)), pltpu.SemaphoreType.DMA((7,))],
    )(v)


def _all_gather_chips(shards, name):
    n = len(shards)

    def body(*refs):
        copies = _chip_gather_copies(refs[:n], refs[n:2 * n], *refs[2 * n:])
        for cp in copies:
            cp.start()
        for cp in copies:
            cp.wait()

    return pl.pallas_call(
        body, name=name,
        out_shape=[jax.ShapeDtypeStruct((4,) + s.shape, s.dtype) for s in shards],
        in_specs=[pl.BlockSpec(memory_space=pl.ANY)] * n,
        out_specs=[pl.BlockSpec(memory_space=pl.ANY)] * n,
        scratch_shapes=[pltpu.SemaphoreType.DMA((3 * n,)), pltpu.SemaphoreType.DMA((3 * n,)),
                        pltpu.SemaphoreType.DMA((n,))],
    )(*shards)


def _exchange_halves(g):
    _, _, R, P = g.shape

    def body(g_ref, o_ref, ssem, rsem):
        x, y, c = _me()
        copies = []
        for s in range(4):
            cp = pltpu.make_async_remote_copy(
                src_ref=g_ref.at[s, 1 - c], dst_ref=o_ref.at[s], send_sem=ssem.at[s], recv_sem=rsem.at[s],
                device_id=(x, y, 1 - c), device_id_type=MESH)
            cp.start()
            copies.append(cp)
        for cp in copies:
            cp.wait()

    return pl.pallas_call(
        body, name="grad_exchange_sibling",
        out_shape=jax.ShapeDtypeStruct((4, R, P), g.dtype),
        in_specs=[pl.BlockSpec(memory_space=pl.ANY)],
        out_specs=pl.BlockSpec(memory_space=pl.ANY),
        scratch_shapes=[pltpu.SemaphoreType.DMA((4,)), pltpu.SemaphoreType.DMA((4,))],
    )(g)


def _exchange_chips(a):
    _, R, P = a.shape

    def body(a_ref, o_ref, ssem, rsem, lsem):
        x, y, c = _me()
        me = 2 * x + y
        own = pltpu.make_async_copy(a_ref.at[me], o_ref.at[me], lsem)
        own.start()
        copies = []
        for k in (1, 2, 3):
            px, py = (1 - x if k & 2 else x), (1 - y if k & 1 else y)
            cp = pltpu.make_async_remote_copy(
                src_ref=a_ref.at[2 * px + py], dst_ref=o_ref.at[me], send_sem=ssem.at[k - 1],
                recv_sem=rsem.at[k - 1], device_id=(px, py, c), device_id_type=MESH)
            cp.start()
            copies.append(cp)
        own.wait()
        for cp in copies:
            cp.wait()

    return pl.pallas_call(
        body, name="grad_exchange_chips",
        out_shape=jax.ShapeDtypeStruct((4, R, P), a.dtype),
        in_specs=[pl.BlockSpec(memory_space=pl.ANY)],
        out_specs=pl.BlockSpec(memory_space=pl.ANY),
        scratch_shapes=[pltpu.SemaphoreType.DMA((3,)), pltpu.SemaphoreType.DMA((3,)), pltpu.SemaphoreType.DMA],
    )(a)


def _share_halves(f):
    R, P = f.shape

    def body(f_ref, o_ref, ssem, rsem):
        x, y, c = _me()
        cp = pltpu.make_async_remote_copy(
            src_ref=f_ref, dst_ref=o_ref, send_sem=ssem, recv_sem=rsem,
            device_id=(x, y, 1 - c), device_id_type=MESH)
        cp.start()
        cp.wait()

    return pl.pallas_call(
        body, name="grad_share_sibling",
        out_shape=jax.ShapeDtypeStruct((R, P), f.dtype),
        in_specs=[pl.BlockSpec(memory_space=pl.ANY)],
        out_specs=pl.BlockSpec(memory_space=pl.ANY),
        scratch_shapes=[pltpu.SemaphoreType.DMA, pltpu.SemaphoreType.DMA],
    )(f)


def _add_halves(g, r, c_idx):
    _, _, R, P = g.shape
    t = PACK_ROWS_ALIGN

    def body(c_ref, g_ref, r_ref, o_ref):
        o_ref[...] = (g_ref[0] + r_ref[...]).astype(BF16)

    return pl.pallas_call(
        body, name="grad_add_halves",
        out_shape=jax.ShapeDtypeStruct((4, R, P), BF16),
        grid_spec=pltpu.PrefetchScalarGridSpec(
            num_scalar_prefetch=1, grid=(4, R // t),
            in_specs=[pl.BlockSpec((1, 1, t, P), lambda s, i, c_ref: (s, c_ref[0], i, 0)),
                      pl.BlockSpec((1, t, P), lambda s, i, c_ref: (s, i, 0))],
            out_specs=pl.BlockSpec((1, t, P), lambda s, i, c_ref: (s, i, 0))),
        compiler_params=_cparams(),
    )(c_idx, g, r)


def _add_chips(b):
    _, R, P = b.shape
    t = PACK_ROWS_ALIGN

    def body(b_ref, o_ref):
        b0, b1, b2, b3 = (b_ref[k].astype(F32) for k in range(4))
        o_ref[...] = ((b0 + b1) + b2) + b3

    return pl.pallas_call(
        body, name="grad_add_chips",
        out_shape=jax.ShapeDtypeStruct((R, P), F32),
        grid=(R // t,),
        in_specs=[pl.BlockSpec((4, t, P), lambda i: (0, i, 0))],
        out_specs=pl.BlockSpec((t, P), lambda i: (i, 0)),
        compiler_params=_cparams(),
    )(b)


def _mm(a, b, *, ta=False, tb=False, out_dtype=F32, res=None, gated=None, name):
    if ta:
        K, M = a.shape
    else:
        M, K = a.shape
    N = b.shape[0] if tb else b.shape[1]
    assert (b.shape[1] if tb else b.shape[0]) == K, (a.shape, b.shape, ta, tb)
    tm, tn, tk = _tile(M, MM_ROWS), _tile(N, MM_COLS), _tile(K, MM_DEPTH)
    nk = K // tk
    has_res = res is not None
    has_gate = gated is not None
    has_norm = has_gate and len(gated) == 4
    assert not has_norm or tn == N, "the norm needs whole rows in a tile"
    dims = (((0 if ta else 1,), (1 if tb else 0,)), ((), ()))

    def body(*refs):
        refs = iter(refs)
        a_ref, b_ref = next(refs), next(refs)
        r_ref = next(refs) if has_res else None
        x_ref, g_ref = (next(refs), next(refs)) if has_gate else (None, None)
        sc_ref, sh_ref = (next(refs), next(refs)) if has_norm else (None, None)
        o_ref = next(refs)
        xo_ref = next(refs) if has_gate else None
        h_ref = next(refs) if has_norm else None
        acc_ref = next(refs) if nk > 1 else None

        def finish(r):
            if has_res:
                r = r + r_ref[...]
            o_ref[...] = r.astype(out_dtype)
            if has_gate:
                xo = x_ref[...] + g_ref[...] * r
                xo_ref[...] = xo
            if has_norm:
                h_ref[...] = (xo * _rstd(xo) * (1.0 + sc_ref[...]) + sh_ref[...]).astype(BF16)

        prod = lax.dot_general(a_ref[...].astype(BF16), b_ref[...].astype(BF16), dims,
                               preferred_element_type=F32)
        if nk == 1:
            finish(prod)
            return
        k = pl.program_id(2)

        @pl.when(k == 0)
        def _():
            acc_ref[...] = prod

        @pl.when(k > 0)
        def _():
            acc_ref[...] += prod

        @pl.when(k == nk - 1)
        def _():
            finish(acc_ref[...])

    a_spec = (pl.BlockSpec((tk, tm), lambda j, i, k: (k, i)) if ta
              else pl.BlockSpec((tm, tk), lambda j, i, k: (i, k)))
    b_spec = (pl.BlockSpec((tn, tk), lambda j, i, k: (j, k)) if tb
              else pl.BlockSpec((tk, tn), lambda j, i, k: (k, j)))
    in_specs = [a_spec, b_spec]
    args = [a, b]
    out_tile = pl.BlockSpec((tm, tn), lambda j, i, k: (i, j))
    if has_res:
        in_specs.append(out_tile)
        args.append(res)
    out_shape, out_specs = jax.ShapeDtypeStruct((M, N), out_dtype), out_tile
    if has_gate:
        vec = pl.BlockSpec((1, tn), lambda j, i, k: (0, j))
        in_specs += [out_tile, vec] + [vec, vec] * has_norm
        args += list(gated)
        out_shape, out_specs = [out_shape, jax.ShapeDtypeStruct((M, N), F32)], [out_tile, out_tile]
        if has_norm:
            out_shape.append(jax.ShapeDtypeStruct((M, N), BF16))
            out_specs.append(out_tile)
    return pl.pallas_call(
        body, name=name, out_shape=out_shape, grid=(N // tn, M // tm, nk), in_specs=in_specs,
        out_specs=out_specs, scratch_shapes=[pltpu.VMEM((tm, tn), F32)] if nk > 1 else [],
        compiler_params=_cparams(),
    )(*args)


def _rstd(xv):
    return lax.rsqrt(jnp.mean(xv * xv, axis=-1, keepdims=True) + NORM_EPS)


def _norm_bwd(n, r, dn):
    return r * (dn - n * jnp.mean(dn * n, axis=-1, keepdims=True))


def _adaln_fwd(x, scale, shift, name):
    S, D = x.shape
    t = _tile(S, 512)

    def body(x_ref, sc_ref, sh_ref, h_ref):
        xv = x_ref[...]
        h_ref[...] = (xv * _rstd(xv) * (1.0 + sc_ref[...]) + sh_ref[...]).astype(BF16)

    row = pl.BlockSpec((t, D), lambda i: (i, 0))
    vec = pl.BlockSpec((1, D), lambda i: (0, 0))
    return pl.pallas_call(
        body, name=name, out_shape=jax.ShapeDtypeStruct((S, D), BF16), grid=(S // t,),
        in_specs=[row, vec, vec], out_specs=row, compiler_params=_cparams(),
    )(x, scale, shift)


def _adaln_bwd(x, dh, dx_next, scale, name):
    S, D = x.shape
    t = _tile(S, 512)

    def body(x_ref, dh_ref, dxn_ref, sc_ref, dx_ref, dsc_ref, dsh_ref):
        @pl.when(pl.program_id(0) == 0)
        def _():
            dsc_ref[...] = jnp.zeros_like(dsc_ref)
            dsh_ref[...] = jnp.zeros_like(dsh_ref)

        xv = x_ref[...]
        r = _rstd(xv)
        n = xv * r
        dh = dh_ref[...]
        dx_ref[...] = dxn_ref[...] + _norm_bwd(n, r, dh * (1.0 + sc_ref[...]))
        dsc_ref[...] += jnp.sum(dh * n, axis=0, keepdims=True)
        dsh_ref[...] += jnp.sum(dh, axis=0, keepdims=True)

    row = pl.BlockSpec((t, D), lambda i: (i, 0))
    vec = pl.BlockSpec((1, D), lambda i: (0, 0))
    return pl.pallas_call(
        body, name=name,
        out_shape=[jax.ShapeDtypeStruct((S, D), F32), jax.ShapeDtypeStruct((1, D), F32),
                   jax.ShapeDtypeStruct((1, D), F32)],
        grid=(S // t,), in_specs=[row, row, row, vec], out_specs=[row, vec, vec],
        compiler_params=_cparams(),
    )(x, dh, dx_next, scale)


def _gate_bwd(dx, y, gate, name):
    S, D = dx.shape
    t = _tile(S, 512)

    def body(dx_ref, y_ref, g_ref, dy_ref, dg_ref):
        @pl.when(pl.program_id(0) == 0)
        def _():
            dg_ref[...] = jnp.zeros_like(dg_ref)

        dxv = dx_ref[...]
        dy_ref[...] = (dxv * g_ref[...]).astype(BF16)
        dg_ref[...] += jnp.sum(dxv * y_ref[...], axis=0, keepdims=True)

    row = pl.BlockSpec((t, D), lambda i: (i, 0))
    vec = pl.BlockSpec((1, D), lambda i: (0, 0))
    return pl.pallas_call(
        body, name=name,
        out_shape=[jax.ShapeDtypeStruct((S, D), BF16), jax.ShapeDtypeStruct((1, D), F32)],
        grid=(S // t,), in_specs=[row, row, vec], out_specs=[row, vec], compiler_params=_cparams(),
    )(dx, y, gate)


def _final_loss(x, g, target):
    S, D = x.shape
    t = _tile(S, 512)

    def body(x_ref, g_ref, t_ref, loss_ref, dx_ref, dg_ref):
        @pl.when(pl.program_id(0) == 0)
        def _():
            loss_ref[...] = jnp.zeros_like(loss_ref)
            dg_ref[...] = jnp.zeros_like(dg_ref)

        xv = x_ref[...]
        gv = g_ref[...]
        r = _rstd(xv)
        n = xv * r
        err = n * gv - t_ref[...]
        part = jnp.sum(jnp.mean(err * err, axis=-1, keepdims=True), axis=0, keepdims=True)
        loss_ref[...] += jnp.broadcast_to(0.5 * part, loss_ref.shape)
        dy = err * (1.0 / D)
        dg_ref[...] += jnp.sum(dy * n, axis=0, keepdims=True)
        dx_ref[...] = _norm_bwd(n, r, dy * gv)

    row = pl.BlockSpec((t, D), lambda i: (i, 0))
    vec = pl.BlockSpec((1, D), lambda i: (0, 0))
    one = pl.BlockSpec((1, LANE), lambda i: (0, 0))
    return pl.pallas_call(
        body, name="final_loss",
        out_shape=[jax.ShapeDtypeStruct((1, LANE), F32), jax.ShapeDtypeStruct((S, D), F32),
                   jax.ShapeDtypeStruct((1, D), F32)],
        grid=(S // t,), in_specs=[row, vec, row], out_specs=[one, row, vec], compiler_params=_cparams(),
    )(x, g, target)


ATTN_TILE = 1024
ATTN_ROWS = 16
ONES_LANE = 127


def _causal_pairs(n, k_major):
    if k_major:
        pairs = [(i, j) for j in range(n) for i in range(j, n)]
    else:
        pairs = [(i, j) for i in range(n) for j in range(i + 1)]
    return (jnp.asarray(np.array([p[0] for p in pairs], np.int32)),
            jnp.asarray(np.array([p[1] for p in pairs], np.int32)))


def _chip_gather_copies(ins, outs, ssem, rsem, lsem):
    x, y, c = _me()
    me = 2 * x + y
    copies = []
    for i, (src, dst) in enumerate(zip(ins, outs)):
        copies.append(pltpu.make_async_copy(src, dst.at[me], lsem.at[i]))
        for k in (1, 2, 3):
            peer = (1 - x if k & 2 else x, 1 - y if k & 1 else y, c)
            copies.append(pltpu.make_async_remote_copy(
                src_ref=src, dst_ref=dst.at[me], send_sem=ssem.at[3 * i + k - 1],
                recv_sem=rsem.at[3 * i + k - 1], device_id=peer, device_id_type=MESH))
    return copies


def _attn_fwd(q_arr, q_off, k_arr, k_off, v_arr, v_off, name, gather=()):
    S = q_arr.shape[0]
    t = _tile(S, 2048)
    n = S // t
    rc = min(ATTN_ROWS, t)
    ii, jj = _causal_pairs(n, k_major=False)
    ng = len(gather)
    last_step = ii.shape[0] - 1

    def body(ii_ref, jj_ref, q_ref, k_ref, v_ref, *rest):
        g_in, (o_ref, lse_ref), g_out = rest[:ng], rest[ng:ng + 2], rest[ng + 2:2 * ng + 2]
        s_s, m_s, a_s, acc_s = rest[2 * ng + 2:2 * ng + 6]
        step = pl.program_id(1)
        i, j = ii_ref[step], jj_ref[step]
        lane = lax.broadcasted_iota(jnp.int32, (t, HEAD_PAD), 1)
        if ng:
            copies = _chip_gather_copies(g_in, g_out, *rest[2 * ng + 6:])

            @pl.when(jnp.logical_and(pl.program_id(0) == 0, step == 0))
            def _():
                for cp in copies:
                    cp.start()

            @pl.when(jnp.logical_and(pl.program_id(0) == HEADS - 1, step == last_step))
            def _():
                for cp in copies:
                    cp.wait()

        @pl.when(j == 0)
        def _():
            m_s[...] = jnp.full_like(m_s, NEG_BIG)
            acc_s[...] = jnp.zeros_like(acc_s)

        def tile(diagonal):
            s_s[...] = lax.dot_general(q_ref[...], k_ref[...], NT, preferred_element_type=F32)
            for r in range(t // rc):
                rows = pl.ds(r * rc, rc)
                sc = s_s[rows, :]
                if diagonal:
                    qpos = r * rc + lax.broadcasted_iota(jnp.int32, (rc, t), 0)
                    kpos = lax.broadcasted_iota(jnp.int32, (rc, t), 1)
                    sc = jnp.where(qpos >= kpos, sc, NEG_BIG)
                m_old = m_s[rows, :]
                m_new = jnp.maximum(m_old, jnp.max(sc, axis=1, keepdims=True))
                s_s[rows, :] = jnp.exp(sc - m_new)
                a_s[rows, :] = jnp.exp(m_old - m_new)
                m_s[rows, :] = m_new
            v_ones = jnp.where(lane == ONES_LANE, 1.0, v_ref[...].astype(F32)).astype(BF16)
            acc_s[...] = a_s[...] * acc_s[...] + lax.dot_general(s_s[...].astype(BF16), v_ones, NN,
                                                                 preferred_element_type=F32)

        @pl.when(j < i)
        def _():
            tile(False)

        @pl.when(j == i)
        def _():
            tile(True)
            acc = acc_s[...]
            l = jnp.sum(jnp.where(lane == ONES_LANE, acc, 0.0), axis=1, keepdims=True)
            o_ref[...] = jnp.where(lane == ONES_LANE, 0.0, acc / l).astype(BF16)
            lse_ref[0] = m_s[...] + jnp.log(l)

    hbm = pl.BlockSpec(memory_space=pl.ANY)
    sems = [pltpu.SemaphoreType.DMA((3 * ng,)), pltpu.SemaphoreType.DMA((3 * ng,)),
            pltpu.SemaphoreType.DMA((ng,))] if ng else []
    grid_spec = pltpu.PrefetchScalarGridSpec(
        num_scalar_prefetch=2, grid=(HEADS, ii.shape[0]),
        in_specs=[pl.BlockSpec((t, HEAD_PAD), lambda h, s, ii, jj: (ii[s], q_off + h)),
                  pl.BlockSpec((t, HEAD_PAD), lambda h, s, ii, jj: (jj[s], k_off + h)),
                  pl.BlockSpec((t, HEAD_PAD), lambda h, s, ii, jj: (jj[s], v_off + h))] + [hbm] * ng,
        out_specs=[pl.BlockSpec((t, HEAD_PAD), lambda h, s, ii, jj: (ii[s], h)),
                   pl.BlockSpec((1, t, 1), lambda h, s, ii, jj: (h, ii[s], 0))] + [hbm] * ng,
        scratch_shapes=[pltpu.VMEM((t, t), F32), pltpu.VMEM((t, 1), F32), pltpu.VMEM((t, 1), F32),
                        pltpu.VMEM((t, HEAD_PAD), F32)] + sems)
    outs = pl.pallas_call(
        body, name=name, grid_spec=grid_spec,
        out_shape=[jax.ShapeDtypeStruct((S, HEADS * HEAD_PAD), BF16), jax.ShapeDtypeStruct((HEADS, S, 1), F32)]
        + [jax.ShapeDtypeStruct((4,) + g.shape, g.dtype) for g in gather],
        compiler_params=_cparams(),
    )(ii, jj, q_arr, k_arr, v_arr, *gather)
    return outs[0], outs[1], list(outs[2:])


def _attn_delta(do, o, name):
    S = do.shape[0]
    t = _tile(S, 512)

    def body(do_ref, o_ref, d_ref):
        lane = lax.broadcasted_iota(jnp.int32, (t, LANE), 1)
        acc = jnp.zeros((t, LANE), F32)
        for h in range(HEADS):
            cols = pl.ds(h * HEAD_PAD, HEAD_PAD)
            d = jnp.sum(do_ref[:, cols].astype(F32) * o_ref[:, cols].astype(F32), axis=1, keepdims=True)
            acc = jnp.where(lane == h, d, acc)
        d_ref[...] = acc

    blk = pl.BlockSpec((t, HEADS * HEAD_PAD), lambda i: (i, 0))
    by_lane = pl.pallas_call(
        body, name=name, out_shape=jax.ShapeDtypeStruct((S, LANE), F32), grid=(S // t,),
        in_specs=[blk, blk], out_specs=pl.BlockSpec((t, LANE), lambda i: (i, 0)),
        compiler_params=_cparams(),
    )(do, o)
    return by_lane[:, :HEADS].T.reshape(HEADS, 1, S)


def _lane_as_row(x, lane):
    pick = (lax.broadcasted_iota(jnp.int32, (8, HEAD_PAD), 1) == lane).astype(BF16)
    acc = None
    for part in _split3(x):
        d = lax.dot_general(pick, part, NT, preferred_element_type=F32)
        acc = d if acc is None else acc + d
    return jnp.max(acc, axis=0, keepdims=True)


def _attn_bwd(q_arr, q_off, k_arr, k_off, v_arr, v_off, do, lse, delta, name, bias_grads=False, dq_dtype=F32):
    S = q_arr.shape[0]
    t = _tile(S, ATTN_TILE)
    n = S // t
    rc = min(ATTN_ROWS, t)
    ii, jj = _causal_pairs(n, k_major=True)
    last_step = ii.shape[0] - 1

    def body(ii_ref, jj_ref, q_ref, k_ref, v_ref, do_ref, lse_ref, dl_ref, dq_ref, dk_ref, dv_ref, *rest):
        if bias_grads:
            dcq_ref, dck_ref = rest[:2]
        s_s, dp_s, dk_s, dv_s, dq_s = rest[-5:]
        step = pl.program_id(1)
        i, j = ii_ref[step], jj_ref[step]

        @pl.when(step == 0)
        def _():
            dq_s[...] = jnp.zeros_like(dq_s)

        @pl.when(i == j)
        def _():
            dk_s[...] = jnp.zeros_like(dk_s)
            dv_s[...] = jnp.zeros_like(dv_s)

        def tile(diagonal):
            q, k, dov = q_ref[...], k_ref[...], do_ref[...]
            s_s[...] = lax.dot_general(k, q, NT, preferred_element_type=F32)
            dp_s[...] = lax.dot_general(v_ref[...], dov, NT, preferred_element_type=F32)
            lse_row, dl_row = lse_ref[0], dl_ref[0]
            for r in range(t // rc):
                rows = pl.ds(r * rc, rc)
                sc = s_s[rows, :]
                if diagonal:
                    kpos = r * rc + lax.broadcasted_iota(jnp.int32, (rc, t), 0)
                    qpos = lax.broadcasted_iota(jnp.int32, (rc, t), 1)
                    sc = jnp.where(qpos >= kpos, sc, NEG_BIG)
                p = jnp.exp(sc - lse_row)
                s_s[rows, :] = p
                dp_s[rows, :] = p * (dp_s[rows, :] - dl_row)
            ds = dp_s[...].astype(BF16)
            dv_s[...] += lax.dot_general(s_s[...].astype(BF16), dov, NN, preferred_element_type=F32)
            dk_s[...] += lax.dot_general(ds, q, NN, preferred_element_type=F32)
            rows = pl.ds(pl.multiple_of(i * t, t), t)
            dq_s[rows, :] += lax.dot_general(ds, k, TN, preferred_element_type=F32)

        @pl.when(i > j)
        def _():
            tile(False)

        @pl.when(i == j)
        def _():
            tile(True)

        @pl.when(i == n - 1)
        def _():
            dk = dk_s[...]
            dk_ref[...] = dk.astype(BF16)
            dv_ref[...] = dv_s[...].astype(BF16)
            if bias_grads:
                dck_ref[0] = _lane_as_row(dk, BIAS_LANE)

        @pl.when(step == last_step)
        def _():
            dq = dq_s[...]
            dq_ref[...] = dq.astype(dq_dtype)
            if bias_grads:
                dcq_ref[0] = _lane_as_row(dq, BIAS_LANE + 3)

    qblk = lambda off: pl.BlockSpec((t, HEAD_PAD), lambda h, s, ii, jj: (ii[s], off + h))
    kblk = lambda off: pl.BlockSpec((t, HEAD_PAD), lambda h, s, ii, jj: (jj[s], off + h))
    stat = pl.BlockSpec((1, 1, t), lambda h, s, ii, jj: (h, 0, ii[s]))
    wide = (S, HEADS * HEAD_PAD)
    out_specs = [pl.BlockSpec((S, HEAD_PAD), lambda h, s, ii, jj: (0, h)), kblk(0), kblk(0)]
    out_shape = [jax.ShapeDtypeStruct(wide, dq_dtype), jax.ShapeDtypeStruct(wide, BF16),
                 jax.ShapeDtypeStruct(wide, BF16)]
    if bias_grads:
        out_specs += [pl.BlockSpec((1, 1, S), lambda h, s, ii, jj: (h, 0, 0)),
                      pl.BlockSpec((1, 1, t), lambda h, s, ii, jj: (h, 0, jj[s]))]
        out_shape += [jax.ShapeDtypeStruct((HEADS, 1, S), F32)] * 2
    grid_spec = pltpu.PrefetchScalarGridSpec(
        num_scalar_prefetch=2, grid=(HEADS, ii.shape[0]),
        in_specs=[qblk(q_off), kblk(k_off), kblk(v_off), qblk(0), stat, stat],
        out_specs=out_specs,
        scratch_shapes=[pltpu.VMEM((t, t), F32), pltpu.VMEM((t, t), F32), pltpu.VMEM((t, HEAD_PAD), F32),
                        pltpu.VMEM((t, HEAD_PAD), F32), pltpu.VMEM((S, HEAD_PAD), F32)])
    return pl.pallas_call(
        body, name=name, grid_spec=grid_spec, out_shape=out_shape, compiler_params=_cparams(),
    )(ii, jj, q_arr, k_arr, v_arr, do, lse, delta)


BIAS_LANE = FOX_HEAD


def _fox_expand(qkv, cum, name):
    S = qkv.shape[0]
    t = _tile(S, 256)
    narrow, wide = HEADS * FOX_HEAD, HEADS * HEAD_PAD

    def body(q_ref, k_ref, v_ref, c_ref, qo_ref, ko_ref, vo_ref):
        lane = lax.broadcasted_iota(jnp.int32, (t, HEAD_PAD), 1)
        cum_all = c_ref[...]

        def head_tile(ref, h):
            x = ref[:, pl.ds((h // 2) * HEAD_PAD, HEAD_PAD)].astype(F32)
            if h % 2:
                x = pltpu.roll(x, FOX_HEAD, 1)
            return jnp.where(lane < FOX_HEAD, x, 0.0)

        def fill(out, first, second):
            for n_, (a, b) in enumerate(zip(first, second)):
                out = jnp.where(lane == BIAS_LANE + n_, a, out)
                out = jnp.where(lane == BIAS_LANE + 3 + n_, b, out)
            return out.astype(BF16)

        ones = [1.0, 1.0, 1.0]
        for h in range(HEADS):
            cols = pl.ds(h * HEAD_PAD, HEAD_PAD)
            c = jnp.sum(jnp.where(lane == h, cum_all, 0.0), axis=1, keepdims=True)
            pieces = [x.astype(F32) for x in _split3(c)]
            qo_ref[:, cols] = fill(head_tile(q_ref, h), ones, pieces)
            ko_ref[:, cols] = fill(head_tile(k_ref, h), [-x for x in pieces], ones)
            vo_ref[:, cols] = head_tile(v_ref, h).astype(BF16)

    part = lambda p: pl.BlockSpec((t, narrow), lambda i: (i, p))
    out = pl.BlockSpec((t, wide), lambda i: (i, 0))
    shp = jax.ShapeDtypeStruct((S, wide), BF16)
    return pl.pallas_call(
        body, name=name, out_shape=[shp, shp, shp], grid=(S // t,),
        in_specs=[part(0), part(1), part(2), pl.BlockSpec((t, LANE), lambda i: (i, 0))],
        out_specs=[out, out, out], compiler_params=_cparams(),
    )(qkv, qkv, qkv, cum)


def _fox_compress(dq, dk, dv, name):
    S = dq.shape[0]
    t = _tile(S, 256)
    narrow, wide = HEADS * FOX_HEAD, HEADS * HEAD_PAD

    def body(dq_ref, dk_ref, dv_ref, o_ref):
        lane = lax.broadcasted_iota(jnp.int32, (t, HEAD_PAD), 1)
        for p, ref in enumerate((dq_ref, dk_ref, dv_ref)):
            for g in range(HEADS // 2):
                even = ref[:, pl.ds(2 * g * HEAD_PAD, HEAD_PAD)].astype(F32)
                odd = ref[:, pl.ds((2 * g + 1) * HEAD_PAD, HEAD_PAD)].astype(F32)
                both = jnp.where(lane < FOX_HEAD, even, pltpu.roll(odd, FOX_HEAD, 1))
                o_ref[:, pl.ds(p * narrow + g * HEAD_PAD, HEAD_PAD)] = both.astype(BF16)

    blk = pl.BlockSpec((t, wide), lambda i: (i, 0))
    return pl.pallas_call(
        body, name=name, out_shape=jax.ShapeDtypeStruct((S, 3 * narrow), BF16), grid=(S // t,),
        in_specs=[blk, blk, blk], out_specs=pl.BlockSpec((t, 3 * narrow), lambda i: (i, 0)),
        compiler_params=_cparams(),
    )(dq, dk, dv)


def _split3(x):
    hi = x.astype(BF16)
    r1 = x - hi.astype(F32)
    mid = r1.astype(BF16)
    lo = (r1 - mid.astype(F32)).astype(BF16)
    return hi, mid, lo


def _tri_matmul(tri, x):
    acc = None
    for part in _split3(x):
        d = lax.dot_general(tri, part, NN, preferred_element_type=F32)
        acc = d if acc is None else acc + d
    return acc


def _log_sigmoid(z):
    return jnp.minimum(z, 0.0) - jnp.log1p(jnp.exp(-jnp.abs(z)))


def _fox_gate_fwd(fl, bf):
    S = fl.shape[0]
    t = _tile(S, 256)

    def body(fl_ref, bf_ref, cum_ref, carry):
        @pl.when(pl.program_id(0) == 0)
        def _():
            carry[...] = jnp.zeros_like(carry)

        logf = _log_sigmoid(fl_ref[...] + bf_ref[...])
        tri = (lax.broadcasted_iota(jnp.int32, (t, t), 0) >= lax.broadcasted_iota(jnp.int32, (t, t), 1)).astype(BF16)
        cum_ref[...] = _tri_matmul(tri, logf) + carry[...]
        carry[...] += jnp.sum(logf, axis=0, keepdims=True)

    row = pl.BlockSpec((t, LANE), lambda i: (i, 0))
    return pl.pallas_call(
        body, name="fox_gate_fwd", out_shape=jax.ShapeDtypeStruct((S, LANE), F32), grid=(S // t,),
        in_specs=[row, pl.BlockSpec((1, LANE), lambda i: (0, 0))], out_specs=row,
        scratch_shapes=[pltpu.VMEM((1, LANE), F32)], compiler_params=_cparams(),
    )(fl, bf)


def _fox_gate_bwd(dcum, fl, bf):
    S = fl.shape[0]
    t = _tile(S, 256)
    n = S // t

    def body(dc_ref, fl_ref, bf_ref, df_ref, db_ref, carry):
        @pl.when(pl.program_id(0) == 0)
        def _():
            carry[...] = jnp.zeros_like(carry)
            db_ref[...] = jnp.zeros_like(db_ref)

        dc = dc_ref[...]
        tri = (lax.broadcasted_iota(jnp.int32, (t, t), 1) >= lax.broadcasted_iota(jnp.int32, (t, t), 0)).astype(BF16)
        dlogf = _tri_matmul(tri, dc) + carry[...]
        carry[...] += jnp.sum(dc, axis=0, keepdims=True)
        z = fl_ref[...] + bf_ref[...]
        e = jnp.exp(-jnp.abs(z))
        sig_neg = jnp.where(z >= 0, e, 1.0) / (1.0 + e)
        lanes = lax.broadcasted_iota(jnp.int32, (t, LANE), 1)
        df = jnp.where(lanes < HEADS, dlogf * sig_neg, 0.0)
        df_ref[...] = df
        db_ref[...] += jnp.sum(df, axis=0, keepdims=True)

    row = pl.BlockSpec((t, LANE), lambda i: (n - 1 - i, 0))
    vec = pl.BlockSpec((1, LANE), lambda i: (0, 0))
    return pl.pallas_call(
        body, name="fox_gate_bwd",
        out_shape=[jax.ShapeDtypeStruct((S, LANE), F32), jax.ShapeDtypeStruct((1, LANE), F32)],
        grid=(n,), in_specs=[row, row, vec], out_specs=[row, vec],
        scratch_shapes=[pltpu.VMEM((1, LANE), F32)], compiler_params=_cparams(),
    )(dcum, fl, bf)


def _rope_tables(S):
    pos = jnp.arange(S, dtype=F32)
    inv_freq = ROPE_BASE ** (-jnp.arange(0, MLA_ROPE, 2, dtype=F32) / MLA_ROPE)
    ang = pos[:, None] * inv_freq[None, :]
    cos, sin = jnp.cos(ang), jnp.sin(ang)
    half = MLA_ROPE // 2
    ones = jnp.ones((S, MLA_NOPE), F32)
    z = lambda w: jnp.zeros((S, w), F32)
    keep = jnp.concatenate([ones, cos, cos, z(HEAD_PAD - MLA_NOPE - MLA_ROPE)], axis=1)
    from_above = jnp.concatenate([z(MLA_NOPE), -sin, z(HEAD_PAD - MLA_NOPE - half)], axis=1)
    from_below = jnp.concatenate([z(MLA_NOPE + half), sin, z(HEAD_PAD - MLA_NOPE - MLA_ROPE)], axis=1)
    return keep, from_above, from_below


def _rope(x, keep, up, down, transpose):
    half = MLA_ROPE // 2
    if transpose:
        return x * keep + pltpu.roll(x * up, half, 1) + pltpu.roll(x * down, HEAD_PAD - half, 1)
    return x * keep + pltpu.roll(x, HEAD_PAD - half, 1) * up + pltpu.roll(x, half, 1) * down


def _rope_heads(x, tables, transpose, scale, name):
    S = x.shape[0]
    t = _tile(S, 256)

    def body(x_ref, a_ref, b_ref, c_ref, o_ref):
        keep, up, down = a_ref[...], b_ref[...], c_ref[...]
        for h in range(HEADS):
            cols = pl.ds(h * HEAD_PAD, HEAD_PAD)
            o_ref[:, cols] = (_rope(x_ref[:, cols], keep, up, down, transpose) * scale).astype(BF16)

    blk = pl.BlockSpec((t, HEADS * HEAD_PAD), lambda i: (i, 0))
    tab = pl.BlockSpec((t, HEAD_PAD), lambda i: (i, 0))
    return pl.pallas_call(
        body, name=name, out_shape=jax.ShapeDtypeStruct(x.shape, BF16), grid=(S // t,),
        in_specs=[blk, tab, tab, tab], out_specs=blk, compiler_params=_cparams(),
    )(x, *tables)


MLA_A_PAD = MLA_QR + MLA_KVR + HEAD_PAD
MLA_KV_IN = MLA_KVR + HEAD_PAD


def _mla_mid_fwd(a, gq, gkv, tables):
    S = a.shape[0]
    t = _tile(S, 512)

    def body(a_ref, gq_ref, gkv_ref, ta_ref, tb_ref, tc_ref, q_ref, kv_ref):
        cq = a_ref[:, :MLA_QR]
        ckv = a_ref[:, MLA_QR:MLA_QR + MLA_KVR]
        kr = a_ref[:, MLA_QR + MLA_KVR:]
        q_ref[...] = (cq * _rstd(cq) * gq_ref[...]).astype(BF16)
        kv_ref[:, :MLA_KVR] = (ckv * _rstd(ckv) * gkv_ref[...]).astype(BF16)
        kv_ref[:, MLA_KVR:] = _rope(kr, ta_ref[...], tb_ref[...], tc_ref[...], False).astype(BF16)

    tab = pl.BlockSpec((t, HEAD_PAD), lambda i: (i, 0))
    return pl.pallas_call(
        body, name="mla_mid_fwd",
        out_shape=[jax.ShapeDtypeStruct((S, MLA_QR), BF16), jax.ShapeDtypeStruct((S, MLA_KV_IN), BF16)],
        grid=(S // t,),
        in_specs=[pl.BlockSpec((t, MLA_A_PAD), lambda i: (i, 0)), pl.BlockSpec((1, MLA_QR), lambda i: (0, 0)),
                  pl.BlockSpec((1, MLA_KVR), lambda i: (0, 0)), tab, tab, tab],
        out_specs=[pl.BlockSpec((t, MLA_QR), lambda i: (i, 0)), pl.BlockSpec((t, MLA_KV_IN), lambda i: (i, 0))],
        compiler_params=_cparams(),
    )(a, gq, gkv, *tables)


def _mla_mid_bwd(a, dq, dkv, gq, gkv, tables):
    S = a.shape[0]
    t = _tile(S, 512)

    def body(a_ref, dq_ref, dkv_ref, gq_ref, gkv_ref, ta_ref, tb_ref, tc_ref, da_ref, dgq_ref, dgkv_ref):
        @pl.when(pl.program_id(0) == 0)
        def _():
            dgq_ref[...] = jnp.zeros_like(dgq_ref)
            dgkv_ref[...] = jnp.zeros_like(dgkv_ref)

        def one(c, dout, g):
            r = _rstd(c)
            n = c * r
            return _norm_bwd(n, r, dout * g), jnp.sum(dout * n, axis=0, keepdims=True)

        dcq, dgq = one(a_ref[:, :MLA_QR], dq_ref[...], gq_ref[...])
        dckv, dgkv = one(a_ref[:, MLA_QR:MLA_QR + MLA_KVR], dkv_ref[:, :MLA_KVR], gkv_ref[...])
        da_ref[:, :MLA_QR] = dcq.astype(BF16)
        da_ref[:, MLA_QR:MLA_QR + MLA_KVR] = dckv.astype(BF16)
        da_ref[:, MLA_QR + MLA_KVR:] = _rope(dkv_ref[:, MLA_KVR:], ta_ref[...], tb_ref[...], tc_ref[...],
                                             True).astype(BF16)
        dgq_ref[...] += dgq
        dgkv_ref[...] += dgkv

    tab = pl.BlockSpec((t, HEAD_PAD), lambda i: (i, 0))
    vq = pl.BlockSpec((1, MLA_QR), lambda i: (0, 0))
    vkv = pl.BlockSpec((1, MLA_KVR), lambda i: (0, 0))
    return pl.pallas_call(
        body, name="mla_mid_bwd",
        out_shape=[jax.ShapeDtypeStruct((S, MLA_A_PAD), BF16), jax.ShapeDtypeStruct((1, MLA_QR), F32),
                   jax.ShapeDtypeStruct((1, MLA_KVR), F32)],
        grid=(S // t,),
        in_specs=[pl.BlockSpec((t, MLA_A_PAD), lambda i: (i, 0)), pl.BlockSpec((t, MLA_QR), lambda i: (i, 0)),
                  pl.BlockSpec((t, MLA_KV_IN), lambda i: (i, 0)), vq, vkv, tab, tab, tab],
        out_specs=[pl.BlockSpec((t, MLA_A_PAD), lambda i: (i, 0)), vq, vkv],
        compiler_params=_cparams(),
    )(a, dq, dkv, gq, gkv, *tables)


FFN_ROWS = 256
FFN_COLS = 1408
HALO = 8


def _shift_down(cur, halo, s, first):
    rolled = pltpu.roll(cur, s, 0)
    hrolled = jnp.where(first, 0.0, pltpu.roll(halo, s, 0))
    row = lax.broadcasted_iota(jnp.int32, (HALO, cur.shape[1]), 0)
    top = jnp.where(row < s, hrolled, rolled[:HALO])
    return jnp.concatenate([top, rolled[HALO:]], axis=0)


def _shift_up(cur, halo, s, last):
    rows = cur.shape[0]
    rolled = pltpu.roll(cur, rows - s, 0)
    hrolled = jnp.where(last, 0.0, pltpu.roll(halo, HALO - s, 0))
    row = lax.broadcasted_iota(jnp.int32, (HALO, cur.shape[1]), 0)
    bottom = jnp.where(row >= HALO - s, hrolled, rolled[rows - HALO:])
    return jnp.concatenate([rolled[:rows - HALO], bottom], axis=0)


def _conv(u, halo, w_ref, b, first):
    u1 = _shift_down(u, halo, 1, first)
    u2 = _shift_down(u, halo, 2, first)
    y = b + w_ref[0:1, :] * u2
    y = y + w_ref[1:2, :] * u1
    y = y + w_ref[2:3, :] * u
    return y, u1, u2


def _ffn_specs(S, tr, tn):
    per = tr // HALO
    tile = pl.BlockSpec((tr, tn), lambda j, i: (i, j))
    before = pl.BlockSpec((HALO, tn), lambda j, i: (jnp.maximum(i * per - 1, 0), j))
    after = pl.BlockSpec((HALO, tn), lambda j, i: (jnp.minimum((i + 1) * per, S // HALO - 1), j))
    w3 = pl.BlockSpec((3, tn), lambda j, i: (0, j))
    w1 = pl.BlockSpec((1, tn), lambda j, i: (0, j))
    return tile, before, after, w3, w1


def _ffn_act_fwd(ug, uv, wg, wv, bg, bv, name):
    S, F = ug.shape
    tr, tn = _tile(S, FFN_ROWS), _tile(F, FFN_COLS)
    tile, before, _, w3, w1 = _ffn_specs(S, tr, tn)

    def body(ug_ref, uv_ref, hg_ref, hv_ref, wg_ref, wv_ref, bg_ref, bv_ref, g_ref, v_ref, o_ref):
        first = pl.program_id(1) == 0
        g, _, _ = _conv(ug_ref[...], hg_ref[...], wg_ref, bg_ref[...], first)
        v, _, _ = _conv(uv_ref[...], hv_ref[...], wv_ref, bv_ref[...], first)
        g_ref[...] = g
        v_ref[...] = v
        o_ref[...] = (g * jax.nn.sigmoid(g) * v).astype(BF16)

    f32 = jax.ShapeDtypeStruct((S, F), F32)
    return pl.pallas_call(
        body, name=name, out_shape=[f32, f32, jax.ShapeDtypeStruct((S, F), BF16)], grid=(F // tn, S // tr),
        in_specs=[tile, tile, before, before, w3, w3, w1, w1], out_specs=[tile, tile, tile],
        compiler_params=_cparams(),
    )(ug, uv, ug, uv, wg, wv, bg, bv)


def _gate_grads(g, v, da):
    sg = jax.nn.sigmoid(g)
    return da * v * (sg * (1.0 + g * (1.0 - sg))), da * (g * sg)


def _ffn_bwd_mid(ug, uv, g, v, dact, wg, wv, name):
    S, F = ug.shape
    tr, tn = _tile(S, FFN_ROWS), _tile(F, FFN_COLS)
    tile, _, after, w3, w1 = _ffn_specs(S, tr, tn)
    n = S // tr

    def body(ug_ref, uv_ref, g_ref, v_ref, ag_ref, av_ref, da_ref, ada_ref, wg_ref, wv_ref,
             dug_ref, duv_ref, dwg_ref, dwv_ref, dbg_ref, dbv_ref):
        last = pl.program_id(1) == n - 1

        @pl.when(pl.program_id(1) == 0)
        def _():
            for r in (dwg_ref, dwv_ref, dbg_ref, dbv_ref):
                r[...] = jnp.zeros_like(r)

        dyg, dyv = _gate_grads(g_ref[...], v_ref[...], da_ref[...])
        dyg_after, dyv_after = _gate_grads(ag_ref[...], av_ref[...], ada_ref[...])

        def back(dy, dy_after, u_ref, w_ref, du_ref, dw_ref, db_ref):
            u = u_ref[...]
            d1 = _shift_up(dy, dy_after, 1, last)
            d2 = _shift_up(dy, dy_after, 2, last)
            du_ref[...] = (w_ref[2:3, :] * dy + w_ref[1:2, :] * d1 + w_ref[0:1, :] * d2).astype(BF16)
            for k, d in enumerate((d2, d1, dy)):
                dw_ref[k:k + 1, :] += jnp.sum(d * u, axis=0, keepdims=True)
            db_ref[...] += jnp.sum(dy, axis=0, keepdims=True)

        back(dyg, dyg_after, ug_ref, wg_ref, dug_ref, dwg_ref, dbg_ref)
        back(dyv, dyv_after, uv_ref, wv_ref, duv_ref, dwv_ref, dbv_ref)

    big = jax.ShapeDtypeStruct((S, F), BF16)
    s3, s1 = jax.ShapeDtypeStruct((3, F), F32), jax.ShapeDtypeStruct((1, F), F32)
    return pl.pallas_call(
        body, name=name, out_shape=[big, big, s3, s3, s1, s1], grid=(F // tn, n),
        in_specs=[tile, tile, tile, tile, after, after, tile, after, w3, w3],
        out_specs=[tile, tile, w3, w3, w1, w1], compiler_params=_cparams(),
    )(ug, uv, g, v, g, v, dact, dact, wg, wv)


def _dot3(a, b):
    a_hi = a.astype(BF16)
    a_lo = (a - a_hi.astype(F32)).astype(BF16)
    b_hi = b.astype(BF16)
    b_lo = (b - b_hi.astype(F32)).astype(BF16)
    d = lambda p, q: lax.dot_general(p, q, NN, preferred_element_type=F32)
    return d(a_hi, b_hi) + (d(a_hi, b_lo) + d(a_lo, b_hi))


def _ada_mod(c_all, w, b):
    nmod, D, n = w.shape

    def body(c_ref, w_ref, b_ref, o_ref):
        cv = c_ref[...]
        o_ref[0] = _dot3(cv * jax.nn.sigmoid(cv), w_ref[0]) + b_ref[0]

    return pl.pallas_call(
        body, name="ada_mod", out_shape=jax.ShapeDtypeStruct((nmod, 8, n), F32), grid=(nmod,),
        in_specs=[pl.BlockSpec((8, D), lambda m: (0, 0)), pl.BlockSpec((1, D, n), lambda m: (m, 0, 0)),
                  pl.BlockSpec((1, 1, n), lambda m: (m, 0, 0))],
        out_specs=pl.BlockSpec((1, 8, n), lambda m: (m, 0, 0)), compiler_params=_cparams(),
    )(c_all, w, b)


def _ada_grad(c_all_t, dmod):
    D = c_all_t.shape[0]
    nmod, _, n = dmod.shape
    tr = 256

    def body(c_ref, d_ref, dw_ref, db_ref):
        cv = c_ref[...]
        sc = cv * jax.nn.sigmoid(cv)
        dv = d_ref[0]
        acc = sc[:, 0:1] * dv[0:1, :]
        tot = dv[0:1, :]
        for k in range(1, 8):
            acc = acc + sc[:, k:k + 1] * dv[k:k + 1, :]
            tot = tot + dv[k:k + 1, :]
        dw_ref[0] = acc
        db_ref[0] = tot

    return pl.pallas_call(
        body, name="ada_grad",
        out_shape=[jax.ShapeDtypeStruct((nmod, D, n), F32), jax.ShapeDtypeStruct((nmod, 1, n), F32)],
        grid=(nmod, D // tr),
        in_specs=[pl.BlockSpec((tr, 8), lambda m, i: (i, 0)), pl.BlockSpec((1, 8, n), lambda m, i: (m, 0, 0))],
        out_specs=[pl.BlockSpec((1, tr, n), lambda m, i: (m, i, 0)), pl.BlockSpec((1, 1, n), lambda m, i: (m, 0, 0))],
        compiler_params=_cparams(),
    )(c_all_t, dmod)


def _adamw(w, g, m, v, name):
    R, C = w.shape
    t = _row_tile(R, C)

    def body(w_ref, g_ref, m_ref, v_ref, d_ref, nm_ref, nv_ref):
        gv = g_ref[...]
        mn = ADAM_B1 * m_ref[...] + (1.0 - ADAM_B1) * gv
        vn = ADAM_B2 * v_ref[...] + (1.0 - ADAM_B2) * (gv * gv)
        m_hat = mn / (1.0 - ADAM_B1 ** ADAM_STEP)
        v_hat = vn / (1.0 - ADAM_B2 ** ADAM_STEP)
        d_ref[...] = -ADAM_LR * (m_hat / (jnp.sqrt(v_hat) + ADAM_EPS) + ADAM_WD * w_ref[...])
        nm_ref[...] = mn
        nv_ref[...] = vn

    blk = pl.BlockSpec((t, C), lambda i: (i, 0))
    shp = jax.ShapeDtypeStruct((R, C), F32)
    return pl.pallas_call(
        body, name=name, out_shape=[shp, shp, shp], grid=(R // t,), in_specs=[blk] * 4, out_specs=[blk] * 3,
        compiler_params=_cparams(),
    )(w, g, m, v)


def _cat_cols(g4):
    return jnp.concatenate([g4[s] for s in range(4)], axis=-1)


def _cat_rows(g4):
    return jnp.concatenate([g4[s] for s in range(4)], axis=-2)


def _pad_head_cols(w, width):
    K = w.shape[0]
    w = w.reshape(K, HEADS, width)
    return jnp.pad(w, ((0, 0), (0, 0), (0, HEAD_PAD - width))).reshape(K, HEADS * HEAD_PAD)


def _unpad_head_cols(w, width):
    K = w.shape[0]
    return w.reshape(K, HEADS, HEAD_PAD)[:, :, :width].reshape(K, HEADS * width)


def _pad_head_rows(w, width):
    N = w.shape[1]
    w = w.reshape(HEADS, width, N)
    return jnp.pad(w, ((0, 0), (0, HEAD_PAD - width), (0, 0))).reshape(HEADS * HEAD_PAD, N)


def _unpad_head_rows(w, width):
    N = w.shape[1]
    return w.reshape(HEADS, HEAD_PAD, N)[:, :width, :].reshape(HEADS * width, N)


def _prepare_fox_in(fox_in):
    W = {}
    w = _cat_cols(fox_in[:, 0])
    hw = HEADS * FOX_HEAD
    q = w[:, :hw] * FOX_SCALE
    W["fox_qkv"] = jnp.concatenate([q, w[:, hw:3 * hw]], axis=1)
    W["fox_f"] = jnp.pad(w[:, 3 * hw:], ((0, 0), (0, LANE - HEADS)))
    return W


def _prepare_weights(gathered):
    fox_o, mla_a, mla_uq, mla_ukv, mla_o, ffn_in, ffn_out = gathered
    W = {}
    W["fox_o"] = _pad_head_rows(_cat_rows(fox_o[:, 0]), FOX_HEAD)
    w = _cat_rows(mla_a[:, 0])
    lat = MLA_QR + MLA_KVR
    W["mla_a"] = jnp.concatenate(
        [w[:, :lat], jnp.zeros((D_MODEL, MLA_NOPE), BF16), w[:, lat:],
         jnp.zeros((D_MODEL, HEAD_PAD - MLA_NOPE - MLA_ROPE), BF16)], axis=1)
    W["mla_uq"] = _pad_head_cols(_cat_cols(mla_uq[:, 0]), MLA_NOPE + MLA_ROPE)
    w = _cat_cols(mla_ukv[:, 0]).reshape(MLA_KVR, HEADS, MLA_NOPE + MLA_V)
    kn = _pad_head_cols(w[:, :, :MLA_NOPE].reshape(MLA_KVR, -1), MLA_NOPE)
    vv = _pad_head_cols(w[:, :, MLA_NOPE:].reshape(MLA_KVR, -1), MLA_V)
    eye = np.zeros((HEAD_PAD, HEADS, HEAD_PAD), np.float32)
    for j in range(MLA_NOPE, MLA_NOPE + MLA_ROPE):
        eye[j, :, j] = 1.0
    eye = jnp.asarray(eye.reshape(HEAD_PAD, HEADS * HEAD_PAD), BF16)
    bottom = jnp.concatenate([eye, jnp.zeros((HEAD_PAD, HEADS * HEAD_PAD), BF16)], axis=1)
    W["mla_kv"] = jnp.concatenate([jnp.concatenate([kn, vv], axis=1), bottom], axis=0)
    W["mla_o"] = _pad_head_rows(_cat_rows(mla_o[:, 0]), MLA_V)
    w = _cat_cols(ffn_in)
    W["ffn_g"] = [w[i, :, :D_FF] for i in range(2)]
    W["ffn_v"] = [w[i, :, D_FF:] for i in range(2)]
    w = _cat_rows(ffn_out)
    W["ffn_out"] = [w[i] for i in range(2)]
    return W


def _ffn_forward(xin, h, mod, W, i, conv_w, conv_b, next_mod=None):
    shift, scale, gate = mod
    ug = _mm(h, W["ffn_g"][i], name=f"ffn{i}_up_gate")
    uv = _mm(h, W["ffn_v"][i], name=f"ffn{i}_up_val")
    wg, wv = conv_w[i][:, :D_FF], conv_w[i][:, D_FF:]
    bg, bv = conv_b[i][None, :D_FF], conv_b[i][None, D_FF:]
    g, v, act = _ffn_act_fwd(ug, uv, wg, wv, bg, bv, name=f"ffn{i}_act")
    extra = () if next_mod is None else (next_mod[1], next_mod[0])
    y, xout, *h_next = _mm(act, W["ffn_out"][i], gated=(xin, gate, *extra), name=f"ffn{i}_down")
    return xout, (xin, h, ug, uv, g, v, act, y, wg, wv), (h_next[0] if h_next else None)


def _ffn_backward(dx_out, saved, mod, W, i):
    xin, h, ug, uv, g, v, act, y, wg, wv = saved
    shift, scale, gate = mod
    dy, dgate = _gate_bwd(dx_out, y, gate, name=f"ffn{i}_gate_bwd")
    d_out = _mm(act, dy, ta=True, name=f"ffn{i}_dw_out")
    dact = _mm(dy, W["ffn_out"][i], tb=True, name=f"ffn{i}_dact")
    dug, duv, dwg, dwv, dbg, dbv = _ffn_bwd_mid(ug, uv, g, v, dact, wg, wv, name=f"ffn{i}_bwd_mid")
    d_in = jnp.concatenate([_mm(h, dug, ta=True, name=f"ffn{i}_dw_gate"),
                            _mm(h, duv, ta=True, name=f"ffn{i}_dw_val")], axis=1)
    dh = _mm(dug, W["ffn_g"][i], tb=True, name=f"ffn{i}_dh_gate")
    dh = _mm(duv, W["ffn_v"][i], tb=True, res=dh, name=f"ffn{i}_dh_val")
    dx, dscale, dshift = _adaln_bwd(xin, dh, dx_out, scale, name=f"ffn{i}_adaln_bwd")
    grads = dict(ffn_w_in=d_in, ffn_w_out=d_out, ffn_conv_w=jnp.concatenate([dwg, dwv], axis=1),
                 ffn_conv_b=jnp.concatenate([dbg, dbv], axis=1)[0])
    return dx, jnp.concatenate([dshift, dscale, dgate], axis=1)[0], grads


def _local_step(x, target, mods, W, small):
    S = x.shape[0]
    tables = _rope_tables(S)
    wide = HEADS * HEAD_PAD
    bf = jnp.pad(small["fox_b_f"], ((0, 0), (0, LANE - HEADS)))
    gq, gkv = small["mla_g_q"], small["mla_g_kv"]

    shift, scale, gate = mods[0]
    h0 = _adaln_fwd(x, scale, shift, name="fox_adaln")
    qkv = _mm(h0, W["fox_qkv"], out_dtype=BF16, name="fox_qkv")
    fl = _mm(h0, W["fox_f"], name="fox_gate_logits")
    cum = _fox_gate_fwd(fl, bf)
    q_fox, k_fox, v_fox = _fox_expand(qkv, cum, name="fox_expand")
    o0, lse0, gathered = _attn_fwd(q_fox, 0, k_fox, 0, v_fox, 0, name="fox_attn_fwd", gather=W["pending"])
    W = {**W, **_prepare_weights(gathered)}
    y0, x1, h1 = _mm(o0, W["fox_o"], gated=(x, gate, mods[1][1], mods[1][0]), name="fox_out")
    x2, ffn0, h2 = _ffn_forward(x1, h1, mods[1], W, 0, small["ffn_conv_w"], small["ffn_conv_b"], next_mod=mods[2])

    shift, scale, gate = mods[2]
    a = _mm(h2, W["mla_a"], name="mla_down")
    cqn, ckv_in = _mla_mid_fwd(a, gq, gkv, tables)
    q_raw = _mm(cqn, W["mla_uq"], name="mla_up_q")
    q_cat = _rope_heads(q_raw, tables, False, MLA_SCALE, name="mla_rope_q")
    kv = _mm(ckv_in, W["mla_kv"], out_dtype=BF16, name="mla_up_kv")
    o1, lse1, _ = _attn_fwd(q_cat, 0, kv, 0, kv, HEADS, name="mla_attn_fwd")
    y2, x3, h3 = _mm(o1, W["mla_o"], gated=(x2, gate, mods[3][1], mods[3][0]), name="mla_out")
    x4, ffn1, _ = _ffn_forward(x3, h3, mods[3], W, 1, small["ffn_conv_w"], small["ffn_conv_b"])

    loss, dx4, d_final_g = _final_loss(x4, small["final_g"], target)

    dx3, dmod3, g_ffn1 = _ffn_backward(dx4, ffn1, mods[3], W, 1)

    shift, scale, gate = mods[2]
    dy, dgate = _gate_bwd(dx3, y2, gate, name="mla_gate_bwd")
    d_mla_o = _mm(o1, dy, ta=True, name="mla_dw_out")
    do = _mm(dy, W["mla_o"], tb=True, out_dtype=BF16, name="mla_do")
    delta = _attn_delta(do, o1, name="mla_attn_delta")
    dq, dk, dv = _attn_bwd(q_cat, 0, kv, 0, kv, HEADS, do, lse1.reshape(HEADS, 1, S),
                           delta, name="mla_attn_bwd")
    dq_raw = _rope_heads(dq, tables, True, MLA_SCALE, name="mla_rope_q_bwd")
    d_mla_uq = _mm(cqn, dq_raw, ta=True, name="mla_dw_uq")
    dcqn = _mm(dq_raw, W["mla_uq"], tb=True, name="mla_dcq")
    dkv = jnp.concatenate([dk, dv], axis=1)
    d_mla_kv = _mm(ckv_in, dkv, ta=True, name="mla_dw_kv")
    dckv_in = _mm(dkv, W["mla_kv"], tb=True, name="mla_dckv")
    da, dgq, dgkv = _mla_mid_bwd(a, dcqn, dckv_in, gq, gkv, tables)
    d_mla_a = _mm(h2, da, ta=True, name="mla_dw_a")
    dh = _mm(da, W["mla_a"], tb=True, name="mla_dh")
    dx2, dscale, dshift = _adaln_bwd(x2, dh, dx3, scale, name="mla_adaln_bwd")
    dmod2 = jnp.concatenate([dshift, dscale, dgate], axis=1)[0]

    dx1, dmod1, g_ffn0 = _ffn_backward(dx2, ffn0, mods[1], W, 0)

    shift, scale, gate = mods[0]
    dy, dgate = _gate_bwd(dx1, y0, gate, name="fox_gate_bwd")
    d_fox_o = _mm(o0, dy, ta=True, name="fox_dw_out")
    do = _mm(dy, W["fox_o"], tb=True, out_dtype=BF16, name="fox_do")
    delta = _attn_delta(do, o0, name="fox_attn_delta")
    dq, dk, dv, dcq, dck = _attn_bwd(q_fox, 0, k_fox, 0, v_fox, 0, do, lse0.reshape(HEADS, 1, S),
                                     delta, name="fox_attn_bwd", bias_grads=True, dq_dtype=BF16)
    dcum = jnp.pad((dcq[:, 0, :] - dck[:, 0, :]).T, ((0, 0), (0, LANE - HEADS)))
    dfl, dbf = _fox_gate_bwd(dcum, fl, bf)
    dqkv = _fox_compress(dq, dk, dv, name="fox_compress")
    d_fox_qkv = _mm(h0, dqkv, ta=True, name="fox_dw_in")
    d_fox_f = _mm(h0, dfl, ta=True, name="fox_dw_gate")
    dh = _mm(dqkv, W["fox_qkv"], tb=True, name="fox_dh")
    dh = _mm(dfl, W["fox_f"], tb=True, res=dh, name="fox_dh_gate")
    dx0, dscale, dshift = _adaln_bwd(x, dh, dx1, scale, name="fox_adaln_bwd")
    dmod0 = jnp.concatenate([dshift, dscale, dgate], axis=1)[0]

    G = {}
    hw = HEADS * FOX_HEAD
    G["fox_w_in"] = jnp.concatenate([d_fox_qkv[:, :hw] * FOX_SCALE,
                                     d_fox_qkv[:, hw:], d_fox_f[:, :HEADS]], axis=1)[None]
    G["fox_b_f"] = dbf[:, :HEADS]
    G["fox_w_o"] = _unpad_head_rows(d_fox_o, FOX_HEAD)[None]
    lat = MLA_QR + MLA_KVR
    G["mla_w_a"] = jnp.concatenate([d_mla_a[:, :lat], d_mla_a[:, lat + MLA_NOPE:lat + MLA_NOPE + MLA_ROPE]],
                                   axis=1)[None]
    G["mla_g_q"] = dgq
    G["mla_g_kv"] = dgkv
    G["mla_w_uq"] = _unpad_head_cols(d_mla_uq, MLA_NOPE + MLA_ROPE)[None]
    dkn = d_mla_kv[:MLA_KVR, :wide].reshape(MLA_KVR, HEADS, HEAD_PAD)[:, :, :MLA_NOPE]
    dvv = d_mla_kv[:MLA_KVR, wide:].reshape(MLA_KVR, HEADS, HEAD_PAD)[:, :, :MLA_V]
    G["mla_w_ukv"] = jnp.concatenate([dkn, dvv], axis=2).reshape(MLA_KVR, -1)[None]
    G["mla_w_o"] = _unpad_head_rows(d_mla_o, MLA_V)[None]
    for name in ("ffn_w_in", "ffn_w_out", "ffn_conv_w", "ffn_conv_b"):
        G[name] = jnp.stack([g_ffn0[name], g_ffn1[name]])
    G["final_g"] = d_final_g[0]
    dmods = jnp.stack([dmod0, dmod1, dmod2, dmod3])
    return loss, dx0, dmods, G


PACKED = [("fox_w_in", 2), ("fox_w_o", 1), ("mla_w_a", 1), ("mla_w_uq", 2), ("mla_w_ukv", 2), ("mla_w_o", 1),
          ("ffn_w_in", 2), ("ffn_w_out", 1), ("fox_b_f", None), ("mla_g_q", 1), ("mla_g_kv", 1),
          ("ffn_conv_w", 2), ("ffn_conv_b", None), ("final_g", None)]


def _shard_of(g, axis, s):
    if axis is None:
        return g
    n = g.shape[axis] // 4
    return lax.slice_in_dim(g, s * n, (s + 1) * n, axis=axis)


def _pack_plan(G):
    shard_size = lambda name, axis: math.prod(_shard_of(G[name], axis, 0).shape)
    big = [(name, axis) for name, axis in PACKED if shard_size(name, axis) % PACK_ROW == 0]
    small = [(name, axis) for name, axis in PACKED if shard_size(name, axis) % PACK_ROW != 0]
    whole_tiles = lambda size: -(-size // (8 * PACK_ROW)) * 8
    runs = [([item], whole_tiles(shard_size(*item))) for item in big]
    runs.append((small, whole_tiles(sum(shard_size(*item) for item in small))))
    used = sum(rows for _, rows in runs)
    unit = 2 * PACK_ROWS_ALIGN
    return runs, -(-used // unit) * unit


def _pack(G):
    runs, total = _pack_plan(G)
    pieces = []
    for items, rows in runs:
        per_chip = [jnp.concatenate([_shard_of(G[name], axis, s).reshape(-1) for name, axis in items])
                    for s in range(4)]
        flat = jnp.stack(per_chip)
        flat = jnp.pad(flat, ((0, 0), (0, rows * PACK_ROW - flat.shape[1])))
        pieces.append(flat.reshape(4, rows, PACK_ROW))
    used = sum(rows for _, rows in runs)
    if total > used:
        pieces.append(jnp.zeros((4, total - used, PACK_ROW), F32))
    return jnp.concatenate(pieces, axis=1).reshape(4, 2, total // 2, PACK_ROW)


def _unpack(both, G_like):
    runs, total = _pack_plan(G_like)
    table = both.reshape(total, PACK_ROW)
    out, row = {}, 0
    for items, rows in runs:
        flat, off = table[row:row + rows].reshape(-1), 0
        for name, axis in items:
            shape = _shard_of(G_like[name], axis, 0).shape
            out[name] = flat[off:off + math.prod(shape)].reshape(shape)
            off += math.prod(shape)
        row += rows
    return out


WEIGHTS = ['ada_w', 'ada_b', 'fox_w_in', 'fox_b_f', 'fox_w_o', 'mla_w_a', 'mla_g_q', 'mla_g_kv', 'mla_w_uq',
           'mla_w_ukv', 'mla_w_o', 'ffn_w_in', 'ffn_conv_w', 'ffn_conv_b', 'ffn_w_out', 'final_g']
BIG = ['fox_w_in', 'fox_w_o', 'mla_w_a', 'mla_w_uq', 'mla_w_ukv', 'mla_w_o', 'ffn_w_in', 'ffn_w_out']
SMALL = ['ada_b', 'fox_b_f', 'mla_g_q', 'mla_g_kv', 'ffn_conv_w', 'ffn_conv_b', 'final_g']


def _as2d(a):
    return a.reshape(-1, a.shape[-1])


def kernel(x, c, ada_w, ada_b, fox_w_in, fox_b_f, fox_w_o, mla_w_a, mla_g_q, mla_g_kv, mla_w_uq, mla_w_ukv, mla_w_o, ffn_w_in, ffn_conv_w, ffn_conv_b, ffn_w_out, final_g, loss_target, m_ada_w, m_ada_b, m_fox_w_in, m_fox_b_f, m_fox_w_o, m_mla_w_a, m_mla_g_q, m_mla_g_kv, m_mla_w_uq, m_mla_w_ukv, m_mla_w_o, m_ffn_w_in, m_ffn_conv_w, m_ffn_conv_b, m_ffn_w_out, m_final_g, v_ada_w, v_ada_b, v_fox_w_in, v_fox_b_f, v_fox_w_o, v_mla_w_a, v_mla_g_q, v_mla_g_kv, v_mla_w_uq, v_mla_w_ukv, v_mla_w_o, v_ffn_w_in, v_ffn_conv_w, v_ffn_conv_b, v_ffn_w_out, v_final_g):
    w = dict(ada_w=ada_w, ada_b=ada_b, fox_w_in=fox_w_in, fox_b_f=fox_b_f, fox_w_o=fox_w_o, mla_w_a=mla_w_a,
             mla_g_q=mla_g_q, mla_g_kv=mla_g_kv, mla_w_uq=mla_w_uq, mla_w_ukv=mla_w_ukv, mla_w_o=mla_w_o,
             ffn_w_in=ffn_w_in, ffn_conv_w=ffn_conv_w, ffn_conv_b=ffn_conv_b, ffn_w_out=ffn_w_out, final_g=final_g)
    m = dict(ada_w=m_ada_w, ada_b=m_ada_b, fox_w_in=m_fox_w_in, fox_b_f=m_fox_b_f, fox_w_o=m_fox_w_o,
             mla_w_a=m_mla_w_a, mla_g_q=m_mla_g_q, mla_g_kv=m_mla_g_kv, mla_w_uq=m_mla_w_uq,
             mla_w_ukv=m_mla_w_ukv, mla_w_o=m_mla_w_o, ffn_w_in=m_ffn_w_in, ffn_conv_w=m_ffn_conv_w,
             ffn_conv_b=m_ffn_conv_b, ffn_w_out=m_ffn_w_out, final_g=m_final_g)
    v = dict(ada_w=v_ada_w, ada_b=v_ada_b, fox_w_in=v_fox_w_in, fox_b_f=v_fox_b_f, fox_w_o=v_fox_w_o,
             mla_w_a=v_mla_w_a, mla_g_q=v_mla_g_q, mla_g_kv=v_mla_g_kv, mla_w_uq=v_mla_w_uq,
             mla_w_ukv=v_mla_w_ukv, mla_w_o=v_mla_w_o, ffn_w_in=v_ffn_w_in, ffn_conv_w=v_ffn_conv_w,
             ffn_conv_b=v_ffn_conv_b, ffn_w_out=v_ffn_w_out, final_g=v_final_g)
    D = D_MODEL
    xi, yi, ci = lax.axis_index("x"), lax.axis_index("y"), lax.axis_index("c")
    dev = 4 * xi + 2 * yi + ci

    n_ada = ada_w.shape[-1]
    small_parts = [c.reshape(-1), ffn_conv_w.reshape(-1), mla_g_q.reshape(-1), mla_g_kv.reshape(-1)]
    sizes = [p.shape[0] for p in small_parts]
    flat = jnp.concatenate(small_parts)
    flat = jnp.pad(flat, (0, -flat.shape[0] % LANE))[None]
    got = _all_gather8(flat, name="gather_cond")[:, 0]
    offs = np.cumsum([0] + sizes)
    c_all = got[:, offs[0]:offs[1]]
    chips = got[0::2]
    conv_w_full = jnp.concatenate(
        [chips[s, offs[1]:offs[2]].reshape(ffn_conv_w.shape) for s in range(4)], axis=2)
    gq_full = jnp.concatenate([chips[s, offs[2]:offs[3]] for s in range(4)])[None]
    gkv_full = jnp.concatenate([chips[s, offs[3]:offs[4]] for s in range(4)])[None]

    mod_shard = _ada_mod(c_all, ada_w.reshape(4, D, n_ada), ada_b.reshape(4, 1, n_ada))
    mod_all = _all_gather8(mod_shard.reshape(4 * 8, n_ada), name="gather_mod")
    mod_all = mod_all[0::2].reshape(4, 4, 8, n_ada)
    mine = lax.dynamic_index_in_dim(mod_all, dev, axis=2, keepdims=False)
    mod_rows = jnp.transpose(mine, (1, 0, 2)).reshape(4, 4 * n_ada)
    mods = [(mod_rows[k:k + 1, :D], mod_rows[k:k + 1, D:2 * D], mod_rows[k:k + 1, 2 * D:]) for k in range(4)]

    shards = [w[name].astype(BF16) for name in BIG]
    W = _prepare_fox_in(_all_gather_chips(shards[:1], name="gather_weights")[0])
    W["pending"] = shards[1:]
    small = dict(fox_b_f=fox_b_f, mla_g_q=gq_full, mla_g_kv=gkv_full, ffn_conv_w=conv_w_full,
                 ffn_conv_b=ffn_conv_b, final_g=final_g[None])

    loss, dx, dmods, G = _local_step(x[0], loss_target[0], mods, W, small)

    loss_row = jnp.pad(loss[:, :1], ((0, 0), (0, dmods.shape[1] - 1)))
    gathered_rows = _all_gather8(jnp.concatenate([dmods, loss_row]), name="gather_dmod")
    dmod_all = gathered_rows[:, :4]
    loss_total = jnp.sum(gathered_rows[:, 4, 0])
    chip = 2 * xi + yi
    dmod_cols = lax.dynamic_slice_in_dim(dmod_all, chip * n_ada, n_ada, axis=2)
    g_ada_w, g_ada_b = _ada_grad(c_all.T, jnp.transpose(dmod_cols, (1, 0, 2)))
    grads = dict(ada_w=g_ada_w.reshape(ada_w.shape), ada_b=g_ada_b.reshape(ada_b.shape))

    packed = _pack(G)
    recv = _exchange_halves(packed)
    part = _add_halves(packed, recv, ci.reshape(1).astype(jnp.int32))
    from_chips = _exchange_chips(part)
    half = _add_chips(from_chips)
    other = _share_halves(half)
    both = jnp.stack([jnp.where(ci == 0, half, other), jnp.where(ci == 0, other, half)])
    grads.update(_unpack(both, G))

    delta, new_m, new_v = {}, {}, {}
    for name in BIG + ["ada_w"]:
        shape = w[name].shape
        d_, m_, v_ = _adamw(_as2d(w[name]), _as2d(grads[name]), _as2d(m[name]), _as2d(v[name]), name=f"adamw_{name}")
        delta[name], new_m[name], new_v[name] = d_.reshape(shape), m_.reshape(shape), v_.reshape(shape)
        grads[name] = grads[name].reshape(shape)
    sizes = [math.prod(w[name].shape) for name in SMALL]
    total = sum(sizes)
    rows = -(-total // LANE)
    rows += -rows % 8

    def pack_small(d):
        flat = jnp.concatenate([d[name].reshape(-1) for name in SMALL])
        return jnp.pad(flat, (0, rows * LANE - total)).reshape(rows, LANE)

    outs = _adamw(pack_small(w), pack_small(grads), pack_small(m), pack_small(v), name="adamw_small")
    off = 0
    for name, size in zip(SMALL, sizes):
        shape = w[name].shape
        for dst, src in zip((delta, new_m, new_v), outs):
            dst[name] = src.reshape(-1)[off:off + size].reshape(shape)
        grads[name] = grads[name].reshape(shape)
        off += size

    return (loss_total, dx[None], *[grads[n] for n in WEIGHTS], *[delta[n] for n in WEIGHTS],
            *[new_m[n] for n in WEIGHTS], *[new_v[n] for n in WEIGHTS])
```

```python
import functools
import math

import numpy as np
import jax
import jax.numpy as jnp
from jax import lax
from jax.experimental import pallas as pl
from jax.experimental.pallas import tpu as pltpu

F32 = jnp.float32
BF16 = jnp.bfloat16
MESH = pl.DeviceIdType.MESH

D_MODEL = 1024
HEADS = 16
HEAD_PAD = 128
FOX_HEAD = 64
MLA_NOPE = 64
MLA_ROPE = 32
MLA_V = 64
MLA_QR = 384
MLA_KVR = 256
D_FF = 2816
NORM_EPS = 1e-6
ROPE_BASE = 10000.0
ADAM_LR = 0.001
ADAM_B1 = 0.9
ADAM_B2 = 0.999
ADAM_EPS = 1e-08
ADAM_WD = 0.01
ADAM_STEP = 10
FOX_SCALE = FOX_HEAD ** -0.5
MLA_SCALE = (MLA_NOPE + MLA_ROPE) ** -0.5
NEG_BIG = -1e30
VMEM_LIMIT = 56 * 1024 * 1024
LANE = 128
MM_ROWS, MM_COLS, MM_DEPTH = 1408, 1408, 1408
PACK_ROW = 1024
PACK_ROWS_ALIGN = 256

NT = (((1,), (1,)), ((), ()))
TN = (((0,), (0,)), ((), ()))
NN = (((1,), (0,)), ((), ()))


def _cparams():
    return pltpu.CompilerParams(vmem_limit_bytes=VMEM_LIMIT)


def _tile(n, cap):
    if n <= cap:
        return n
    for t in range(cap - cap % LANE, 0, -LANE):
        if n % t == 0:
            return t
    raise ValueError((n, cap))


def _row_tile(rows, cols, limit_bytes=1 << 20):
    best = None
    for t in range(8, rows + 1, 8):
        if rows % t == 0 and t * cols * 4 <= limit_bytes:
            best = t
    return best if best is not None else rows


def _me():
    return lax.axis_index("x"), lax.axis_index("y"), lax.axis_index("c")


def _all_gather8(v, name):
    R, N = v.shape

    def body(v_ref, o_ref, ssem, rsem):
        x, y, c = _me()
        me = 4 * x + 2 * y + c
        o_ref[me] = v_ref[...]
        copies = []
        for k in range(1, 8):
            peer = (1 - x if k & 4 else x, 1 - y if k & 2 else y, 1 - c if k & 1 else c)
            cp = pltpu.make_async_remote_copy(
                src_ref=v_ref, dst_ref=o_ref.at[me], send_sem=ssem.at[k - 1], recv_sem=rsem.at[k - 1],
                device_id=peer, device_id_type=MESH)
            cp.start()
            copies.append(cp)
        for cp in copies:
            cp.wait()

    return pl.pallas_call(
        body, name=name,
        out_shape=jax.ShapeDtypeStruct((8, R, N), v.dtype),
        in_specs=[pl.BlockSpec(memory_space=pltpu.VMEM)],
        out_specs=pl.BlockSpec(memory_space=pltpu.VMEM),
        scratch_shapes=[pltpu.SemaphoreType.DMA((7,)), pltpu.SemaphoreType.DMA((7,))],
    )(v)


def _all_gather_chips(shards, name):
    n = len(shards)

    def body(*refs):
        copies = _chip_gather_copies(refs[:n], refs[n:2 * n], *refs[2 * n:])
        for cp in copies:
            cp.start()
        for cp in copies:
            cp.wait()

    return pl.pallas_call(
        body, name=name,
        out_shape=[jax.ShapeDtypeStruct((4,) + s.shape, s.dtype) for s in shards],
        in_specs=[pl.BlockSpec(memory_space=pl.ANY)] * n,
        out_specs=[pl.BlockSpec(memory_space=pl.ANY)] * n,
        scratch_shapes=[pltpu.SemaphoreType.DMA((3 * n,)), pltpu.SemaphoreType.DMA((3 * n,)),
                        pltpu.SemaphoreType.DMA((n,))],
    )(*shards)


def _exchange_halves(g):
    _, _, R, P = g.shape

    def body(g_ref, o_ref, ssem, rsem):
        x, y, c = _me()
        copies = []
        for s in range(4):
            cp = pltpu.make_async_remote_copy(
                src_ref=g_ref.at[s, 1 - c], dst_ref=o_ref.at[s], send_sem=ssem.at[s], recv_sem=rsem.at[s],
                device_id=(x, y, 1 - c), device_id_type=MESH)
            cp.start()
            copies.append(cp)
        for cp in copies:
            cp.wait()

    return pl.pallas_call(
        body, name="grad_exchange_sibling",
        out_shape=jax.ShapeDtypeStruct((4, R, P), g.dtype),
        in_specs=[pl.BlockSpec(memory_space=pl.ANY)],
        out_specs=pl.BlockSpec(memory_space=pl.ANY),
        scratch_shapes=[pltpu.SemaphoreType.DMA((4,)), pltpu.SemaphoreType.DMA((4,))],
    )(g)


def _exchange_chips(a):
    _, R, P = a.shape

    def body(a_ref, o_ref, ssem, rsem, lsem):
        x, y, c = _me()
        me = 2 * x + y
        own = pltpu.make_async_copy(a_ref.at[me], o_ref.at[me], lsem)
        own.start()
        copies = []
        for k in (1, 2, 3):
            px, py = (1 - x if k & 2 else x), (1 - y if k & 1 else y)
            cp = pltpu.make_async_remote_copy(
                src_ref=a_ref.at[2 * px + py], dst_ref=o_ref.at[me], send_sem=ssem.at[k - 1],
                recv_sem=rsem.at[k - 1], device_id=(px, py, c), device_id_type=MESH)
            cp.start()
            copies.append(cp)
        own.wait()
        for cp in copies:
            cp.wait()

    return pl.pallas_call(
        body, name="grad_exchange_chips",
        out_shape=jax.ShapeDtypeStruct((4, R, P), a.dtype),
        in_specs=[pl.BlockSpec(memory_space=pl.ANY)],
        out_specs=pl.BlockSpec(memory_space=pl.ANY),
        scratch_shapes=[pltpu.SemaphoreType.DMA((3,)), pltpu.SemaphoreType.DMA((3,)), pltpu.SemaphoreType.DMA],
    )(a)


def _share_halves(f):
    R, P = f.shape

    def body(f_ref, o_ref, ssem, rsem):
        x, y, c = _me()
        cp = pltpu.make_async_remote_copy(
            src_ref=f_ref, dst_ref=o_ref, send_sem=ssem, recv_sem=rsem,
            device_id=(x, y, 1 - c), device_id_type=MESH)
        cp.start()
        cp.wait()

    return pl.pallas_call(
        body, name="grad_share_sibling",
        out_shape=jax.ShapeDtypeStruct((R, P), f.dtype),
        in_specs=[pl.BlockSpec(memory_space=pl.ANY)],
        out_specs=pl.BlockSpec(memory_space=pl.ANY),
        scratch_shapes=[pltpu.SemaphoreType.DMA, pltpu.SemaphoreType.DMA],
    )(f)


def _add_halves(g, r, c_idx):
    _, _, R, P = g.shape
    t = PACK_ROWS_ALIGN

    def body(c_ref, g_ref, r_ref, o_ref):
        o_ref[...] = (g_ref[0] + r_ref[...]).astype(BF16)

    return pl.pallas_call(
        body, name="grad_add_halves",
        out_shape=jax.ShapeDtypeStruct((4, R, P), BF16),
        grid_spec=pltpu.PrefetchScalarGridSpec(
            num_scalar_prefetch=1, grid=(4, R // t),
            in_specs=[pl.BlockSpec((1, 1, t, P), lambda s, i, c_ref: (s, c_ref[0], i, 0)),
                      pl.BlockSpec((1, t, P), lambda s, i, c_ref: (s, i, 0))],
            out_specs=pl.BlockSpec((1, t, P), lambda s, i, c_ref: (s, i, 0))),
        compiler_params=_cparams(),
    )(c_idx, g, r)


def _add_chips(b):
    _, R, P = b.shape
    t = PACK_ROWS_ALIGN

    def body(b_ref, o_ref):
        b0, b1, b2, b3 = (b_ref[k].astype(F32) for k in range(4))
        o_ref[...] = ((b0 + b1) + b2) + b3

    return pl.pallas_call(
        body, name="grad_add_chips",
        out_shape=jax.ShapeDtypeStruct((R, P), F32),
        grid=(R // t,),
        in_specs=[pl.BlockSpec((4, t, P), lambda i: (0, i, 0))],
        out_specs=pl.BlockSpec((t, P), lambda i: (i, 0)),
        compiler_params=_cparams(),
    )(b)


def _mm(a, b, *, ta=False, tb=False, out_dtype=F32, res=None, gated=None, name):
    if ta:
        K, M = a.shape
    else:
        M, K = a.shape
    N = b.shape[0] if tb else b.shape[1]
    assert (b.shape[1] if tb else b.shape[0]) == K, (a.shape, b.shape, ta, tb)
    tm, tn, tk = _tile(M, MM_ROWS), _tile(N, MM_COLS), _tile(K, MM_DEPTH)
    nk = K // tk
    has_res = res is not None
    has_gate = gated is not None
    has_norm = has_gate and len(gated) == 4
    assert not has_norm or tn == N, "the norm needs whole rows in a tile"
    dims = (((0 if ta else 1,), (1 if tb else 0,)), ((), ()))

    def body(*refs):
        refs = iter(refs)
        a_ref, b_ref = next(refs), next(refs)
        r_ref = next(refs) if has_res else None
        x_ref, g_ref = (next(refs), next(refs)) if has_gate else (None, None)
        sc_ref, sh_ref = (next(refs), next(refs)) if has_norm else (None, None)
        o_ref = next(refs)
        xo_ref = next(refs) if has_gate else None
        h_ref = next(refs) if has_norm else None
        acc_ref = next(refs) if nk > 1 else None

        def finish(r):
            if has_res:
                r = r + r_ref[...]
            o_ref[...] = r.astype(out_dtype)
            if has_gate:
                xo = x_ref[...] + g_ref[...] * r
                xo_ref[...] = xo
            if has_norm:
                h_ref[...] = (xo * _rstd(xo) * (1.0 + sc_ref[...]) + sh_ref[...]).astype(BF16)

        prod = lax.dot_general(a_ref[...].astype(BF16), b_ref[...].astype(BF16), dims,
                               preferred_element_type=F32)
        if nk == 1:
            finish(prod)
            return
        k = pl.program_id(2)

        @pl.when(k == 0)
        def _():
            acc_ref[...] = prod

        @pl.when(k > 0)
        def _():
            acc_ref[...] += prod

        @pl.when(k == nk - 1)
        def _():
            finish(acc_ref[...])

    a_spec = (pl.BlockSpec((tk, tm), lambda j, i, k: (k, i)) if ta
              else pl.BlockSpec((tm, tk), lambda j, i, k: (i, k)))
    b_spec = (pl.BlockSpec((tn, tk), lambda j, i, k: (j, k)) if tb
              else pl.BlockSpec((tk, tn), lambda j, i, k: (k, j)))
    in_specs = [a_spec, b_spec]
    args = [a, b]
    out_tile = pl.BlockSpec((tm, tn), lambda j, i, k: (i, j))
    if has_res:
        in_specs.append(out_tile)
        args.append(res)
    out_shape, out_specs = jax.ShapeDtypeStruct((M, N), out_dtype), out_tile
    if has_gate:
        vec = pl.BlockSpec((1, tn), lambda j, i, k: (0, j))
        in_specs += [out_tile, vec] + [vec, vec] * has_norm
        args += list(gated)
        out_shape, out_specs = [out_shape, jax.ShapeDtypeStruct((M, N), F32)], [out_tile, out_tile]
        if has_norm:
            out_shape.append(jax.ShapeDtypeStruct((M, N), BF16))
            out_specs.append(out_tile)
    return pl.pallas_call(
        body, name=name, out_shape=out_shape, grid=(N // tn, M // tm, nk), in_specs=in_specs,
        out_specs=out_specs, scratch_shapes=[pltpu.VMEM((tm, tn), F32)] if nk > 1 else [],
        compiler_params=_cparams(),
    )(*args)


def _rstd(xv):
    return lax.rsqrt(jnp.mean(xv * xv, axis=-1, keepdims=True) + NORM_EPS)


def _norm_bwd(n, r, dn):
    return r * (dn - n * jnp.mean(dn * n, axis=-1, keepdims=True))


def _adaln_fwd(x, scale, shift, name):
    S, D = x.shape
    t = _tile(S, 512)

    def body(x_ref, sc_ref, sh_ref, h_ref):
        xv = x_ref[...]
        h_ref[...] = (xv * _rstd(xv) * (1.0 + sc_ref[...]) + sh_ref[...]).astype(BF16)

    row = pl.BlockSpec((t, D), lambda i: (i, 0))
    vec = pl.BlockSpec((1, D), lambda i: (0, 0))
    return pl.pallas_call(
        body, name=name, out_shape=jax.ShapeDtypeStruct((S, D), BF16), grid=(S // t,),
        in_specs=[row, vec, vec], out_specs=row, compiler_params=_cparams(),
    )(x, scale, shift)


def _adaln_bwd(x, dh, dx_next, scale, name):
    S, D = x.shape
    t = _tile(S, 512)

    def body(x_ref, dh_ref, dxn_ref, sc_ref, dx_ref, dsc_ref, dsh_ref):
        @pl.when(pl.program_id(0) == 0)
        def _():
            dsc_ref[...] = jnp.zeros_like(dsc_ref)
            dsh_ref[...] = jnp.zeros_like(dsh_ref)

        xv = x_ref[...]
        r = _rstd(xv)
        n = xv * r
        dh = dh_ref[...]
        dx_ref[...] = dxn_ref[...] + _norm_bwd(n, r, dh * (1.0 + sc_ref[...]))
        dsc_ref[...] += jnp.sum(dh * n, axis=0, keepdims=True)
        dsh_ref[...] += jnp.sum(dh, axis=0, keepdims=True)

    row = pl.BlockSpec((t, D), lambda i: (i, 0))
    vec = pl.BlockSpec((1, D), lambda i: (0, 0))
    return pl.pallas_call(
        body, name=name,
        out_shape=[jax.ShapeDtypeStruct((S, D), F32), jax.ShapeDtypeStruct((1, D), F32),
                   jax.ShapeDtypeStruct((1, D), F32)],
        grid=(S // t,), in_specs=[row, row, row, vec], out_specs=[row, vec, vec],
        compiler_params=_cparams(),
    )(x, dh, dx_next, scale)


def _gate_bwd(dx, y, gate, name):
    S, D = dx.shape
    t = _tile(S, 512)

    def body(dx_ref, y_ref, g_ref, dy_ref, dg_ref):
        @pl.when(pl.program_id(0) == 0)
        def _():
            dg_ref[...] = jnp.zeros_like(dg_ref)

        dxv = dx_ref[...]
        dy_ref[...] = (dxv * g_ref[...]).astype(BF16)
        dg_ref[...] += jnp.sum(dxv * y_ref[...], axis=0, keepdims=True)

    row = pl.BlockSpec((t, D), lambda i: (i, 0))
    vec = pl.BlockSpec((1, D), lambda i: (0, 0))
    return pl.pallas_call(
        body, name=name,
        out_shape=[jax.ShapeDtypeStruct((S, D), BF16), jax.ShapeDtypeStruct((1, D), F32)],
        grid=(S // t,), in_specs=[row, row, vec], out_specs=[row, vec], compiler_params=_cparams(),
    )(dx, y, gate)


def _final_loss(x, g, target):
    S, D = x.shape
    t = _tile(S, 512)

    def body(x_ref, g_ref, t_ref, loss_ref, dx_ref, dg_ref):
        @pl.when(pl.program_id(0) == 0)
        def _():
            loss_ref[...] = jnp.zeros_like(loss_ref)
            dg_ref[...] = jnp.zeros_like(dg_ref)

        xv = x_ref[...]
        gv = g_ref[...]
        r = _rstd(xv)
        n = xv * r
        err = n * gv - t_ref[...]
        part = jnp.sum(jnp.mean(err * err, axis=-1, keepdims=True), axis=0, keepdims=True)
        loss_ref[...] += jnp.broadcast_to(0.5 * part, loss_ref.shape)
        dy = err * (1.0 / D)
        dg_ref[...] += jnp.sum(dy * n, axis=0, keepdims=True)
        dx_ref[...] = _norm_bwd(n, r, dy * gv)

    row = pl.BlockSpec((t, D), lambda i: (i, 0))
    vec = pl.BlockSpec((1, D), lambda i: (0, 0))
    one = pl.BlockSpec((1, LANE), lambda i: (0, 0))
    return pl.pallas_call(
        body, name="final_loss",
        out_shape=[jax.ShapeDtypeStruct((1, LANE), F32), jax.ShapeDtypeStruct((S, D), F32),
                   jax.ShapeDtypeStruct((1, D), F32)],
        grid=(S // t,), in_specs=[row, vec, row], out_specs=[one, row, vec], compiler_params=_cparams(),
    )(x, g, target)


ATTN_TILE = 1024
ATTN_ROWS = 32
ONES_LANE = 127


def _causal_pairs(n, k_major):
    if k_major:
        pairs = [(i, j) for j in range(n) for i in range(j, n)]
    else:
        pairs = [(i, j) for i in range(n) for j in range(i + 1)]
    return (jnp.asarray(np.array([p[0] for p in pairs], np.int32)),
            jnp.asarray(np.array([p[1] for p in pairs], np.int32)))


def _chip_gather_copies(ins, outs, ssem, rsem, lsem):
    x, y, c = _me()
    me = 2 * x + y
    copies = []
    for i, (src, dst) in enumerate(zip(ins, outs)):
        copies.append(pltpu.make_async_copy(src, dst.at[me], lsem.at[i]))
        for k in (1, 2, 3):
            peer = (1 - x if k & 2 else x, 1 - y if k & 1 else y, c)
            copies.append(pltpu.make_async_remote_copy(
                src_ref=src, dst_ref=dst.at[me], send_sem=ssem.at[3 * i + k - 1],
                recv_sem=rsem.at[3 * i + k - 1], device_id=peer, device_id_type=MESH))
    return copies


def _attn_fwd(q_arr, q_off, k_arr, k_off, v_arr, v_off, name, gather=()):
    S = q_arr.shape[0]
    t = _tile(S, 2048)
    n = S // t
    rc = min(ATTN_ROWS, t)
    ii, jj = _causal_pairs(n, k_major=False)
    ng = len(gather)
    last_step = ii.shape[0] - 1

    def body(ii_ref, jj_ref, q_ref, k_ref, v_ref, *rest):
        g_in, (o_ref, lse_ref), g_out = rest[:ng], rest[ng:ng + 2], rest[ng + 2:2 * ng + 2]
        s_s, m_s, a_s, acc_s = rest[2 * ng + 2:2 * ng + 6]
        step = pl.program_id(1)
        i, j = ii_ref[step], jj_ref[step]
        lane = lax.broadcasted_iota(jnp.int32, (t, HEAD_PAD), 1)
        if ng:
            copies = _chip_gather_copies(g_in, g_out, *rest[2 * ng + 6:])

            @pl.when(jnp.logical_and(pl.program_id(0) == 0, step == 0))
            def _():
                for cp in copies:
                    cp.start()

            @pl.when(jnp.logical_and(pl.program_id(0) == HEADS - 1, step == last_step))
            def _():
                for cp in copies:
                    cp.wait()

        @pl.when(j == 0)
        def _():
            m_s[...] = jnp.full_like(m_s, NEG_BIG)
            acc_s[...] = jnp.zeros_like(acc_s)

        def tile(diagonal):
            s_s[...] = lax.dot_general(q_ref[...], k_ref[...], NT, preferred_element_type=F32)
            for r in range(t // rc):
                rows = pl.ds(r * rc, rc)
                sc = s_s[rows, :]
                if diagonal:
                    qpos = r * rc + lax.broadcasted_iota(jnp.int32, (rc, t), 0)
                    kpos = lax.broadcasted_iota(jnp.int32, (rc, t), 1)
                    sc = jnp.where(qpos >= kpos, sc, NEG_BIG)
                m_old = m_s[rows, :]
                m_new = jnp.maximum(m_old, jnp.max(sc, axis=1, keepdims=True))
                s_s[rows, :] = jnp.exp(sc - m_new)
                a_s[rows, :] = jnp.exp(m_old - m_new)
                m_s[rows, :] = m_new
            v_ones = jnp.where(lane == ONES_LANE, 1.0, v_ref[...].astype(F32)).astype(BF16)
            acc_s[...] = a_s[...] * acc_s[...] + lax.dot_general(s_s[...].astype(BF16), v_ones, NN,
                                                                 preferred_element_type=F32)

        @pl.when(j < i)
        def _():
            tile(False)

        @pl.when(j == i)
        def _():
            tile(True)
            acc = acc_s[...]
            l = jnp.sum(jnp.where(lane == ONES_LANE, acc, 0.0), axis=1, keepdims=True)
            o_ref[...] = jnp.where(lane == ONES_LANE, 0.0, acc / l).astype(BF16)
            lse_ref[0] = m_s[...] + jnp.log(l)

    hbm = pl.BlockSpec(memory_space=pl.ANY)
    sems = [pltpu.SemaphoreType.DMA((3 * ng,)), pltpu.SemaphoreType.DMA((3 * ng,)),
            pltpu.SemaphoreType.DMA((ng,))] if ng else []
    grid_spec = pltpu.PrefetchScalarGridSpec(
        num_scalar_prefetch=2, grid=(HEADS, ii.shape[0]),
        in_specs=[pl.BlockSpec((t, HEAD_PAD), lambda h, s, ii, jj: (ii[s], q_off + h)),
                  pl.BlockSpec((t, HEAD_PAD), lambda h, s, ii, jj: (jj[s], k_off + h)),
                  pl.BlockSpec((t, HEAD_PAD), lambda h, s, ii, jj: (jj[s], v_off + h))] + [hbm] * ng,
        out_specs=[pl.BlockSpec((t, HEAD_PAD), lambda h, s, ii, jj: (ii[s], h)),
                   pl.BlockSpec((1, t, 1), lambda h, s, ii, jj: (h, ii[s], 0))] + [hbm] * ng,
        scratch_shapes=[pltpu.VMEM((t, t), F32), pltpu.VMEM((t, 1), F32), pltpu.VMEM((t, 1), F32),
                        pltpu.VMEM((t, HEAD_PAD), F32)] + sems)
    outs = pl.pallas_call(
        body, name=name, grid_spec=grid_spec,
        out_shape=[jax.ShapeDtypeStruct((S, HEADS * HEAD_PAD), BF16), jax.ShapeDtypeStruct((HEADS, S, 1), F32)]
        + [jax.ShapeDtypeStruct((4,) + g.shape, g.dtype) for g in gather],
        compiler_params=_cparams(),
    )(ii, jj, q_arr, k_arr, v_arr, *gather)
    return outs[0], outs[1], list(outs[2:])


def _attn_delta(do, o, name):
    S = do.shape[0]
    t = _tile(S, 512)

    def body(do_ref, o_ref, d_ref):
        lane = lax.broadcasted_iota(jnp.int32, (t, LANE), 1)
        acc = jnp.zeros((t, LANE), F32)
        for h in range(HEADS):
            cols = pl.ds(h * HEAD_PAD, HEAD_PAD)
            d = jnp.sum(do_ref[:, cols].astype(F32) * o_ref[:, cols].astype(F32), axis=1, keepdims=True)
            acc = jnp.where(lane == h, d, acc)
        d_ref[...] = acc

    blk = pl.BlockSpec((t, HEADS * HEAD_PAD), lambda i: (i, 0))
    by_lane = pl.pallas_call(
        body, name=name, out_shape=jax.ShapeDtypeStruct((S, LANE), F32), grid=(S // t,),
        in_specs=[blk, blk], out_specs=pl.BlockSpec((t, LANE), lambda i: (i, 0)),
        compiler_params=_cparams(),
    )(do, o)
    return by_lane[:, :HEADS].T.reshape(HEADS, 1, S)


def _lane_as_row(x, lane):
    pick = (lax.broadcasted_iota(jnp.int32, (8, HEAD_PAD), 1) == lane).astype(BF16)
    acc = None
    for part in _split3(x):
        d = lax.dot_general(pick, part, NT, preferred_element_type=F32)
        acc = d if acc is None else acc + d
    return jnp.max(acc, axis=0, keepdims=True)


def _attn_bwd(q_arr, q_off, k_arr, k_off, v_arr, v_off, do, lse, delta, name, bias_grads=False, dq_dtype=F32):
    S = q_arr.shape[0]
    t = _tile(S, ATTN_TILE)
    n = S // t
    rc = min(ATTN_ROWS, t)
    ii, jj = _causal_pairs(n, k_major=True)
    last_step = ii.shape[0] - 1

    def body(ii_ref, jj_ref, q_ref, k_ref, v_ref, do_ref, lse_ref, dl_ref, dq_ref, dk_ref, dv_ref, *rest):
        if bias_grads:
            dcq_ref, dck_ref = rest[:2]
        s_s, dp_s, dk_s, dv_s, dq_s = rest[-5:]
        step = pl.program_id(1)
        i, j = ii_ref[step], jj_ref[step]

        @pl.when(step == 0)
        def _():
            dq_s[...] = jnp.zeros_like(dq_s)

        @pl.when(i == j)
        def _():
            dk_s[...] = jnp.zeros_like(dk_s)
            dv_s[...] = jnp.zeros_like(dv_s)

        def tile(diagonal):
            q, k, dov = q_ref[...], k_ref[...], do_ref[...]
            s_s[...] = lax.dot_general(k, q, NT, preferred_element_type=F32)
            dp_s[...] = lax.dot_general(v_ref[...], dov, NT, preferred_element_type=F32)
            lse_row, dl_row = lse_ref[0], dl_ref[0]
            for r in range(t // rc):
                rows = pl.ds(r * rc, rc)
                sc = s_s[rows, :]
                if diagonal:
                    kpos = r * rc + lax.broadcasted_iota(jnp.int32, (rc, t), 0)
                    qpos = lax.broadcasted_iota(jnp.int32, (rc, t), 1)
                    sc = jnp.where(qpos >= kpos, sc, NEG_BIG)
                p = jnp.exp(sc - lse_row)
                s_s[rows, :] = p
                dp_s[rows, :] = p * (dp_s[rows, :] - dl_row)
            ds = dp_s[...].astype(BF16)
            dv_s[...] += lax.dot_general(s_s[...].astype(BF16), dov, NN, preferred_element_type=F32)
            dk_s[...] += lax.dot_general(ds, q, NN, preferred_element_type=F32)
            rows = pl.ds(pl.multiple_of(i * t, t), t)
            dq_s[rows, :] += lax.dot_general(ds, k, TN, preferred_element_type=F32)

        @pl.when(i > j)
        def _():
            tile(False)

        @pl.when(i == j)
        def _():
            tile(True)

        @pl.when(i == n - 1)
        def _():
            dk = dk_s[...]
            dk_ref[...] = dk.astype(BF16)
            dv_ref[...] = dv_s[...].astype(BF16)
            if bias_grads:
                dck_ref[0] = _lane_as_row(dk, BIAS_LANE)

        @pl.when(step == last_step)
        def _():
            dq = dq_s[...]
            dq_ref[...] = dq.astype(dq_dtype)
            if bias_grads:
                dcq_ref[0] = _lane_as_row(dq, BIAS_LANE + 3)

    qblk = lambda off: pl.BlockSpec((t, HEAD_PAD), lambda h, s, ii, jj: (ii[s], off + h))
    kblk = lambda off: pl.BlockSpec((t, HEAD_PAD), lambda h, s, ii, jj: (jj[s], off + h))
    stat = pl.BlockSpec((1, 1, t), lambda h, s, ii, jj: (h, 0, ii[s]))
    wide = (S, HEADS * HEAD_PAD)
    out_specs = [pl.BlockSpec((S, HEAD_PAD), lambda h, s, ii, jj: (0, h)), kblk(0), kblk(0)]
    out_shape = [jax.ShapeDtypeStruct(wide, dq_dtype), jax.ShapeDtypeStruct(wide, BF16),
                 jax.ShapeDtypeStruct(wide, BF16)]
    if bias_grads:
        out_specs += [pl.BlockSpec((1, 1, S), lambda h, s, ii, jj: (h, 0, 0)),
                      pl.BlockSpec((1, 1, t), lambda h, s, ii, jj: (h, 0, jj[s]))]
        out_shape += [jax.ShapeDtypeStruct((HEADS, 1, S), F32)] * 2
    grid_spec = pltpu.PrefetchScalarGridSpec(
        num_scalar_prefetch=2, grid=(HEADS, ii.shape[0]),
        in_specs=[qblk(q_off), kblk(k_off), kblk(v_off), qblk(0), stat, stat],
        out_specs=out_specs,
        scratch_shapes=[pltpu.VMEM((t, t), F32), pltpu.VMEM((t, t), F32), pltpu.VMEM((t, HEAD_PAD), F32),
                        pltpu.VMEM((t, HEAD_PAD), F32), pltpu.VMEM((S, HEAD_PAD), F32)])
    return pl.pallas_call(
        body, name=name, grid_spec=grid_spec, out_shape=out_shape, compiler_params=_cparams(),
    )(ii, jj, q_arr, k_arr, v_arr, do, lse, delta)


BIAS_LANE = FOX_HEAD


def _fox_expand(qkv, cum, name):
    S = qkv.shape[0]
    t = _tile(S, 256)
    narrow, wide = HEADS * FOX_HEAD, HEADS * HEAD_PAD

    def body(q_ref, k_ref, v_ref, c_ref, qo_ref, ko_ref, vo_ref):
        lane = lax.broadcasted_iota(jnp.int32, (t, HEAD_PAD), 1)
        cum_all = c_ref[...]

        def head_tile(ref, h):
            x = ref[:, pl.ds((h // 2) * HEAD_PAD, HEAD_PAD)].astype(F32)
            if h % 2:
                x = pltpu.roll(x, FOX_HEAD, 1)
            return jnp.where(lane < FOX_HEAD, x, 0.0)

        def fill(out, first, second):
            for n_, (a, b) in enumerate(zip(first, second)):
                out = jnp.where(lane == BIAS_LANE + n_, a, out)
                out = jnp.where(lane == BIAS_LANE + 3 + n_, b, out)
            return out.astype(BF16)

        ones = [1.0, 1.0, 1.0]
        for h in range(HEADS):
            cols = pl.ds(h * HEAD_PAD, HEAD_PAD)
            c = jnp.sum(jnp.where(lane == h, cum_all, 0.0), axis=1, keepdims=True)
            pieces = [x.astype(F32) for x in _split3(c)]
            qo_ref[:, cols] = fill(head_tile(q_ref, h), ones, pieces)
            ko_ref[:, cols] = fill(head_tile(k_ref, h), [-x for x in pieces], ones)
            vo_ref[:, cols] = head_tile(v_ref, h).astype(BF16)

    part = lambda p: pl.BlockSpec((t, narrow), lambda i: (i, p))
    out = pl.BlockSpec((t, wide), lambda i: (i, 0))
    shp = jax.ShapeDtypeStruct((S, wide), BF16)
    return pl.pallas_call(
        body, name=name, out_shape=[shp, shp, shp], grid=(S // t,),
        in_specs=[part(0), part(1), part(2), pl.BlockSpec((t, LANE), lambda i: (i, 0))],
        out_specs=[out, out, out], compiler_params=_cparams(),
    )(qkv, qkv, qkv, cum)


def _fox_compress(dq, dk, dv, name):
    S = dq.shape[0]
    t = _tile(S, 256)
    narrow, wide = HEADS * FOX_HEAD, HEADS * HEAD_PAD

    def body(dq_ref, dk_ref, dv_ref, o_ref):
        lane = lax.broadcasted_iota(jnp.int32, (t, HEAD_PAD), 1)
        for p, ref in enumerate((dq_ref, dk_ref, dv_ref)):
            for g in range(HEADS // 2):
                even = ref[:, pl.ds(2 * g * HEAD_PAD, HEAD_PAD)].astype(F32)
                odd = ref[:, pl.ds((2 * g + 1) * HEAD_PAD, HEAD_PAD)].astype(F32)
                both = jnp.where(lane < FOX_HEAD, even, pltpu.roll(odd, FOX_HEAD, 1))
                o_ref[:, pl.ds(p * narrow + g * HEAD_PAD, HEAD_PAD)] = both.astype(BF16)

    blk = pl.BlockSpec((t, wide), lambda i: (i, 0))
    return pl.pallas_call(
        body, name=name, out_shape=jax.ShapeDtypeStruct((S, 3 * narrow), BF16), grid=(S // t,),
        in_specs=[blk, blk, blk], out_specs=pl.BlockSpec((t, 3 * narrow), lambda i: (i, 0)),
        compiler_params=_cparams(),
    )(dq, dk, dv)


def _split3(x):
    hi = x.astype(BF16)
    r1 = x - hi.astype(F32)
    mid = r1.astype(BF16)
    lo = (r1 - mid.astype(F32)).astype(BF16)
    return hi, mid, lo


def _tri_matmul(tri, x):
    acc = None
    for part in _split3(x):
        d = lax.dot_general(tri, part, NN, preferred_element_type=F32)
        acc = d if acc is None else acc + d
    return acc


def _log_sigmoid(z):
    return jnp.minimum(z, 0.0) - jnp.log1p(jnp.exp(-jnp.abs(z)))


def _fox_gate_fwd(fl, bf):
    S = fl.shape[0]
    t = _tile(S, 256)

    def body(fl_ref, bf_ref, cum_ref, carry):
        @pl.when(pl.program_id(0) == 0)
        def _():
            carry[...] = jnp.zeros_like(carry)

        logf = _log_sigmoid(fl_ref[...] + bf_ref[...])
        tri = (lax.broadcasted_iota(jnp.int32, (t, t), 0) >= lax.broadcasted_iota(jnp.int32, (t, t), 1)).astype(BF16)
        cum_ref[...] = _tri_matmul(tri, logf) + carry[...]
        carry[...] += jnp.sum(logf, axis=0, keepdims=True)

    row = pl.BlockSpec((t, LANE), lambda i: (i, 0))
    return pl.pallas_call(
        body, name="fox_gate_fwd", out_shape=jax.ShapeDtypeStruct((S, LANE), F32), grid=(S // t,),
        in_specs=[row, pl.BlockSpec((1, LANE), lambda i: (0, 0))], out_specs=row,
        scratch_shapes=[pltpu.VMEM((1, LANE), F32)], compiler_params=_cparams(),
    )(fl, bf)


def _fox_gate_bwd(dcum, fl, bf):
    S = fl.shape[0]
    t = _tile(S, 256)
    n = S // t

    def body(dc_ref, fl_ref, bf_ref, df_ref, db_ref, carry):
        @pl.when(pl.program_id(0) == 0)
        def _():
            carry[...] = jnp.zeros_like(carry)
            db_ref[...] = jnp.zeros_like(db_ref)

        dc = dc_ref[...]
        tri = (lax.broadcasted_iota(jnp.int32, (t, t), 1) >= lax.broadcasted_iota(jnp.int32, (t, t), 0)).astype(BF16)
        dlogf = _tri_matmul(tri, dc) + carry[...]
        carry[...] += jnp.sum(dc, axis=0, keepdims=True)
        z = fl_ref[...] + bf_ref[...]
        e = jnp.exp(-jnp.abs(z))
        sig_neg = jnp.where(z >= 0, e, 1.0) / (1.0 + e)
        lanes = lax.broadcasted_iota(jnp.int32, (t, LANE), 1)
        df = jnp.where(lanes < HEADS, dlogf * sig_neg, 0.0)
        df_ref[...] = df
        db_ref[...] += jnp.sum(df, axis=0, keepdims=True)

    row = pl.BlockSpec((t, LANE), lambda i: (n - 1 - i, 0))
    vec = pl.BlockSpec((1, LANE), lambda i: (0, 0))
    return pl.pallas_call(
        body, name="fox_gate_bwd",
        out_shape=[jax.ShapeDtypeStruct((S, LANE), F32), jax.ShapeDtypeStruct((1, LANE), F32)],
        grid=(n,), in_specs=[row, row, vec], out_specs=[row, vec],
        scratch_shapes=[pltpu.VMEM((1, LANE), F32)], compiler_params=_cparams(),
    )(dcum, fl, bf)


def _rope_tables(S):
    pos = jnp.arange(S, dtype=F32)
    inv_freq = ROPE_BASE ** (-jnp.arange(0, MLA_ROPE, 2, dtype=F32) / MLA_ROPE)
    ang = pos[:, None] * inv_freq[None, :]
    cos, sin = jnp.cos(ang), jnp.sin(ang)
    half = MLA_ROPE // 2
    ones = jnp.ones((S, MLA_NOPE), F32)
    z = lambda w: jnp.zeros((S, w), F32)
    keep = jnp.concatenate([ones, cos, cos, z(HEAD_PAD - MLA_NOPE - MLA_ROPE)], axis=1)
    from_above = jnp.concatenate([z(MLA_NOPE), -sin, z(HEAD_PAD - MLA_NOPE - half)], axis=1)
    from_below = jnp.concatenate([z(MLA_NOPE + half), sin, z(HEAD_PAD - MLA_NOPE - MLA_ROPE)], axis=1)
    return keep, from_above, from_below


def _rope(x, keep, up, down, transpose):
    half = MLA_ROPE // 2
    if transpose:
        return x * keep + pltpu.roll(x * up, half, 1) + pltpu.roll(x * down, HEAD_PAD - half, 1)
    return x * keep + pltpu.roll(x, HEAD_PAD - half, 1) * up + pltpu.roll(x, half, 1) * down


def _rope_heads(x, tables, transpose, scale, name):
    S = x.shape[0]
    t = _tile(S, 256)

    def body(x_ref, a_ref, b_ref, c_ref, o_ref):
        keep, up, down = a_ref[...], b_ref[...], c_ref[...]
        for h in range(HEADS):
            cols = pl.ds(h * HEAD_PAD, HEAD_PAD)
            o_ref[:, cols] = (_rope(x_ref[:, cols], keep, up, down, transpose) * scale).astype(BF16)

    blk = pl.BlockSpec((t, HEADS * HEAD_PAD), lambda i: (i, 0))
    tab = pl.BlockSpec((t, HEAD_PAD), lambda i: (i, 0))
    return pl.pallas_call(
        body, name=name, out_shape=jax.ShapeDtypeStruct(x.shape, BF16), grid=(S // t,),
        in_specs=[blk, tab, tab, tab], out_specs=blk, compiler_params=_cparams(),
    )(x, *tables)


MLA_A_PAD = MLA_QR + MLA_KVR + HEAD_PAD
MLA_KV_IN = MLA_KVR + HEAD_PAD


def _mla_mid_fwd(a, gq, gkv, tables):
    S = a.shape[0]
    t = _tile(S, 512)

    def body(a_ref, gq_ref, gkv_ref, ta_ref, tb_ref, tc_ref, q_ref, kv_ref):
        cq = a_ref[:, :MLA_QR]
        ckv = a_ref[:, MLA_QR:MLA_QR + MLA_KVR]
        kr = a_ref[:, MLA_QR + MLA_KVR:]
        q_ref[...] = (cq * _rstd(cq) * gq_ref[...]).astype(BF16)
        kv_ref[:, :MLA_KVR] = (ckv * _rstd(ckv) * gkv_ref[...]).astype(BF16)
        kv_ref[:, MLA_KVR:] = _rope(kr, ta_ref[...], tb_ref[...], tc_ref[...], False).astype(BF16)

    tab = pl.BlockSpec((t, HEAD_PAD), lambda i: (i, 0))
    return pl.pallas_call(
        body, name="mla_mid_fwd",
        out_shape=[jax.ShapeDtypeStruct((S, MLA_QR), BF16), jax.ShapeDtypeStruct((S, MLA_KV_IN), BF16)],
        grid=(S // t,),
        in_specs=[pl.BlockSpec((t, MLA_A_PAD), lambda i: (i, 0)), pl.BlockSpec((1, MLA_QR), lambda i: (0, 0)),
                  pl.BlockSpec((1, MLA_KVR), lambda i: (0, 0)), tab, tab, tab],
        out_specs=[pl.BlockSpec((t, MLA_QR), lambda i: (i, 0)), pl.BlockSpec((t, MLA_KV_IN), lambda i: (i, 0))],
        compiler_params=_cparams(),
    )(a, gq, gkv, *tables)


def _mla_mid_bwd(a, dq, dkv, gq, gkv, tables):
    S = a.shape[0]
    t = _tile(S, 512)

    def body(a_ref, dq_ref, dkv_ref, gq_ref, gkv_ref, ta_ref, tb_ref, tc_ref, da_ref, dgq_ref, dgkv_ref):
        @pl.when(pl.program_id(0) == 0)
        def _():
            dgq_ref[...] = jnp.zeros_like(dgq_ref)
            dgkv_ref[...] = jnp.zeros_like(dgkv_ref)

        def one(c, dout, g):
            r = _rstd(c)
            n = c * r
            return _norm_bwd(n, r, dout * g), jnp.sum(dout * n, axis=0, keepdims=True)

        dcq, dgq = one(a_ref[:, :MLA_QR], dq_ref[...], gq_ref[...])
        dckv, dgkv = one(a_ref[:, MLA_QR:MLA_QR + MLA_KVR], dkv_ref[:, :MLA_KVR], gkv_ref[...])
        da_ref[:, :MLA_QR] = dcq.astype(BF16)
        da_ref[:, MLA_QR:MLA_QR + MLA_KVR] = dckv.astype(BF16)
        da_ref[:, MLA_QR + MLA_KVR:] = _rope(dkv_ref[:, MLA_KVR:], ta_ref[...], tb_ref[...], tc_ref[...],
                                             True).astype(BF16)
        dgq_ref[...] += dgq
        dgkv_ref[...] += dgkv

    tab = pl.BlockSpec((t, HEAD_PAD), lambda i: (i, 0))
    vq = pl.BlockSpec((1, MLA_QR), lambda i: (0, 0))
    vkv = pl.BlockSpec((1, MLA_KVR), lambda i: (0, 0))
    return pl.pallas_call(
        body, name="mla_mid_bwd",
        out_shape=[jax.ShapeDtypeStruct((S, MLA_A_PAD), BF16), jax.ShapeDtypeStruct((1, MLA_QR), F32),
                   jax.ShapeDtypeStruct((1, MLA_KVR), F32)],
        grid=(S // t,),
        in_specs=[pl.BlockSpec((t, MLA_A_PAD), lambda i: (i, 0)), pl.BlockSpec((t, MLA_QR), lambda i: (i, 0)),
                  pl.BlockSpec((t, MLA_KV_IN), lambda i: (i, 0)), vq, vkv, tab, tab, tab],
        out_specs=[pl.BlockSpec((t, MLA_A_PAD), lambda i: (i, 0)), vq, vkv],
        compiler_params=_cparams(),
    )(a, dq, dkv, gq, gkv, *tables)


FFN_ROWS = 256
FFN_COLS = 1408
HALO = 8


def _shift_down(cur, halo, s, first):
    rolled = pltpu.roll(cur, s, 0)
    hrolled = jnp.where(first, 0.0, pltpu.roll(halo, s, 0))
    row = lax.broadcasted_iota(jnp.int32, (HALO, cur.shape[1]), 0)
    top = jnp.where(row < s, hrolled, rolled[:HALO])
    return jnp.concatenate([top, rolled[HALO:]], axis=0)


def _shift_up(cur, halo, s, last):
    rows = cur.shape[0]
    rolled = pltpu.roll(cur, rows - s, 0)
    hrolled = jnp.where(last, 0.0, pltpu.roll(halo, HALO - s, 0))
    row = lax.broadcasted_iota(jnp.int32, (HALO, cur.shape[1]), 0)
    bottom = jnp.where(row >= HALO - s, hrolled, rolled[rows - HALO:])
    return jnp.concatenate([rolled[:rows - HALO], bottom], axis=0)


def _conv(u, halo, w_ref, b, first):
    u1 = _shift_down(u, halo, 1, first)
    u2 = _shift_down(u, halo, 2, first)
    y = b + w_ref[0:1, :] * u2
    y = y + w_ref[1:2, :] * u1
    y = y + w_ref[2:3, :] * u
    return y, u1, u2


def _ffn_specs(S, tr, tn):
    per = tr // HALO
    tile = pl.BlockSpec((tr, tn), lambda j, i: (i, j))
    before = pl.BlockSpec((HALO, tn), lambda j, i: (jnp.maximum(i * per - 1, 0), j))
    after = pl.BlockSpec((HALO, tn), lambda j, i: (jnp.minimum((i + 1) * per, S // HALO - 1), j))
    w3 = pl.BlockSpec((3, tn), lambda j, i: (0, j))
    w1 = pl.BlockSpec((1, tn), lambda j, i: (0, j))
    return tile, before, after, w3, w1


def _ffn_act_fwd(ug, uv, wg, wv, bg, bv, name):
    S, F = ug.shape
    tr, tn = _tile(S, FFN_ROWS), _tile(F, FFN_COLS)
    tile, before, _, w3, w1 = _ffn_specs(S, tr, tn)

    def body(ug_ref, uv_ref, hg_ref, hv_ref, wg_ref, wv_ref, bg_ref, bv_ref, g_ref, v_ref, o_ref):
        first = pl.program_id(1) == 0
        g, _, _ = _conv(ug_ref[...], hg_ref[...], wg_ref, bg_ref[...], first)
        v, _, _ = _conv(uv_ref[...], hv_ref[...], wv_ref, bv_ref[...], first)
        g_ref[...] = g
        v_ref[...] = v
        o_ref[...] = (g * jax.nn.sigmoid(g) * v).astype(BF16)

    f32 = jax.ShapeDtypeStruct((S, F), F32)
    return pl.pallas_call(
        body, name=name, out_shape=[f32, f32, jax.ShapeDtypeStruct((S, F), BF16)], grid=(F // tn, S // tr),
        in_specs=[tile, tile, before, before, w3, w3, w1, w1], out_specs=[tile, tile, tile],
        compiler_params=_cparams(),
    )(ug, uv, ug, uv, wg, wv, bg, bv)


def _gate_grads(g, v, da):
    sg = jax.nn.sigmoid(g)
    return da * v * (sg * (1.0 + g * (1.0 - sg))), da * (g * sg)


def _ffn_bwd_mid(ug, uv, g, v, dact, wg, wv, name):
    S, F = ug.shape
    tr, tn = _tile(S, FFN_ROWS), _tile(F, FFN_COLS)
    tile, _, after, w3, w1 = _ffn_specs(S, tr, tn)
    n = S // tr

    def body(ug_ref, uv_ref, g_ref, v_ref, ag_ref, av_ref, da_ref, ada_ref, wg_ref, wv_ref,
             dug_ref, duv_ref, dwg_ref, dwv_ref, dbg_ref, dbv_ref):
        last = pl.program_id(1) == n - 1

        @pl.when(pl.program_id(1) == 0)
        def _():
            for r in (dwg_ref, dwv_ref, dbg_ref, dbv_ref):
                r[...] = jnp.zeros_like(r)

        dyg, dyv = _gate_grads(g_ref[...], v_ref[...], da_ref[...])
        dyg_after, dyv_after = _gate_grads(ag_ref[...], av_ref[...], ada_ref[...])

        def back(dy, dy_after, u_ref, w_ref, du_ref, dw_ref, db_ref):
            u = u_ref[...]
            d1 = _shift_up(dy, dy_after, 1, last)
            d2 = _shift_up(dy, dy_after, 2, last)
            du_ref[...] = (w_ref[2:3, :] * dy + w_ref[1:2, :] * d1 + w_ref[0:1, :] * d2).astype(BF16)
            for k, d in enumerate((d2, d1, dy)):
                dw_ref[k:k + 1, :] += jnp.sum(d * u, axis=0, keepdims=True)
            db_ref[...] += jnp.sum(dy, axis=0, keepdims=True)

        back(dyg, dyg_after, ug_ref, wg_ref, dug_ref, dwg_ref, dbg_ref)
        back(dyv, dyv_after, uv_ref, wv_ref, duv_ref, dwv_ref, dbv_ref)

    big = jax.ShapeDtypeStruct((S, F), BF16)
    s3, s1 = jax.ShapeDtypeStruct((3, F), F32), jax.ShapeDtypeStruct((1, F), F32)
    return pl.pallas_call(
        body, name=name, out_shape=[big, big, s3, s3, s1, s1], grid=(F // tn, n),
        in_specs=[tile, tile, tile, tile, after, after, tile, after, w3, w3],
        out_specs=[tile, tile, w3, w3, w1, w1], compiler_params=_cparams(),
    )(ug, uv, g, v, g, v, dact, dact, wg, wv)


def _dot3(a, b):
    a_hi = a.astype(BF16)
    a_lo = (a - a_hi.astype(F32)).astype(BF16)
    b_hi = b.astype(BF16)
    b_lo = (b - b_hi.astype(F32)).astype(BF16)
    d = lambda p, q: lax.dot_general(p, q, NN, preferred_element_type=F32)
    return d(a_hi, b_hi) + (d(a_hi, b_lo) + d(a_lo, b_hi))


def _ada_mod(c_all, w, b):
    nmod, D, n = w.shape

    def body(c_ref, w_ref, b_ref, o_ref):
        cv = c_ref[...]
        o_ref[0] = _dot3(cv * jax.nn.sigmoid(cv), w_ref[0]) + b_ref[0]

    return pl.pallas_call(
        body, name="ada_mod", out_shape=jax.ShapeDtypeStruct((nmod, 8, n), F32), grid=(nmod,),
        in_specs=[pl.BlockSpec((8, D), lambda m: (0, 0)), pl.BlockSpec((1, D, n), lambda m: (m, 0, 0)),
                  pl.BlockSpec((1, 1, n), lambda m: (m, 0, 0))],
        out_specs=pl.BlockSpec((1, 8, n), lambda m: (m, 0, 0)), compiler_params=_cparams(),
    )(c_all, w, b)


def _ada_grad(c_all_t, dmod):
    D = c_all_t.shape[0]
    nmod, _, n = dmod.shape
    tr = 256

    def body(c_ref, d_ref, dw_ref, db_ref):
        cv = c_ref[...]
        sc = cv * jax.nn.sigmoid(cv)
        dv = d_ref[0]
        acc = sc[:, 0:1] * dv[0:1, :]
        tot = dv[0:1, :]
        for k in range(1, 8):
            acc = acc + sc[:, k:k + 1] * dv[k:k + 1, :]
            tot = tot + dv[k:k + 1, :]
        dw_ref[0] = acc
        db_ref[0] = tot

    return pl.pallas_call(
        body, name="ada_grad",
        out_shape=[jax.ShapeDtypeStruct((nmod, D, n), F32), jax.ShapeDtypeStruct((nmod, 1, n), F32)],
        grid=(nmod, D // tr),
        in_specs=[pl.BlockSpec((tr, 8), lambda m, i: (i, 0)), pl.BlockSpec((1, 8, n), lambda m, i: (m, 0, 0))],
        out_specs=[pl.BlockSpec((1, tr, n), lambda m, i: (m, i, 0)), pl.BlockSpec((1, 1, n), lambda m, i: (m, 0, 0))],
        compiler_params=_cparams(),
    )(c_all_t, dmod)


def _adamw(w, g, m, v, name):
    R, C = w.shape
    t = _row_tile(R, C)

    def body(w_ref, g_ref, m_ref, v_ref, d_ref, nm_ref, nv_ref):
        gv = g_ref[...]
        mn = ADAM_B1 * m_ref[...] + (1.0 - ADAM_B1) * gv
        vn = ADAM_B2 * v_ref[...] + (1.0 - ADAM_B2) * (gv * gv)
        m_hat = mn / (1.0 - ADAM_B1 ** ADAM_STEP)
        v_hat = vn / (1.0 - ADAM_B2 ** ADAM_STEP)
        d_ref[...] = -ADAM_LR * (m_hat / (jnp.sqrt(v_hat) + ADAM_EPS) + ADAM_WD * w_ref[...])
        nm_ref[...] = mn
        nv_ref[...] = vn

    blk = pl.BlockSpec((t, C), lambda i: (i, 0))
    shp = jax.ShapeDtypeStruct((R, C), F32)
    return pl.pallas_call(
        body, name=name, out_shape=[shp, shp, shp], grid=(R // t,), in_specs=[blk] * 4, out_specs=[blk] * 3,
        compiler_params=_cparams(),
    )(w, g, m, v)


def _cat_cols(g4):
    return jnp.concatenate([g4[s] for s in range(4)], axis=-1)


def _cat_rows(g4):
    return jnp.concatenate([g4[s] for s in range(4)], axis=-2)


def _pad_head_cols(w, width):
    K = w.shape[0]
    w = w.reshape(K, HEADS, width)
    return jnp.pad(w, ((0, 0), (0, 0), (0, HEAD_PAD - width))).reshape(K, HEADS * HEAD_PAD)


def _unpad_head_cols(w, width):
    K = w.shape[0]
    return w.reshape(K, HEADS, HEAD_PAD)[:, :, :width].reshape(K, HEADS * width)


def _pad_head_rows(w, width):
    N = w.shape[1]
    w = w.reshape(HEADS, width, N)
    return jnp.pad(w, ((0, 0), (0, HEAD_PAD - width), (0, 0))).reshape(HEADS * HEAD_PAD, N)


def _unpad_head_rows(w, width):
    N = w.shape[1]
    return w.reshape(HEADS, HEAD_PAD, N)[:, :width, :].reshape(HEADS * width, N)


def _prepare_fox_in(fox_in):
    W = {}
    w = _cat_cols(fox_in[:, 0])
    hw = HEADS * FOX_HEAD
    q = w[:, :hw] * FOX_SCALE
    W["fox_qkv"] = jnp.concatenate([q, w[:, hw:3 * hw]], axis=1)
    W["fox_f"] = jnp.pad(w[:, 3 * hw:], ((0, 0), (0, LANE - HEADS)))
    return W


def _prepare_weights(gathered):
    fox_o, mla_a, mla_uq, mla_ukv, mla_o, ffn_in, ffn_out = gathered
    W = {}
    W["fox_o"] = _pad_head_rows(_cat_rows(fox_o[:, 0]), FOX_HEAD)
    w = _cat_rows(mla_a[:, 0])
    lat = MLA_QR + MLA_KVR
    W["mla_a"] = jnp.concatenate(
        [w[:, :lat], jnp.zeros((D_MODEL, MLA_NOPE), BF16), w[:, lat:],
         jnp.zeros((D_MODEL, HEAD_PAD - MLA_NOPE - MLA_ROPE), BF16)], axis=1)
    W["mla_uq"] = _pad_head_cols(_cat_cols(mla_uq[:, 0]), MLA_NOPE + MLA_ROPE)
    w = _cat_cols(mla_ukv[:, 0]).reshape(MLA_KVR, HEADS, MLA_NOPE + MLA_V)
    kn = _pad_head_cols(w[:, :, :MLA_NOPE].reshape(MLA_KVR, -1), MLA_NOPE)
    vv = _pad_head_cols(w[:, :, MLA_NOPE:].reshape(MLA_KVR, -1), MLA_V)
    eye = np.zeros((HEAD_PAD, HEADS, HEAD_PAD), np.float32)
    for j in range(MLA_NOPE, MLA_NOPE + MLA_ROPE):
        eye[j, :, j] = 1.0
    eye = jnp.asarray(eye.reshape(HEAD_PAD, HEADS * HEAD_PAD), BF16)
    bottom = jnp.concatenate([eye, jnp.zeros((HEAD_PAD, HEADS * HEAD_PAD), BF16)], axis=1)
    W["mla_kv"] = jnp.concatenate([jnp.concatenate([kn, vv], axis=1), bottom], axis=0)
    W["mla_o"] = _pad_head_rows(_cat_rows(mla_o[:, 0]), MLA_V)
    w = _cat_cols(ffn_in)
    W["ffn_g"] = [w[i, :, :D_FF] for i in range(2)]
    W["ffn_v"] = [w[i, :, D_FF:] for i in range(2)]
    w = _cat_rows(ffn_out)
    W["ffn_out"] = [w[i] for i in range(2)]
    return W


def _ffn_forward(xin, h, mod, W, i, conv_w, conv_b, next_mod=None):
    shift, scale, gate = mod
    ug = _mm(h, W["ffn_g"][i], name=f"ffn{i}_up_gate")
    uv = _mm(h, W["ffn_v"][i], name=f"ffn{i}_up_val")
    wg, wv = conv_w[i][:, :D_FF], conv_w[i][:, D_FF:]
    bg, bv = conv_b[i][None, :D_FF], conv_b[i][None, D_FF:]
    g, v, act = _ffn_act_fwd(ug, uv, wg, wv, bg, bv, name=f"ffn{i}_act")
    extra = () if next_mod is None else (next_mod[1], next_mod[0])
    y, xout, *h_next = _mm(act, W["ffn_out"][i], gated=(xin, gate, *extra), name=f"ffn{i}_down")
    return xout, (xin, h, ug, uv, g, v, act, y, wg, wv), (h_next[0] if h_next else None)


def _ffn_backward(dx_out, saved, mod, W, i):
    xin, h, ug, uv, g, v, act, y, wg, wv = saved
    shift, scale, gate = mod
    dy, dgate = _gate_bwd(dx_out, y, gate, name=f"ffn{i}_gate_bwd")
    d_out = _mm(act, dy, ta=True, name=f"ffn{i}_dw_out")
    dact = _mm(dy, W["ffn_out"][i], tb=True, name=f"ffn{i}_dact")
    dug, duv, dwg, dwv, dbg, dbv = _ffn_bwd_mid(ug, uv, g, v, dact, wg, wv, name=f"ffn{i}_bwd_mid")
    d_in = jnp.concatenate([_mm(h, dug, ta=True, name=f"ffn{i}_dw_gate"),
                            _mm(h, duv, ta=True, name=f"ffn{i}_dw_val")], axis=1)
    dh = _mm(dug, W["ffn_g"][i], tb=True, name=f"ffn{i}_dh_gate")
    dh = _mm(duv, W["ffn_v"][i], tb=True, res=dh, name=f"ffn{i}_dh_val")
    dx, dscale, dshift = _adaln_bwd(xin, dh, dx_out, scale, name=f"ffn{i}_adaln_bwd")
    grads = dict(ffn_w_in=d_in, ffn_w_out=d_out, ffn_conv_w=jnp.concatenate([dwg, dwv], axis=1),
                 ffn_conv_b=jnp.concatenate([dbg, dbv], axis=1)[0])
    return dx, jnp.concatenate([dshift, dscale, dgate], axis=1)[0], grads


def _local_step(x, target, mods, W, small):
    S = x.shape[0]
    tables = _rope_tables(S)
    wide = HEADS * HEAD_PAD
    bf = jnp.pad(small["fox_b_f"], ((0, 0), (0, LANE - HEADS)))
    gq, gkv = small["mla_g_q"], small["mla_g_kv"]

    shift, scale, gate = mods[0]
    h0 = _adaln_fwd(x, scale, shift, name="fox_adaln")
    qkv = _mm(h0, W["fox_qkv"], out_dtype=BF16, name="fox_qkv")
    fl = _mm(h0, W["fox_f"], name="fox_gate_logits")
    cum = _fox_gate_fwd(fl, bf)
    q_fox, k_fox, v_fox = _fox_expand(qkv, cum, name="fox_expand")
    o0, lse0, gathered = _attn_fwd(q_fox, 0, k_fox, 0, v_fox, 0, name="fox_attn_fwd", gather=W["pending"])
    W = {**W, **_prepare_weights(gathered)}
    y0, x1, h1 = _mm(o0, W["fox_o"], gated=(x, gate, mods[1][1], mods[1][0]), name="fox_out")
    x2, ffn0, h2 = _ffn_forward(x1, h1, mods[1], W, 0, small["ffn_conv_w"], small["ffn_conv_b"], next_mod=mods[2])

    shift, scale, gate = mods[2]
    a = _mm(h2, W["mla_a"], name="mla_down")
    cqn, ckv_in = _mla_mid_fwd(a, gq, gkv, tables)
    q_raw = _mm(cqn, W["mla_uq"], name="mla_up_q")
    q_cat = _rope_heads(q_raw, tables, False, MLA_SCALE, name="mla_rope_q")
    kv = _mm(ckv_in, W["mla_kv"], out_dtype=BF16, name="mla_up_kv")
    o1, lse1, _ = _attn_fwd(q_cat, 0, kv, 0, kv, HEADS, name="mla_attn_fwd")
    y2, x3, h3 = _mm(o1, W["mla_o"], gated=(x2, gate, mods[3][1], mods[3][0]), name="mla_out")
    x4, ffn1, _ = _ffn_forward(x3, h3, mods[3], W, 1, small["ffn_conv_w"], small["ffn_conv_b"])

    loss, dx4, d_final_g = _final_loss(x4, small["final_g"], target)

    dx3, dmod3, g_ffn1 = _ffn_backward(dx4, ffn1, mods[3], W, 1)

    shift, scale, gate = mods[2]
    dy, dgate = _gate_bwd(dx3, y2, gate, name="mla_gate_bwd")
    d_mla_o = _mm(o1, dy, ta=True, name="mla_dw_out")
    do = _mm(dy, W["mla_o"], tb=True, out_dtype=BF16, name="mla_do")
    delta = _attn_delta(do, o1, name="mla_attn_delta")
    dq, dk, dv = _attn_bwd(q_cat, 0, kv, 0, kv, HEADS, do, lse1.reshape(HEADS, 1, S),
                           delta, name="mla_attn_bwd")
    dq_raw = _rope_heads(dq, tables, True, MLA_SCALE, name="mla_rope_q_bwd")
    d_mla_uq = _mm(cqn, dq_raw, ta=True, name="mla_dw_uq")
    dcqn = _mm(dq_raw, W["mla_uq"], tb=True, name="mla_dcq")
    dkv = jnp.concatenate([dk, dv], axis=1)
    d_mla_kv = _mm(ckv_in, dkv, ta=True, name="mla_dw_kv")
    dckv_in = _mm(dkv, W["mla_kv"], tb=True, name="mla_dckv")
    da, dgq, dgkv = _mla_mid_bwd(a, dcqn, dckv_in, gq, gkv, tables)
    d_mla_a = _mm(h2, da, ta=True, name="mla_dw_a")
    dh = _mm(da, W["mla_a"], tb=True, name="mla_dh")
    dx2, dscale, dshift = _adaln_bwd(x2, dh, dx3, scale, name="mla_adaln_bwd")
    dmod2 = jnp.concatenate([dshift, dscale, dgate], axis=1)[0]

    dx1, dmod1, g_ffn0 = _ffn_backward(dx2, ffn0, mods[1], W, 0)

    shift, scale, gate = mods[0]
    dy, dgate = _gate_bwd(dx1, y0, gate, name="fox_gate_bwd")
    d_fox_o = _mm(o0, dy, ta=True, name="fox_dw_out")
    do = _mm(dy, W["fox_o"], tb=True, out_dtype=BF16, name="fox_do")
    delta = _attn_delta(do, o0, name="fox_attn_delta")
    dq, dk, dv, dcq, dck = _attn_bwd(q_fox, 0, k_fox, 0, v_fox, 0, do, lse0.reshape(HEADS, 1, S),
                                     delta, name="fox_attn_bwd", bias_grads=True, dq_dtype=BF16)
    dcum = jnp.pad((dcq[:, 0, :] - dck[:, 0, :]).T, ((0, 0), (0, LANE - HEADS)))
    dfl, dbf = _fox_gate_bwd(dcum, fl, bf)
    dqkv = _fox_compress(dq, dk, dv, name="fox_compress")
    d_fox_qkv = _mm(h0, dqkv, ta=True, name="fox_dw_in")
    d_fox_f = _mm(h0, dfl, ta=True, name="fox_dw_gate")
    dh = _mm(dqkv, W["fox_qkv"], tb=True, name="fox_dh")
    dh = _mm(dfl, W["fox_f"], tb=True, res=dh, name="fox_dh_gate")
    dx0, dscale, dshift = _adaln_bwd(x, dh, dx1, scale, name="fox_adaln_bwd")
    dmod0 = jnp.concatenate([dshift, dscale, dgate], axis=1)[0]

    G = {}
    hw = HEADS * FOX_HEAD
    G["fox_w_in"] = jnp.concatenate([d_fox_qkv[:, :hw] * FOX_SCALE,
                                     d_fox_qkv[:, hw:], d_fox_f[:, :HEADS]], axis=1)[None]
    G["fox_b_f"] = dbf[:, :HEADS]
    G["fox_w_o"] = _unpad_head_rows(d_fox_o, FOX_HEAD)[None]
    lat = MLA_QR + MLA_KVR
    G["mla_w_a"] = jnp.concatenate([d_mla_a[:, :lat], d_mla_a[:, lat + MLA_NOPE:lat + MLA_NOPE + MLA_ROPE]],
                                   axis=1)[None]
    G["mla_g_q"] = dgq
    G["mla_g_kv"] = dgkv
    G["mla_w_uq"] = _unpad_head_cols(d_mla_uq, MLA_NOPE + MLA_ROPE)[None]
    dkn = d_mla_kv[:MLA_KVR, :wide].reshape(MLA_KVR, HEADS, HEAD_PAD)[:, :, :MLA_NOPE]
    dvv = d_mla_kv[:MLA_KVR, wide:].reshape(MLA_KVR, HEADS, HEAD_PAD)[:, :, :MLA_V]
    G["mla_w_ukv"] = jnp.concatenate([dkn, dvv], axis=2).reshape(MLA_KVR, -1)[None]
    G["mla_w_o"] = _unpad_head_rows(d_mla_o, MLA_V)[None]
    for name in ("ffn_w_in", "ffn_w_out", "ffn_conv_w", "ffn_conv_b"):
        G[name] = jnp.stack([g_ffn0[name], g_ffn1[name]])
    G["final_g"] = d_final_g[0]
    dmods = jnp.stack([dmod0, dmod1, dmod2, dmod3])
    return loss, dx0, dmods, G


PACKED = [("fox_w_in", 2), ("fox_w_o", 1), ("mla_w_a", 1), ("mla_w_uq", 2), ("mla_w_ukv", 2), ("mla_w_o", 1),
          ("ffn_w_in", 2), ("ffn_w_out", 1), ("fox_b_f", None), ("mla_g_q", 1), ("mla_g_kv", 1),
          ("ffn_conv_w", 2), ("ffn_conv_b", None), ("final_g", None)]


def _shard_of(g, axis, s):
    if axis is None:
        return g
    n = g.shape[axis] // 4
    return lax.slice_in_dim(g, s * n, (s + 1) * n, axis=axis)


def _pack_plan(G):
    shard_size = lambda name, axis: math.prod(_shard_of(G[name], axis, 0).shape)
    big = [(name, axis) for name, axis in PACKED if shard_size(name, axis) % PACK_ROW == 0]
    small = [(name, axis) for name, axis in PACKED if shard_size(name, axis) % PACK_ROW != 0]
    whole_tiles = lambda size: -(-size // (8 * PACK_ROW)) * 8
    runs = [([item], whole_tiles(shard_size(*item))) for item in big]
    runs.append((small, whole_tiles(sum(shard_size(*item) for item in small))))
    used = sum(rows for _, rows in runs)
    unit = 2 * PACK_ROWS_ALIGN
    return runs, -(-used // unit) * unit


def _pack(G):
    runs, total = _pack_plan(G)
    pieces = []
    for items, rows in runs:
        per_chip = [jnp.concatenate([_shard_of(G[name], axis, s).reshape(-1) for name, axis in items])
                    for s in range(4)]
        flat = jnp.stack(per_chip)
        flat = jnp.pad(flat, ((0, 0), (0, rows * PACK_ROW - flat.shape[1])))
        pieces.append(flat.reshape(4, rows, PACK_ROW))
    used = sum(rows for _, rows in runs)
    if total > used:
        pieces.append(jnp.zeros((4, total - used, PACK_ROW), F32))
    return jnp.concatenate(pieces, axis=1).reshape(4, 2, total // 2, PACK_ROW)


def _unpack(both, G_like):
    runs, total = _pack_plan(G_like)
    table = both.reshape(total, PACK_ROW)
    out, row = {}, 0
    for items, rows in runs:
        flat, off = table[row:row + rows].reshape(-1), 0
        for name, axis in items:
            shape = _shard_of(G_like[name], axis, 0).shape
            out[name] = flat[off:off + math.prod(shape)].reshape(shape)
            off += math.prod(shape)
        row += rows
    return out


WEIGHTS = ['ada_w', 'ada_b', 'fox_w_in', 'fox_b_f', 'fox_w_o', 'mla_w_a', 'mla_g_q', 'mla_g_kv', 'mla_w_uq',
           'mla_w_ukv', 'mla_w_o', 'ffn_w_in', 'ffn_conv_w', 'ffn_conv_b', 'ffn_w_out', 'final_g']
BIG = ['fox_w_in', 'fox_w_o', 'mla_w_a', 'mla_w_uq', 'mla_w_ukv', 'mla_w_o', 'ffn_w_in', 'ffn_w_out']
SMALL = ['ada_b', 'fox_b_f', 'mla_g_q', 'mla_g_kv', 'ffn_conv_w', 'ffn_conv_b', 'final_g']


def _as2d(a):
    return a.reshape(-1, a.shape[-1])


def kernel(x, c, ada_w, ada_b, fox_w_in, fox_b_f, fox_w_o, mla_w_a, mla_g_q, mla_g_kv, mla_w_uq, mla_w_ukv, mla_w_o, ffn_w_in, ffn_conv_w, ffn_conv_b, ffn_w_out, final_g, loss_target, m_ada_w, m_ada_b, m_fox_w_in, m_fox_b_f, m_fox_w_o, m_mla_w_a, m_mla_g_q, m_mla_g_kv, m_mla_w_uq, m_mla_w_ukv, m_mla_w_o, m_ffn_w_in, m_ffn_conv_w, m_ffn_conv_b, m_ffn_w_out, m_final_g, v_ada_w, v_ada_b, v_fox_w_in, v_fox_b_f, v_fox_w_o, v_mla_w_a, v_mla_g_q, v_mla_g_kv, v_mla_w_uq, v_mla_w_ukv, v_mla_w_o, v_ffn_w_in, v_ffn_conv_w, v_ffn_conv_b, v_ffn_w_out, v_final_g):
    w = dict(ada_w=ada_w, ada_b=ada_b, fox_w_in=fox_w_in, fox_b_f=fox_b_f, fox_w_o=fox_w_o, mla_w_a=mla_w_a,
             mla_g_q=mla_g_q, mla_g_kv=mla_g_kv, mla_w_uq=mla_w_uq, mla_w_ukv=mla_w_ukv, mla_w_o=mla_w_o,
             ffn_w_in=ffn_w_in, ffn_conv_w=ffn_conv_w, ffn_conv_b=ffn_conv_b, ffn_w_out=ffn_w_out, final_g=final_g)
    m = dict(ada_w=m_ada_w, ada_b=m_ada_b, fox_w_in=m_fox_w_in, fox_b_f=m_fox_b_f, fox_w_o=m_fox_w_o,
             mla_w_a=m_mla_w_a, mla_g_q=m_mla_g_q, mla_g_kv=m_mla_g_kv, mla_w_uq=m_mla_w_uq,
             mla_w_ukv=m_mla_w_ukv, mla_w_o=m_mla_w_o, ffn_w_in=m_ffn_w_in, ffn_conv_w=m_ffn_conv_w,
             ffn_conv_b=m_ffn_conv_b, ffn_w_out=m_ffn_w_out, final_g=m_final_g)
    v = dict(ada_w=v_ada_w, ada_b=v_ada_b, fox_w_in=v_fox_w_in, fox_b_f=v_fox_b_f, fox_w_o=v_fox_w_o,
             mla_w_a=v_mla_w_a, mla_g_q=v_mla_g_q, mla_g_kv=v_mla_g_kv, mla_w_uq=v_mla_w_uq,
             mla_w_ukv=v_mla_w_ukv, mla_w_o=v_mla_w_o, ffn_w_in=v_ffn_w_in, ffn_conv_w=v_ffn_conv_w,
             ffn_conv_b=v_ffn_conv_b, ffn_w_out=v_ffn_w_out, final_g=v_final_g)
    D = D_MODEL
    xi, yi, ci = lax.axis_index("x"), lax.axis_index("y"), lax.axis_index("c")
    dev = 4 * xi + 2 * yi + ci

    n_ada = ada_w.shape[-1]
    small_parts = [c.reshape(-1), ffn_conv_w.reshape(-1), mla_g_q.reshape(-1), mla_g_kv.reshape(-1)]
    sizes = [p.shape[0] for p in small_parts]
    flat = jnp.concatenate(small_parts)
    flat = jnp.pad(flat, (0, -flat.shape[0] % LANE))[None]
    got = _all_gather8(flat, name="gather_cond")[:, 0]
    offs = np.cumsum([0] + sizes)
    c_all = got[:, offs[0]:offs[1]]
    chips = got[0::2]
    conv_w_full = jnp.concatenate(
        [chips[s, offs[1]:offs[2]].reshape(ffn_conv_w.shape) for s in range(4)], axis=2)
    gq_full = jnp.concatenate([chips[s, offs[2]:offs[3]] for s in range(4)])[None]
    gkv_full = jnp.concatenate([chips[s, offs[3]:offs[4]] for s in range(4)])[None]

    mod_shard = _ada_mod(c_all, ada_w.reshape(4, D, n_ada), ada_b.reshape(4, 1, n_ada))
    mod_all = _all_gather8(mod_shard.reshape(4 * 8, n_ada), name="gather_mod")
    mod_all = mod_all[0::2].reshape(4, 4, 8, n_ada)
    mine = lax.dynamic_index_in_dim(mod_all, dev, axis=2, keepdims=False)
    mod_rows = jnp.transpose(mine, (1, 0, 2)).reshape(4, 4 * n_ada)
    mods = [(mod_rows[k:k + 1, :D], mod_rows[k:k + 1, D:2 * D], mod_rows[k:k + 1, 2 * D:]) for k in range(4)]

    shards = [w[name].astype(BF16) for name in BIG]
    W = _prepare_fox_in(_all_gather_chips(shards[:1], name="gather_weights")[0])
    W["pending"] = shards[1:]
    small = dict(fox_b_f=fox_b_f, mla_g_q=gq_full, mla_g_kv=gkv_full, ffn_conv_w=conv_w_full,
                 ffn_conv_b=ffn_conv_b, final_g=final_g[None])

    loss, dx, dmods, G = _local_step(x[0], loss_target[0], mods, W, small)

    loss_row = jnp.pad(loss[:, :1], ((0, 0), (0, dmods.shape[1] - 1)))
    gathered_rows = _all_gather8(jnp.concatenate([dmods, loss_row]), name="gather_dmod")
    dmod_all = gathered_rows[:, :4]
    loss_total = jnp.sum(gathered_rows[:, 4, 0])
    chip = 2 * xi + yi
    dmod_cols = lax.dynamic_slice_in_dim(dmod_all, chip * n_ada, n_ada, axis=2)
    g_ada_w, g_ada_b = _ada_grad(c_all.T, jnp.transpose(dmod_cols, (1, 0, 2)))
    grads = dict(ada_w=g_ada_w.reshape(ada_w.shape), ada_b=g_ada_b.reshape(ada_b.shape))

    packed = _pack(G)
    recv = _exchange_halves(packed)
    part = _add_halves(packed, recv, ci.reshape(1).astype(jnp.int32))
    from_chips = _exchange_chips(part)
    half = _add_chips(from_chips)
    other = _share_halves(half)
    both = jnp.stack([jnp.where(ci == 0, half, other), jnp.where(ci == 0, other, half)])
    grads.update(_unpack(both, G))

    delta, new_m, new_v = {}, {}, {}
    for name in BIG + ["ada_w"]:
        shape = w[name].shape
        d_, m_, v_ = _adamw(_as2d(w[name]), _as2d(grads[name]), _as2d(m[name]), _as2d(v[name]), name=f"adamw_{name}")
        delta[name], new_m[name], new_v[name] = d_.reshape(shape), m_.reshape(shape), v_.reshape(shape)
        grads[name] = grads[name].reshape(shape)
    sizes = [math.prod(w[name].shape) for name in SMALL]
    total = sum(sizes)
    rows = -(-total // LANE)
    rows += -rows % 8

    def pack_small(d):
        flat = jnp.concatenate([d[name].reshape(-1) for name in SMALL])
        return jnp.pad(flat, (0, rows * LANE - total)).reshape(rows, LANE)

    outs = _adamw(pack_small(w), pack_small(grads), pack_small(m), pack_small(v), name="adamw_small")
    off = 0
    for name, size in zip(SMALL, sizes):
        shape = w[name].shape
        for dst, src in zip((delta, new_m, new_v), outs):
            dst[name] = src.reshape(-1)[off:off + size].reshape(shape)
        grads[name] = grads[name].reshape(shape)
        off += size

    return (loss_total, dx[None], *[grads[n] for n in WEIGHTS], *[delta[n] for n in WEIGHTS],
            *[new_m[n] for n in WEIGHTS], *[new_v[n] for n in WEIGHTS])
```
